```python
import jax, jax.numpy as jnp
from jax import lax
import numpy as np

D_MODEL = 1024
BATCH = 2
SEQ = 16384
DEPTH = 2

GRID_W = 64
CTX_LEN = 256
N_MOD = 6
EPS = 1e-6
ROPE_THETA = 10000.0
BLOCK = 128
CONV_DIM = 512
MLA_HEADS = 8
MLA_Q_LORA = 256
MLA_KV_LORA = 128
MLA_NOPE = 64
MLA_ROPE = 32
MLA_V = 64
SWA_HEADS = 16
SWA_KV_HEADS = 4
SWA_HEAD_DIM = 64
WINDOW = 128
N_EXPERTS = 16
EC_CAPACITY = 2
D_FF_EXPERT = 1024
AB_SPLITS = [CONV_DIM, 2 * CONV_DIM, 3 * CONV_DIM, 3 * CONV_DIM + MLA_Q_LORA, 3 * CONV_DIM + MLA_Q_LORA + MLA_KV_LORA]
AB_IN = 3 * CONV_DIM + MLA_Q_LORA + MLA_KV_LORA + MLA_ROPE
AB_OUT = CONV_DIM + MLA_HEADS * MLA_V
SWA_QKV = (SWA_HEADS + 2 * SWA_KV_HEADS) * SWA_HEAD_DIM
N_EVEN = (DEPTH + 1) // 2
N_ODD = DEPTH // 2

kernel_name = "hybrid_dit_conv_mla_swa_ecmoe"


def rms_norm(x, g):
    xf = x.astype(jnp.float32)
    y = xf * lax.rsqrt(jnp.mean(xf * xf, axis=-1, keepdims=True) + EPS)
    return (y * g.astype(jnp.float32)).astype(x.dtype)


def modulate(z, shift, scale):
    return z * (1 + scale) + shift


def axial_rope_tables(row, col, dim):
    half = dim // 2
    freqs = ROPE_THETA ** (-jnp.arange(0, half, 2, dtype=jnp.float32) / half)
    ang = jnp.concatenate([row[:, None] * freqs, col[:, None] * freqs], axis=-1)
    return jnp.cos(ang), jnp.sin(ang)


def apply_axial_rope(x, cos, sin):
    b, n, h, d = x.shape
    q = d // 4
    xr = x.astype(jnp.float32).reshape(b, n, h, 2, 2, q)
    x1, x2 = xr[..., 0, :], xr[..., 1, :]
    c = cos.reshape(n, 1, 2, q)
    s = sin.reshape(n, 1, 2, q)
    out = jnp.stack([x1 * c - x2 * s, x1 * s + x2 * c], axis=-2)
    return out.reshape(b, n, h, d).astype(x.dtype)


def softmax_attend(q, k, v, scale):
    s = jnp.einsum('bqhd,bkhd->bhqk', q, k).astype(jnp.float32) * scale
    p = jax.nn.softmax(s, axis=-1).astype(v.dtype)
    return jnp.einsum('bhqk,bkhd->bqhd', p, v)


def blocked_dense_attend(q, k, v, scale):
    b, n, h, dk = q.shape
    nb = n // BLOCK
    qb = jnp.moveaxis(q.reshape(b, nb, BLOCK, h, dk), 1, 0)
    ob = lax.map(lambda qi: softmax_attend(qi, k, v, scale), qb)
    return jnp.moveaxis(ob, 0, 1).reshape(b, n, h * v.shape[-1])


def short_conv_mixer(gate_b, gate_c, u, conv_w, conv_b):
    z = gate_c * u
    zp = jnp.pad(z, ((0, 0), (1, 1), (0, 0)))
    y = zp[:, :-2] * conv_w[0] + zp[:, 1:-1] * conv_w[1] + zp[:, 2:] * conv_w[2] + conv_b
    return gate_b * y


def mla_qkv(q_lat, kv_lat, k_rope, q_norm_g, w_uq, kv_norm_g, w_ukv, rope):
    b, n, _ = q_lat.shape
    q = (rms_norm(q_lat, q_norm_g) @ w_uq).reshape(b, n, MLA_HEADS, MLA_NOPE + MLA_ROPE)
    kv = (rms_norm(kv_lat, kv_norm_g) @ w_ukv).reshape(b, n, MLA_HEADS, MLA_NOPE + MLA_V)
    q_nope, q_rot = q[..., :MLA_NOPE], q[..., MLA_NOPE:]
    k_nope, v = kv[..., :MLA_NOPE], kv[..., MLA_NOPE:]
    k_rot = k_rope[:, :, None, :]
    if rope is not None:
        q_rot = apply_axial_rope(q_rot, *rope)
        k_rot = apply_axial_rope(k_rot, *rope)
    q = jnp.concatenate([q_nope, q_rot], axis=-1)
    k = jnp.concatenate([k_nope, jnp.broadcast_to(k_rot, (b, n, MLA_HEADS, MLA_ROPE))], axis=-1)
    return q, k, v


def mixer_conv_mla(h, hc, w_in, conv_w, conv_b, q_norm_g, w_uq, kv_norm_g, w_ukv, w_out, rope, need_ctx):
    b, n, _ = h.shape
    scale = (MLA_NOPE + MLA_ROPE) ** -0.5
    gb, gc, u, ql, kvl, kr = jnp.split(h @ w_in, AB_SPLITS, axis=-1)
    gbc, gcc, uc, qlc, kvlc, krc = jnp.split(hc @ w_in, AB_SPLITS, axis=-1)
    q, k, v = mla_qkv(ql, kvl, kr, q_norm_g, w_uq, kv_norm_g, w_ukv, rope)
    qc, kc, vc = mla_qkv(qlc, kvlc, krc, q_norm_g, w_uq, kv_norm_g, w_ukv, None)
    att = blocked_dense_attend(q, jnp.concatenate([kc, k], axis=1), jnp.concatenate([vc, v], axis=1), scale)
    y = jnp.concatenate([short_conv_mixer(gb, gc, u, conv_w, conv_b), att], axis=-1) @ w_out
    if not need_ctx:
        return y, None
    att_c = softmax_attend(qc, kc, vc, scale).reshape(b, hc.shape[1], MLA_HEADS * MLA_V)
    yc = jnp.concatenate([short_conv_mixer(gbc, gcc, uc, conv_w, conv_b), att_c], axis=-1) @ w_out
    return y, yc


def mixer_window_gqa(h, hc, w_qkv, sink, w_out, rope, need_ctx):
    b, n, _ = h.shape
    hd, kvh = SWA_HEAD_DIM, SWA_KV_HEADS
    grp = SWA_HEADS // kvh
    scale = hd ** -0.5

    def qkv(z):
        L = z.shape[1]
        p = z @ w_qkv
        q = p[..., :SWA_HEADS * hd].reshape(b, L, SWA_HEADS, hd)
        k = p[..., SWA_HEADS * hd:(SWA_HEADS + kvh) * hd].reshape(b, L, kvh, hd)
        v = p[..., (SWA_HEADS + kvh) * hd:].reshape(b, L, kvh, hd)
        return q, k, v

    q, k, v = qkv(h)
    qc, kc, vc = qkv(hc)
    L_ctx = hc.shape[1]
    q = apply_axial_rope(q, *rope).reshape(b, n, kvh, grp, hd)
    k = apply_axial_rope(k, *rope)
    sink_l = sink.astype(jnp.float32).reshape(kvh, grp)
    kp = jnp.pad(k, ((0, 0), (BLOCK, BLOCK), (0, 0), (0, 0)))
    vp = jnp.pad(v, ((0, 0), (BLOCK, BLOCK), (0, 0), (0, 0)))
    r = jnp.arange(BLOCK, dtype=jnp.int32)
    s = jnp.arange(3 * BLOCK, dtype=jnp.int32)

    def block(args):
        qi, i = args
        start = i * BLOCK
        kb = lax.dynamic_slice_in_dim(kp, start, 3 * BLOCK, axis=1)
        vb = lax.dynamic_slice_in_dim(vp, start, 3 * BLOCK, axis=1)
        qpos = start + r
        kpos = start - BLOCK + s
        valid = (jnp.abs(qpos[:, None] - kpos[None, :]) <= WINDOW) & (kpos[None, :] >= 0) & (kpos[None, :] < n)
        s_loc = jnp.einsum('bqkgd,bskd->bkgqs', qi, kb).astype(jnp.float32) * scale
        s_loc = jnp.where(valid, s_loc, -1e30)
        s_ctx = jnp.einsum('bqkgd,bskd->bkgqs', qi, kc).astype(jnp.float32) * scale
        sink_col = jnp.broadcast_to(sink_l[None, :, :, None, None], (b, kvh, grp, BLOCK, 1))
        p = jax.nn.softmax(jnp.concatenate([s_ctx, s_loc, sink_col], axis=-1), axis=-1).astype(v.dtype)
        return (jnp.einsum('bkgqs,bskd->bqkgd', p[..., :L_ctx], vc)
                + jnp.einsum('bkgqs,bskd->bqkgd', p[..., L_ctx:L_ctx + 3 * BLOCK], vb))

    nb = n // BLOCK
    qb = jnp.moveaxis(q.reshape(b, nb, BLOCK, kvh, grp, hd), 1, 0)
    ob = lax.map(block, (qb, jnp.arange(nb, dtype=jnp.int32)))
    y = jnp.moveaxis(ob, 0, 1).reshape(b, n, SWA_HEADS * hd) @ w_out
    if not need_ctx:
        return y, None
    qc = qc.reshape(b, L_ctx, kvh, grp, hd)
    sc = jnp.einsum('bqkgd,bskd->bkgqs', qc, kc).astype(jnp.float32) * scale
    sink_c = jnp.broadcast_to(sink_l[None, :, :, None, None], (b, kvh, grp, L_ctx, 1))
    pc = jax.nn.softmax(jnp.concatenate([sc, sink_c], axis=-1), axis=-1)[..., :L_ctx].astype(vc.dtype)
    yc = jnp.einsum('bkgqs,bskd->bqkgd', pc, vc).reshape(b, L_ctx, SWA_HEADS * hd) @ w_out
    return y, yc


def ec_moe(h, router_w, w_gate, w_up, w_down):
    b, n, _ = h.shape
    cap = max(1, (EC_CAPACITY * n) // N_EXPERTS)
    aff = jax.nn.softmax(jnp.einsum('bnd,de->bne', h, router_w).astype(jnp.float32), axis=-1)
    g, idx = lax.top_k(jnp.swapaxes(aff, 1, 2), cap)
    bidx = jnp.arange(b)[:, None, None]
    xg = h[bidx, idx]
    hid = jax.nn.silu(jnp.einsum('becd,edf->becf', xg, w_gate)) * jnp.einsum('becd,edf->becf', xg, w_up)
    ye = jnp.einsum('becf,efd->becd', hid, w_down) * g[..., None].astype(h.dtype)
    return jnp.zeros_like(h).at[bidx, idx].add(ye)


def setup_inputs(seed: int = 0) -> dict:
    key = jax.random.key(seed)
    ks = jax.random.split(key, 32)
    nrm = jax.random.normal
    D, F, E = D_MODEL, D_FF_EXPERT, N_EXPERTS
    f32 = jnp.float32
    return {
        "x": nrm(ks[0], (BATCH, SEQ, D), f32),
        "c": nrm(ks[1], (BATCH, D), f32),
        "ctx": nrm(ks[2], (BATCH, CTX_LEN, D), f32),
        "c_ctx": nrm(ks[3], (D,), f32),
        "mod_w": nrm(ks[4], (DEPTH, D, N_MOD * D), f32) * (0.5 * D ** -0.5),
        "mod_b": nrm(ks[5], (DEPTH, N_MOD * D), f32) * 0.01,
        "norm1_g": 1.0 + 0.01 * nrm(ks[6], (DEPTH, D), f32),
        "norm2_g": 1.0 + 0.01 * nrm(ks[7], (DEPTH, D), f32),
        "ab_w_in": nrm(ks[8], (N_EVEN, D, AB_IN), f32) * D ** -0.5,
        "conv_w": nrm(ks[9], (N_EVEN, 3, CONV_DIM), f32) * 3 ** -0.5,
        "conv_b": nrm(ks[10], (N_EVEN, CONV_DIM), f32) * 0.01,
        "mla_q_norm_g": 1.0 + 0.01 * nrm(ks[11], (N_EVEN, MLA_Q_LORA), f32),
        "mla_w_uq": nrm(ks[12], (N_EVEN, MLA_Q_LORA, MLA_HEADS * (MLA_NOPE + MLA_ROPE)), f32) * MLA_Q_LORA ** -0.5,
        "mla_kv_norm_g": 1.0 + 0.01 * nrm(ks[13], (N_EVEN, MLA_KV_LORA), f32),
        "mla_w_ukv": nrm(ks[14], (N_EVEN, MLA_KV_LORA, MLA_HEADS * (MLA_NOPE + MLA_V)), f32) * MLA_KV_LORA ** -0.5,
        "ab_w_out": nrm(ks[15], (N_EVEN, AB_OUT, D), f32) * AB_OUT ** -0.5,
        "swa_w_qkv": nrm(ks[16], (N_ODD, D, SWA_QKV), f32) * D ** -0.5,
        "swa_sink": nrm(ks[17], (N_ODD, SWA_HEADS), f32) * 0.5,
        "swa_w_out": nrm(ks[18], (N_ODD, SWA_HEADS * SWA_HEAD_DIM, D), f32) * (SWA_HEADS * SWA_HEAD_DIM) ** -0.5,
        "router_w": nrm(ks[19], (DEPTH, D, E), f32) * D ** -0.5,
        "exp_w_gate": nrm(ks[20], (DEPTH, E, D, F), f32) * D ** -0.5,
        "exp_w_up": nrm(ks[21], (DEPTH, E, D, F), f32) * D ** -0.5,
        "exp_w_down": nrm(ks[22], (DEPTH, E, F, D), f32) * F ** -0.5,
        "final_g": 1.0 + 0.01 * nrm(ks[23], (D,), f32),
    }


def reference(x, c, ctx, c_ctx, mod_w, mod_b, norm1_g, norm2_g, ab_w_in, conv_w, conv_b,
              mla_q_norm_g, mla_w_uq, mla_kv_norm_g, mla_w_ukv, ab_w_out, swa_w_qkv, swa_sink,
              swa_w_out, router_w, exp_w_gate, exp_w_up, exp_w_down, final_g):
    n = x.shape[1]
    n_rows = n // GRID_W
    row = jnp.repeat(jnp.arange(n_rows, dtype=jnp.float32), GRID_W)
    col = jnp.tile(jnp.arange(GRID_W, dtype=jnp.float32), n_rows)
    rope_mla = axial_rope_tables(row, col, MLA_ROPE)
    rope_swa = axial_rope_tables(row, col, SWA_HEAD_DIM)

    xs, xc = x, ctx
    for layer in range(DEPTH):
        need_ctx = layer < DEPTH - 1
        mod = jax.nn.silu(c) @ mod_w[layer] + mod_b[layer]
        sh1, sc1, g1, sh2, sc2, g2 = jnp.split(mod[:, None, :], N_MOD, axis=-1)
        modc = jax.nn.silu(c_ctx) @ mod_w[layer] + mod_b[layer]
        csh1, csc1, cg1, csh2, csc2, cg2 = jnp.split(modc, N_MOD, axis=-1)

        h = modulate(rms_norm(xs, norm1_g[layer]), sh1, sc1)
        hc = modulate(rms_norm(xc, norm1_g[layer]), csh1, csc1)
        if layer % 2 == 0:
            e = layer // 2
            y, yc = mixer_conv_mla(h, hc, ab_w_in[e], conv_w[e], conv_b[e], mla_q_norm_g[e], mla_w_uq[e],
                                   mla_kv_norm_g[e], mla_w_ukv[e], ab_w_out[e], rope_mla, need_ctx)
        else:
            o = layer // 2
            y, yc = mixer_window_gqa(h, hc, swa_w_qkv[o], swa_sink[o], swa_w_out[o], rope_swa, need_ctx)
        xs = xs + g1 * y

        h2 = modulate(rms_norm(xs, norm2_g[layer]), sh2, sc2)
        xs = xs + g2 * ec_moe(h2, router_w[layer], exp_w_gate[layer], exp_w_up[layer], exp_w_down[layer])
        if need_ctx:
            xc = xc + cg1 * yc
            hc2 = modulate(rms_norm(xc, norm2_g[layer]), csh2, csc2)
            xc = xc + cg2 * ec_moe(hc2, router_w[layer], exp_w_gate[layer], exp_w_up[layer], exp_w_down[layer])
    return rms_norm(xs, final_g)
```

```python
import functools
import math

import jax
import jax.numpy as jnp
import numpy as np
from jax import lax
from jax.experimental import pallas as pl
from jax.experimental.pallas import tpu as pltpu

F32 = jnp.float32
BF16 = jnp.bfloat16
I32 = jnp.int32

GRID_W = 64
EPS = 1e-6
ROPE_THETA = 10000.0
CONV_DIM = 512
MLA_HEADS = 8
MLA_Q_LORA = 256
MLA_KV_LORA = 128
MLA_NOPE = 64
MLA_ROPE = 32
MLA_V = 64
SWA_HEADS = 16
SWA_KV_HEADS = 4
SWA_HEAD_DIM = 64
WINDOW = 128
SWA_BLOCK = 128
N_EXPERTS = 16
EC_CAPACITY = 2
N_MOD = 6

LANES = 128
ROW_ALIGN = 16
VMEM_LIMIT = 56 * 1024 * 1024


def _cparams(sem, vmem=VMEM_LIMIT):
    return pltpu.CompilerParams(dimension_semantics=sem, vmem_limit_bytes=vmem)


def _dot(a, b):
    return jnp.dot(a, b, preferred_element_type=F32)


def _dot_nt(a, b):
    return lax.dot_general(a, b, (((1,), (1,)), ((), ())), preferred_element_type=F32)


def _dot_tn(a, b):
    return lax.dot_general(a, b, (((0,), (0,)), ((), ())), preferred_element_type=F32)


def _dot_hi(a, b):
    return jnp.dot(a, b, preferred_element_type=F32, precision=lax.Precision.HIGHEST)


def _silu(x):
    return x * (1.0 / (1.0 + jnp.exp(-x)))


def _rms_mod(x, g, shift, scale):
    y = x * lax.rsqrt(jnp.mean(x * x, axis=-1, keepdims=True) + EPS)
    return (y * g) * (1.0 + scale) + shift


def _mod_row(mod_ref, ctx, which):
    r = 2 if ctx else pl.program_id(0)
    return mod_ref[pl.ds(r, 1), which * 1024:(which + 1) * 1024]


def _mod_kernel(cs_ref, w_ref, b_ref, o_ref):
    s = _silu(cs_ref[...])
    o_ref[0] = _dot_hi(s, w_ref[0]) + b_ref[0]


def modulation(cs, mod_w, mod_b):
    depth, d, n6 = mod_w.shape
    tn = 1536
    return pl.pallas_call(
        _mod_kernel,
        grid=(depth, n6 // tn),
        in_specs=[
            pl.BlockSpec((8, d), lambda l, j: (0, 0)),
            pl.BlockSpec((1, d, tn), lambda l, j: (l, 0, j)),
            pl.BlockSpec((1, 1, tn), lambda l, j: (l, 0, j)),
        ],
        out_specs=pl.BlockSpec((1, 8, tn), lambda l, j: (l, 0, j)),
        out_shape=jax.ShapeDtypeStruct((depth, 8, n6), F32),
        compiler_params=_cparams(("arbitrary", "arbitrary")),
        name="modulation",
    )(cs, mod_w, mod_b.reshape(depth, 1, n6))


def _rope_tables(n, dim):
    n_rows = n // GRID_W
    row = jnp.repeat(jnp.arange(n_rows, dtype=F32), GRID_W)
    col = jnp.tile(jnp.arange(GRID_W, dtype=F32), n_rows)
    half = dim // 2
    qd = dim // 4
    freqs = ROPE_THETA ** (-jnp.arange(0, half, 2, dtype=F32) / half)
    ang = jnp.concatenate([row[:, None] * freqs, col[:, None] * freqs], axis=-1)
    cos, sin = jnp.cos(ang), jnp.sin(ang)
    cos_full = jnp.concatenate([cos[:, :qd], cos[:, :qd], cos[:, qd:], cos[:, qd:]], axis=-1)
    sin_sgn = jnp.concatenate([-sin[:, :qd], sin[:, :qd], -sin[:, qd:], sin[:, qd:]], axis=-1)
    return cos_full, sin_sgn


def _rope_perm(dim):
    qd = dim // 4
    ch = np.arange(dim)
    pair = (ch // qd) % 2
    return np.where(pair == 0, ch + qd, ch - qd)


def _ab_in_kernel(x_ref, mod_ref, g_ref, win_ref, qg_ref, wq_ref, kvg_ref, wkv_ref, cos_ref, sin_ref,
                  gb_ref, z_ref, q_ref, k_ref, v_ref, *, ctx, scale):
    x = x_ref[0]
    h = _rms_mod(x, g_ref[...], _mod_row(mod_ref, ctx, 0), _mod_row(mod_ref, ctx, 1))
    p = _dot(h.astype(BF16), win_ref[...])
    gb_ref[0] = p[:, 0:512].astype(BF16)
    z_ref[0] = (p[:, 512:1024] * p[:, 1024:1536]).astype(BF16)
    ql = p[:, 1536:1792]
    kvl = p[:, 1792:1920]
    kra = p[:, 1920:2048]
    krb = p[:, 2048:2176]
    cos = cos_ref[...]
    sin = sin_ref[...]
    qn = ql * lax.rsqrt(jnp.mean(ql * ql, axis=-1, keepdims=True) + EPS) * qg_ref[...]
    qq = _dot(qn.astype(BF16), wq_ref[...])
    kvn = kvl * lax.rsqrt(jnp.mean(kvl * kvl, axis=-1, keepdims=True) + EPS) * kvg_ref[...]
    kv = _dot(kvn.astype(BF16), wkv_ref[...])
    krope = kra * cos + krb * sin
    cs = cos * scale
    ss = sin * scale
    for hd in range(MLA_HEADS):
        a = qq[:, hd * 128:(hd + 1) * 128]
        b = qq[:, 1024 + hd * 128:1024 + (hd + 1) * 128]
        q_ref[0, hd] = (a * cs + b * ss).astype(BF16)
        k_ref[0, hd] = (kv[:, hd * 128:(hd + 1) * 128] + krope).astype(BF16)
        v_ref[0, hd] = kv[:, 1024 + hd * 128:1024 + (hd + 1) * 128].astype(BF16)


def ab_in(x, mod_l, g, win2, qg, wq2, kvg, wkv2, cos_t, sin_t, *, ctx, tm):
    bsz, n, d = x.shape
    tm = min(tm, n)
    scale = (MLA_NOPE + MLA_ROPE) ** -0.5
    full = lambda a: pl.BlockSpec(a.shape, lambda b, i: (0,) * a.ndim)
    hshape = jax.ShapeDtypeStruct((bsz, MLA_HEADS, n, 128), BF16)
    hspec = pl.BlockSpec((1, MLA_HEADS, tm, 128), lambda b, i: (b, 0, i, 0))
    return pl.pallas_call(
        functools.partial(_ab_in_kernel, ctx=ctx, scale=scale),
        grid=(bsz, n // tm),
        in_specs=[
            pl.BlockSpec((1, tm, d), lambda b, i: (b, i, 0)),
            full(mod_l), full(g), full(win2), full(qg), full(wq2), full(kvg), full(wkv2),
            pl.BlockSpec((tm, 128), lambda b, i: (i, 0)),
            pl.BlockSpec((tm, 128), lambda b, i: (i, 0)),
        ],
        out_specs=[
            pl.BlockSpec((1, tm, 512), lambda b, i: (b, i, 0)),
            pl.BlockSpec((1, tm, 512), lambda b, i: (b, i, 0)),
            hspec, hspec, hspec,
        ],
        out_shape=[
            jax.ShapeDtypeStruct((bsz, n, 512), BF16),
            jax.ShapeDtypeStruct((bsz, n, 512), BF16),
            hshape, hshape, hshape,
        ],
        compiler_params=_cparams(("arbitrary", "arbitrary")),
        name="ab_in_ctx" if ctx else "ab_in",
    )(x, mod_l, g, win2, qg, wq2, kvg, wkv2, cos_t, sin_t)


def _mla_kernel(q_ref, kc_ref, vc_ref, *rest, tk, n_chunks):
    if n_chunks:
        k_ref, v_ref, o_ref = rest
    else:
        (o_ref,) = rest
    q = q_ref[0, 0]
    s = _dot_nt(q, kc_ref[0, 0])
    m = jnp.max(s, axis=-1, keepdims=True)
    p = jnp.exp(s - m)
    l = jnp.sum(p, axis=-1, keepdims=True)
    acc = _dot(p.astype(BF16), vc_ref[0, 0])

    if n_chunks:
        def body(j, carry):
            m, l, acc = carry
            off = pl.multiple_of(j * tk, tk)
            kj = k_ref[0, 0, pl.ds(off, tk), :]
            vj = v_ref[0, 0, pl.ds(off, tk), :]
            s = _dot_nt(q, kj)
            m_new = jnp.maximum(m, jnp.max(s, axis=-1, keepdims=True))
            alpha = jnp.exp(m - m_new)
            p = jnp.exp(s - m_new)
            l = alpha * l + jnp.sum(p, axis=-1, keepdims=True)
            acc = alpha * acc + _dot(p.astype(BF16), vj)
            return m_new, l, acc

        m, l, acc = lax.fori_loop(0, n_chunks, body, (m, l, acc))
    o_ref[0] = (acc * (1.0 / l)).astype(BF16)


def mla_attention(q, kc, vc, k=None, v=None, *, tq, tk):
    bsz, nh, n, dk = q.shape
    lc = kc.shape[2]
    tq = min(tq, n)
    n_chunks = 0
    ins = [q, kc, vc]
    in_specs = [
        pl.BlockSpec((1, 1, tq, dk), lambda b, h, i: (b, h, i, 0)),
        pl.BlockSpec((1, 1, lc, dk), lambda b, h, i: (b, h, 0, 0)),
        pl.BlockSpec((1, 1, lc, dk), lambda b, h, i: (b, h, 0, 0)),
    ]
    if k is not None:
        nk = k.shape[2]
        tk = min(tk, nk)
        n_chunks = nk // tk
        ins += [k, v]
        in_specs += [
            pl.BlockSpec((1, 1, nk, dk), lambda b, h, i: (b, h, 0, 0)),
            pl.BlockSpec((1, 1, nk, dk), lambda b, h, i: (b, h, 0, 0)),
        ]
    return pl.pallas_call(
        functools.partial(_mla_kernel, tk=tk, n_chunks=n_chunks),
        grid=(bsz, nh, n // tq),
        in_specs=in_specs,
        out_specs=pl.BlockSpec((1, tq, dk), lambda b, h, i: (b, i, h)),
        out_shape=jax.ShapeDtypeStruct((bsz, n, nh * dk), BF16),
        compiler_params=_cparams(("arbitrary", "arbitrary", "arbitrary")),
        name="mla_attention" if k is not None else "mla_attention_ctx",
    )(*ins)


def _mix_out_kernel(*refs, ctx, has_conv, tm, n_tiles):
    if has_conv:
        (gb_ref, z_ref, zp_ref, zn_ref, cw_ref, cb_ref, wc_ref,
         att_ref, wa_ref, x_ref, mod_ref, g2_ref, rw_ref, xs_ref, h2_ref, aff_ref) = refs
    else:
        att_ref, wa_ref, x_ref, mod_ref, g2_ref, rw_ref, xs_ref, h2_ref, aff_ref = refs
    y = _dot(att_ref[0], wa_ref[...])
    if has_conv:
        i = pl.program_id(1)
        z = z_ref[0].astype(F32)
        rows = lax.broadcasted_iota(I32, z.shape, 0)
        zprev_halo = jnp.where(i > 0, zp_ref[0, ROW_ALIGN - 1:ROW_ALIGN, :].astype(F32), 0.0)
        znext_halo = jnp.where(i < n_tiles - 1, zn_ref[0, 0:1, :].astype(F32), 0.0)
        zprev = jnp.where(rows == 0, zprev_halo, pltpu.roll(z, 1, 0))
        znext = jnp.where(rows == tm - 1, znext_halo, pltpu.roll(z, tm - 1, 0))
        cw = cw_ref[...]
        conv = gb_ref[0].astype(F32) * (zprev * cw[0:1] + z * cw[1:2] + znext * cw[2:3] + cb_ref[...])
        y = y + _dot(conv.astype(BF16), wc_ref[...])
    xs = x_ref[0] + _mod_row(mod_ref, ctx, 2) * y
    xs_ref[0] = xs
    h2 = _rms_mod(xs, g2_ref[...], _mod_row(mod_ref, ctx, 3), _mod_row(mod_ref, ctx, 4))
    h2_ref[0] = h2.astype(BF16)
    logits = _dot_hi(h2, rw_ref[...])
    lane = lax.broadcasted_iota(I32, logits.shape, 1)
    logits = jnp.where(lane < N_EXPERTS, logits, -1e30)
    mx = jnp.max(logits, axis=-1, keepdims=True)
    ex = jnp.exp(logits - mx)
    aff = ex / jnp.sum(ex, axis=-1, keepdims=True)
    aff_ref[0] = aff[:, :N_EXPERTS]


def mix_out(att, wa, x, mod_l, g2, rw, conv=None, *, ctx, tm):
    bsz, n, d = x.shape
    tm = min(tm, n)
    n_tiles = n // tm
    full = lambda a: pl.BlockSpec(a.shape, lambda b, i: (0,) * a.ndim)
    row = lambda w: pl.BlockSpec((1, tm, w), lambda b, i: (b, i, 0))
    ins, in_specs = [], []
    if conv is not None:
        gb, z, cw, cb, wc = conv
        r8 = tm // ROW_ALIGN
        nb8 = n // ROW_ALIGN
        ins += [gb, z, z, z, cw, cb, wc]
        in_specs += [
            row(512), row(512),
            pl.BlockSpec((1, ROW_ALIGN, 512), lambda b, i: (b, jnp.maximum(i * r8 - 1, 0), 0)),
            pl.BlockSpec((1, ROW_ALIGN, 512), lambda b, i: (b, jnp.minimum((i + 1) * r8, nb8 - 1), 0)),
            full(cw), full(cb), full(wc),
        ]
    ins += [att, wa, x, mod_l, g2, rw]
    in_specs += [row(att.shape[-1]), full(wa), row(d), full(mod_l), full(g2), full(rw)]
    return pl.pallas_call(
        functools.partial(_mix_out_kernel, ctx=ctx, has_conv=conv is not None, tm=tm, n_tiles=n_tiles),
        grid=(bsz, n_tiles),
        in_specs=in_specs,
        out_specs=[row(d), row(d), row(N_EXPERTS)],
        out_shape=[
            jax.ShapeDtypeStruct((bsz, n, d), F32),
            jax.ShapeDtypeStruct((bsz, n, d), BF16),
            jax.ShapeDtypeStruct((bsz, n, N_EXPERTS), F32),
        ],
        compiler_params=_cparams(("arbitrary", "arbitrary")),
        name=("mix_out_ctx" if ctx else "mix_out") + ("_conv" if conv is not None else ""),
    )(*ins)


def _route_kernel(aff_ref, ls_ref, us_ref, eye_ref, rank_ref, rankt_ref, oa_ref, nw_ref,
                  thr_s, need_s, eq_s, oa_s, *, n, cap, tb, win, chunk):
    n_chunks = n // chunk
    nblk = n // tb
    k = pl.program_id(1)

    def count(pred_fn):
        def body(c, acc):
            kb = pltpu.bitcast(aff_ref[0, pl.ds(pl.multiple_of(c * chunk, chunk), chunk), :], I32)
            return acc + jnp.sum(pred_fn(kb).astype(I32), axis=0, keepdims=True)
        return lax.fori_loop(0, n_chunks, body, jnp.zeros((1, N_EXPERTS), I32))

    @pl.when(k == 0)
    def _():
        def bit_body(i, thr):
            cand = thr | jnp.left_shift(jnp.int32(1), 30 - i)
            cnt = count(lambda kb: kb >= cand)
            return jnp.where(cnt >= cap, cand, thr)

        thr = lax.fori_loop(0, 31, bit_body, jnp.zeros((1, N_EXPERTS), I32))
        thr_s[...] = thr
        need_s[...] = (cap - count(lambda kb: kb > thr)).astype(F32)
        eq_s[...] = jnp.zeros((1, N_EXPERTS), F32)
        oa_s[...] = jnp.zeros((1, N_EXPERTS), F32)

    thr = thr_s[...]
    off = pl.multiple_of(k * tb, tb)
    kb = pltpu.bitcast(aff_ref[0, pl.ds(off, tb), :], I32)
    gt = kb > thr
    eq = kb == thr
    eqf = jnp.where(eq, 1.0, 0.0)
    eqrank = eq_s[...] + _dot(ls_ref[...], eqf.astype(BF16))
    sel = jnp.logical_or(gt, jnp.logical_and(eq, eqrank < need_s[...]))
    self_ = jnp.where(sel, 1.0, 0.0)
    selb = self_.astype(BF16)
    lrank = _dot(ls_ref[...], selb)
    rank_ref[0] = jnp.where(sel, lrank, -1.0)
    lrank_t = _dot_tn(selb, us_ref[...])
    sel_t = _dot_tn(selb, eye_ref[...])
    rankt_ref[0] = jnp.where(sel_t > 0.5, lrank_t, -1.0)
    c = jnp.sum(self_, axis=0, keepdims=True)
    oa_run = oa_s[...]
    oa_ref[0, pl.ds(k, 1), :] = oa_run.astype(I32)
    nwin = jnp.max(jnp.ceil(c * (1.0 / win)), axis=1, keepdims=True)
    nw_ref[0, pl.ds(k, 1), :] = jnp.broadcast_to(nwin, (1, N_EXPERTS)).astype(I32)
    oa_new = oa_run + jnp.ceil(c * (1.0 / ROW_ALIGN)) * ROW_ALIGN
    oa_s[...] = oa_new
    eq_s[...] = eq_s[...] + jnp.sum(eqf, axis=0, keepdims=True)

    @pl.when(k == nblk - 1)
    def _():
        oa_ref[0, pl.ds(nblk, 1), :] = oa_new.astype(I32)


def route(aff, *, tb, win):
    bsz, n, ne = aff.shape
    cap = max(1, (EC_CAPACITY * n) // ne)
    nblk = n // tb
    chunk = min(1024, n)
    ii = np.arange(tb)
    ls = jnp.asarray(ii[None, :] < ii[:, None], BF16)
    us = jnp.asarray(ii[:, None] < ii[None, :], BF16)
    eye = jnp.asarray(ii[:, None] == ii[None, :], BF16)
    full = lambda a: pl.BlockSpec(a.shape, lambda b, k: (0,) * a.ndim)
    return pl.pallas_call(
        functools.partial(_route_kernel, n=n, cap=cap, tb=tb, win=win, chunk=chunk),
        grid=(bsz, nblk),
        in_specs=[pl.BlockSpec((1, n, ne), lambda b, k: (b, 0, 0)), full(ls), full(us), full(eye)],
        out_specs=[
            pl.BlockSpec((1, tb, ne), lambda b, k: (b, k, 0)),
            pl.BlockSpec((1, ne, tb), lambda b, k: (b, 0, k)),
            pl.BlockSpec((1, nblk + 1, ne), lambda b, k: (b, 0, 0)),
            pl.BlockSpec((1, nblk, ne), lambda b, k: (b, 0, 0)),
        ],
        out_shape=[
            jax.ShapeDtypeStruct((bsz, n, ne), F32),
            jax.ShapeDtypeStruct((bsz, ne, n), F32),
            jax.ShapeDtypeStruct((bsz, nblk + 1, ne), I32),
            jax.ShapeDtypeStruct((bsz, nblk, ne), I32),
        ],
        scratch_shapes=[pltpu.VMEM((1, ne), I32), pltpu.VMEM((1, ne), F32),
                        pltpu.VMEM((1, ne), F32), pltpu.VMEM((1, ne), F32)],
        compiler_params=_cparams(("arbitrary", "arbitrary")),
        name="route",
    )(aff, ls, us, eye)


def _dispatch_kernel(oa_ref, nw_ref, rankt_ref, h_ref, xg_in_ref, xg_ref, stack, sem,
                     *, nblk, n_steps, win, tb, group):
    del xg_in_ref
    b = pl.program_id(0)
    k = pl.program_id(1)
    step = b * nblk + k
    slot = step % 2
    ne = N_EXPERTS

    def fill(sl, j):
        h = h_ref[0]
        for g0 in range(0, ne, group):
            pieces = []
            for e in range(g0, g0 + group):
                r = rankt_ref[0, e:e + 1, :]
                srow = lax.broadcasted_iota(I32, (win, tb), 0).astype(F32) + (j * win).astype(F32)
                pieces.append((r == srow).astype(BF16))
            oh_t = jnp.concatenate(pieces, axis=0)
            stack[sl, g0 * win:(g0 + group) * win, :] = _dot(oh_t, h).astype(BF16)

    def copies(sl, bb, kk, j):
        out = []
        for e in range(ne):
            off = pl.multiple_of(oa_ref[(bb * (nblk + 1) + kk) * ne + e] + j * win, ROW_ALIGN)
            out.append(pltpu.make_async_copy(
                stack.at[sl, pl.ds(e * win, win), :],
                xg_ref.at[bb, e, pl.ds(off, win), :],
                sem.at[sl]))
        return out

    fill(slot, jnp.int32(0))

    @pl.when(step > 0)
    def _():
        for c in copies(1 - slot, b, k, 0):
            c.wait()

    for c in copies(slot, b, k, 0):
        c.start()

    nwx = nw_ref[(b * nblk + k) * ne]

    def extra(j, carry):
        fill(2, j)
        cs = copies(2, b, k, j)
        for c in cs:
            c.start()
        for c in cs:
            c.wait()
        return carry

    lax.fori_loop(1, jnp.maximum(nwx, 1), extra, 0)

    @pl.when(step == n_steps - 1)
    def _():
        for c in copies(slot, b, k, 0):
            c.wait()


def dispatch(oa, nw, rank_t, h2, *, tb, win, c_pad):
    bsz, n, d = h2.shape
    ne = N_EXPERTS
    nblk = n // tb
    group = 4
    xg0 = jnp.zeros((bsz, ne, c_pad, d), BF16)
    grid_spec = pltpu.PrefetchScalarGridSpec(
        num_scalar_prefetch=2,
        grid=(bsz, nblk),
        in_specs=[
            pl.BlockSpec((1, ne, tb), lambda b, k, *_: (b, 0, k)),
            pl.BlockSpec((1, tb, d), lambda b, k, *_: (b, k, 0)),
            pl.BlockSpec(memory_space=pl.ANY),
        ],
        out_specs=pl.BlockSpec(memory_space=pl.ANY),
        scratch_shapes=[
            pltpu.VMEM((3, ne * win, d), BF16),
            pltpu.SemaphoreType.DMA((3,)),
        ],
    )
    return pl.pallas_call(
        functools.partial(_dispatch_kernel, nblk=nblk, n_steps=bsz * nblk, win=win, tb=tb, group=group),
        grid_spec=grid_spec,
        out_shape=jax.ShapeDtypeStruct(xg0.shape, BF16),
        input_output_aliases={4: 0},
        compiler_params=_cparams(("arbitrary", "arbitrary")),
        name="dispatch",
    )(oa.reshape(-1), nw.reshape(-1), rank_t, h2, xg0)


def _ffn_kernel(tot_ref, x_ref, wg_ref, wu_ref, wd_ref, y_ref, wgb, wub, wdb, *, tm, nblk):
    e = pl.program_id(0)
    b = pl.program_id(1)
    i = pl.program_id(2)

    @pl.when(jnp.logical_and(b == 0, i == 0))
    def _():
        wgb[...] = wg_ref[0].astype(BF16)
        wub[...] = wu_ref[0].astype(BF16)
        wdb[...] = wd_ref[0].astype(BF16)

    total = tot_ref[(b * (nblk + 1) + nblk) * N_EXPERTS + e]

    @pl.when(i * tm < total)
    def _():
        x = x_ref[0, 0]
        a = _dot(x, wgb[...])
        u = _dot(x, wub[...])
        hid = (_silu(a) * u).astype(BF16)
        y_ref[0, 0] = _dot(hid, wdb[...]).astype(BF16)

    @pl.when(i * tm >= total)
    def _():
        y_ref[0, 0] = jnp.zeros(y_ref.shape[2:], BF16)


def expert_ffn(oa, xg, w_gate, w_up, w_down, *, tm, nblk):
    bsz, ne, c_pad, d = xg.shape
    f = w_gate.shape[-1]
    grid_spec = pltpu.PrefetchScalarGridSpec(
        num_scalar_prefetch=1,
        grid=(ne, bsz, c_pad // tm),
        in_specs=[
            pl.BlockSpec((1, 1, tm, d), lambda e, b, i, *_: (b, e, i, 0)),
            pl.BlockSpec((1, d, f), lambda e, b, i, *_: (e, 0, 0)),
            pl.BlockSpec((1, d, f), lambda e, b, i, *_: (e, 0, 0)),
            pl.BlockSpec((1, f, d), lambda e, b, i, *_: (e, 0, 0)),
        ],
        out_specs=pl.BlockSpec((1, 1, tm, d), lambda e, b, i, *_: (b, e, i, 0)),
        scratch_shapes=[pltpu.VMEM((d, f), BF16), pltpu.VMEM((d, f), BF16), pltpu.VMEM((f, d), BF16)],
    )
    return pl.pallas_call(
        functools.partial(_ffn_kernel, tm=tm, nblk=nblk),
        grid_spec=grid_spec,
        out_shape=jax.ShapeDtypeStruct(xg.shape, BF16),
        compiler_params=_cparams(("arbitrary", "arbitrary", "arbitrary")),
        name="expert_ffn",
    )(oa.reshape(-1), xg, w_gate, w_up, w_down)


def _combine_kernel(oa_ref, nw_ref, rank_ref, aff_ref, xs_ref, mod_ref, *rest,
                    ctx, nblk, n_steps, win, tb, final):
    if final:
        fg_ref, ye_ref, o_ref, stack, sem = rest
    else:
        ye_ref, o_ref, stack, sem = rest
    b = pl.program_id(0)
    k = pl.program_id(1)
    step = b * nblk + k
    slot = step % 2
    ne = N_EXPERTS

    def copies(sl, bb, kk, j):
        out = []
        for e in range(ne):
            off = pl.multiple_of(oa_ref[(bb * (nblk + 1) + kk) * ne + e] + j * win, ROW_ALIGN)
            out.append(pltpu.make_async_copy(
                ye_ref.at[bb, e, pl.ds(off, win), :],
                stack.at[sl, pl.ds(e * win, win), :],
                sem.at[sl]))
        return out

    @pl.when(step == 0)
    def _():
        for c in copies(0, b, k, 0):
            c.start()

    @pl.when(step + 1 < n_steps)
    def _():
        nxt = step + 1
        for c in copies(1 - slot, nxt // nblk, nxt % nblk, 0):
            c.start()

    for c in copies(slot, b, k, 0):
        c.wait()

    def gated_onehot(j):
        rank = rank_ref[0]
        aff = aff_ref[0]
        pieces = []
        for e in range(ne):
            lane = lax.broadcasted_iota(I32, (tb, win), 1).astype(F32) + (j * win).astype(F32)
            pieces.append(jnp.where(rank[:, e:e + 1] == lane, aff[:, e:e + 1], 0.0).astype(BF16))
        return jnp.concatenate(pieces, axis=1)

    y = _dot(gated_onehot(jnp.int32(0)), stack[slot])

    nwx = nw_ref[(b * nblk + k) * ne]

    def extra(j, y):
        cs = copies(2, b, k, j)
        for c in cs:
            c.start()
        for c in cs:
            c.wait()
        return y + _dot(gated_onehot(j), stack[2])

    y = lax.fori_loop(1, jnp.maximum(nwx, 1), extra, y)
    out = xs_ref[0] + _mod_row(mod_ref, ctx, 5) * y
    if final:
        out = out * lax.rsqrt(jnp.mean(out * out, axis=-1, keepdims=True) + EPS) * fg_ref[...]
    o_ref[0] = out


def combine(oa, nw, rank, aff, xs, mod_l, ye, final_g=None, *, ctx, tb, win):
    bsz, n, d = xs.shape
    ne = N_EXPERTS
    nblk = n // tb
    final = final_g is not None
    full = lambda a: pl.BlockSpec(a.shape, lambda b, k, *_: (0,) * a.ndim)
    ins = [rank, aff, xs, mod_l]
    in_specs = [
        pl.BlockSpec((1, tb, ne), lambda b, k, *_: (b, k, 0)),
        pl.BlockSpec((1, tb, ne), lambda b, k, *_: (b, k, 0)),
        pl.BlockSpec((1, tb, d), lambda b, k, *_: (b, k, 0)),
        full(mod_l),
    ]
    if final:
        ins.append(final_g)
        in_specs.append(full(final_g))
    ins.append(ye)
    in_specs.append(pl.BlockSpec(memory_space=pl.ANY))
    grid_spec = pltpu.PrefetchScalarGridSpec(
        num_scalar_prefetch=2,
        grid=(bsz, nblk),
        in_specs=in_specs,
        out_specs=pl.BlockSpec((1, tb, d), lambda b, k, *_: (b, k, 0)),
        scratch_shapes=[
            pltpu.VMEM((3, ne * win, d), BF16),
            pltpu.SemaphoreType.DMA((3,)),
        ],
    )
    return pl.pallas_call(
        functools.partial(_combine_kernel, ctx=ctx, nblk=nblk, n_steps=bsz * nblk, win=win, tb=tb, final=final),
        grid_spec=grid_spec,
        out_shape=jax.ShapeDtypeStruct(xs.shape, F32),
        compiler_params=_cparams(("arbitrary", "arbitrary")),
        name="combine_final" if final else ("combine_ctx" if ctx else "combine"),
    )(oa.reshape(-1), nw.reshape(-1), *ins)


def moe(xs1, h2, aff, mod_l, w_gate, w_up, w_down, final_g=None, *, ctx):
    bsz, n, d = xs1.shape
    cap = max(1, (EC_CAPACITY * n) // N_EXPERTS)
    tb = min(512, n)
    win = LANES
    nblk = n // tb
    tm = 384 if n >= 4096 else 128
    need = cap + ROW_ALIGN * nblk + win
    c_pad = -(-need // tm) * tm
    rank, rank_t, oa, nw = route(aff, tb=tb, win=win)
    xg = dispatch(oa, nw, rank_t, h2, tb=tb, win=win, c_pad=c_pad)
    ye = expert_ffn(oa, xg, w_gate, w_up, w_down, tm=tm, nblk=nblk)
    return combine(oa, nw, rank, aff, xs1, mod_l, ye, final_g, ctx=ctx, tb=tb, win=win)


def _swa_in_kernel(x_ref, mod_ref, g_ref, w_ref, cos_ref, sin_ref, q_ref, k_ref, v_ref, *, ctx, scale):
    h = _rms_mod(x_ref[0], g_ref[...], _mod_row(mod_ref, ctx, 0), _mod_row(mod_ref, ctx, 1))
    p = _dot(h.astype(BF16), w_ref[...])
    cos = cos_ref[...]
    sin = sin_ref[...]
    cs = cos * scale
    ss = sin * scale
    for j in range(8):
        a = p[:, j * 128:(j + 1) * 128]
        bsw = p[:, 1024 + j * 128:1024 + (j + 1) * 128]
        q_ref[0, :, j * 128:(j + 1) * 128] = (a * cs + bsw * ss).astype(BF16)
    lane = lax.broadcasted_iota(I32, cos.shape, 1)
    low = lane < 64
    for pr in range(2):
        kp = p[:, 2048 + pr * 128:2048 + (pr + 1) * 128] * cos + p[:, 2304 + pr * 128:2304 + (pr + 1) * 128] * sin
        vp = p[:, 2560 + pr * 128:2560 + (pr + 1) * 128]
        for src, dst in ((kp, k_ref), (vp, v_ref)):
            ev_lo = jnp.where(low, src, 0.0)
            od_hi = jnp.where(low, 0.0, src)
            ev_hi = pltpu.roll(ev_lo, 64, 1)
            od_lo = pltpu.roll(od_hi, 64, 1)
            base = pr * 512
            dst[0, :, base:base + 128] = ev_lo.astype(BF16)
            dst[0, :, base + 128:base + 256] = ev_hi.astype(BF16)
            dst[0, :, base + 256:base + 384] = od_lo.astype(BF16)
            dst[0, :, base + 384:base + 512] = od_hi.astype(BF16)


def swa_in(x, mod_l, g, w2, cos_t, sin_t, *, ctx, tm):
    bsz, n, d = x.shape
    tm = min(tm, n)
    full = lambda a: pl.BlockSpec(a.shape, lambda b, i: (0,) * a.ndim)
    row = lambda w: pl.BlockSpec((1, tm, w), lambda b, i: (b, i, 0))
    return pl.pallas_call(
        functools.partial(_swa_in_kernel, ctx=ctx, scale=SWA_HEAD_DIM ** -0.5),
        grid=(bsz, n // tm),
        in_specs=[row(d), full(mod_l), full(g), full(w2),
                  pl.BlockSpec((tm, 128), lambda b, i: (i, 0)),
                  pl.BlockSpec((tm, 128), lambda b, i: (i, 0))],
        out_specs=[row(1024), row(1024), row(1024)],
        out_shape=[jax.ShapeDtypeStruct((bsz, n, 1024), BF16)] * 3,
        compiler_params=_cparams(("arbitrary", "arbitrary")),
        name="swa_in_ctx" if ctx else "swa_in",
    )(x, mod_l, g, w2, cos_t, sin_t)


def _swa_kernel(sink_ref, q_ref, km_ref, kp_ref, kn_ref, vm_ref, vp_ref, vn_ref, kc_ref, vc_ref,
                o_ref, kwin, vwin, *, tq, n_tiles):
    i = pl.program_id(1)
    blk = SWA_BLOCK
    kwin[0:blk] = kp_ref[0]
    kwin[blk:blk + tq] = km_ref[0]
    kwin[blk + tq:blk + tq + blk] = kn_ref[0]
    vwin[0:blk] = vp_ref[0]
    vwin[blk:blk + tq] = vm_ref[0]
    vwin[blk + tq:blk + tq + blk] = vn_ref[0]
    r = lax.broadcasted_iota(I32, (blk, 3 * blk), 0)
    s = lax.broadcasted_iota(I32, (blk, 3 * blk), 1)
    band = jnp.abs(r - (s - blk)) <= WINDOW

    def qblock(qb, carry):
        row0 = pl.multiple_of(qb * blk, blk)
        first = jnp.logical_and(i == 0, qb == 0)
        last = jnp.logical_and(i == n_tiles - 1, qb == tq // blk - 1)
        valid = jnp.logical_and(band, jnp.logical_and(jnp.logical_or(s >= blk, jnp.logical_not(first)),
                                                      jnp.logical_or(s < 2 * blk, jnp.logical_not(last))))
        for pair in range(SWA_HEADS // 2):
            kvh = pair // 2
            qp = q_ref[0, pl.ds(row0, blk), pair * 128:(pair + 1) * 128]
            o_pair = jnp.zeros((blk, 128), F32)
            for half in range(2):
                head = pair * 2 + half
                c0 = kvh * 256 + half * 128
                kl = kwin[pl.ds(row0, 3 * blk), c0:c0 + 128]
                vl = vwin[pl.ds(row0, 3 * blk), c0:c0 + 128]
                s_loc = jnp.where(valid, _dot_nt(qp, kl), -1e30)
                s_ctx = _dot_nt(qp, kc_ref[0, :, c0:c0 + 128])
                sk = sink_ref[head]
                m = jnp.maximum(jnp.maximum(jnp.max(s_loc, axis=-1, keepdims=True),
                                            jnp.max(s_ctx, axis=-1, keepdims=True)), sk)
                p_loc = jnp.exp(s_loc - m)
                p_ctx = jnp.exp(s_ctx - m)
                den = (jnp.sum(p_loc, axis=-1, keepdims=True) + jnp.sum(p_ctx, axis=-1, keepdims=True)
                       + jnp.exp(sk - m))
                inv = 1.0 / den
                o_pair = o_pair + _dot((p_ctx * inv).astype(BF16), vc_ref[0, :, c0:c0 + 128])
                o_pair = o_pair + _dot((p_loc * inv).astype(BF16), vl)
            o_ref[0, pl.ds(row0, blk), pair * 128:(pair + 1) * 128] = o_pair.astype(BF16)
        return carry

    lax.fori_loop(0, tq // blk, qblock, 0)


def swa_attention(q, k2, v2, kc2, vc2, sink, *, tq):
    bsz, n, _ = q.shape
    lc = kc2.shape[1]
    tq = min(tq, n)
    n_tiles = n // tq
    rb = tq // SWA_BLOCK
    nb = n // SWA_BLOCK
    main = pl.BlockSpec((1, tq, 1024), lambda b, i, *_: (b, i, 0))
    prev = pl.BlockSpec((1, SWA_BLOCK, 1024), lambda b, i, *_: (b, jnp.maximum(i * rb - 1, 0), 0))
    nxt = pl.BlockSpec((1, SWA_BLOCK, 1024), lambda b, i, *_: (b, jnp.minimum((i + 1) * rb, nb - 1), 0))
    cspec = pl.BlockSpec((1, lc, 1024), lambda b, i, *_: (b, 0, 0))
    grid_spec = pltpu.PrefetchScalarGridSpec(
        num_scalar_prefetch=1,
        grid=(bsz, n_tiles),
        in_specs=[main, main, prev, nxt, main, prev, nxt, cspec, cspec],
        out_specs=main,
        scratch_shapes=[pltpu.VMEM((tq + 2 * SWA_BLOCK, 1024), BF16)] * 2,
    )
    return pl.pallas_call(
        functools.partial(_swa_kernel, tq=tq, n_tiles=n_tiles),
        grid_spec=grid_spec,
        out_shape=jax.ShapeDtypeStruct((bsz, n, 1024), BF16),
        compiler_params=_cparams(("arbitrary", "arbitrary")),
        name="swa_attention",
    )(sink, q, k2, k2, k2, v2, v2, v2, kc2, vc2)


def _prep_ab(w_in, w_uq, w_ukv, w_out):
    d = w_in.shape[0]
    perm = _rope_perm(MLA_ROPE)
    o = 3 * CONV_DIM
    kr = w_in[:, o + MLA_Q_LORA + MLA_KV_LORA:]
    z64 = jnp.zeros((d, 64), F32)
    z32 = jnp.zeros((d, 32), F32)
    win2 = jnp.concatenate([w_in[:, :o + MLA_Q_LORA + MLA_KV_LORA], z64, kr, z32, z64, kr[:, perm], z32],
                           axis=1).astype(BF16)
    dq = MLA_NOPE + MLA_ROPE
    wq = w_uq.reshape(MLA_Q_LORA, MLA_HEADS, dq)
    zq = jnp.zeros((MLA_Q_LORA, MLA_HEADS, 32), F32)
    zq64 = jnp.zeros((MLA_Q_LORA, MLA_HEADS, 64), F32)
    qa = jnp.concatenate([wq, zq], axis=-1).reshape(MLA_Q_LORA, MLA_HEADS * 128)
    qb = jnp.concatenate([zq64, wq[:, :, MLA_NOPE:][:, :, perm], zq], axis=-1).reshape(MLA_Q_LORA, MLA_HEADS * 128)
    wq2 = jnp.concatenate([qa, qb], axis=1).astype(BF16)
    wkv = w_ukv.reshape(MLA_KV_LORA, MLA_HEADS, MLA_NOPE + MLA_V)
    zk = jnp.zeros((MLA_KV_LORA, MLA_HEADS, 64), F32)
    ka = jnp.concatenate([wkv[:, :, :MLA_NOPE], zk], axis=-1).reshape(MLA_KV_LORA, MLA_HEADS * 128)
    va = jnp.concatenate([wkv[:, :, MLA_NOPE:], zk], axis=-1).reshape(MLA_KV_LORA, MLA_HEADS * 128)
    wkv2 = jnp.concatenate([ka, va], axis=1).astype(BF16)
    wc = w_out[:CONV_DIM].astype(BF16)
    wa = w_out[CONV_DIM:].reshape(MLA_HEADS, MLA_V, d)
    wa = jnp.concatenate([wa, jnp.zeros_like(wa)], axis=1).reshape(MLA_HEADS * 128, d).astype(BF16)
    return win2, wq2, wkv2, wc, wa


def _prep_swa(w_qkv):
    hd = SWA_HEAD_DIM
    perm = _rope_perm(hd)
    nq = SWA_HEADS * hd
    nk = SWA_KV_HEADS * hd
    wq = w_qkv[:, :nq]
    wk = w_qkv[:, nq:nq + nk]
    wv = w_qkv[:, nq + nk:]
    d = w_qkv.shape[0]
    wq_sw = wq.reshape(d, SWA_HEADS, hd)[:, :, perm].reshape(d, nq)
    wk_sw = wk.reshape(d, SWA_KV_HEADS, hd)[:, :, perm].reshape(d, nk)
    return jnp.concatenate([wq, wq_sw, wk, wk_sw, wv], axis=1).astype(BF16)


def _pad_lanes(t, left, width=LANES, fill=0.0):
    n, w = t.shape
    return jnp.concatenate([jnp.full((n, left), fill, F32), t, jnp.zeros((n, width - left - w), F32)], axis=1)


def kernel(x, c, ctx, c_ctx, mod_w, mod_b, norm1_g, norm2_g, ab_w_in, conv_w, conv_b, mla_q_norm_g, mla_w_uq,
           mla_kv_norm_g, mla_w_ukv, ab_w_out, swa_w_qkv, swa_sink, swa_w_out, router_w, exp_w_gate, exp_w_up,
           exp_w_down, final_g):
    bsz, n, d = x.shape
    lc = ctx.shape[1]
    depth = mod_w.shape[0]
    assert bsz <= 2 and d == 1024

    cs = jnp.concatenate([c, c_ctx[None, :], jnp.zeros((8 - bsz - 1, d), F32)], axis=0)
    mod = modulation(cs, mod_w, mod_b)

    cos_m, sin_m = _rope_tables(n, MLA_ROPE)
    cos_mla = _pad_lanes(cos_m, 64, fill=1.0)
    sin_mla = _pad_lanes(sin_m, 64)
    cos_mla_c = jnp.concatenate([jnp.ones((lc, 96), F32), jnp.zeros((lc, 32), F32)], axis=1)
    zeros_c = jnp.zeros((lc, 128), F32)
    cos_s, sin_s = _rope_tables(n, SWA_HEAD_DIM)
    cos_swa = jnp.concatenate([cos_s, cos_s], axis=1)
    sin_swa = jnp.concatenate([sin_s, sin_s], axis=1)
    ones_c = jnp.ones((lc, 128), F32)

    row2 = lambda v: v.reshape(1, -1)
    xs, xc = x, ctx
    for layer in range(depth):
        need_ctx = layer < depth - 1
        last = layer == depth - 1
        mod_l = mod[layer]
        g1 = row2(norm1_g[layer])
        g2 = row2(norm2_g[layer])
        rw = jnp.concatenate([router_w[layer], jnp.zeros((d, LANES - N_EXPERTS), F32)], axis=1)
        wg, wu, wd = exp_w_gate[layer], exp_w_up[layer], exp_w_down[layer]
        if layer % 2 == 0:
            e = layer // 2
            win2, wq2, wkv2, wc, wa = _prep_ab(ab_w_in[e], mla_w_uq[e], mla_w_ukv[e], ab_w_out[e])
            qg, kvg = row2(mla_q_norm_g[e]), row2(mla_kv_norm_g[e])
            cw, cb = conv_w[e], row2(conv_b[e])
            gb, z, q, k, v = ab_in(xs, mod_l, g1, win2, qg, wq2, kvg, wkv2, cos_mla, sin_mla, ctx=False, tm=512)
            gbc, zc, qc, kc, vc = ab_in(xc, mod_l, g1, win2, qg, wq2, kvg, wkv2, cos_mla_c, zeros_c, ctx=True, tm=256)
            att = mla_attention(q, kc, vc, k, v, tq=512, tk=1024)
            xs1, h2, aff = mix_out(att, wa, xs, mod_l, g2, rw, (gb, z, cw, cb, wc), ctx=False, tm=512)
            if need_ctx:
                att_c = mla_attention(qc, kc, vc, tq=256, tk=256)
                xc1, hc2, affc = mix_out(att_c, wa, xc, mod_l, g2, rw, (gbc, zc, cw, cb, wc), ctx=True, tm=256)
        else:
            o = layer // 2
            w2 = _prep_swa(swa_w_qkv[o])
            wo = swa_w_out[o].astype(BF16)
            q, k2, v2 = swa_in(xs, mod_l, g1, w2, cos_swa, sin_swa, ctx=False, tm=512)
            qc, kc2, vc2 = swa_in(xc, mod_l, g1, w2, ones_c, zeros_c, ctx=True, tm=256)
            att = swa_attention(q, k2, v2, kc2, vc2, swa_sink[o], tq=512)
            xs1, h2, aff = mix_out(att, wo, xs, mod_l, g2, rw, ctx=False, tm=512)
            if need_ctx:
                raise NotImplementedError("context self-attention for windowed layers below the last")
        xs = moe(xs1, h2, aff, mod_l, wg, wu, wd, final_g=row2(final_g) if last else None, ctx=False)
        if need_ctx:
            xc = moe(xc1, hc2, affc, mod_l, wg, wu, wd, ctx=True)
    return xs
```

```python
import functools
import math

import jax
import jax.numpy as jnp
import numpy as np
from jax import lax
from jax.experimental import pallas as pl
from jax.experimental.pallas import tpu as pltpu

F32 = jnp.float32
BF16 = jnp.bfloat16
I32 = jnp.int32

GRID_W = 64
EPS = 1e-6
ROPE_THETA = 10000.0
CONV_DIM = 512
MLA_HEADS = 8
MLA_Q_LORA = 256
MLA_KV_LORA = 128
MLA_NOPE = 64
MLA_ROPE = 32
MLA_V = 64
SWA_HEADS = 16
SWA_KV_HEADS = 4
SWA_HEAD_DIM = 64
WINDOW = 128
SWA_BLOCK = 128
N_EXPERTS = 16
EC_CAPACITY = 2
N_MOD = 6

LANES = 128
ROW_ALIGN = 16
VMEM_LIMIT = 56 * 1024 * 1024


def _cparams(sem, vmem=VMEM_LIMIT):
    return pltpu.CompilerParams(dimension_semantics=sem, vmem_limit_bytes=vmem)


def _dot(a, b):
    return jnp.dot(a, b, preferred_element_type=F32)


def _dot_nt(a, b):
    return lax.dot_general(a, b, (((1,), (1,)), ((), ())), preferred_element_type=F32)


def _dot_tn(a, b):
    return lax.dot_general(a, b, (((0,), (0,)), ((), ())), preferred_element_type=F32)


def _dot_hi(a, b):
    return jnp.dot(a, b, preferred_element_type=F32, precision=lax.Precision.HIGHEST)


def _silu(x):
    return x * (1.0 / (1.0 + jnp.exp(-x)))


def _rms_mod(x, g, shift, scale):
    y = x * lax.rsqrt(jnp.mean(x * x, axis=-1, keepdims=True) + EPS)
    return (y * g) * (1.0 + scale) + shift


def _mod_row(mod_ref, ctx, which):
    r = 2 if ctx else pl.program_id(0)
    return mod_ref[pl.ds(r, 1), which * 1024:(which + 1) * 1024]


def _mod_kernel(cs_ref, w_ref, b_ref, o_ref):
    s = _silu(cs_ref[...])
    o_ref[0] = _dot_hi(s, w_ref[0]) + b_ref[0]


def modulation(cs, mod_w, mod_b):
    depth, d, n6 = mod_w.shape
    tn = 1536
    return pl.pallas_call(
        _mod_kernel,
        grid=(depth, n6 // tn),
        in_specs=[
            pl.BlockSpec((8, d), lambda l, j: (0, 0)),
            pl.BlockSpec((1, d, tn), lambda l, j: (l, 0, j)),
            pl.BlockSpec((1, 1, tn), lambda l, j: (l, 0, j)),
        ],
        out_specs=pl.BlockSpec((1, 8, tn), lambda l, j: (l, 0, j)),
        out_shape=jax.ShapeDtypeStruct((depth, 8, n6), F32),
        compiler_params=_cparams(("arbitrary", "arbitrary")),
        name="modulation",
    )(cs, mod_w, mod_b.reshape(depth, 1, n6))


def _rope_tables(n, dim):
    n_rows = n // GRID_W
    row = jnp.repeat(jnp.arange(n_rows, dtype=F32), GRID_W)
    col = jnp.tile(jnp.arange(GRID_W, dtype=F32), n_rows)
    half = dim // 2
    qd = dim // 4
    freqs = ROPE_THETA ** (-jnp.arange(0, half, 2, dtype=F32) / half)
    ang = jnp.concatenate([row[:, None] * freqs, col[:, None] * freqs], axis=-1)
    cos, sin = jnp.cos(ang), jnp.sin(ang)
    cos_full = jnp.concatenate([cos[:, :qd], cos[:, :qd], cos[:, qd:], cos[:, qd:]], axis=-1)
    sin_sgn = jnp.concatenate([-sin[:, :qd], sin[:, :qd], -sin[:, qd:], sin[:, qd:]], axis=-1)
    return cos_full, sin_sgn


def _rope_perm(dim):
    qd = dim // 4
    ch = np.arange(dim)
    pair = (ch // qd) % 2
    return np.where(pair == 0, ch + qd, ch - qd)


def _ab_in_kernel(x_ref, mod_ref, g_ref, win_ref, qg_ref, wq_ref, kvg_ref, wkv_ref, cos_ref, sin_ref,
                  gb_ref, z_ref, q_ref, k_ref, v_ref, *, ctx, scale):
    x = x_ref[0]
    h = _rms_mod(x, g_ref[...], _mod_row(mod_ref, ctx, 0), _mod_row(mod_ref, ctx, 1))
    p = _dot(h.astype(BF16), win_ref[...])
    gb_ref[0] = p[:, 0:512].astype(BF16)
    z_ref[0] = (p[:, 512:1024] * p[:, 1024:1536]).astype(BF16)
    ql = p[:, 1536:1792]
    kvl = p[:, 1792:1920]
    kra = p[:, 1920:2048]
    krb = p[:, 2048:2176]
    cos = cos_ref[...]
    sin = sin_ref[...]
    qn = ql * lax.rsqrt(jnp.mean(ql * ql, axis=-1, keepdims=True) + EPS) * qg_ref[...]
    qq = _dot(qn.astype(BF16), wq_ref[...])
    kvn = kvl * lax.rsqrt(jnp.mean(kvl * kvl, axis=-1, keepdims=True) + EPS) * kvg_ref[...]
    kv = _dot(kvn.astype(BF16), wkv_ref[...])
    krope = kra * cos + krb * sin
    cs = cos * scale
    ss = sin * scale
    ones_lane = lax.broadcasted_iota(I32, cos.shape, 1) == MLA_V
    for hd in range(MLA_HEADS):
        a = qq[:, hd * 128:(hd + 1) * 128]
        b = qq[:, 1024 + hd * 128:1024 + (hd + 1) * 128]
        q_ref[0, hd] = (a * cs + b * ss).astype(BF16)
        k_ref[0, hd] = (kv[:, hd * 128:(hd + 1) * 128] + krope).astype(BF16)
        v_ref[0, hd] = jnp.where(ones_lane, 1.0, kv[:, 1024 + hd * 128:1024 + (hd + 1) * 128]).astype(BF16)


def ab_in(x, mod_l, g, win2, qg, wq2, kvg, wkv2, cos_t, sin_t, *, ctx, tm):
    bsz, n, d = x.shape
    tm = min(tm, n)
    scale = (MLA_NOPE + MLA_ROPE) ** -0.5 * math.log2(math.e)
    full = lambda a: pl.BlockSpec(a.shape, lambda b, i: (0,) * a.ndim)
    hshape = jax.ShapeDtypeStruct((bsz, MLA_HEADS, n, 128), BF16)
    hspec = pl.BlockSpec((1, MLA_HEADS, tm, 128), lambda b, i: (b, 0, i, 0))
    return pl.pallas_call(
        functools.partial(_ab_in_kernel, ctx=ctx, scale=scale),
        grid=(bsz, n // tm),
        in_specs=[
            pl.BlockSpec((1, tm, d), lambda b, i: (b, i, 0)),
            full(mod_l), full(g), full(win2), full(qg), full(wq2), full(kvg), full(wkv2),
            pl.BlockSpec((tm, 128), lambda b, i: (i, 0)),
            pl.BlockSpec((tm, 128), lambda b, i: (i, 0)),
        ],
        out_specs=[
            pl.BlockSpec((1, tm, 512), lambda b, i: (b, i, 0)),
            pl.BlockSpec((1, tm, 512), lambda b, i: (b, i, 0)),
            hspec, hspec, hspec,
        ],
        out_shape=[
            jax.ShapeDtypeStruct((bsz, n, 512), BF16),
            jax.ShapeDtypeStruct((bsz, n, 512), BF16),
            hshape, hshape, hshape,
        ],
        compiler_params=_cparams(("arbitrary", "arbitrary")),
        name="ab_in_ctx" if ctx else "ab_in",
    )(x, mod_l, g, win2, qg, wq2, kvg, wkv2, cos_t, sin_t)


def _mla_kernel(q_ref, kc_ref, vc_ref, *rest, tk, n_chunks):
    if n_chunks:
        k_ref, v_ref, o_ref, *bufs = rest
    else:
        (o_ref,) = rest
    q = q_ref[0, 0]

    def chunk(ref, j):
        return ref[0, 0, pl.ds(pl.multiple_of(j * tk, tk), tk), :]

    def scores(j, s_buf):
        s = _dot_nt(q, chunk(k_ref, j))
        s_buf[...] = s
        return jnp.max(s, axis=-1, keepdims=True)

    def weights(s_buf, p_buf, mx, m):
        m_new = jnp.maximum(m, mx)
        p_buf[...] = jnp.exp2(s_buf[...] - m_new).astype(BF16)
        return m_new, jnp.exp2(m - m_new)

    s = _dot_nt(q, kc_ref[0, 0])
    m = jnp.max(s, axis=-1, keepdims=True)
    acc = _dot(jnp.exp2(s - m).astype(BF16), vc_ref[0, 0])

    if n_chunks:
        s_a, s_b, p_a, p_b = bufs
        mx_a = scores(0, s_a)
        m, alpha_a = weights(s_a, p_a, mx_a, m)
        mx_b = scores(1, s_b)

        def trip(j, s_cur, s_nxt, p_prev, p_cur, mx_cur, alpha_prev, m, acc):
            mx_nxt = scores(j + 1, s_nxt)
            pv = _dot(p_prev[...], chunk(v_ref, j - 1))
            m, alpha_cur = weights(s_cur, p_cur, mx_cur, m)
            return mx_nxt, alpha_cur, m, alpha_prev * acc + pv

        def body(t, carry):
            mx_b, alpha_a, m, acc = carry
            j = 2 * t + 1
            mx_a, alpha_b, m, acc = trip(j, s_b, s_a, p_a, p_b, mx_b, alpha_a, m, acc)
            mx_b, alpha_a, m, acc = trip(j + 1, s_a, s_b, p_b, p_a, mx_a, alpha_b, m, acc)
            return mx_b, alpha_a, m, acc

        mx_b, alpha_a, m, acc = lax.fori_loop(0, (n_chunks - 2) // 2, body, (mx_b, alpha_a, m, acc))
        acc = alpha_a * acc + _dot(p_a[...], chunk(v_ref, n_chunks - 2))
        m, alpha_b = weights(s_b, p_b, mx_b, m)
        acc = alpha_b * acc + _dot(p_b[...], chunk(v_ref, n_chunks - 1))
    o_ref[0] = (acc * (1.0 / acc[:, MLA_V:MLA_V + 1])).astype(BF16)


def mla_attention(q, kc, vc, k=None, v=None, *, tq, tk):
    bsz, nh, n, dk = q.shape
    lc = kc.shape[2]
    tq = min(tq, n)
    n_chunks = 0
    ins = [q, kc, vc]
    in_specs = [
        pl.BlockSpec((1, 1, tq, dk), lambda b, h, i: (b, h, i, 0)),
        pl.BlockSpec((1, 1, lc, dk), lambda b, h, i: (b, h, 0, 0)),
        pl.BlockSpec((1, 1, lc, dk), lambda b, h, i: (b, h, 0, 0)),
    ]
    if k is not None:
        nk = k.shape[2]
        tk = min(tk, max(nk // 4, 128))
        n_chunks = nk // tk
        assert n_chunks >= 2 and n_chunks % 2 == 0
        ins += [k, v]
        in_specs += [
            pl.BlockSpec((1, 1, nk, dk), lambda b, h, i: (b, h, 0, 0)),
            pl.BlockSpec((1, 1, nk, dk), lambda b, h, i: (b, h, 0, 0)),
        ]
        scratch = [pltpu.VMEM((tq, tk), F32)] * 2 + [pltpu.VMEM((tq, tk), BF16)] * 2
    else:
        scratch = []
    return pl.pallas_call(
        functools.partial(_mla_kernel, tk=tk, n_chunks=n_chunks),
        grid=(bsz, nh, n // tq),
        in_specs=in_specs,
        out_specs=pl.BlockSpec((1, tq, dk), lambda b, h, i: (b, i, h)),
        out_shape=jax.ShapeDtypeStruct((bsz, n, nh * dk), BF16),
        scratch_shapes=scratch,
        compiler_params=_cparams(("arbitrary", "arbitrary", "arbitrary")),
        name="mla_attention" if k is not None else "mla_attention_ctx",
    )(*ins)


def _mix_out_kernel(*refs, ctx, has_conv, tm, n_tiles):
    if has_conv:
        (gb_ref, z_ref, zp_ref, zn_ref, cw_ref, cb_ref, wc_ref,
         att_ref, wa_ref, x_ref, mod_ref, g2_ref, rw_ref, xs_ref, h2_ref, aff_ref) = refs
    else:
        att_ref, wa_ref, x_ref, mod_ref, g2_ref, rw_ref, xs_ref, h2_ref, aff_ref = refs
    y = _dot(att_ref[0], wa_ref[...])
    if has_conv:
        i = pl.program_id(1)
        z = z_ref[0].astype(F32)
        rows = lax.broadcasted_iota(I32, z.shape, 0)
        zprev_halo = jnp.where(i > 0, zp_ref[0, ROW_ALIGN - 1:ROW_ALIGN, :].astype(F32), 0.0)
        znext_halo = jnp.where(i < n_tiles - 1, zn_ref[0, 0:1, :].astype(F32), 0.0)
        zprev = jnp.where(rows == 0, zprev_halo, pltpu.roll(z, 1, 0))
        znext = jnp.where(rows == tm - 1, znext_halo, pltpu.roll(z, tm - 1, 0))
        cw = cw_ref[...]
        conv = gb_ref[0].astype(F32) * (zprev * cw[0:1] + z * cw[1:2] + znext * cw[2:3] + cb_ref[...])
        y = y + _dot(conv.astype(BF16), wc_ref[...])
    xs = x_ref[0] + _mod_row(mod_ref, ctx, 2) * y
    xs_ref[0] = xs
    h2 = _rms_mod(xs, g2_ref[...], _mod_row(mod_ref, ctx, 3), _mod_row(mod_ref, ctx, 4))
    h2_ref[0] = h2.astype(BF16)
    logits = _dot_hi(h2, rw_ref[...])
    lane = lax.broadcasted_iota(I32, logits.shape, 1)
    logits = jnp.where(lane < N_EXPERTS, logits, -1e30)
    mx = jnp.max(logits, axis=-1, keepdims=True)
    ex = jnp.exp(logits - mx)
    aff = ex / jnp.sum(ex, axis=-1, keepdims=True)
    aff_ref[0] = aff[:, :N_EXPERTS]


def mix_out(att, wa, x, mod_l, g2, rw, conv=None, *, ctx, tm):
    bsz, n, d = x.shape
    tm = min(tm, n)
    n_tiles = n // tm
    full = lambda a: pl.BlockSpec(a.shape, lambda b, i: (0,) * a.ndim)
    row = lambda w: pl.BlockSpec((1, tm, w), lambda b, i: (b, i, 0))
    ins, in_specs = [], []
    if conv is not None:
        gb, z, cw, cb, wc = conv
        r8 = tm // ROW_ALIGN
        nb8 = n // ROW_ALIGN
        ins += [gb, z, z, z, cw, cb, wc]
        in_specs += [
            row(512), row(512),
            pl.BlockSpec((1, ROW_ALIGN, 512), lambda b, i: (b, jnp.maximum(i * r8 - 1, 0), 0)),
            pl.BlockSpec((1, ROW_ALIGN, 512), lambda b, i: (b, jnp.minimum((i + 1) * r8, nb8 - 1), 0)),
            full(cw), full(cb), full(wc),
        ]
    ins += [att, wa, x, mod_l, g2, rw]
    in_specs += [row(att.shape[-1]), full(wa), row(d), full(mod_l), full(g2), full(rw)]
    return pl.pallas_call(
        functools.partial(_mix_out_kernel, ctx=ctx, has_conv=conv is not None, tm=tm, n_tiles=n_tiles),
        grid=(bsz, n_tiles),
        in_specs=in_specs,
        out_specs=[row(d), row(d), row(N_EXPERTS)],
        out_shape=[
            jax.ShapeDtypeStruct((bsz, n, d), F32),
            jax.ShapeDtypeStruct((bsz, n, d), BF16),
            jax.ShapeDtypeStruct((bsz, n, N_EXPERTS), F32),
        ],
        compiler_params=_cparams(("arbitrary", "arbitrary")),
        name=("mix_out_ctx" if ctx else "mix_out") + ("_conv" if conv is not None else ""),
    )(*ins)


def _route_kernel(aff_ref, ls_ref, us_ref, eye_ref, rank_ref, rankt_ref, oa_ref, nw_ref,
                  thr_s, need_s, eq_s, oa_s, *, n, cap, tb, win, chunk):
    n_chunks = n // chunk
    nblk = n // tb
    k = pl.program_id(1)

    def count(pred_fn):
        def body(c, acc):
            kb = pltpu.bitcast(aff_ref[0, pl.ds(pl.multiple_of(c * chunk, chunk), chunk), :], I32)
            return acc + jnp.sum(pred_fn(kb).astype(I32), axis=0, keepdims=True)
        return lax.fori_loop(0, n_chunks, body, jnp.zeros((1, N_EXPERTS), I32))

    @pl.when(k == 0)
    def _():
        def bit_body(i, thr):
            cand = thr | jnp.left_shift(jnp.int32(1), 30 - i)
            cnt = count(lambda kb: kb >= cand)
            return jnp.where(cnt >= cap, cand, thr)

        thr = lax.fori_loop(0, 31, bit_body, jnp.zeros((1, N_EXPERTS), I32))
        thr_s[...] = thr
        need_s[...] = (cap - count(lambda kb: kb > thr)).astype(F32)
        eq_s[...] = jnp.zeros((1, N_EXPERTS), F32)
        oa_s[...] = jnp.zeros((1, N_EXPERTS), F32)

    thr = thr_s[...]
    off = pl.multiple_of(k * tb, tb)
    kb = pltpu.bitcast(aff_ref[0, pl.ds(off, tb), :], I32)
    gt = kb > thr
    eq = kb == thr
    eqf = jnp.where(eq, 1.0, 0.0)
    eqrank = eq_s[...] + _dot(ls_ref[...], eqf.astype(BF16))
    sel = jnp.logical_or(gt, jnp.logical_and(eq, eqrank < need_s[...]))
    self_ = jnp.where(sel, 1.0, 0.0)
    selb = self_.astype(BF16)
    lrank = _dot(ls_ref[...], selb)
    rank_ref[0] = jnp.where(sel, lrank, -1.0)
    lrank_t = _dot_tn(selb, us_ref[...])
    sel_t = _dot_tn(selb, eye_ref[...])
    rankt_ref[0] = jnp.where(sel_t > 0.5, lrank_t, -1.0)
    c = jnp.sum(self_, axis=0, keepdims=True)
    oa_run = oa_s[...]
    oa_ref[0, pl.ds(k, 1), :] = oa_run.astype(I32)
    nwin = jnp.max(jnp.ceil(c * (1.0 / win)), axis=1, keepdims=True)
    nw_ref[0, pl.ds(k, 1), :] = jnp.broadcast_to(nwin, (1, N_EXPERTS)).astype(I32)
    oa_new = oa_run + jnp.ceil(c * (1.0 / ROW_ALIGN)) * ROW_ALIGN
    oa_s[...] = oa_new
    eq_s[...] = eq_s[...] + jnp.sum(eqf, axis=0, keepdims=True)

    @pl.when(k == nblk - 1)
    def _():
        oa_ref[0, pl.ds(nblk, 1), :] = oa_new.astype(I32)


def route(aff, *, tb, win):
    bsz, n, ne = aff.shape
    cap = max(1, (EC_CAPACITY * n) // ne)
    nblk = n // tb
    chunk = min(1024, n)
    ii = np.arange(tb)
    ls = jnp.asarray(ii[None, :] < ii[:, None], BF16)
    us = jnp.asarray(ii[:, None] < ii[None, :], BF16)
    eye = jnp.asarray(ii[:, None] == ii[None, :], BF16)
    full = lambda a: pl.BlockSpec(a.shape, lambda b, k: (0,) * a.ndim)
    return pl.pallas_call(
        functools.partial(_route_kernel, n=n, cap=cap, tb=tb, win=win, chunk=chunk),
        grid=(bsz, nblk),
        in_specs=[pl.BlockSpec((1, n, ne), lambda b, k: (b, 0, 0)), full(ls), full(us), full(eye)],
        out_specs=[
            pl.BlockSpec((1, tb, ne), lambda b, k: (b, k, 0)),
            pl.BlockSpec((1, ne, tb), lambda b, k: (b, 0, k)),
            pl.BlockSpec((1, nblk + 1, ne), lambda b, k: (b, 0, 0)),
            pl.BlockSpec((1, nblk, ne), lambda b, k: (b, 0, 0)),
        ],
        out_shape=[
            jax.ShapeDtypeStruct((bsz, n, ne), F32),
            jax.ShapeDtypeStruct((bsz, ne, n), F32),
            jax.ShapeDtypeStruct((bsz, nblk + 1, ne), I32),
            jax.ShapeDtypeStruct((bsz, nblk, ne), I32),
        ],
        scratch_shapes=[pltpu.VMEM((1, ne), I32), pltpu.VMEM((1, ne), F32),
                        pltpu.VMEM((1, ne), F32), pltpu.VMEM((1, ne), F32)],
        compiler_params=_cparams(("arbitrary", "arbitrary")),
        name="route",
    )(aff, ls, us, eye)


def _dispatch_kernel(oa_ref, nw_ref, rankt_ref, h_ref, xg_in_ref, xg_ref, stack, sem,
                     *, nblk, n_steps, win, tb, group):
    del xg_in_ref
    b = pl.program_id(0)
    k = pl.program_id(1)
    step = b * nblk + k
    slot = step % 2
    ne = N_EXPERTS

    def fill(sl, j):
        h = h_ref[0]
        for g0 in range(0, ne, group):
            pieces = []
            for e in range(g0, g0 + group):
                r = rankt_ref[0, e:e + 1, :]
                srow = lax.broadcasted_iota(I32, (win, tb), 0).astype(F32) + (j * win).astype(F32)
                pieces.append((r == srow).astype(BF16))
            oh_t = jnp.concatenate(pieces, axis=0)
            stack[sl, g0 * win:(g0 + group) * win, :] = _dot(oh_t, h).astype(BF16)

    def copies(sl, bb, kk, j):
        out = []
        for e in range(ne):
            off = pl.multiple_of(oa_ref[(bb * (nblk + 1) + kk) * ne + e] + j * win, ROW_ALIGN)
            out.append(pltpu.make_async_copy(
                stack.at[sl, pl.ds(e * win, win), :],
                xg_ref.at[bb, e, pl.ds(off, win), :],
                sem.at[sl]))
        return out

    fill(slot, jnp.int32(0))

    @pl.when(step > 0)
    def _():
        for c in copies(1 - slot, b, k, 0):
            c.wait()

    for c in copies(slot, b, k, 0):
        c.start()

    nwx = nw_ref[(b * nblk + k) * ne]

    def extra(j, carry):
        fill(2, j)
        cs = copies(2, b, k, j)
        for c in cs:
            c.start()
        for c in cs:
            c.wait()
        return carry

    lax.fori_loop(1, jnp.maximum(nwx, 1), extra, 0)

    @pl.when(step == n_steps - 1)
    def _():
        for c in copies(slot, b, k, 0):
            c.wait()


def dispatch(oa, nw, rank_t, h2, *, tb, win, c_pad):
    bsz, n, d = h2.shape
    ne = N_EXPERTS
    nblk = n // tb
    group = 4
    xg0 = jnp.zeros((bsz, ne, c_pad, d), BF16)
    grid_spec = pltpu.PrefetchScalarGridSpec(
        num_scalar_prefetch=2,
        grid=(bsz, nblk),
        in_specs=[
            pl.BlockSpec((1, ne, tb), lambda b, k, *_: (b, 0, k)),
            pl.BlockSpec((1, tb, d), lambda b, k, *_: (b, k, 0)),
            pl.BlockSpec(memory_space=pl.ANY),
        ],
        out_specs=pl.BlockSpec(memory_space=pl.ANY),
        scratch_shapes=[
            pltpu.VMEM((3, ne * win, d), BF16),
            pltpu.SemaphoreType.DMA((3,)),
        ],
    )
    return pl.pallas_call(
        functools.partial(_dispatch_kernel, nblk=nblk, n_steps=bsz * nblk, win=win, tb=tb, group=group),
        grid_spec=grid_spec,
        out_shape=jax.ShapeDtypeStruct(xg0.shape, BF16),
        input_output_aliases={4: 0},
        compiler_params=_cparams(("arbitrary", "arbitrary")),
        name="dispatch",
    )(oa.reshape(-1), nw.reshape(-1), rank_t, h2, xg0)


def _ffn_kernel(tot_ref, x_ref, wg_ref, wu_ref, wd_ref, y_ref, wgb, wub, wdb, *, tm, nblk):
    e = pl.program_id(0)
    b = pl.program_id(1)
    i = pl.program_id(2)

    @pl.when(jnp.logical_and(b == 0, i == 0))
    def _():
        wgb[...] = wg_ref[0].astype(BF16)
        wub[...] = wu_ref[0].astype(BF16)
        wdb[...] = wd_ref[0].astype(BF16)

    total = tot_ref[(b * (nblk + 1) + nblk) * N_EXPERTS + e]

    @pl.when(i * tm < total)
    def _():
        x = x_ref[0, 0]
        a = _dot(x, wgb[...])
        u = _dot(x, wub[...])
        hid = (_silu(a) * u).astype(BF16)
        y_ref[0, 0] = _dot(hid, wdb[...]).astype(BF16)

    @pl.when(i * tm >= total)
    def _():
        y_ref[0, 0] = jnp.zeros(y_ref.shape[2:], BF16)


def expert_ffn(oa, xg, w_gate, w_up, w_down, *, tm, nblk):
    bsz, ne, c_pad, d = xg.shape
    f = w_gate.shape[-1]
    grid_spec = pltpu.PrefetchScalarGridSpec(
        num_scalar_prefetch=1,
        grid=(ne, bsz, c_pad // tm),
        in_specs=[
            pl.BlockSpec((1, 1, tm, d), lambda e, b, i, *_: (b, e, i, 0)),
            pl.BlockSpec((1, d, f), lambda e, b, i, *_: (e, 0, 0)),
            pl.BlockSpec((1, d, f), lambda e, b, i, *_: (e, 0, 0)),
            pl.BlockSpec((1, f, d), lambda e, b, i, *_: (e, 0, 0)),
        ],
        out_specs=pl.BlockSpec((1, 1, tm, d), lambda e, b, i, *_: (b, e, i, 0)),
        scratch_shapes=[pltpu.VMEM((d, f), BF16), pltpu.VMEM((d, f), BF16), pltpu.VMEM((f, d), BF16)],
    )
    return pl.pallas_call(
        functools.partial(_ffn_kernel, tm=tm, nblk=nblk),
        grid_spec=grid_spec,
        out_shape=jax.ShapeDtypeStruct(xg.shape, BF16),
        compiler_params=_cparams(("arbitrary", "arbitrary", "arbitrary")),
        name="expert_ffn",
    )(oa.reshape(-1), xg, w_gate, w_up, w_down)


def _combine_kernel(oa_ref, nw_ref, rank_ref, aff_ref, xs_ref, mod_ref, *rest,
                    ctx, nblk, n_steps, win, tb, final):
    if final:
        fg_ref, ye_ref, o_ref, stack, sem = rest
    else:
        ye_ref, o_ref, stack, sem = rest
    b = pl.program_id(0)
    k = pl.program_id(1)
    step = b * nblk + k
    slot = step % 2
    ne = N_EXPERTS

    def copies(sl, bb, kk, j):
        out = []
        for e in range(ne):
            off = pl.multiple_of(oa_ref[(bb * (nblk + 1) + kk) * ne + e] + j * win, ROW_ALIGN)
            out.append(pltpu.make_async_copy(
                ye_ref.at[bb, e, pl.ds(off, win), :],
                stack.at[sl, pl.ds(e * win, win), :],
                sem.at[sl]))
        return out

    @pl.when(step == 0)
    def _():
        for c in copies(0, b, k, 0):
            c.start()

    @pl.when(step + 1 < n_steps)
    def _():
        nxt = step + 1
        for c in copies(1 - slot, nxt // nblk, nxt % nblk, 0):
            c.start()

    for c in copies(slot, b, k, 0):
        c.wait()

    def gated_onehot(j):
        rank = rank_ref[0]
        aff = aff_ref[0]
        pieces = []
        for e in range(ne):
            lane = lax.broadcasted_iota(I32, (tb, win), 1).astype(F32) + (j * win).astype(F32)
            pieces.append(jnp.where(rank[:, e:e + 1] == lane, aff[:, e:e + 1], 0.0).astype(BF16))
        return jnp.concatenate(pieces, axis=1)

    y = _dot(gated_onehot(jnp.int32(0)), stack[slot])

    nwx = nw_ref[(b * nblk + k) * ne]

    def extra(j, y):
        cs = copies(2, b, k, j)
        for c in cs:
            c.start()
        for c in cs:
            c.wait()
        return y + _dot(gated_onehot(j), stack[2])

    y = lax.fori_loop(1, jnp.maximum(nwx, 1), extra, y)
    out = xs_ref[0] + _mod_row(mod_ref, ctx, 5) * y
    if final:
        out = out * lax.rsqrt(jnp.mean(out * out, axis=-1, keepdims=True) + EPS) * fg_ref[...]
    o_ref[0] = out


def combine(oa, nw, rank, aff, xs, mod_l, ye, final_g=None, *, ctx, tb, win):
    bsz, n, d = xs.shape
    ne = N_EXPERTS
    nblk = n // tb
    final = final_g is not None
    full = lambda a: pl.BlockSpec(a.shape, lambda b, k, *_: (0,) * a.ndim)
    ins = [rank, aff, xs, mod_l]
    in_specs = [
        pl.BlockSpec((1, tb, ne), lambda b, k, *_: (b, k, 0)),
        pl.BlockSpec((1, tb, ne), lambda b, k, *_: (b, k, 0)),
        pl.BlockSpec((1, tb, d), lambda b, k, *_: (b, k, 0)),
        full(mod_l),
    ]
    if final:
        ins.append(final_g)
        in_specs.append(full(final_g))
    ins.append(ye)
    in_specs.append(pl.BlockSpec(memory_space=pl.ANY))
    grid_spec = pltpu.PrefetchScalarGridSpec(
        num_scalar_prefetch=2,
        grid=(bsz, nblk),
        in_specs=in_specs,
        out_specs=pl.BlockSpec((1, tb, d), lambda b, k, *_: (b, k, 0)),
        scratch_shapes=[
            pltpu.VMEM((3, ne * win, d), BF16),
            pltpu.SemaphoreType.DMA((3,)),
        ],
    )
    return pl.pallas_call(
        functools.partial(_combine_kernel, ctx=ctx, nblk=nblk, n_steps=bsz * nblk, win=win, tb=tb, final=final),
        grid_spec=grid_spec,
        out_shape=jax.ShapeDtypeStruct(xs.shape, F32),
        compiler_params=_cparams(("arbitrary", "arbitrary")),
        name="combine_final" if final else ("combine_ctx" if ctx else "combine"),
    )(oa.reshape(-1), nw.reshape(-1), *ins)


def moe(xs1, h2, aff, mod_l, w_gate, w_up, w_down, final_g=None, *, ctx):
    bsz, n, d = xs1.shape
    cap = max(1, (EC_CAPACITY * n) // N_EXPERTS)
    tb = min(512, n)
    win = LANES
    nblk = n // tb
    tm = 384 if n >= 4096 else 128
    need = cap + ROW_ALIGN * nblk + win
    c_pad = -(-need // tm) * tm
    rank, rank_t, oa, nw = route(aff, tb=tb, win=win)
    xg = dispatch(oa, nw, rank_t, h2, tb=tb, win=win, c_pad=c_pad)
    ye = expert_ffn(oa, xg, w_gate, w_up, w_down, tm=tm, nblk=nblk)
    return combine(oa, nw, rank, aff, xs1, mod_l, ye, final_g, ctx=ctx, tb=tb, win=win)


def _swa_in_kernel(x_ref, mod_ref, g_ref, w_ref, cos_ref, sin_ref, q_ref, k_ref, v_ref, *, ctx, scale):
    h = _rms_mod(x_ref[0], g_ref[...], _mod_row(mod_ref, ctx, 0), _mod_row(mod_ref, ctx, 1))
    p = _dot(h.astype(BF16), w_ref[...])
    cos = cos_ref[...]
    sin = sin_ref[...]
    cs = cos * scale
    ss = sin * scale
    for j in range(8):
        a = p[:, j * 128:(j + 1) * 128]
        bsw = p[:, 1024 + j * 128:1024 + (j + 1) * 128]
        q_ref[0, :, j * 128:(j + 1) * 128] = (a * cs + bsw * ss).astype(BF16)
    lane = lax.broadcasted_iota(I32, cos.shape, 1)
    low = lane < 64
    for pr in range(2):
        kp = p[:, 2048 + pr * 128:2048 + (pr + 1) * 128] * cos + p[:, 2304 + pr * 128:2304 + (pr + 1) * 128] * sin
        vp = p[:, 2560 + pr * 128:2560 + (pr + 1) * 128]
        for src, dst in ((kp, k_ref), (vp, v_ref)):
            ev_lo = jnp.where(low, src, 0.0)
            od_hi = jnp.where(low, 0.0, src)
            ev_hi = pltpu.roll(ev_lo, 64, 1)
            od_lo = pltpu.roll(od_hi, 64, 1)
            base = pr * 512
            dst[0, :, base:base + 128] = ev_lo.astype(BF16)
            dst[0, :, base + 128:base + 256] = ev_hi.astype(BF16)
            dst[0, :, base + 256:base + 384] = od_lo.astype(BF16)
            dst[0, :, base + 384:base + 512] = od_hi.astype(BF16)


def swa_in(x, mod_l, g, w2, cos_t, sin_t, *, ctx, tm):
    bsz, n, d = x.shape
    tm = min(tm, n)
    full = lambda a: pl.BlockSpec(a.shape, lambda b, i: (0,) * a.ndim)
    row = lambda w: pl.BlockSpec((1, tm, w), lambda b, i: (b, i, 0))
    return pl.pallas_call(
        functools.partial(_swa_in_kernel, ctx=ctx, scale=SWA_HEAD_DIM ** -0.5 * math.log2(math.e)),
        grid=(bsz, n // tm),
        in_specs=[row(d), full(mod_l), full(g), full(w2),
                  pl.BlockSpec((tm, 128), lambda b, i: (i, 0)),
                  pl.BlockSpec((tm, 128), lambda b, i: (i, 0))],
        out_specs=[row(1024), row(1024), row(1024)],
        out_shape=[jax.ShapeDtypeStruct((bsz, n, 1024), BF16)] * 3,
        compiler_params=_cparams(("arbitrary", "arbitrary")),
        name="swa_in_ctx" if ctx else "swa_in",
    )(x, mod_l, g, w2, cos_t, sin_t)


def _swa_kernel(sink_ref, q_ref, km_ref, kp_ref, kn_ref, vm_ref, vp_ref, vn_ref, kc_ref, vc_ref,
                o_ref, kwin, vwin, *, tq, n_tiles):
    i = pl.program_id(1)
    blk = SWA_BLOCK
    kwin[0:blk] = kp_ref[0]
    kwin[blk:blk + tq] = km_ref[0]
    kwin[blk + tq:blk + tq + blk] = kn_ref[0]
    vwin[0:blk] = vp_ref[0]
    vwin[blk:blk + tq] = vm_ref[0]
    vwin[blk + tq:blk + tq + blk] = vn_ref[0]
    r = lax.broadcasted_iota(I32, (blk, 3 * blk), 0)
    s = lax.broadcasted_iota(I32, (blk, 3 * blk), 1)
    band = jnp.abs(r - (s - blk)) <= WINDOW
    low_lanes = lax.broadcasted_iota(I32, (blk, 128), 1) < 64

    def qblock(qb, carry):
        row0 = pl.multiple_of(qb * blk, blk)
        first = jnp.logical_and(i == 0, qb == 0)
        last = jnp.logical_and(i == n_tiles - 1, qb == tq // blk - 1)
        valid = jnp.logical_and(band, jnp.logical_and(jnp.logical_or(s >= blk, jnp.logical_not(first)),
                                                      jnp.logical_or(s < 2 * blk, jnp.logical_not(last))))
        bias = jnp.where(valid, 0.0, -1e30)
        for kvh in range(SWA_KV_HEADS):
            c0 = kvh * 256
            q4 = jnp.concatenate([q_ref[0, pl.ds(row0, blk), c0:c0 + 128],
                                  q_ref[0, pl.ds(row0, blk), c0 + 128:c0 + 256]], axis=0)
            kl = jnp.concatenate([kwin[pl.ds(row0, 3 * blk), c0:c0 + 128],
                                  kwin[pl.ds(row0, 3 * blk), c0 + 128:c0 + 256]], axis=0)
            vl = jnp.concatenate([vwin[pl.ds(row0, 3 * blk), c0:c0 + 128],
                                  vwin[pl.ds(row0, 3 * blk), c0 + 128:c0 + 256]], axis=0)
            s_loc = _dot_nt(q4, kl)
            s_ctx = _dot_nt(q4, kc_ref[0, kvh])
            lc = s_ctx.shape[1] // 2
            p_loc, p_ctx, inv = [], [], []
            for pp in range(2):
                pl_row, pc_row, inv_row = [], [], []
                for hf in range(2):
                    sk = sink_ref[kvh * 4 + pp * 2 + hf]
                    sl = s_loc[pp * blk:(pp + 1) * blk, hf * 3 * blk:(hf + 1) * 3 * blk] + bias
                    sc = s_ctx[pp * blk:(pp + 1) * blk, hf * lc:(hf + 1) * lc]
                    m = jnp.maximum(jnp.maximum(jnp.max(sl, axis=-1, keepdims=True),
                                                jnp.max(sc, axis=-1, keepdims=True)), sk)
                    el = jnp.exp2(sl - m)
                    ec = jnp.exp2(sc - m)
                    den = (jnp.sum(el, axis=-1, keepdims=True) + jnp.sum(ec, axis=-1, keepdims=True)
                           + jnp.exp2(sk - m))
                    pl_row.append(el.astype(BF16))
                    pc_row.append(ec.astype(BF16))
                    inv_row.append(1.0 / den)
                p_loc.append(jnp.concatenate(pl_row, axis=1))
                p_ctx.append(jnp.concatenate(pc_row, axis=1))
                inv.append(jnp.where(low_lanes, inv_row[0], inv_row[1]))
            o4 = _dot(jnp.concatenate(p_loc, axis=0), vl) + _dot(jnp.concatenate(p_ctx, axis=0), vc_ref[0, kvh])
            for pp in range(2):
                pair = kvh * 2 + pp
                o_ref[0, pl.ds(row0, blk), pair * 128:(pair + 1) * 128] = (
                    o4[pp * blk:(pp + 1) * blk] * inv[pp]).astype(BF16)
        return carry

    lax.fori_loop(0, tq // blk, qblock, 0)


def swa_attention(q, k2, v2, kc2, vc2, sink, *, tq):
    bsz, n, _ = q.shape
    lc = kc2.shape[1]
    tq = min(tq, n)
    n_tiles = n // tq
    rb = tq // SWA_BLOCK
    nb = n // SWA_BLOCK
    main = pl.BlockSpec((1, tq, 1024), lambda b, i, *_: (b, i, 0))
    prev = pl.BlockSpec((1, SWA_BLOCK, 1024), lambda b, i, *_: (b, jnp.maximum(i * rb - 1, 0), 0))
    nxt = pl.BlockSpec((1, SWA_BLOCK, 1024), lambda b, i, *_: (b, jnp.minimum((i + 1) * rb, nb - 1), 0))
    stack = lambda a: a.reshape(bsz, lc, SWA_KV_HEADS, 2, 128).transpose(0, 2, 3, 1, 4).reshape(
        bsz, SWA_KV_HEADS, 2 * lc, 128)
    kc2, vc2 = stack(kc2), stack(vc2)
    sink = sink * math.log2(math.e)
    cspec = pl.BlockSpec((1, SWA_KV_HEADS, 2 * lc, 128), lambda b, i, *_: (b, 0, 0, 0))
    grid_spec = pltpu.PrefetchScalarGridSpec(
        num_scalar_prefetch=1,
        grid=(bsz, n_tiles),
        in_specs=[main, main, prev, nxt, main, prev, nxt, cspec, cspec],
        out_specs=main,
        scratch_shapes=[pltpu.VMEM((tq + 2 * SWA_BLOCK, 1024), BF16)] * 2,
    )
    return pl.pallas_call(
        functools.partial(_swa_kernel, tq=tq, n_tiles=n_tiles),
        grid_spec=grid_spec,
        out_shape=jax.ShapeDtypeStruct((bsz, n, 1024), BF16),
        compiler_params=_cparams(("arbitrary", "arbitrary")),
        name="swa_attention",
    )(sink, q, k2, k2, k2, v2, v2, v2, kc2, vc2)


def _prep_ab(w_in, w_uq, w_ukv, w_out):
    d = w_in.shape[0]
    perm = _rope_perm(MLA_ROPE)
    o = 3 * CONV_DIM
    kr = w_in[:, o + MLA_Q_LORA + MLA_KV_LORA:]
    z64 = jnp.zeros((d, 64), F32)
    z32 = jnp.zeros((d, 32), F32)
    win2 = jnp.concatenate([w_in[:, :o + MLA_Q_LORA + MLA_KV_LORA], z64, kr, z32, z64, kr[:, perm], z32],
                           axis=1).astype(BF16)
    dq = MLA_NOPE + MLA_ROPE
    wq = w_uq.reshape(MLA_Q_LORA, MLA_HEADS, dq)
    zq = jnp.zeros((MLA_Q_LORA, MLA_HEADS, 32), F32)
    zq64 = jnp.zeros((MLA_Q_LORA, MLA_HEADS, 64), F32)
    qa = jnp.concatenate([wq, zq], axis=-1).reshape(MLA_Q_LORA, MLA_HEADS * 128)
    qb = jnp.concatenate([zq64, wq[:, :, MLA_NOPE:][:, :, perm], zq], axis=-1).reshape(MLA_Q_LORA, MLA_HEADS * 128)
    wq2 = jnp.concatenate([qa, qb], axis=1).astype(BF16)
    wkv = w_ukv.reshape(MLA_KV_LORA, MLA_HEADS, MLA_NOPE + MLA_V)
    zk = jnp.zeros((MLA_KV_LORA, MLA_HEADS, 64), F32)
    ka = jnp.concatenate([wkv[:, :, :MLA_NOPE], zk], axis=-1).reshape(MLA_KV_LORA, MLA_HEADS * 128)
    va = jnp.concatenate([wkv[:, :, MLA_NOPE:], zk], axis=-1).reshape(MLA_KV_LORA, MLA_HEADS * 128)
    wkv2 = jnp.concatenate([ka, va], axis=1).astype(BF16)
    wc = w_out[:CONV_DIM].astype(BF16)
    wa = w_out[CONV_DIM:].reshape(MLA_HEADS, MLA_V, d)
    wa = jnp.concatenate([wa, jnp.zeros_like(wa)], axis=1).reshape(MLA_HEADS * 128, d).astype(BF16)
    return win2, wq2, wkv2, wc, wa


def _prep_swa(w_qkv):
    hd = SWA_HEAD_DIM
    perm = _rope_perm(hd)
    nq = SWA_HEADS * hd
    nk = SWA_KV_HEADS * hd
    wq = w_qkv[:, :nq]
    wk = w_qkv[:, nq:nq + nk]
    wv = w_qkv[:, nq + nk:]
    d = w_qkv.shape[0]
    wq_sw = wq.reshape(d, SWA_HEADS, hd)[:, :, perm].reshape(d, nq)
    wk_sw = wk.reshape(d, SWA_KV_HEADS, hd)[:, :, perm].reshape(d, nk)
    return jnp.concatenate([wq, wq_sw, wk, wk_sw, wv], axis=1).astype(BF16)


def _pad_lanes(t, left, width=LANES, fill=0.0):
    n, w = t.shape
    return jnp.concatenate([jnp.full((n, left), fill, F32), t, jnp.zeros((n, width - left - w), F32)], axis=1)


def kernel(x, c, ctx, c_ctx, mod_w, mod_b, norm1_g, norm2_g, ab_w_in, conv_w, conv_b, mla_q_norm_g, mla_w_uq,
           mla_kv_norm_g, mla_w_ukv, ab_w_out, swa_w_qkv, swa_sink, swa_w_out, router_w, exp_w_gate, exp_w_up,
           exp_w_down, final_g):
    bsz, n, d = x.shape
    lc = ctx.shape[1]
    depth = mod_w.shape[0]
    assert bsz <= 2 and d == 1024

    cs = jnp.concatenate([c, c_ctx[None, :], jnp.zeros((8 - bsz - 1, d), F32)], axis=0)
    mod = modulation(cs, mod_w, mod_b)

    cos_m, sin_m = _rope_tables(n, MLA_ROPE)
    cos_mla = _pad_lanes(cos_m, 64, fill=1.0)
    sin_mla = _pad_lanes(sin_m, 64)
    cos_mla_c = jnp.concatenate([jnp.ones((lc, 96), F32), jnp.zeros((lc, 32), F32)], axis=1)
    zeros_c = jnp.zeros((lc, 128), F32)
    cos_s, sin_s = _rope_tables(n, SWA_HEAD_DIM)
    cos_swa = jnp.concatenate([cos_s, cos_s], axis=1)
    sin_swa = jnp.concatenate([sin_s, sin_s], axis=1)
    ones_c = jnp.ones((lc, 128), F32)

    row2 = lambda v: v.reshape(1, -1)
    xs, xc = x, ctx
    for layer in range(depth):
        need_ctx = layer < depth - 1
        last = layer == depth - 1
        mod_l = mod[layer]
        g1 = row2(norm1_g[layer])
        g2 = row2(norm2_g[layer])
        rw = jnp.concatenate([router_w[layer], jnp.zeros((d, LANES - N_EXPERTS), F32)], axis=1)
        wg, wu, wd = exp_w_gate[layer], exp_w_up[layer], exp_w_down[layer]
        if layer % 2 == 0:
            e = layer // 2
            win2, wq2, wkv2, wc, wa = _prep_ab(ab_w_in[e], mla_w_uq[e], mla_w_ukv[e], ab_w_out[e])
            qg, kvg = row2(mla_q_norm_g[e]), row2(mla_kv_norm_g[e])
            cw, cb = conv_w[e], row2(conv_b[e])
            gb, z, q, k, v = ab_in(xs, mod_l, g1, win2, qg, wq2, kvg, wkv2, cos_mla, sin_mla, ctx=False, tm=512)
            gbc, zc, qc, kc, vc = ab_in(xc, mod_l, g1, win2, qg, wq2, kvg, wkv2, cos_mla_c, zeros_c, ctx=True, tm=256)
            att = mla_attention(q, kc, vc, k, v, tq=512, tk=2048)
            xs1, h2, aff = mix_out(att, wa, xs, mod_l, g2, rw, (gb, z, cw, cb, wc), ctx=False, tm=512)
            if need_ctx:
                att_c = mla_attention(qc, kc, vc, tq=256, tk=256)
                xc1, hc2, affc = mix_out(att_c, wa, xc, mod_l, g2, rw, (gbc, zc, cw, cb, wc), ctx=True, tm=256)
        else:
            o = layer // 2
            w2 = _prep_swa(swa_w_qkv[o])
            wo = swa_w_out[o].astype(BF16)
            q, k2, v2 = swa_in(xs, mod_l, g1, w2, cos_swa, sin_swa, ctx=False, tm=512)
            qc, kc2, vc2 = swa_in(xc, mod_l, g1, w2, ones_c, zeros_c, ctx=True, tm=256)
            att = swa_attention(q, k2, v2, kc2, vc2, swa_sink[o], tq=512)
            xs1, h2, aff = mix_out(att, wo, xs, mod_l, g2, rw, ctx=False, tm=512)
            if need_ctx:
                raise NotImplementedError("context self-attention for windowed layers below the last")
        xs = moe(xs1, h2, aff, mod_l, wg, wu, wd, final_g=row2(final_g) if last else None, ctx=False)
        if need_ctx:
            xc = moe(xc1, hc2, affc, mod_l, wg, wu, wd, ctx=True)
    return xs
```

```python
import functools
import math

import jax
import jax.numpy as jnp
import numpy as np
from jax import lax
from jax.experimental import pallas as pl
from jax.experimental.pallas import tpu as pltpu

F32 = jnp.float32
BF16 = jnp.bfloat16
I32 = jnp.int32

GRID_W = 64
EPS = 1e-6
ROPE_THETA = 10000.0
CONV_DIM = 512
MLA_HEADS = 8
MLA_Q_LORA = 256
MLA_KV_LORA = 128
MLA_NOPE = 64
MLA_ROPE = 32
MLA_V = 64
SWA_HEADS = 16
SWA_KV_HEADS = 4
SWA_HEAD_DIM = 64
WINDOW = 128
SWA_BLOCK = 128
N_EXPERTS = 16
EC_CAPACITY = 2
N_MOD = 6

LANES = 128
ROW_ALIGN = 16
VMEM_LIMIT = 56 * 1024 * 1024


def _cparams(sem, vmem=VMEM_LIMIT):
    return pltpu.CompilerParams(dimension_semantics=sem, vmem_limit_bytes=vmem)


def _dot(a, b):
    return jnp.dot(a, b, preferred_element_type=F32)


def _dot_nt(a, b):
    return lax.dot_general(a, b, (((1,), (1,)), ((), ())), preferred_element_type=F32)


def _dot_tn(a, b):
    return lax.dot_general(a, b, (((0,), (0,)), ((), ())), preferred_element_type=F32)


def _dot_hi(a, b):
    return jnp.dot(a, b, preferred_element_type=F32, precision=lax.Precision.HIGHEST)


def _silu(x):
    return x * (1.0 / (1.0 + jnp.exp(-x)))


def _rms_mod(x, g, shift, scale):
    y = x * lax.rsqrt(jnp.mean(x * x, axis=-1, keepdims=True) + EPS)
    return (y * g) * (1.0 + scale) + shift


def _mod_row(mod_ref, ctx, which):
    r = 2 if ctx else pl.program_id(0)
    return mod_ref[pl.ds(r, 1), which * 1024:(which + 1) * 1024]


def _mod_kernel(cs_ref, w_ref, b_ref, o_ref):
    s = _silu(cs_ref[...])
    o_ref[0] = _dot_hi(s, w_ref[0]) + b_ref[0]


def modulation(cs, mod_w, mod_b):
    depth, d, n6 = mod_w.shape
    tn = 1536
    return pl.pallas_call(
        _mod_kernel,
        grid=(depth, n6 // tn),
        in_specs=[
            pl.BlockSpec((8, d), lambda l, j: (0, 0)),
            pl.BlockSpec((1, d, tn), lambda l, j: (l, 0, j)),
            pl.BlockSpec((1, 1, tn), lambda l, j: (l, 0, j)),
        ],
        out_specs=pl.BlockSpec((1, 8, tn), lambda l, j: (l, 0, j)),
        out_shape=jax.ShapeDtypeStruct((depth, 8, n6), F32),
        compiler_params=_cparams(("arbitrary", "arbitrary")),
        name="modulation",
    )(cs, mod_w, mod_b.reshape(depth, 1, n6))


def _rope_tables(n, dim):
    n_rows = n // GRID_W
    row = jnp.repeat(jnp.arange(n_rows, dtype=F32), GRID_W)
    col = jnp.tile(jnp.arange(GRID_W, dtype=F32), n_rows)
    half = dim // 2
    qd = dim // 4
    freqs = ROPE_THETA ** (-jnp.arange(0, half, 2, dtype=F32) / half)
    ang = jnp.concatenate([row[:, None] * freqs, col[:, None] * freqs], axis=-1)
    cos, sin = jnp.cos(ang), jnp.sin(ang)
    cos_full = jnp.concatenate([cos[:, :qd], cos[:, :qd], cos[:, qd:], cos[:, qd:]], axis=-1)
    sin_sgn = jnp.concatenate([-sin[:, :qd], sin[:, :qd], -sin[:, qd:], sin[:, qd:]], axis=-1)
    return cos_full, sin_sgn


def _rope_perm(dim):
    qd = dim // 4
    ch = np.arange(dim)
    pair = (ch // qd) % 2
    return np.where(pair == 0, ch + qd, ch - qd)


def _ab_in_kernel(x_ref, mod_ref, g_ref, win_ref, qg_ref, wq_ref, kvg_ref, wkv_ref, cos_ref, sin_ref,
                  gb_ref, z_ref, q_ref, k_ref, v_ref, *, ctx, scale):
    x = x_ref[0]
    h = _rms_mod(x, g_ref[...], _mod_row(mod_ref, ctx, 0), _mod_row(mod_ref, ctx, 1))
    p = _dot(h.astype(BF16), win_ref[...])
    gb_ref[0] = p[:, 0:512].astype(BF16)
    z_ref[0] = (p[:, 512:1024] * p[:, 1024:1536]).astype(BF16)
    ql = p[:, 1536:1792]
    kvl = p[:, 1792:1920]
    kra = p[:, 1920:2048]
    krb = p[:, 2048:2176]
    cos = cos_ref[...]
    sin = sin_ref[...]
    qn = ql * lax.rsqrt(jnp.mean(ql * ql, axis=-1, keepdims=True) + EPS) * qg_ref[...]
    qq = _dot(qn.astype(BF16), wq_ref[...])
    kvn = kvl * lax.rsqrt(jnp.mean(kvl * kvl, axis=-1, keepdims=True) + EPS) * kvg_ref[...]
    kv = _dot(kvn.astype(BF16), wkv_ref[...])
    krope = kra * cos + krb * sin
    cs = cos * scale
    ss = sin * scale
    ones_lane = lax.broadcasted_iota(I32, cos.shape, 1) == MLA_V
    for hd in range(MLA_HEADS):
        a = qq[:, hd * 128:(hd + 1) * 128]
        b = qq[:, 1024 + hd * 128:1024 + (hd + 1) * 128]
        q_ref[0, hd] = (a * cs + b * ss).astype(BF16)
        k_ref[0, hd] = (kv[:, hd * 128:(hd + 1) * 128] + krope).astype(BF16)
        v_ref[0, hd] = jnp.where(ones_lane, 1.0, kv[:, 1024 + hd * 128:1024 + (hd + 1) * 128]).astype(BF16)


def ab_in(x, mod_l, g, win2, qg, wq2, kvg, wkv2, cos_t, sin_t, *, ctx, tm):
    bsz, n, d = x.shape
    tm = min(tm, n)
    scale = (MLA_NOPE + MLA_ROPE) ** -0.5 * math.log2(math.e)
    full = lambda a: pl.BlockSpec(a.shape, lambda b, i: (0,) * a.ndim)
    hshape = jax.ShapeDtypeStruct((bsz, MLA_HEADS, n, 128), BF16)
    hspec = pl.BlockSpec((1, MLA_HEADS, tm, 128), lambda b, i: (b, 0, i, 0))
    return pl.pallas_call(
        functools.partial(_ab_in_kernel, ctx=ctx, scale=scale),
        grid=(bsz, n // tm),
        in_specs=[
            pl.BlockSpec((1, tm, d), lambda b, i: (b, i, 0)),
            full(mod_l), full(g), full(win2), full(qg), full(wq2), full(kvg), full(wkv2),
            pl.BlockSpec((tm, 128), lambda b, i: (i, 0)),
            pl.BlockSpec((tm, 128), lambda b, i: (i, 0)),
        ],
        out_specs=[
            pl.BlockSpec((1, tm, 512), lambda b, i: (b, i, 0)),
            pl.BlockSpec((1, tm, 512), lambda b, i: (b, i, 0)),
            hspec, hspec, hspec,
        ],
        out_shape=[
            jax.ShapeDtypeStruct((bsz, n, 512), BF16),
            jax.ShapeDtypeStruct((bsz, n, 512), BF16),
            hshape, hshape, hshape,
        ],
        compiler_params=_cparams(("arbitrary", "arbitrary")),
        name="ab_in_ctx" if ctx else "ab_in",
    )(x, mod_l, g, win2, qg, wq2, kvg, wkv2, cos_t, sin_t)


def _mla_kernel(q_ref, kc_ref, vc_ref, *rest, tk, n_chunks):
    if n_chunks:
        k_ref, v_ref, o_ref, *bufs = rest
    else:
        (o_ref,) = rest
    q = q_ref[0, 0]

    def chunk(ref, j):
        return ref[0, 0, pl.ds(pl.multiple_of(j * tk, tk), tk), :]

    def scores(j, s_buf):
        s = _dot_nt(q, chunk(k_ref, j))
        s_buf[...] = s
        return jnp.max(s, axis=-1, keepdims=True)

    def weights(s_buf, p_buf, mx, m):
        m_new = jnp.maximum(m, mx)
        p_buf[...] = jnp.exp2(s_buf[...] - m_new).astype(BF16)
        return m_new, jnp.exp2(m - m_new)

    s = _dot_nt(q, kc_ref[0, 0])
    m = jnp.max(s, axis=-1, keepdims=True)
    acc = _dot(jnp.exp2(s - m).astype(BF16), vc_ref[0, 0])

    if n_chunks:
        s_a, s_b, p_a, p_b = bufs
        mx_a = scores(0, s_a)
        m, alpha_a = weights(s_a, p_a, mx_a, m)
        mx_b = scores(1, s_b)

        def trip(j, s_cur, s_nxt, p_prev, p_cur, mx_cur, alpha_prev, m, acc):
            mx_nxt = scores(j + 1, s_nxt)
            pv = _dot(p_prev[...], chunk(v_ref, j - 1))
            m, alpha_cur = weights(s_cur, p_cur, mx_cur, m)
            return mx_nxt, alpha_cur, m, alpha_prev * acc + pv

        def body(t, carry):
            mx_b, alpha_a, m, acc = carry
            j = 2 * t + 1
            mx_a, alpha_b, m, acc = trip(j, s_b, s_a, p_a, p_b, mx_b, alpha_a, m, acc)
            mx_b, alpha_a, m, acc = trip(j + 1, s_a, s_b, p_b, p_a, mx_a, alpha_b, m, acc)
            return mx_b, alpha_a, m, acc

        mx_b, alpha_a, m, acc = lax.fori_loop(0, (n_chunks - 2) // 2, body, (mx_b, alpha_a, m, acc))
        acc = alpha_a * acc + _dot(p_a[...], chunk(v_ref, n_chunks - 2))
        m, alpha_b = weights(s_b, p_b, mx_b, m)
        acc = alpha_b * acc + _dot(p_b[...], chunk(v_ref, n_chunks - 1))
    o_ref[0] = (acc * (1.0 / acc[:, MLA_V:MLA_V + 1])).astype(BF16)


def mla_attention(q, kc, vc, k=None, v=None, *, tq, tk):
    bsz, nh, n, dk = q.shape
    lc = kc.shape[2]
    tq = min(tq, n)
    n_chunks = 0
    ins = [q, kc, vc]
    in_specs = [
        pl.BlockSpec((1, 1, tq, dk), lambda b, h, i: (b, h, i, 0)),
        pl.BlockSpec((1, 1, lc, dk), lambda b, h, i: (b, h, 0, 0)),
        pl.BlockSpec((1, 1, lc, dk), lambda b, h, i: (b, h, 0, 0)),
    ]
    if k is not None:
        nk = k.shape[2]
        tk = min(tk, max(nk // 4, 128))
        n_chunks = nk // tk
        assert n_chunks >= 2 and n_chunks % 2 == 0
        ins += [k, v]
        in_specs += [
            pl.BlockSpec((1, 1, nk, dk), lambda b, h, i: (b, h, 0, 0)),
            pl.BlockSpec((1, 1, nk, dk), lambda b, h, i: (b, h, 0, 0)),
        ]
        scratch = [pltpu.VMEM((tq, tk), F32)] * 2 + [pltpu.VMEM((tq, tk), BF16)] * 2
    else:
        scratch = []
    return pl.pallas_call(
        functools.partial(_mla_kernel, tk=tk, n_chunks=n_chunks),
        grid=(bsz, nh, n // tq),
        in_specs=in_specs,
        out_specs=pl.BlockSpec((1, tq, dk), lambda b, h, i: (b, i, h)),
        out_shape=jax.ShapeDtypeStruct((bsz, n, nh * dk), BF16),
        scratch_shapes=scratch,
        compiler_params=_cparams(("arbitrary", "arbitrary", "arbitrary")),
        name="mla_attention" if k is not None else "mla_attention_ctx",
    )(*ins)


def _mix_out_kernel(*refs, ctx, has_conv, tm, n_tiles):
    if has_conv:
        (gb_ref, z_ref, zp_ref, zn_ref, cw_ref, cb_ref, wc_ref,
         att_ref, wa_ref, x_ref, mod_ref, g2_ref, rw_ref, xs_ref, h2_ref, aff_ref) = refs
    else:
        att_ref, wa_ref, x_ref, mod_ref, g2_ref, rw_ref, xs_ref, h2_ref, aff_ref = refs
    y = _dot(att_ref[0], wa_ref[...])
    if has_conv:
        i = pl.program_id(1)
        z = z_ref[0].astype(F32)
        rows = lax.broadcasted_iota(I32, z.shape, 0)
        zprev_halo = jnp.where(i > 0, zp_ref[0, ROW_ALIGN - 1:ROW_ALIGN, :].astype(F32), 0.0)
        znext_halo = jnp.where(i < n_tiles - 1, zn_ref[0, 0:1, :].astype(F32), 0.0)
        zprev = jnp.where(rows == 0, zprev_halo, pltpu.roll(z, 1, 0))
        znext = jnp.where(rows == tm - 1, znext_halo, pltpu.roll(z, tm - 1, 0))
        cw = cw_ref[...]
        conv = gb_ref[0].astype(F32) * (zprev * cw[0:1] + z * cw[1:2] + znext * cw[2:3] + cb_ref[...])
        y = y + _dot(conv.astype(BF16), wc_ref[...])
    xs = x_ref[0] + _mod_row(mod_ref, ctx, 2) * y
    xs_ref[0] = xs
    h2 = _rms_mod(xs, g2_ref[...], _mod_row(mod_ref, ctx, 3), _mod_row(mod_ref, ctx, 4))
    h2_ref[0] = h2.astype(BF16)
    h2_hi = h2.astype(BF16)
    h2_lo = (h2 - h2_hi.astype(F32)).astype(BF16)
    hh = _dot(h2_hi, rw_ref[...])
    logits = hh[:, :LANES] + hh[:, LANES:] + _dot(h2_lo, rw_ref[:, :LANES])
    lane = lax.broadcasted_iota(I32, logits.shape, 1)
    logits = jnp.where(lane < N_EXPERTS, logits, -1e30)
    mx = jnp.max(logits, axis=-1, keepdims=True)
    ex = jnp.exp(logits - mx)
    aff = ex / jnp.sum(ex, axis=-1, keepdims=True)
    aff_ref[0] = aff[:, :N_EXPERTS]


def mix_out(att, wa, x, mod_l, g2, rw, conv=None, *, ctx, tm):
    bsz, n, d = x.shape
    tm = min(tm, n)
    n_tiles = n // tm
    full = lambda a: pl.BlockSpec(a.shape, lambda b, i: (0,) * a.ndim)
    row = lambda w: pl.BlockSpec((1, tm, w), lambda b, i: (b, i, 0))
    ins, in_specs = [], []
    if conv is not None:
        gb, z, cw, cb, wc = conv
        r8 = tm // ROW_ALIGN
        nb8 = n // ROW_ALIGN
        ins += [gb, z, z, z, cw, cb, wc]
        in_specs += [
            row(512), row(512),
            pl.BlockSpec((1, ROW_ALIGN, 512), lambda b, i: (b, jnp.maximum(i * r8 - 1, 0), 0)),
            pl.BlockSpec((1, ROW_ALIGN, 512), lambda b, i: (b, jnp.minimum((i + 1) * r8, nb8 - 1), 0)),
            full(cw), full(cb), full(wc),
        ]
    ins += [att, wa, x, mod_l, g2, rw]
    in_specs += [row(att.shape[-1]), full(wa), row(d), full(mod_l), full(g2), full(rw)]
    return pl.pallas_call(
        functools.partial(_mix_out_kernel, ctx=ctx, has_conv=conv is not None, tm=tm, n_tiles=n_tiles),
        grid=(bsz, n_tiles),
        in_specs=in_specs,
        out_specs=[row(d), row(d), row(N_EXPERTS)],
        out_shape=[
            jax.ShapeDtypeStruct((bsz, n, d), F32),
            jax.ShapeDtypeStruct((bsz, n, d), BF16),
            jax.ShapeDtypeStruct((bsz, n, N_EXPERTS), F32),
        ],
        compiler_params=_cparams(("arbitrary", "arbitrary")),
        name=("mix_out_ctx" if ctx else "mix_out") + ("_conv" if conv is not None else ""),
    )(*ins)


def _route_kernel(aff_ref, ls_ref, us_ref, eye_ref, rank_ref, rankt_ref, oa_ref, nw_ref,
                  thr_s, need_s, eq_s, oa_s, *, n, cap, tb, win, chunk):
    n_chunks = n // chunk
    nblk = n // tb
    k = pl.program_id(1)

    def count(pred_fn):
        def body(c, acc):
            kb = pltpu.bitcast(aff_ref[0, pl.ds(pl.multiple_of(c * chunk, chunk), chunk), :], I32)
            return acc + jnp.sum(pred_fn(kb).astype(I32), axis=0, keepdims=True)
        return lax.fori_loop(0, n_chunks, body, jnp.zeros((1, N_EXPERTS), I32))

    @pl.when(k == 0)
    def _():
        def bit_body(i, thr):
            cand = thr | jnp.left_shift(jnp.int32(1), 30 - i)
            cnt = count(lambda kb: kb >= cand)
            return jnp.where(cnt >= cap, cand, thr)

        thr = lax.fori_loop(0, 31, bit_body, jnp.zeros((1, N_EXPERTS), I32))
        thr_s[...] = thr
        need_s[...] = (cap - count(lambda kb: kb > thr)).astype(F32)
        eq_s[...] = jnp.zeros((1, N_EXPERTS), F32)
        oa_s[...] = jnp.zeros((1, N_EXPERTS), F32)

    thr = thr_s[...]
    off = pl.multiple_of(k * tb, tb)
    kb = pltpu.bitcast(aff_ref[0, pl.ds(off, tb), :], I32)
    gt = kb > thr
    eq = kb == thr
    eqf = jnp.where(eq, 1.0, 0.0)
    eqrank = eq_s[...] + _dot(ls_ref[...], eqf.astype(BF16))
    sel = jnp.logical_or(gt, jnp.logical_and(eq, eqrank < need_s[...]))
    self_ = jnp.where(sel, 1.0, 0.0)
    selb = self_.astype(BF16)
    lrank = _dot(ls_ref[...], selb)
    rank_ref[0] = jnp.where(sel, lrank, -1.0)
    lrank_t = _dot_tn(selb, us_ref[...])
    sel_t = _dot_tn(selb, eye_ref[...])
    rankt_ref[0] = jnp.where(sel_t > 0.5, lrank_t, -1.0)
    c = jnp.sum(self_, axis=0, keepdims=True)
    oa_run = oa_s[...]
    oa_ref[0, pl.ds(k, 1), :] = oa_run.astype(I32)
    nwin = jnp.max(jnp.ceil(c * (1.0 / win)), axis=1, keepdims=True)
    nw_ref[0, pl.ds(k, 1), :] = jnp.broadcast_to(nwin, (1, N_EXPERTS)).astype(I32)
    oa_new = oa_run + jnp.ceil(c * (1.0 / ROW_ALIGN)) * ROW_ALIGN
    oa_s[...] = oa_new
    eq_s[...] = eq_s[...] + jnp.sum(eqf, axis=0, keepdims=True)

    @pl.when(k == nblk - 1)
    def _():
        oa_ref[0, pl.ds(nblk, 1), :] = oa_new.astype(I32)


def route(aff, *, tb, win):
    bsz, n, ne = aff.shape
    cap = max(1, (EC_CAPACITY * n) // ne)
    nblk = n // tb
    chunk = min(1024, n)
    ii = np.arange(tb)
    ls = jnp.asarray(ii[None, :] < ii[:, None], BF16)
    us = jnp.asarray(ii[:, None] < ii[None, :], BF16)
    eye = jnp.asarray(ii[:, None] == ii[None, :], BF16)
    full = lambda a: pl.BlockSpec(a.shape, lambda b, k: (0,) * a.ndim)
    return pl.pallas_call(
        functools.partial(_route_kernel, n=n, cap=cap, tb=tb, win=win, chunk=chunk),
        grid=(bsz, nblk),
        in_specs=[pl.BlockSpec((1, n, ne), lambda b, k: (b, 0, 0)), full(ls), full(us), full(eye)],
        out_specs=[
            pl.BlockSpec((1, tb, ne), lambda b, k: (b, k, 0)),
            pl.BlockSpec((1, ne, tb), lambda b, k: (b, 0, k)),
            pl.BlockSpec((1, nblk + 1, ne), lambda b, k: (b, 0, 0)),
            pl.BlockSpec((1, nblk, ne), lambda b, k: (b, 0, 0)),
        ],
        out_shape=[
            jax.ShapeDtypeStruct((bsz, n, ne), F32),
            jax.ShapeDtypeStruct((bsz, ne, n), F32),
            jax.ShapeDtypeStruct((bsz, nblk + 1, ne), I32),
            jax.ShapeDtypeStruct((bsz, nblk, ne), I32),
        ],
        scratch_shapes=[pltpu.VMEM((1, ne), I32), pltpu.VMEM((1, ne), F32),
                        pltpu.VMEM((1, ne), F32), pltpu.VMEM((1, ne), F32)],
        compiler_params=_cparams(("arbitrary", "arbitrary")),
        name="route",
    )(aff, ls, us, eye)


def _dispatch_kernel(oa_ref, nw_ref, rankt_ref, h_ref, xg_in_ref, xg_ref, stack, sem,
                     *, nblk, n_steps, win, tb, group):
    del xg_in_ref
    b = pl.program_id(0)
    k = pl.program_id(1)
    step = b * nblk + k
    slot = step % 2
    ne = N_EXPERTS

    def fill(sl, j):
        h = h_ref[0]
        for g0 in range(0, ne, group):
            pieces = []
            for e in range(g0, g0 + group):
                r = rankt_ref[0, e:e + 1, :]
                srow = lax.broadcasted_iota(I32, (win, tb), 0).astype(F32) + (j * win).astype(F32)
                pieces.append((r == srow).astype(BF16))
            oh_t = jnp.concatenate(pieces, axis=0)
            stack[sl, g0 * win:(g0 + group) * win, :] = _dot(oh_t, h).astype(BF16)

    def copies(sl, bb, kk, j):
        out = []
        for e in range(ne):
            off = pl.multiple_of(oa_ref[(bb * (nblk + 1) + kk) * ne + e] + j * win, ROW_ALIGN)
            out.append(pltpu.make_async_copy(
                stack.at[sl, pl.ds(e * win, win), :],
                xg_ref.at[bb, e, pl.ds(off, win), :],
                sem.at[sl]))
        return out

    fill(slot, jnp.int32(0))

    @pl.when(step > 0)
    def _():
        for c in copies(1 - slot, b, k, 0):
            c.wait()

    for c in copies(slot, b, k, 0):
        c.start()

    nwx = nw_ref[(b * nblk + k) * ne]

    def extra(j, carry):
        fill(2, j)
        cs = copies(2, b, k, j)
        for c in cs:
            c.start()
        for c in cs:
            c.wait()
        return carry

    lax.fori_loop(1, jnp.maximum(nwx, 1), extra, 0)

    @pl.when(step == n_steps - 1)
    def _():
        for c in copies(slot, b, k, 0):
            c.wait()


def dispatch(oa, nw, rank_t, h2, *, tb, win, c_pad):
    bsz, n, d = h2.shape
    ne = N_EXPERTS
    nblk = n // tb
    group = 4
    xg0 = jnp.zeros((bsz, ne, c_pad, d), BF16)
    grid_spec = pltpu.PrefetchScalarGridSpec(
        num_scalar_prefetch=2,
        grid=(bsz, nblk),
        in_specs=[
            pl.BlockSpec((1, ne, tb), lambda b, k, *_: (b, 0, k)),
            pl.BlockSpec((1, tb, d), lambda b, k, *_: (b, k, 0)),
            pl.BlockSpec(memory_space=pl.ANY),
        ],
        out_specs=pl.BlockSpec(memory_space=pl.ANY),
        scratch_shapes=[
            pltpu.VMEM((3, ne * win, d), BF16),
            pltpu.SemaphoreType.DMA((3,)),
        ],
    )
    return pl.pallas_call(
        functools.partial(_dispatch_kernel, nblk=nblk, n_steps=bsz * nblk, win=win, tb=tb, group=group),
        grid_spec=grid_spec,
        out_shape=jax.ShapeDtypeStruct(xg0.shape, BF16),
        input_output_aliases={4: 0},
        compiler_params=_cparams(("arbitrary", "arbitrary")),
        name="dispatch",
    )(oa.reshape(-1), nw.reshape(-1), rank_t, h2, xg0)


def _ffn_kernel(tot_ref, x_ref, wg_ref, wu_ref, wd_ref, y_ref, wgb, wub, wdb, *, tm, nblk, fc):
    e = pl.program_id(0)
    b = pl.program_id(1)
    i = pl.program_id(2)

    @pl.when(jnp.logical_and(b == 0, i == 0))
    def _():
        wgb[...] = wg_ref[0, 0].astype(BF16)
        wub[...] = wu_ref[0, 0].astype(BF16)
        wdb[...] = wd_ref[0, 0].astype(BF16)

    total = tot_ref[(b * (nblk + 1) + nblk) * N_EXPERTS + e]

    @pl.when(i * tm < total)
    def _():
        x = x_ref[0, 0]
        y = None
        for c0 in range(0, wgb.shape[1], fc):
            a = _dot(x, wgb[:, c0:c0 + fc])
            u = _dot(x, wub[:, c0:c0 + fc])
            part = _dot((_silu(a) * u).astype(BF16), wdb[c0:c0 + fc, :])
            y = part if y is None else y + part
        y_ref[0, 0] = y.astype(BF16)

    @pl.when(i * tm >= total)
    def _():
        y_ref[0, 0] = jnp.zeros(y_ref.shape[2:], BF16)


def expert_ffn(oa, xg, w_gate, w_up, w_down, layer, *, tm, nblk):
    bsz, ne, c_pad, d = xg.shape
    f = w_gate.shape[-1]
    grid_spec = pltpu.PrefetchScalarGridSpec(
        num_scalar_prefetch=1,
        grid=(ne, bsz, c_pad // tm),
        in_specs=[
            pl.BlockSpec((1, 1, tm, d), lambda e, b, i, *_: (b, e, i, 0)),
            pl.BlockSpec((1, 1, d, f), lambda e, b, i, *_: (layer, e, 0, 0)),
            pl.BlockSpec((1, 1, d, f), lambda e, b, i, *_: (layer, e, 0, 0)),
            pl.BlockSpec((1, 1, f, d), lambda e, b, i, *_: (layer, e, 0, 0)),
        ],
        out_specs=pl.BlockSpec((1, 1, tm, d), lambda e, b, i, *_: (b, e, i, 0)),
        scratch_shapes=[pltpu.VMEM((d, f), BF16), pltpu.VMEM((d, f), BF16), pltpu.VMEM((f, d), BF16)],
    )
    return pl.pallas_call(
        functools.partial(_ffn_kernel, tm=tm, nblk=nblk, fc=256),
        grid_spec=grid_spec,
        out_shape=jax.ShapeDtypeStruct(xg.shape, BF16),
        compiler_params=_cparams(("arbitrary", "arbitrary", "arbitrary")),
        name="expert_ffn",
    )(oa.reshape(-1), xg, w_gate, w_up, w_down)


def _combine_kernel(oa_ref, nw_ref, rank_ref, aff_ref, xs_ref, mod_ref, *rest,
                    ctx, nblk, n_steps, win, tb, final):
    if final:
        fg_ref, ye_ref, o_ref, stack, sem = rest
    else:
        ye_ref, o_ref, stack, sem = rest
    b = pl.program_id(0)
    k = pl.program_id(1)
    step = b * nblk + k
    slot = step % 2
    ne = N_EXPERTS

    def copies(sl, bb, kk, j):
        out = []
        for e in range(ne):
            off = pl.multiple_of(oa_ref[(bb * (nblk + 1) + kk) * ne + e] + j * win, ROW_ALIGN)
            out.append(pltpu.make_async_copy(
                ye_ref.at[bb, e, pl.ds(off, win), :],
                stack.at[sl, pl.ds(e * win, win), :],
                sem.at[sl]))
        return out

    @pl.when(step == 0)
    def _():
        for c in copies(0, b, k, 0):
            c.start()

    @pl.when(step + 1 < n_steps)
    def _():
        nxt = step + 1
        for c in copies(1 - slot, nxt // nblk, nxt % nblk, 0):
            c.start()

    for c in copies(slot, b, k, 0):
        c.wait()

    def gated_onehot(j):
        rank = rank_ref[0]
        aff = aff_ref[0]
        pieces = []
        for e in range(ne):
            lane = lax.broadcasted_iota(I32, (tb, win), 1).astype(F32) + (j * win).astype(F32)
            pieces.append(jnp.where(rank[:, e:e + 1] == lane, aff[:, e:e + 1], 0.0).astype(BF16))
        return jnp.concatenate(pieces, axis=1)

    y = _dot(gated_onehot(jnp.int32(0)), stack[slot])

    nwx = nw_ref[(b * nblk + k) * ne]

    def extra(j, y):
        cs = copies(2, b, k, j)
        for c in cs:
            c.start()
        for c in cs:
            c.wait()
        return y + _dot(gated_onehot(j), stack[2])

    y = lax.fori_loop(1, jnp.maximum(nwx, 1), extra, y)
    out = xs_ref[0] + _mod_row(mod_ref, ctx, 5) * y
    if final:
        out = out * lax.rsqrt(jnp.mean(out * out, axis=-1, keepdims=True) + EPS) * fg_ref[...]
    o_ref[0] = out


def combine(oa, nw, rank, aff, xs, mod_l, ye, final_g=None, *, ctx, tb, win):
    bsz, n, d = xs.shape
    ne = N_EXPERTS
    nblk = n // tb
    final = final_g is not None
    full = lambda a: pl.BlockSpec(a.shape, lambda b, k, *_: (0,) * a.ndim)
    ins = [rank, aff, xs, mod_l]
    in_specs = [
        pl.BlockSpec((1, tb, ne), lambda b, k, *_: (b, k, 0)),
        pl.BlockSpec((1, tb, ne), lambda b, k, *_: (b, k, 0)),
        pl.BlockSpec((1, tb, d), lambda b, k, *_: (b, k, 0)),
        full(mod_l),
    ]
    if final:
        ins.append(final_g)
        in_specs.append(full(final_g))
    ins.append(ye)
    in_specs.append(pl.BlockSpec(memory_space=pl.ANY))
    grid_spec = pltpu.PrefetchScalarGridSpec(
        num_scalar_prefetch=2,
        grid=(bsz, nblk),
        in_specs=in_specs,
        out_specs=pl.BlockSpec((1, tb, d), lambda b, k, *_: (b, k, 0)),
        scratch_shapes=[
            pltpu.VMEM((3, ne * win, d), BF16),
            pltpu.SemaphoreType.DMA((3,)),
        ],
    )
    return pl.pallas_call(
        functools.partial(_combine_kernel, ctx=ctx, nblk=nblk, n_steps=bsz * nblk, win=win, tb=tb, final=final),
        grid_spec=grid_spec,
        out_shape=jax.ShapeDtypeStruct(xs.shape, F32),
        compiler_params=_cparams(("arbitrary", "arbitrary")),
        name="combine_final" if final else ("combine_ctx" if ctx else "combine"),
    )(oa.reshape(-1), nw.reshape(-1), *ins)


def moe(xs1, h2, aff, mod_l, w_gate, w_up, w_down, layer, final_g=None, *, ctx):
    bsz, n, d = xs1.shape
    cap = max(1, (EC_CAPACITY * n) // N_EXPERTS)
    tb = min(512, n)
    win = LANES
    nblk = n // tb
    tm = 384 if n >= 4096 else 128
    need = cap + ROW_ALIGN * nblk + win
    c_pad = -(-need // tm) * tm
    rank, rank_t, oa, nw = route(aff, tb=tb, win=win)
    xg = dispatch(oa, nw, rank_t, h2, tb=tb, win=win, c_pad=c_pad)
    ye = expert_ffn(oa, xg, w_gate, w_up, w_down, layer, tm=tm, nblk=nblk)
    return combine(oa, nw, rank, aff, xs1, mod_l, ye, final_g, ctx=ctx, tb=tb, win=win)


def _swa_in_kernel(x_ref, mod_ref, g_ref, w_ref, cos_ref, sin_ref, q_ref, k_ref, v_ref, *, ctx, scale):
    h = _rms_mod(x_ref[0], g_ref[...], _mod_row(mod_ref, ctx, 0), _mod_row(mod_ref, ctx, 1))
    p = _dot(h.astype(BF16), w_ref[...])
    cos = cos_ref[...]
    sin = sin_ref[...]
    cs = cos * scale
    ss = sin * scale
    for j in range(8):
        a = p[:, j * 128:(j + 1) * 128]
        bsw = p[:, 1024 + j * 128:1024 + (j + 1) * 128]
        q_ref[0, :, j * 128:(j + 1) * 128] = (a * cs + bsw * ss).astype(BF16)
    lane = lax.broadcasted_iota(I32, cos.shape, 1)
    low = lane < 64
    for pr in range(2):
        kp = p[:, 2048 + pr * 128:2048 + (pr + 1) * 128] * cos + p[:, 2304 + pr * 128:2304 + (pr + 1) * 128] * sin
        vp = p[:, 2560 + pr * 128:2560 + (pr + 1) * 128]
        for src, dst in ((kp, k_ref), (vp, v_ref)):
            ev_lo = jnp.where(low, src, 0.0)
            od_hi = jnp.where(low, 0.0, src)
            ev_hi = pltpu.roll(ev_lo, 64, 1)
            od_lo = pltpu.roll(od_hi, 64, 1)
            base = pr * 512
            dst[0, :, base:base + 128] = ev_lo.astype(BF16)
            dst[0, :, base + 128:base + 256] = ev_hi.astype(BF16)
            dst[0, :, base + 256:base + 384] = od_lo.astype(BF16)
            dst[0, :, base + 384:base + 512] = od_hi.astype(BF16)


def swa_in(x, mod_l, g, w2, cos_t, sin_t, *, ctx, tm):
    bsz, n, d = x.shape
    tm = min(tm, n)
    full = lambda a: pl.BlockSpec(a.shape, lambda b, i: (0,) * a.ndim)
    row = lambda w: pl.BlockSpec((1, tm, w), lambda b, i: (b, i, 0))
    return pl.pallas_call(
        functools.partial(_swa_in_kernel, ctx=ctx, scale=SWA_HEAD_DIM ** -0.5 * math.log2(math.e)),
        grid=(bsz, n // tm),
        in_specs=[row(d), full(mod_l), full(g), full(w2),
                  pl.BlockSpec((tm, 128), lambda b, i: (i, 0)),
                  pl.BlockSpec((tm, 128), lambda b, i: (i, 0))],
        out_specs=[row(1024), row(1024), row(1024)],
        out_shape=[jax.ShapeDtypeStruct((bsz, n, 1024), BF16)] * 3,
        compiler_params=_cparams(("arbitrary", "arbitrary")),
        name="swa_in_ctx" if ctx else "swa_in",
    )(x, mod_l, g, w2, cos_t, sin_t)


def _swa_kernel(sink_ref, q_ref, km_ref, kp_ref, kn_ref, vm_ref, vp_ref, vn_ref, kc_ref, vc_ref,
                o_ref, kwin, vwin, *, tq, n_tiles):
    i = pl.program_id(1)
    blk = SWA_BLOCK
    kwin[0:blk] = kp_ref[0]
    kwin[blk:blk + tq] = km_ref[0]
    kwin[blk + tq:blk + tq + blk] = kn_ref[0]
    vwin[0:blk] = vp_ref[0]
    vwin[blk:blk + tq] = vm_ref[0]
    vwin[blk + tq:blk + tq + blk] = vn_ref[0]
    r = lax.broadcasted_iota(I32, (blk, 3 * blk), 0)
    s = lax.broadcasted_iota(I32, (blk, 3 * blk), 1)
    band = jnp.abs(r - (s - blk)) <= WINDOW
    low_lanes = lax.broadcasted_iota(I32, (blk, 128), 1) < 64

    def qblock(qb, carry):
        row0 = pl.multiple_of(qb * blk, blk)
        first = jnp.logical_and(i == 0, qb == 0)
        last = jnp.logical_and(i == n_tiles - 1, qb == tq // blk - 1)
        valid = jnp.logical_and(band, jnp.logical_and(jnp.logical_or(s >= blk, jnp.logical_not(first)),
                                                      jnp.logical_or(s < 2 * blk, jnp.logical_not(last))))
        bias = jnp.where(valid, 0.0, -1e30)
        for kvh in range(SWA_KV_HEADS):
            c0 = kvh * 256
            q4 = jnp.concatenate([q_ref[0, pl.ds(row0, blk), c0:c0 + 128],
                                  q_ref[0, pl.ds(row0, blk), c0 + 128:c0 + 256]], axis=0)
            kl = jnp.concatenate([kwin[pl.ds(row0, 3 * blk), c0:c0 + 128],
                                  kwin[pl.ds(row0, 3 * blk), c0 + 128:c0 + 256]], axis=0)
            vl = jnp.concatenate([vwin[pl.ds(row0, 3 * blk), c0:c0 + 128],
                                  vwin[pl.ds(row0, 3 * blk), c0 + 128:c0 + 256]], axis=0)
            s_loc = _dot_nt(q4, kl)
            s_ctx = _dot_nt(q4, kc_ref[0, kvh])
            lc = s_ctx.shape[1] // 2
            p_loc, p_ctx, inv = [], [], []
            for pp in range(2):
                pl_row, pc_row, inv_row = [], [], []
                for hf in range(2):
                    sk = sink_ref[kvh * 4 + pp * 2 + hf]
                    sl = s_loc[pp * blk:(pp + 1) * blk, hf * 3 * blk:(hf + 1) * 3 * blk] + bias
                    sc = s_ctx[pp * blk:(pp + 1) * blk, hf * lc:(hf + 1) * lc]
                    m = jnp.maximum(jnp.maximum(jnp.max(sl, axis=-1, keepdims=True),
                                                jnp.max(sc, axis=-1, keepdims=True)), sk)
                    el = jnp.exp2(sl - m)
                    ec = jnp.exp2(sc - m)
                    den = (jnp.sum(el, axis=-1, keepdims=True) + jnp.sum(ec, axis=-1, keepdims=True)
                           + jnp.exp2(sk - m))
                    pl_row.append(el.astype(BF16))
                    pc_row.append(ec.astype(BF16))
                    inv_row.append(1.0 / den)
                p_loc.append(jnp.concatenate(pl_row, axis=1))
                p_ctx.append(jnp.concatenate(pc_row, axis=1))
                inv.append(jnp.where(low_lanes, inv_row[0], inv_row[1]))
            o4 = _dot(jnp.concatenate(p_loc, axis=0), vl) + _dot(jnp.concatenate(p_ctx, axis=0), vc_ref[0, kvh])
            for pp in range(2):
                pair = kvh * 2 + pp
                o_ref[0, pl.ds(row0, blk), pair * 128:(pair + 1) * 128] = (
                    o4[pp * blk:(pp + 1) * blk] * inv[pp]).astype(BF16)
        return carry

    lax.fori_loop(0, tq // blk, qblock, 0)


def swa_attention(q, k2, v2, kc2, vc2, sink, *, tq):
    bsz, n, _ = q.shape
    lc = kc2.shape[1]
    tq = min(tq, n)
    n_tiles = n // tq
    rb = tq // SWA_BLOCK
    nb = n // SWA_BLOCK
    main = pl.BlockSpec((1, tq, 1024), lambda b, i, *_: (b, i, 0))
    prev = pl.BlockSpec((1, SWA_BLOCK, 1024), lambda b, i, *_: (b, jnp.maximum(i * rb - 1, 0), 0))
    nxt = pl.BlockSpec((1, SWA_BLOCK, 1024), lambda b, i, *_: (b, jnp.minimum((i + 1) * rb, nb - 1), 0))
    stack = lambda a: a.reshape(bsz, lc, SWA_KV_HEADS, 2, 128).transpose(0, 2, 3, 1, 4).reshape(
        bsz, SWA_KV_HEADS, 2 * lc, 128)
    kc2, vc2 = stack(kc2), stack(vc2)
    sink = sink * math.log2(math.e)
    cspec = pl.BlockSpec((1, SWA_KV_HEADS, 2 * lc, 128), lambda b, i, *_: (b, 0, 0, 0))
    grid_spec = pltpu.PrefetchScalarGridSpec(
        num_scalar_prefetch=1,
        grid=(bsz, n_tiles),
        in_specs=[main, main, prev, nxt, main, prev, nxt, cspec, cspec],
        out_specs=main,
        scratch_shapes=[pltpu.VMEM((tq + 2 * SWA_BLOCK, 1024), BF16)] * 2,
    )
    return pl.pallas_call(
        functools.partial(_swa_kernel, tq=tq, n_tiles=n_tiles),
        grid_spec=grid_spec,
        out_shape=jax.ShapeDtypeStruct((bsz, n, 1024), BF16),
        compiler_params=_cparams(("arbitrary", "arbitrary")),
        name="swa_attention",
    )(sink, q, k2, k2, k2, v2, v2, v2, kc2, vc2)


def _prep_ab(w_in, w_uq, w_ukv, w_out):
    d = w_in.shape[0]
    perm = _rope_perm(MLA_ROPE)
    o = 3 * CONV_DIM
    kr = w_in[:, o + MLA_Q_LORA + MLA_KV_LORA:]
    z64 = jnp.zeros((d, 64), F32)
    z32 = jnp.zeros((d, 32), F32)
    win2 = jnp.concatenate([w_in[:, :o + MLA_Q_LORA + MLA_KV_LORA], z64, kr, z32, z64, kr[:, perm], z32],
                           axis=1).astype(BF16)
    dq = MLA_NOPE + MLA_ROPE
    wq = w_uq.reshape(MLA_Q_LORA, MLA_HEADS, dq)
    zq = jnp.zeros((MLA_Q_LORA, MLA_HEADS, 32), F32)
    zq64 = jnp.zeros((MLA_Q_LORA, MLA_HEADS, 64), F32)
    qa = jnp.concatenate([wq, zq], axis=-1).reshape(MLA_Q_LORA, MLA_HEADS * 128)
    qb = jnp.concatenate([zq64, wq[:, :, MLA_NOPE:][:, :, perm], zq], axis=-1).reshape(MLA_Q_LORA, MLA_HEADS * 128)
    wq2 = jnp.concatenate([qa, qb], axis=1).astype(BF16)
    wkv = w_ukv.reshape(MLA_KV_LORA, MLA_HEADS, MLA_NOPE + MLA_V)
    zk = jnp.zeros((MLA_KV_LORA, MLA_HEADS, 64), F32)
    ka = jnp.concatenate([wkv[:, :, :MLA_NOPE], zk], axis=-1).reshape(MLA_KV_LORA, MLA_HEADS * 128)
    va = jnp.concatenate([wkv[:, :, MLA_NOPE:], zk], axis=-1).reshape(MLA_KV_LORA, MLA_HEADS * 128)
    wkv2 = jnp.concatenate([ka, va], axis=1).astype(BF16)
    wc = w_out[:CONV_DIM].astype(BF16)
    wa = w_out[CONV_DIM:].reshape(MLA_HEADS, MLA_V, d)
    wa = jnp.concatenate([wa, jnp.zeros_like(wa)], axis=1).reshape(MLA_HEADS * 128, d).astype(BF16)
    return win2, wq2, wkv2, wc, wa


def _prep_swa(w_qkv):
    hd = SWA_HEAD_DIM
    perm = _rope_perm(hd)
    nq = SWA_HEADS * hd
    nk = SWA_KV_HEADS * hd
    wq = w_qkv[:, :nq]
    wk = w_qkv[:, nq:nq + nk]
    wv = w_qkv[:, nq + nk:]
    d = w_qkv.shape[0]
    wq_sw = wq.reshape(d, SWA_HEADS, hd)[:, :, perm].reshape(d, nq)
    wk_sw = wk.reshape(d, SWA_KV_HEADS, hd)[:, :, perm].reshape(d, nk)
    return jnp.concatenate([wq, wq_sw, wk, wk_sw, wv], axis=1).astype(BF16)


def _pad_lanes(t, left, width=LANES, fill=0.0):
    n, w = t.shape
    return jnp.concatenate([jnp.full((n, left), fill, F32), t, jnp.zeros((n, width - left - w), F32)], axis=1)


def kernel(x, c, ctx, c_ctx, mod_w, mod_b, norm1_g, norm2_g, ab_w_in, conv_w, conv_b, mla_q_norm_g, mla_w_uq,
           mla_kv_norm_g, mla_w_ukv, ab_w_out, swa_w_qkv, swa_sink, swa_w_out, router_w, exp_w_gate, exp_w_up,
           exp_w_down, final_g):
    bsz, n, d = x.shape
    lc = ctx.shape[1]
    depth = mod_w.shape[0]
    assert bsz <= 2 and d == 1024

    cs = jnp.concatenate([c, c_ctx[None, :], jnp.zeros((8 - bsz - 1, d), F32)], axis=0)
    mod = modulation(cs, mod_w, mod_b)

    cos_m, sin_m = _rope_tables(n, MLA_ROPE)
    cos_mla = _pad_lanes(cos_m, 64, fill=1.0)
    sin_mla = _pad_lanes(sin_m, 64)
    cos_mla_c = jnp.concatenate([jnp.ones((lc, 96), F32), jnp.zeros((lc, 32), F32)], axis=1)
    zeros_c = jnp.zeros((lc, 128), F32)
    cos_s, sin_s = _rope_tables(n, SWA_HEAD_DIM)
    cos_swa = jnp.concatenate([cos_s, cos_s], axis=1)
    sin_swa = jnp.concatenate([sin_s, sin_s], axis=1)
    ones_c = jnp.ones((lc, 128), F32)

    row2 = lambda v: v.reshape(1, -1)
    xs, xc = x, ctx
    for layer in range(depth):
        need_ctx = layer < depth - 1
        last = layer == depth - 1
        mod_l = mod[layer]
        g1 = row2(norm1_g[layer])
        g2 = row2(norm2_g[layer])
        rw_f = jnp.concatenate([router_w[layer], jnp.zeros((d, LANES - N_EXPERTS), F32)], axis=1)
        rw_hi = rw_f.astype(BF16)
        rw = jnp.concatenate([rw_hi, (rw_f - rw_hi.astype(F32)).astype(BF16)], axis=1)
        wg, wu, wd = exp_w_gate, exp_w_up, exp_w_down
        if layer % 2 == 0:
            e = layer // 2
            win2, wq2, wkv2, wc, wa = _prep_ab(ab_w_in[e], mla_w_uq[e], mla_w_ukv[e], ab_w_out[e])
            qg, kvg = row2(mla_q_norm_g[e]), row2(mla_kv_norm_g[e])
            cw, cb = conv_w[e], row2(conv_b[e])
            gb, z, q, k, v = ab_in(xs, mod_l, g1, win2, qg, wq2, kvg, wkv2, cos_mla, sin_mla, ctx=False, tm=512)
            gbc, zc, qc, kc, vc = ab_in(xc, mod_l, g1, win2, qg, wq2, kvg, wkv2, cos_mla_c, zeros_c, ctx=True, tm=256)
            att = mla_attention(q, kc, vc, k, v, tq=512, tk=2048)
            xs1, h2, aff = mix_out(att, wa, xs, mod_l, g2, rw, (gb, z, cw, cb, wc), ctx=False, tm=512)
            if need_ctx:
                att_c = mla_attention(qc, kc, vc, tq=256, tk=256)
                xc1, hc2, affc = mix_out(att_c, wa, xc, mod_l, g2, rw, (gbc, zc, cw, cb, wc), ctx=True, tm=256)
        else:
            o = layer // 2
            w2 = _prep_swa(swa_w_qkv[o])
            wo = swa_w_out[o].astype(BF16)
            q, k2, v2 = swa_in(xs, mod_l, g1, w2, cos_swa, sin_swa, ctx=False, tm=512)
            qc, kc2, vc2 = swa_in(xc, mod_l, g1, w2, ones_c, zeros_c, ctx=True, tm=256)
            att = swa_attention(q, k2, v2, kc2, vc2, swa_sink[o], tq=512)
            xs1, h2, aff = mix_out(att, wo, xs, mod_l, g2, rw, ctx=False, tm=512)
            if need_ctx:
                raise NotImplementedError("context self-attention for windowed layers below the last")
        xs = moe(xs1, h2, aff, mod_l, wg, wu, wd, layer, final_g=row2(final_g) if last else None, ctx=False)
        if need_ctx:
            xc = moe(xc1, hc2, affc, mod_l, wg, wu, wd, layer, ctx=True)
    return xs
```

```python
import functools
import math

import jax
import jax.numpy as jnp
import numpy as np
from jax import lax
from jax.experimental import pallas as pl
from jax.experimental.pallas import tpu as pltpu

F32 = jnp.float32
BF16 = jnp.bfloat16
I32 = jnp.int32

GRID_W = 64
EPS = 1e-6
ROPE_THETA = 10000.0
CONV_DIM = 512
MLA_HEADS = 8
MLA_Q_LORA = 256
MLA_KV_LORA = 128
MLA_NOPE = 64
MLA_ROPE = 32
MLA_V = 64
SWA_HEADS = 16
SWA_KV_HEADS = 4
SWA_HEAD_DIM = 64
WINDOW = 128
SWA_BLOCK = 128
N_EXPERTS = 16
EC_CAPACITY = 2
N_MOD = 6

LANES = 128
ROW_ALIGN = 16
VMEM_LIMIT = 56 * 1024 * 1024


def _cparams(sem, vmem=VMEM_LIMIT):
    return pltpu.CompilerParams(dimension_semantics=sem, vmem_limit_bytes=vmem)


def _dot(a, b):
    return jnp.dot(a, b, preferred_element_type=F32)


def _dot_nt(a, b):
    return lax.dot_general(a, b, (((1,), (1,)), ((), ())), preferred_element_type=F32)


def _dot_tn(a, b):
    return lax.dot_general(a, b, (((0,), (0,)), ((), ())), preferred_element_type=F32)


def _dot_hi(a, b):
    return jnp.dot(a, b, preferred_element_type=F32, precision=lax.Precision.HIGHEST)


def _silu(x):
    return x * (1.0 / (1.0 + jnp.exp(-x)))


def _rms_mod(x, g, shift, scale):
    y = x * lax.rsqrt(jnp.mean(x * x, axis=-1, keepdims=True) + EPS)
    return (y * g) * (1.0 + scale) + shift


def _mod_row(mod_ref, ctx, which):
    r = 2 if ctx else pl.program_id(0)
    return mod_ref[pl.ds(r, 1), which * 1024:(which + 1) * 1024]


def _mod_kernel(cs_ref, w_ref, b_ref, o_ref):
    s = _silu(cs_ref[...])
    o_ref[0] = _dot_hi(s, w_ref[0]) + b_ref[0]


def modulation(cs, mod_w, mod_b):
    depth, d, n6 = mod_w.shape
    tn = 1536
    return pl.pallas_call(
        _mod_kernel,
        grid=(depth, n6 // tn),
        in_specs=[
            pl.BlockSpec((8, d), lambda l, j: (0, 0)),
            pl.BlockSpec((1, d, tn), lambda l, j: (l, 0, j)),
            pl.BlockSpec((1, 1, tn), lambda l, j: (l, 0, j)),
        ],
        out_specs=pl.BlockSpec((1, 8, tn), lambda l, j: (l, 0, j)),
        out_shape=jax.ShapeDtypeStruct((depth, 8, n6), F32),
        compiler_params=_cparams(("arbitrary", "arbitrary")),
        name="modulation",
    )(cs, mod_w, mod_b.reshape(depth, 1, n6))


def _rope_tables(n, dim):
    n_rows = n // GRID_W
    row = jnp.repeat(jnp.arange(n_rows, dtype=F32), GRID_W)
    col = jnp.tile(jnp.arange(GRID_W, dtype=F32), n_rows)
    half = dim // 2
    qd = dim // 4
    freqs = ROPE_THETA ** (-jnp.arange(0, half, 2, dtype=F32) / half)
    ang = jnp.concatenate([row[:, None] * freqs, col[:, None] * freqs], axis=-1)
    cos, sin = jnp.cos(ang), jnp.sin(ang)
    cos_full = jnp.concatenate([cos[:, :qd], cos[:, :qd], cos[:, qd:], cos[:, qd:]], axis=-1)
    sin_sgn = jnp.concatenate([-sin[:, :qd], sin[:, :qd], -sin[:, qd:], sin[:, qd:]], axis=-1)
    return cos_full, sin_sgn


def _rope_perm(dim):
    qd = dim // 4
    ch = np.arange(dim)
    pair = (ch // qd) % 2
    return np.where(pair == 0, ch + qd, ch - qd)


def _ab_in_kernel(x_ref, mod_ref, g_ref, win_ref, qg_ref, wq_ref, kvg_ref, wkv_ref, cos_ref, sin_ref,
                  gb_ref, z_ref, q_ref, k_ref, v_ref, *, ctx, scale):
    x = x_ref[0]
    h = _rms_mod(x, g_ref[...], _mod_row(mod_ref, ctx, 0), _mod_row(mod_ref, ctx, 1))
    p = _dot(h.astype(BF16), win_ref[...])
    gb_ref[0] = p[:, 0:512].astype(BF16)
    z_ref[0] = (p[:, 512:1024] * p[:, 1024:1536]).astype(BF16)
    ql = p[:, 1536:1792]
    kvl = p[:, 1792:1920]
    kra = p[:, 1920:2048]
    krb = p[:, 2048:2176]
    cos = cos_ref[...]
    sin = sin_ref[...]
    qn = ql * lax.rsqrt(jnp.mean(ql * ql, axis=-1, keepdims=True) + EPS) * qg_ref[...]
    qq = _dot(qn.astype(BF16), wq_ref[...])
    kvn = kvl * lax.rsqrt(jnp.mean(kvl * kvl, axis=-1, keepdims=True) + EPS) * kvg_ref[...]
    kv = _dot(kvn.astype(BF16), wkv_ref[...])
    krope = kra * cos + krb * sin
    cs = cos * scale
    ss = sin * scale
    ones_lane = lax.broadcasted_iota(I32, cos.shape, 1) == MLA_V
    for hd in range(MLA_HEADS):
        a = qq[:, hd * 128:(hd + 1) * 128]
        b = qq[:, 1024 + hd * 128:1024 + (hd + 1) * 128]
        q_ref[0, hd] = (a * cs + b * ss).astype(BF16)
        k_ref[0, hd] = (kv[:, hd * 128:(hd + 1) * 128] + krope).astype(BF16)
        v_ref[0, hd] = jnp.where(ones_lane, 1.0, kv[:, 1024 + hd * 128:1024 + (hd + 1) * 128]).astype(BF16)


def ab_in(x, mod_l, g, win2, qg, wq2, kvg, wkv2, cos_t, sin_t, *, ctx, tm):
    bsz, n, d = x.shape
    tm = min(tm, n)
    scale = (MLA_NOPE + MLA_ROPE) ** -0.5 * math.log2(math.e)
    full = lambda a: pl.BlockSpec(a.shape, lambda b, i: (0,) * a.ndim)
    hshape = jax.ShapeDtypeStruct((bsz, MLA_HEADS, n, 128), BF16)
    hspec = pl.BlockSpec((1, MLA_HEADS, tm, 128), lambda b, i: (b, 0, i, 0))
    return pl.pallas_call(
        functools.partial(_ab_in_kernel, ctx=ctx, scale=scale),
        grid=(bsz, n // tm),
        in_specs=[
            pl.BlockSpec((1, tm, d), lambda b, i: (b, i, 0)),
            full(mod_l), full(g), full(win2), full(qg), full(wq2), full(kvg), full(wkv2),
            pl.BlockSpec((tm, 128), lambda b, i: (i, 0)),
            pl.BlockSpec((tm, 128), lambda b, i: (i, 0)),
        ],
        out_specs=[
            pl.BlockSpec((1, tm, 512), lambda b, i: (b, i, 0)),
            pl.BlockSpec((1, tm, 512), lambda b, i: (b, i, 0)),
            hspec, hspec, hspec,
        ],
        out_shape=[
            jax.ShapeDtypeStruct((bsz, n, 512), BF16),
            jax.ShapeDtypeStruct((bsz, n, 512), BF16),
            hshape, hshape, hshape,
        ],
        compiler_params=_cparams(("arbitrary", "arbitrary")),
        name="ab_in_ctx" if ctx else "ab_in",
    )(x, mod_l, g, win2, qg, wq2, kvg, wkv2, cos_t, sin_t)


def _mla_ctx_kernel(q_ref, kc_ref, vc_ref, o_ref):
    s = _dot_nt(q_ref[0, 0], kc_ref[0, 0])
    m = jnp.max(s, axis=-1, keepdims=True)
    acc = _dot(jnp.exp2(s - m).astype(BF16), vc_ref[0, 0])
    o_ref[0] = (acc * (1.0 / acc[:, MLA_V:MLA_V + 1])).astype(BF16)


def mla_attention_ctx(q, kc, vc):
    bsz, nh, lc, dk = q.shape
    kv = pl.BlockSpec((1, 1, lc, dk), lambda b, h: (b, h, 0, 0))
    return pl.pallas_call(
        _mla_ctx_kernel,
        grid=(bsz, nh),
        in_specs=[kv, kv, kv],
        out_specs=pl.BlockSpec((1, lc, dk), lambda b, h: (b, 0, h)),
        out_shape=jax.ShapeDtypeStruct((bsz, lc, nh * dk), BF16),
        compiler_params=_cparams(("arbitrary", "arbitrary")),
        name="mla_attention_ctx",
    )(q, kc, vc)


def _mla_flat_kernel(q_ref, k_ref, v_ref, kc_ref, vc_ref, o_ref, kall, vall, s_a, s_b, p_a, p_b,
                     *, tq, tk, n_tiles, n_chunks, spt):
    n_stages = n_tiles * n_chunks
    n_lat, n_ctx = k_ref.shape[2], kc_ref.shape[2]
    cp = math.gcd(n_lat, 2048)

    def copy_rows(c, carry):
        sl = pl.ds(pl.multiple_of(c * cp, cp), cp)
        kall[sl, :] = k_ref[0, 0, sl, :]
        vall[sl, :] = v_ref[0, 0, sl, :]
        return carry

    lax.fori_loop(0, n_lat // cp, copy_rows, 0)
    kall[n_lat:n_lat + n_ctx, :] = kc_ref[0, 0]
    vall[n_lat:n_lat + n_ctx, :] = vc_ref[0, 0]

    def nxt(c):
        i, j = c
        wrap = j + 1 == n_chunks
        return jnp.where(wrap, i + 1, i), jnp.where(wrap, 0, j + 1)

    def rows(idx, size):
        return pl.ds(pl.multiple_of(idx * size, size), size)

    def scores(c, s_buf):
        s = _dot_nt(q_ref[0, 0, rows(c[0], tq), :], kall[rows(c[1], tk), :])
        s_buf[...] = s
        return jnp.max(s, axis=-1, keepdims=True)

    def weights(c, s_buf, p_buf, mx, m):
        m_prev = jnp.where(c[1] == 0, -1e30, m)
        m_new = jnp.maximum(m_prev, mx)
        p_buf[...] = jnp.exp2(s_buf[...] - m_new).astype(BF16)
        return m_new, jnp.exp2(m_prev - m_new)

    def wsum(c, p_buf, alpha, acc):
        acc = alpha * acc + _dot(p_buf[...], vall[rows(c[1], tk), :])
        o_ref[0, pl.ds(pl.multiple_of(c[0] * tq, tq), tq), :] = (
            acc * (1.0 / acc[:, MLA_V:MLA_V + 1])).astype(BF16)
        return acc

    def trip(c_pv, c_w, c_s, s_cur, s_nxt, p_prev, p_cur, mx_cur, alpha_prev, m, acc):
        mx_nxt = scores(c_s, s_nxt) if c_s is not None else None
        acc = wsum(c_pv, p_prev, alpha_prev, acc)
        m, alpha_cur = weights(c_w, s_cur, p_cur, mx_cur, m)
        return mx_nxt, alpha_cur, m, acc

    def pair(c, mx_b, alpha_a, m, acc):
        mx_a, alpha_b, m, acc = trip(c[0], c[1], c[2], s_b, s_a, p_a, p_b, mx_b, alpha_a, m, acc)
        mx_b, alpha_a, m, acc = trip(c[1], c[2], c[3], s_a, s_b, p_b, p_a, mx_a, alpha_b, m, acc)
        return mx_b, alpha_a, m, acc

    def chain(c, k):
        out = [c]
        for _ in range(k):
            out.append(nxt(out[-1]))
        return out

    zero = jnp.int32(0)
    c0 = (zero, zero)
    mx_a = scores(c0, s_a)
    m, alpha_a = weights(c0, s_a, p_a, mx_a, jnp.zeros((tq, 1), F32))
    mx_b = scores(nxt(c0), s_b)
    acc = jnp.zeros((tq, LANES), F32)

    def body(_, carry):
        i, j, mx_b, alpha_a, m, acc = carry
        c = chain((i, j), spt + 1)
        for u in range(0, spt, 2):
            mx_b, alpha_a, m, acc = pair(c[u:u + 4], mx_b, alpha_a, m, acc)
        return c[spt][0], c[spt][1], mx_b, alpha_a, m, acc

    n_trips = (n_stages - 4) // spt
    i, j, mx_b, alpha_a, m, acc = lax.fori_loop(0, n_trips, body, (zero, zero, mx_b, alpha_a, m, acc))
    c = chain((i, j), 3)
    mx_b, alpha_a, m, acc = pair(c[0:4], mx_b, alpha_a, m, acc)
    mx_a, alpha_b, m, acc = trip(c[2], c[3], None, s_b, s_a, p_a, p_b, mx_b, alpha_a, m, acc)
    wsum(c[3], p_b, alpha_b, acc)


def mla_attention_flat(q, k, v, kc, vc, *, tq, tk_max, spt):
    bsz, nh, n, dk = q.shape
    lc = kc.shape[2]
    nk = n + lc
    tq = min(tq, n)
    tk = max(t for t in range(256, tk_max + 1, 256) if nk % t == 0)
    n_tiles, n_chunks = n // tq, nk // tk
    assert spt in (2, 4) and (n_tiles * n_chunks) % spt == 0 and n_tiles * n_chunks >= 4 + spt
    head = lambda rows_: pl.BlockSpec((1, 1, rows_, dk), lambda b, h: (b, h, 0, 0))
    return pl.pallas_call(
        functools.partial(_mla_flat_kernel, tq=tq, tk=tk, n_tiles=n_tiles, n_chunks=n_chunks, spt=spt),
        grid=(bsz, nh),
        in_specs=[head(n), head(n), head(n), head(lc), head(lc)],
        out_specs=pl.BlockSpec((1, n, dk), lambda b, h: (b, 0, h)),
        out_shape=jax.ShapeDtypeStruct((bsz, n, nh * dk), BF16),
        scratch_shapes=([pltpu.VMEM((nk, dk), BF16)] * 2 + [pltpu.VMEM((tq, tk), F32)] * 2
                        + [pltpu.VMEM((tq, tk), BF16)] * 2),
        compiler_params=_cparams(("arbitrary", "arbitrary")),
        name="mla_attention",
    )(q, k, v, kc, vc)


def _mix_out_kernel(*refs, ctx, has_conv, tm, n_tiles):
    if has_conv:
        (gb_ref, z_ref, zp_ref, zn_ref, cw_ref, cb_ref, wc_ref,
         att_ref, wa_ref, x_ref, mod_ref, g2_ref, rw_ref, xs_ref, h2_ref, aff_ref) = refs
    else:
        att_ref, wa_ref, x_ref, mod_ref, g2_ref, rw_ref, xs_ref, h2_ref, aff_ref = refs
    y = _dot(att_ref[0], wa_ref[...])
    if has_conv:
        i = pl.program_id(1)
        z = z_ref[0].astype(F32)
        rows = lax.broadcasted_iota(I32, z.shape, 0)
        zprev_halo = jnp.where(i > 0, zp_ref[0, ROW_ALIGN - 1:ROW_ALIGN, :].astype(F32), 0.0)
        znext_halo = jnp.where(i < n_tiles - 1, zn_ref[0, 0:1, :].astype(F32), 0.0)
        zprev = jnp.where(rows == 0, zprev_halo, pltpu.roll(z, 1, 0))
        znext = jnp.where(rows == tm - 1, znext_halo, pltpu.roll(z, tm - 1, 0))
        cw = cw_ref[...]
        conv = gb_ref[0].astype(F32) * (zprev * cw[0:1] + z * cw[1:2] + znext * cw[2:3] + cb_ref[...])
        y = y + _dot(conv.astype(BF16), wc_ref[...])
    xs = x_ref[0] + _mod_row(mod_ref, ctx, 2) * y
    xs_ref[0] = xs
    h2 = _rms_mod(xs, g2_ref[...], _mod_row(mod_ref, ctx, 3), _mod_row(mod_ref, ctx, 4))
    h2_ref[0] = h2.astype(BF16)
    h2_hi = h2.astype(BF16)
    h2_lo = (h2 - h2_hi.astype(F32)).astype(BF16)
    hh = _dot(h2_hi, rw_ref[...])
    logits = hh[:, :LANES] + hh[:, LANES:] + _dot(h2_lo, rw_ref[:, :LANES])
    lane = lax.broadcasted_iota(I32, logits.shape, 1)
    logits = jnp.where(lane < N_EXPERTS, logits, -1e30)
    mx = jnp.max(logits, axis=-1, keepdims=True)
    ex = jnp.exp(logits - mx)
    aff = ex / jnp.sum(ex, axis=-1, keepdims=True)
    aff_ref[0] = aff[:, :N_EXPERTS]


def mix_out(att, wa, x, mod_l, g2, rw, conv=None, *, ctx, tm):
    bsz, n, d = x.shape
    tm = min(tm, n)
    n_tiles = n // tm
    full = lambda a: pl.BlockSpec(a.shape, lambda b, i: (0,) * a.ndim)
    row = lambda w: pl.BlockSpec((1, tm, w), lambda b, i: (b, i, 0))
    ins, in_specs = [], []
    if conv is not None:
        gb, z, cw, cb, wc = conv
        r8 = tm // ROW_ALIGN
        nb8 = n // ROW_ALIGN
        ins += [gb, z, z, z, cw, cb, wc]
        in_specs += [
            row(512), row(512),
            pl.BlockSpec((1, ROW_ALIGN, 512), lambda b, i: (b, jnp.maximum(i * r8 - 1, 0), 0)),
            pl.BlockSpec((1, ROW_ALIGN, 512), lambda b, i: (b, jnp.minimum((i + 1) * r8, nb8 - 1), 0)),
            full(cw), full(cb), full(wc),
        ]
    ins += [att, wa, x, mod_l, g2, rw]
    in_specs += [row(att.shape[-1]), full(wa), row(d), full(mod_l), full(g2), full(rw)]
    return pl.pallas_call(
        functools.partial(_mix_out_kernel, ctx=ctx, has_conv=conv is not None, tm=tm, n_tiles=n_tiles),
        grid=(bsz, n_tiles),
        in_specs=in_specs,
        out_specs=[row(d), row(d), row(N_EXPERTS)],
        out_shape=[
            jax.ShapeDtypeStruct((bsz, n, d), F32),
            jax.ShapeDtypeStruct((bsz, n, d), BF16),
            jax.ShapeDtypeStruct((bsz, n, N_EXPERTS), F32),
        ],
        compiler_params=_cparams(("arbitrary", "arbitrary")),
        name=("mix_out_ctx" if ctx else "mix_out") + ("_conv" if conv is not None else ""),
    )(*ins)


def _route_kernel(aff_ref, aff8_ref, ls_ref, us_ref, eye_ref, rank_ref, rankt_ref, oa_ref, nw_ref,
                  thr_s, need_s, eq_s, oa_s, *, n, cap, tb, win, chunk):
    n_chunks = (n // 8) // chunk
    nblk = n // tb
    k = pl.program_id(1)

    def count(pred_fn):
        def body(c, acc):
            kb = pltpu.bitcast(aff8_ref[0, pl.ds(pl.multiple_of(c * chunk, chunk), chunk), :], I32)
            return acc + jnp.sum(pred_fn(kb).astype(I32), axis=0, keepdims=True)
        acc = lax.fori_loop(0, n_chunks, body, jnp.zeros((1, LANES), I32))
        for sh in (64, 32, 16):
            acc = acc + pltpu.roll(acc, sh, 1)
        return acc

    @pl.when(k == 0)
    def _():
        def bit_body(i, thr):
            cand = thr | jnp.left_shift(jnp.int32(1), 30 - i)
            cnt = count(lambda kb: kb >= cand)
            return jnp.where(cnt >= cap, cand, thr)

        thr = lax.fori_loop(0, 31, bit_body, jnp.zeros((1, LANES), I32))
        thr_s[...] = thr[:, :N_EXPERTS]
        need_s[...] = (cap - count(lambda kb: kb > thr))[:, :N_EXPERTS].astype(F32)
        eq_s[...] = jnp.zeros((1, N_EXPERTS), F32)
        oa_s[...] = jnp.zeros((1, N_EXPERTS), F32)

    thr = thr_s[...]
    off = pl.multiple_of(k * tb, tb)
    kb = pltpu.bitcast(aff_ref[0, pl.ds(off, tb), :], I32)
    gt = kb > thr
    eq = kb == thr
    eqf = jnp.where(eq, 1.0, 0.0)
    eqrank = eq_s[...] + _dot(ls_ref[...], eqf.astype(BF16))
    sel = jnp.logical_or(gt, jnp.logical_and(eq, eqrank < need_s[...]))
    self_ = jnp.where(sel, 1.0, 0.0)
    selb = self_.astype(BF16)
    lrank = _dot(ls_ref[...], selb)
    gate_bits = pltpu.bitcast(aff_ref[0, pl.ds(off, tb), :].astype(BF16).astype(F32), I32)
    rank_ref[0] = gate_bits | jnp.where(sel, lrank.astype(I32) + 1, 0)
    lrank_t = _dot_tn(selb, us_ref[...])
    sel_t = _dot_tn(selb, eye_ref[...])
    rankt_ref[0] = jnp.where(sel_t > 0.5, lrank_t, -1.0)
    c = jnp.sum(self_, axis=0, keepdims=True)
    oa_run = oa_s[...]
    oa_ref[0, pl.ds(k, 1), :] = oa_run.astype(I32)
    nwin = jnp.max(jnp.ceil(c * (1.0 / win)), axis=1, keepdims=True)
    nw_ref[0, pl.ds(k, 1), :] = jnp.broadcast_to(nwin, (1, N_EXPERTS)).astype(I32)
    oa_new = oa_run + jnp.ceil(c * (1.0 / ROW_ALIGN)) * ROW_ALIGN
    oa_s[...] = oa_new
    eq_s[...] = eq_s[...] + jnp.sum(eqf, axis=0, keepdims=True)

    @pl.when(k == nblk - 1)
    def _():
        oa_ref[0, pl.ds(nblk, 1), :] = oa_new.astype(I32)


def route(aff, *, tb, win):
    bsz, n, ne = aff.shape
    cap = max(1, (EC_CAPACITY * n) // ne)
    nblk = n // tb
    assert ne * 8 == LANES
    aff8 = aff.reshape(bsz, n // 8, LANES)
    chunk = min(256, n // 8)
    ii = np.arange(tb)
    ls = jnp.asarray(ii[None, :] < ii[:, None], BF16)
    us = jnp.asarray(ii[:, None] < ii[None, :], BF16)
    eye = jnp.asarray(ii[:, None] == ii[None, :], BF16)
    full = lambda a: pl.BlockSpec(a.shape, lambda b, k: (0,) * a.ndim)
    return pl.pallas_call(
        functools.partial(_route_kernel, n=n, cap=cap, tb=tb, win=win, chunk=chunk),
        grid=(bsz, nblk),
        in_specs=[pl.BlockSpec((1, n, ne), lambda b, k: (b, 0, 0)),
                  pl.BlockSpec((1, n // 8, LANES), lambda b, k: (b, 0, 0)), full(ls), full(us), full(eye)],
        out_specs=[
            pl.BlockSpec((1, tb, ne), lambda b, k: (b, k, 0)),
            pl.BlockSpec((1, ne, tb), lambda b, k: (b, 0, k)),
            pl.BlockSpec((1, nblk + 1, ne), lambda b, k: (b, 0, 0)),
            pl.BlockSpec((1, nblk, ne), lambda b, k: (b, 0, 0)),
        ],
        out_shape=[
            jax.ShapeDtypeStruct((bsz, n, ne), I32),
            jax.ShapeDtypeStruct((bsz, ne, n), F32),
            jax.ShapeDtypeStruct((bsz, nblk + 1, ne), I32),
            jax.ShapeDtypeStruct((bsz, nblk, ne), I32),
        ],
        scratch_shapes=[pltpu.VMEM((1, ne), I32), pltpu.VMEM((1, ne), F32),
                        pltpu.VMEM((1, ne), F32), pltpu.VMEM((1, ne), F32)],
        compiler_params=_cparams(("arbitrary", "arbitrary")),
        name="route",
    )(aff, aff8, ls, us, eye)


def _dispatch_kernel(oa_ref, nw_ref, rankt_ref, h_ref, xg_in_ref, xg_ref, stack, sem,
                     *, nblk, n_steps, win, tb, group):
    del xg_in_ref
    b = pl.program_id(0)
    k = pl.program_id(1)
    step = b * nblk + k
    slot = step % 2
    ne = N_EXPERTS

    def fill(sl, j):
        h = h_ref[0]
        for g0 in range(0, ne, group):
            pieces = []
            for e in range(g0, g0 + group):
                r = rankt_ref[0, e:e + 1, :]
                srow = lax.broadcasted_iota(I32, (win, tb), 0).astype(F32) + (j * win).astype(F32)
                pieces.append((r == srow).astype(BF16))
            oh_t = jnp.concatenate(pieces, axis=0)
            stack[sl, g0 * win:(g0 + group) * win, :] = _dot(oh_t, h).astype(BF16)

    def copies(sl, bb, kk, j):
        out = []
        for e in range(ne):
            off = pl.multiple_of(oa_ref[(bb * (nblk + 1) + kk) * ne + e] + j * win, ROW_ALIGN)
            out.append(pltpu.make_async_copy(
                stack.at[sl, pl.ds(e * win, win), :],
                xg_ref.at[bb, e, pl.ds(off, win), :],
                sem.at[sl]))
        return out

    fill(slot, jnp.int32(0))

    @pl.when(step > 0)
    def _():
        for c in copies(1 - slot, b, k, 0):
            c.wait()

    for c in copies(slot, b, k, 0):
        c.start()

    nwx = nw_ref[(b * nblk + k) * ne]

    def extra(j, carry):
        fill(2, j)
        cs = copies(2, b, k, j)
        for c in cs:
            c.start()
        for c in cs:
            c.wait()
        return carry

    lax.fori_loop(1, jnp.maximum(nwx, 1), extra, 0)

    @pl.when(step == n_steps - 1)
    def _():
        for c in copies(slot, b, k, 0):
            c.wait()


def dispatch(oa, nw, rank_t, h2, *, tb, win, c_pad):
    bsz, n, d = h2.shape
    ne = N_EXPERTS
    nblk = n // tb
    group = 4
    xg0 = jnp.zeros((bsz, ne, c_pad, d), BF16)
    grid_spec = pltpu.PrefetchScalarGridSpec(
        num_scalar_prefetch=2,
        grid=(bsz, nblk),
        in_specs=[
            pl.BlockSpec((1, ne, tb), lambda b, k, *_: (b, 0, k)),
            pl.BlockSpec((1, tb, d), lambda b, k, *_: (b, k, 0)),
            pl.BlockSpec(memory_space=pl.ANY),
        ],
        out_specs=pl.BlockSpec(memory_space=pl.ANY),
        scratch_shapes=[
            pltpu.VMEM((3, ne * win, d), BF16),
            pltpu.SemaphoreType.DMA((3,)),
        ],
    )
    return pl.pallas_call(
        functools.partial(_dispatch_kernel, nblk=nblk, n_steps=bsz * nblk, win=win, tb=tb, group=group),
        grid_spec=grid_spec,
        out_shape=jax.ShapeDtypeStruct(xg0.shape, BF16),
        input_output_aliases={4: 0},
        compiler_params=_cparams(("arbitrary", "arbitrary")),
        name="dispatch",
    )(oa.reshape(-1), nw.reshape(-1), rank_t, h2, xg0)


def _ffn_kernel(tot_ref, x_ref, wg_ref, wu_ref, wd_ref, y_ref, wgb, wub, wdb, *, tm, nblk):
    e = pl.program_id(0)
    b = pl.program_id(1)
    i = pl.program_id(2)

    @pl.when(jnp.logical_and(b == 0, i == 0))
    def _():
        wgb[...] = wg_ref[0, 0].astype(BF16)
        wub[...] = wu_ref[0, 0].astype(BF16)
        wdb[...] = wd_ref[0, 0].astype(BF16)

    total = tot_ref[(b * (nblk + 1) + nblk) * N_EXPERTS + e]

    @pl.when(i * tm < total)
    def _():
        x = x_ref[0, 0]
        hid = (_silu(_dot(x, wgb[...])) * _dot(x, wub[...])).astype(BF16)
        y_ref[0, 0] = _dot(hid, wdb[...]).astype(BF16)

    @pl.when(i * tm >= total)
    def _():
        y_ref[0, 0] = jnp.zeros(y_ref.shape[2:], BF16)


def expert_ffn(oa, xg, w_gate, w_up, w_down, layer, *, tm, nblk):
    bsz, ne, c_pad, d = xg.shape
    f = w_gate.shape[-1]
    grid_spec = pltpu.PrefetchScalarGridSpec(
        num_scalar_prefetch=1,
        grid=(ne, bsz, c_pad // tm),
        in_specs=[
            pl.BlockSpec((1, 1, tm, d), lambda e, b, i, *_: (b, e, i, 0)),
            pl.BlockSpec((1, 1, d, f), lambda e, b, i, *_: (layer, e, 0, 0)),
            pl.BlockSpec((1, 1, d, f), lambda e, b, i, *_: (layer, e, 0, 0)),
            pl.BlockSpec((1, 1, f, d), lambda e, b, i, *_: (layer, e, 0, 0)),
        ],
        out_specs=pl.BlockSpec((1, 1, tm, d), lambda e, b, i, *_: (b, e, i, 0)),
        scratch_shapes=[pltpu.VMEM((d, f), BF16), pltpu.VMEM((d, f), BF16), pltpu.VMEM((f, d), BF16)],
    )
    return pl.pallas_call(
        functools.partial(_ffn_kernel, tm=tm, nblk=nblk),
        grid_spec=grid_spec,
        out_shape=jax.ShapeDtypeStruct(xg.shape, BF16),
        compiler_params=_cparams(("arbitrary", "arbitrary", "arbitrary")),
        name="expert_ffn",
    )(oa.reshape(-1), xg, w_gate, w_up, w_down)


def _combine_kernel(oa_ref, nw_ref, rank_ref, xs_ref, mod_ref, *rest,
                    ctx, nblk, n_steps, win, tb, final):
    if final:
        fg_ref, ye_ref, o_ref, stack, sem = rest
    else:
        ye_ref, o_ref, stack, sem = rest
    b = pl.program_id(0)
    k = pl.program_id(1)
    step = b * nblk + k
    slot = step % 2
    ne = N_EXPERTS

    def copies(sl, bb, kk, j):
        out = []
        for e in range(ne):
            off = pl.multiple_of(oa_ref[(bb * (nblk + 1) + kk) * ne + e] + j * win, ROW_ALIGN)
            out.append(pltpu.make_async_copy(
                ye_ref.at[bb, e, pl.ds(off, win), :],
                stack.at[sl, pl.ds(e * win, win), :],
                sem.at[sl]))
        return out

    @pl.when(step == 0)
    def _():
        for c in copies(0, b, k, 0):
            c.start()

    @pl.when(step + 1 < n_steps)
    def _():
        nxt = step + 1
        for c in copies(1 - slot, nxt // nblk, nxt % nblk, 0):
            c.start()

    for c in copies(slot, b, k, 0):
        c.wait()

    def weighted_sum(j, sl):
        word = rank_ref[0]
        slot1 = lax.broadcasted_iota(I32, (tb, win), 1) + (j * win + 1)
        y = None
        for e0 in range(0, ne, 2):
            pieces = []
            for e in (e0, e0 + 1):
                wb = jnp.broadcast_to(word[:, e:e + 1], (tb, win))
                gate = pltpu.bitcast(wb & jnp.int32(-65536), F32)
                pieces.append(jnp.where((wb & 0xFFFF) == slot1, gate, 0.0).astype(BF16))
            part = _dot(jnp.concatenate(pieces, axis=1), stack[sl, e0 * win:(e0 + 2) * win, :])
            y = part if y is None else y + part
        return y

    y = weighted_sum(jnp.int32(0), slot)

    nwx = nw_ref[(b * nblk + k) * ne]

    def extra(j, y):
        cs = copies(2, b, k, j)
        for c in cs:
            c.start()
        for c in cs:
            c.wait()
        return y + weighted_sum(j, 2)

    y = lax.fori_loop(1, jnp.maximum(nwx, 1), extra, y)
    out = xs_ref[0] + _mod_row(mod_ref, ctx, 5) * y
    if final:
        out = out * lax.rsqrt(jnp.mean(out * out, axis=-1, keepdims=True) + EPS) * fg_ref[...]
    o_ref[0] = out


def combine(oa, nw, rank, xs, mod_l, ye, final_g=None, *, ctx, tb, win):
    bsz, n, d = xs.shape
    ne = N_EXPERTS
    nblk = n // tb
    final = final_g is not None
    full = lambda a: pl.BlockSpec(a.shape, lambda b, k, *_: (0,) * a.ndim)
    ins = [rank, xs, mod_l]
    in_specs = [
        pl.BlockSpec((1, tb, ne), lambda b, k, *_: (b, k, 0)),
        pl.BlockSpec((1, tb, d), lambda b, k, *_: (b, k, 0)),
        full(mod_l),
    ]
    if final:
        ins.append(final_g)
        in_specs.append(full(final_g))
    ins.append(ye)
    in_specs.append(pl.BlockSpec(memory_space=pl.ANY))
    grid_spec = pltpu.PrefetchScalarGridSpec(
        num_scalar_prefetch=2,
        grid=(bsz, nblk),
        in_specs=in_specs,
        out_specs=pl.BlockSpec((1, tb, d), lambda b, k, *_: (b, k, 0)),
        scratch_shapes=[
            pltpu.VMEM((3, ne * win, d), BF16),
            pltpu.SemaphoreType.DMA((3,)),
        ],
    )
    return pl.pallas_call(
        functools.partial(_combine_kernel, ctx=ctx, nblk=nblk, n_steps=bsz * nblk, win=win, tb=tb, final=final),
        grid_spec=grid_spec,
        out_shape=jax.ShapeDtypeStruct(xs.shape, F32),
        compiler_params=_cparams(("arbitrary", "arbitrary")),
        name="combine_final" if final else ("combine_ctx" if ctx else "combine"),
    )(oa.reshape(-1), nw.reshape(-1), *ins)


def moe(xs1, h2, aff, mod_l, w_gate, w_up, w_down, layer, final_g=None, *, ctx):
    bsz, n, d = xs1.shape
    cap = max(1, (EC_CAPACITY * n) // N_EXPERTS)
    tb = min(512, n)
    win = LANES
    nblk = n // tb
    tm = 384 if n >= 4096 else 128
    need = cap + ROW_ALIGN * nblk + win
    c_pad = -(-need // tm) * tm
    rank, rank_t, oa, nw = route(aff, tb=tb, win=win)
    xg = dispatch(oa, nw, rank_t, h2, tb=tb, win=win, c_pad=c_pad)
    ye = expert_ffn(oa, xg, w_gate, w_up, w_down, layer, tm=tm, nblk=nblk)
    return combine(oa, nw, rank, xs1, mod_l, ye, final_g, ctx=ctx, tb=tb, win=win)


def _swa_in_kernel(x_ref, mod_ref, g_ref, w_ref, cos_ref, sin_ref, q_ref, k_ref, v_ref, *, ctx, scale):
    h = _rms_mod(x_ref[0], g_ref[...], _mod_row(mod_ref, ctx, 0), _mod_row(mod_ref, ctx, 1))
    p = _dot(h.astype(BF16), w_ref[...])
    cos = cos_ref[...]
    sin = sin_ref[...]
    cs = cos * scale
    ss = sin * scale
    for j in range(8):
        a = p[:, j * 128:(j + 1) * 128]
        bsw = p[:, 1024 + j * 128:1024 + (j + 1) * 128]
        q_ref[0, :, j * 128:(j + 1) * 128] = (a * cs + bsw * ss).astype(BF16)
    lane = lax.broadcasted_iota(I32, cos.shape, 1)
    low = lane < 64
    for pr in range(2):
        kp = p[:, 2048 + pr * 128:2048 + (pr + 1) * 128] * cos + p[:, 2304 + pr * 128:2304 + (pr + 1) * 128] * sin
        vp = p[:, 2560 + pr * 128:2560 + (pr + 1) * 128]
        for src, dst in ((kp, k_ref), (vp, v_ref)):
            ev_lo = jnp.where(low, src, 0.0)
            od_hi = jnp.where(low, 0.0, src)
            ev_hi = pltpu.roll(ev_lo, 64, 1)
            od_lo = pltpu.roll(od_hi, 64, 1)
            base = pr * 512
            dst[0, :, base:base + 128] = ev_lo.astype(BF16)
            dst[0, :, base + 128:base + 256] = ev_hi.astype(BF16)
            dst[0, :, base + 256:base + 384] = od_lo.astype(BF16)
            dst[0, :, base + 384:base + 512] = od_hi.astype(BF16)


def swa_in(x, mod_l, g, w2, cos_t, sin_t, *, ctx, tm):
    bsz, n, d = x.shape
    tm = min(tm, n)
    full = lambda a: pl.BlockSpec(a.shape, lambda b, i: (0,) * a.ndim)
    row = lambda w: pl.BlockSpec((1, tm, w), lambda b, i: (b, i, 0))
    return pl.pallas_call(
        functools.partial(_swa_in_kernel, ctx=ctx, scale=SWA_HEAD_DIM ** -0.5 * math.log2(math.e)),
        grid=(bsz, n // tm),
        in_specs=[row(d), full(mod_l), full(g), full(w2),
                  pl.BlockSpec((tm, 128), lambda b, i: (i, 0)),
                  pl.BlockSpec((tm, 128), lambda b, i: (i, 0))],
        out_specs=[row(1024), row(1024), row(1024)],
        out_shape=[jax.ShapeDtypeStruct((bsz, n, 1024), BF16)] * 3,
        compiler_params=_cparams(("arbitrary", "arbitrary")),
        name="swa_in_ctx" if ctx else "swa_in",
    )(x, mod_l, g, w2, cos_t, sin_t)


def _swa_kernel(sink_ref, q_ref, km_ref, kp_ref, kn_ref, vm_ref, vp_ref, vn_ref, kc_ref, vc_ref,
                o_ref, kwin, vwin, *, tq, n_tiles):
    i = pl.program_id(1)
    blk = SWA_BLOCK
    kwin[0:blk] = kp_ref[0]
    kwin[blk:blk + tq] = km_ref[0]
    kwin[blk + tq:blk + tq + blk] = kn_ref[0]
    vwin[0:blk] = vp_ref[0]
    vwin[blk:blk + tq] = vm_ref[0]
    vwin[blk + tq:blk + tq + blk] = vn_ref[0]
    r = lax.broadcasted_iota(I32, (blk, 3 * blk), 0)
    s = lax.broadcasted_iota(I32, (blk, 3 * blk), 1)
    band = jnp.abs(r - (s - blk)) <= WINDOW
    low_lanes = lax.broadcasted_iota(I32, (blk, 128), 1) < 64

    def qblock(qb, carry):
        row0 = pl.multiple_of(qb * blk, blk)
        first = jnp.logical_and(i == 0, qb == 0)
        last = jnp.logical_and(i == n_tiles - 1, qb == tq // blk - 1)
        valid = jnp.logical_and(band, jnp.logical_and(jnp.logical_or(s >= blk, jnp.logical_not(first)),
                                                      jnp.logical_or(s < 2 * blk, jnp.logical_not(last))))
        bias = jnp.where(valid, 0.0, -1e30)
        for kvh in range(SWA_KV_HEADS):
            c0 = kvh * 256
            q4 = jnp.concatenate([q_ref[0, pl.ds(row0, blk), c0:c0 + 128],
                                  q_ref[0, pl.ds(row0, blk), c0 + 128:c0 + 256]], axis=0)
            kl = jnp.concatenate([kwin[pl.ds(row0, 3 * blk), c0:c0 + 128],
                                  kwin[pl.ds(row0, 3 * blk), c0 + 128:c0 + 256]], axis=0)
            vl = jnp.concatenate([vwin[pl.ds(row0, 3 * blk), c0:c0 + 128],
                                  vwin[pl.ds(row0, 3 * blk), c0 + 128:c0 + 256]], axis=0)
            s_loc = _dot_nt(q4, kl)
            s_ctx = _dot_nt(q4, kc_ref[0, kvh])
            lc = s_ctx.shape[1] // 2
            p_loc, p_ctx, inv = [], [], []
            for pp in range(2):
                pl_row, pc_row, inv_row = [], [], []
                for hf in range(2):
                    sk = sink_ref[kvh * 4 + pp * 2 + hf]
                    sl = s_loc[pp * blk:(pp + 1) * blk, hf * 3 * blk:(hf + 1) * 3 * blk] + bias
                    sc = s_ctx[pp * blk:(pp + 1) * blk, hf * lc:(hf + 1) * lc]
                    m = jnp.maximum(jnp.maximum(jnp.max(sl, axis=-1, keepdims=True),
                                                jnp.max(sc, axis=-1, keepdims=True)), sk)
                    el = jnp.exp2(sl - m)
                    ec = jnp.exp2(sc - m)
                    den = (jnp.sum(el, axis=-1, keepdims=True) + jnp.sum(ec, axis=-1, keepdims=True)
                           + jnp.exp2(sk - m))
                    pl_row.append(el.astype(BF16))
                    pc_row.append(ec.astype(BF16))
                    inv_row.append(1.0 / den)
                p_loc.append(jnp.concatenate(pl_row, axis=1))
                p_ctx.append(jnp.concatenate(pc_row, axis=1))
                inv.append(jnp.where(low_lanes, inv_row[0], inv_row[1]))
            o4 = _dot(jnp.concatenate(p_loc, axis=0), vl) + _dot(jnp.concatenate(p_ctx, axis=0), vc_ref[0, kvh])
            for pp in range(2):
                pair = kvh * 2 + pp
                o_ref[0, pl.ds(row0, blk), pair * 128:(pair + 1) * 128] = (
                    o4[pp * blk:(pp + 1) * blk] * inv[pp]).astype(BF16)
        return carry

    lax.fori_loop(0, tq // blk, qblock, 0)


def swa_attention(q, k2, v2, kc2, vc2, sink, *, tq):
    bsz, n, _ = q.shape
    lc = kc2.shape[1]
    tq = min(tq, n)
    n_tiles = n // tq
    rb = tq // SWA_BLOCK
    nb = n // SWA_BLOCK
    main = pl.BlockSpec((1, tq, 1024), lambda b, i, *_: (b, i, 0))
    prev = pl.BlockSpec((1, SWA_BLOCK, 1024), lambda b, i, *_: (b, jnp.maximum(i * rb - 1, 0), 0))
    nxt = pl.BlockSpec((1, SWA_BLOCK, 1024), lambda b, i, *_: (b, jnp.minimum((i + 1) * rb, nb - 1), 0))
    stack = lambda a: a.reshape(bsz, lc, SWA_KV_HEADS, 2, 128).transpose(0, 2, 3, 1, 4).reshape(
        bsz, SWA_KV_HEADS, 2 * lc, 128)
    kc2, vc2 = stack(kc2), stack(vc2)
    sink = sink * math.log2(math.e)
    cspec = pl.BlockSpec((1, SWA_KV_HEADS, 2 * lc, 128), lambda b, i, *_: (b, 0, 0, 0))
    grid_spec = pltpu.PrefetchScalarGridSpec(
        num_scalar_prefetch=1,
        grid=(bsz, n_tiles),
        in_specs=[main, main, prev, nxt, main, prev, nxt, cspec, cspec],
        out_specs=main,
        scratch_shapes=[pltpu.VMEM((tq + 2 * SWA_BLOCK, 1024), BF16)] * 2,
    )
    return pl.pallas_call(
        functools.partial(_swa_kernel, tq=tq, n_tiles=n_tiles),
        grid_spec=grid_spec,
        out_shape=jax.ShapeDtypeStruct((bsz, n, 1024), BF16),
        compiler_params=_cparams(("arbitrary", "arbitrary")),
        name="swa_attention",
    )(sink, q, k2, k2, k2, v2, v2, v2, kc2, vc2)


def _prep_ab(w_in, w_uq, w_ukv, w_out):
    d = w_in.shape[0]
    perm = _rope_perm(MLA_ROPE)
    o = 3 * CONV_DIM
    kr = w_in[:, o + MLA_Q_LORA + MLA_KV_LORA:]
    z64 = jnp.zeros((d, 64), F32)
    z32 = jnp.zeros((d, 32), F32)
    win2 = jnp.concatenate([w_in[:, :o + MLA_Q_LORA + MLA_KV_LORA], z64, kr, z32, z64, kr[:, perm], z32],
                           axis=1).astype(BF16)
    dq = MLA_NOPE + MLA_ROPE
    wq = w_uq.reshape(MLA_Q_LORA, MLA_HEADS, dq)
    zq = jnp.zeros((MLA_Q_LORA, MLA_HEADS, 32), F32)
    zq64 = jnp.zeros((MLA_Q_LORA, MLA_HEADS, 64), F32)
    qa = jnp.concatenate([wq, zq], axis=-1).reshape(MLA_Q_LORA, MLA_HEADS * 128)
    qb = jnp.concatenate([zq64, wq[:, :, MLA_NOPE:][:, :, perm], zq], axis=-1).reshape(MLA_Q_LORA, MLA_HEADS * 128)
    wq2 = jnp.concatenate([qa, qb], axis=1).astype(BF16)
    wkv = w_ukv.reshape(MLA_KV_LORA, MLA_HEADS, MLA_NOPE + MLA_V)
    zk = jnp.zeros((MLA_KV_LORA, MLA_HEADS, 64), F32)
    ka = jnp.concatenate([wkv[:, :, :MLA_NOPE], zk], axis=-1).reshape(MLA_KV_LORA, MLA_HEADS * 128)
    va = jnp.concatenate([wkv[:, :, MLA_NOPE:], zk], axis=-1).reshape(MLA_KV_LORA, MLA_HEADS * 128)
    wkv2 = jnp.concatenate([ka, va], axis=1).astype(BF16)
    wc = w_out[:CONV_DIM].astype(BF16)
    wa = w_out[CONV_DIM:].reshape(MLA_HEADS, MLA_V, d)
    wa = jnp.concatenate([wa, jnp.zeros_like(wa)], axis=1).reshape(MLA_HEADS * 128, d).astype(BF16)
    return win2, wq2, wkv2, wc, wa


def _prep_swa(w_qkv):
    hd = SWA_HEAD_DIM
    perm = _rope_perm(hd)
    nq = SWA_HEADS * hd
    nk = SWA_KV_HEADS * hd
    wq = w_qkv[:, :nq]
    wk = w_qkv[:, nq:nq + nk]
    wv = w_qkv[:, nq + nk:]
    d = w_qkv.shape[0]
    wq_sw = wq.reshape(d, SWA_HEADS, hd)[:, :, perm].reshape(d, nq)
    wk_sw = wk.reshape(d, SWA_KV_HEADS, hd)[:, :, perm].reshape(d, nk)
    return jnp.concatenate([wq, wq_sw, wk, wk_sw, wv], axis=1).astype(BF16)


def _pad_lanes(t, left, width=LANES, fill=0.0):
    n, w = t.shape
    return jnp.concatenate([jnp.full((n, left), fill, F32), t, jnp.zeros((n, width - left - w), F32)], axis=1)


def kernel(x, c, ctx, c_ctx, mod_w, mod_b, norm1_g, norm2_g, ab_w_in, conv_w, conv_b, mla_q_norm_g, mla_w_uq,
           mla_kv_norm_g, mla_w_ukv, ab_w_out, swa_w_qkv, swa_sink, swa_w_out, router_w, exp_w_gate, exp_w_up,
           exp_w_down, final_g):
    bsz, n, d = x.shape
    lc = ctx.shape[1]
    depth = mod_w.shape[0]
    assert bsz <= 2 and d == 1024

    cs = jnp.concatenate([c, c_ctx[None, :], jnp.zeros((8 - bsz - 1, d), F32)], axis=0)
    mod = modulation(cs, mod_w, mod_b)

    cos_m, sin_m = _rope_tables(n, MLA_ROPE)
    cos_mla = _pad_lanes(cos_m, 64, fill=1.0)
    sin_mla = _pad_lanes(sin_m, 64)
    cos_mla_c = jnp.concatenate([jnp.ones((lc, 96), F32), jnp.zeros((lc, 32), F32)], axis=1)
    zeros_c = jnp.zeros((lc, 128), F32)
    cos_s, sin_s = _rope_tables(n, SWA_HEAD_DIM)
    cos_swa = jnp.concatenate([cos_s, cos_s], axis=1)
    sin_swa = jnp.concatenate([sin_s, sin_s], axis=1)
    ones_c = jnp.ones((lc, 128), F32)

    row2 = lambda v: v.reshape(1, -1)
    xs, xc = x, ctx
    for layer in range(depth):
        need_ctx = layer < depth - 1
        last = layer == depth - 1
        mod_l = mod[layer]
        g1 = row2(norm1_g[layer])
        g2 = row2(norm2_g[layer])
        rw_f = jnp.concatenate([router_w[layer], jnp.zeros((d, LANES - N_EXPERTS), F32)], axis=1)
        rw_hi = rw_f.astype(BF16)
        rw = jnp.concatenate([rw_hi, (rw_f - rw_hi.astype(F32)).astype(BF16)], axis=1)
        wg, wu, wd = exp_w_gate, exp_w_up, exp_w_down
        if layer % 2 == 0:
            e = layer // 2
            win2, wq2, wkv2, wc, wa = _prep_ab(ab_w_in[e], mla_w_uq[e], mla_w_ukv[e], ab_w_out[e])
            qg, kvg = row2(mla_q_norm_g[e]), row2(mla_kv_norm_g[e])
            cw, cb = conv_w[e], row2(conv_b[e])
            gb, z, q, k, v = ab_in(xs, mod_l, g1, win2, qg, wq2, kvg, wkv2, cos_mla, sin_mla, ctx=False, tm=512)
            gbc, zc, qc, kc, vc = ab_in(xc, mod_l, g1, win2, qg, wq2, kvg, wkv2, cos_mla_c, zeros_c, ctx=True, tm=256)
            att = mla_attention_flat(q, k, v, kc, vc, tq=512, tk_max=1280, spt=4)
            xs1, h2, aff = mix_out(att, wa, xs, mod_l, g2, rw, (gb, z, cw, cb, wc), ctx=False, tm=512)
            if need_ctx:
                att_c = mla_attention_ctx(qc, kc, vc)
                xc1, hc2, affc = mix_out(att_c, wa, xc, mod_l, g2, rw, (gbc, zc, cw, cb, wc), ctx=True, tm=256)
        else:
            o = layer // 2
            w2 = _prep_swa(swa_w_qkv[o])
            wo = swa_w_out[o].astype(BF16)
            q, k2, v2 = swa_in(xs, mod_l, g1, w2, cos_swa, sin_swa, ctx=False, tm=512)
            qc, kc2, vc2 = swa_in(xc, mod_l, g1, w2, ones_c, zeros_c, ctx=True, tm=256)
            att = swa_attention(q, k2, v2, kc2, vc2, swa_sink[o], tq=512)
            xs1, h2, aff = mix_out(att, wo, xs, mod_l, g2, rw, ctx=False, tm=512)
            if need_ctx:
                raise NotImplementedError("context self-attention for windowed layers below the last")
        xs = moe(xs1, h2, aff, mod_l, wg, wu, wd, layer, final_g=row2(final_g) if last else None, ctx=False)
        if need_ctx:
            xc = moe(xc1, hc2, affc, mod_l, wg, wu, wd, layer, ctx=True)
    return xs
```

```python
import functools
import math

import jax
import jax.numpy as jnp
import numpy as np
from jax import lax
from jax.experimental import pallas as pl
from jax.experimental.pallas import tpu as pltpu

F32 = jnp.float32
BF16 = jnp.bfloat16
I32 = jnp.int32

GRID_W = 64
EPS = 1e-6
ROPE_THETA = 10000.0
CONV_DIM = 512
MLA_HEADS = 8
MLA_Q_LORA = 256
MLA_KV_LORA = 128
MLA_NOPE = 64
MLA_ROPE = 32
MLA_V = 64
SWA_HEADS = 16
SWA_KV_HEADS = 4
SWA_HEAD_DIM = 64
WINDOW = 128
SWA_BLOCK = 128
N_EXPERTS = 16
EC_CAPACITY = 2
N_MOD = 6

LANES = 128
ROW_ALIGN = 16
VMEM_LIMIT = 56 * 1024 * 1024


def _cparams(sem, vmem=VMEM_LIMIT):
    return pltpu.CompilerParams(dimension_semantics=sem, vmem_limit_bytes=vmem)


def _dot(a, b):
    return jnp.dot(a, b, preferred_element_type=F32)


def _dot_nt(a, b):
    return lax.dot_general(a, b, (((1,), (1,)), ((), ())), preferred_element_type=F32)


def _dot_tn(a, b):
    return lax.dot_general(a, b, (((0,), (0,)), ((), ())), preferred_element_type=F32)


def _dot_hi(a, b):
    return jnp.dot(a, b, preferred_element_type=F32, precision=lax.Precision.HIGHEST)


def _silu(x):
    return x * (1.0 / (1.0 + jnp.exp(-x)))


def _rms_mod(x, g, shift, scale):
    y = x * lax.rsqrt(jnp.mean(x * x, axis=-1, keepdims=True) + EPS)
    return (y * g) * (1.0 + scale) + shift


def _mod_row(mod_ref, ctx, which):
    r = 2 if ctx else pl.program_id(0)
    return mod_ref[pl.ds(r, 1), which * 1024:(which + 1) * 1024]


def _mod_kernel(cs_ref, w_ref, b_ref, o_ref):
    s = _silu(cs_ref[...])
    o_ref[0] = _dot_hi(s, w_ref[0]) + b_ref[0]


def modulation(cs, mod_w, mod_b):
    depth, d, n6 = mod_w.shape
    tn = 1536
    return pl.pallas_call(
        _mod_kernel,
        grid=(depth, n6 // tn),
        in_specs=[
            pl.BlockSpec((8, d), lambda l, j: (0, 0)),
            pl.BlockSpec((1, d, tn), lambda l, j: (l, 0, j)),
            pl.BlockSpec((1, 1, tn), lambda l, j: (l, 0, j)),
        ],
        out_specs=pl.BlockSpec((1, 8, tn), lambda l, j: (l, 0, j)),
        out_shape=jax.ShapeDtypeStruct((depth, 8, n6), F32),
        compiler_params=_cparams(("arbitrary", "arbitrary")),
        name="modulation",
    )(cs, mod_w, mod_b.reshape(depth, 1, n6))


def _rope_tables(n, dim):
    n_rows = n // GRID_W
    row = jnp.repeat(jnp.arange(n_rows, dtype=F32), GRID_W)
    col = jnp.tile(jnp.arange(GRID_W, dtype=F32), n_rows)
    half = dim // 2
    qd = dim // 4
    freqs = ROPE_THETA ** (-jnp.arange(0, half, 2, dtype=F32) / half)
    ang = jnp.concatenate([row[:, None] * freqs, col[:, None] * freqs], axis=-1)
    cos, sin = jnp.cos(ang), jnp.sin(ang)
    cos_full = jnp.concatenate([cos[:, :qd], cos[:, :qd], cos[:, qd:], cos[:, qd:]], axis=-1)
    sin_sgn = jnp.concatenate([-sin[:, :qd], sin[:, :qd], -sin[:, qd:], sin[:, qd:]], axis=-1)
    return cos_full, sin_sgn


def _rope_perm(dim):
    qd = dim // 4
    ch = np.arange(dim)
    pair = (ch // qd) % 2
    return np.where(pair == 0, ch + qd, ch - qd)


def _ab_in_kernel(x_ref, mod_ref, g_ref, win_ref, qg_ref, wq_ref, kvg_ref, wkv_ref, cos_ref, sin_ref,
                  gb_ref, z_ref, q_ref, k_ref, v_ref, *, ctx, scale):
    x = x_ref[0]
    h = _rms_mod(x, g_ref[...], _mod_row(mod_ref, ctx, 0), _mod_row(mod_ref, ctx, 1))
    p = _dot(h.astype(BF16), win_ref[...])
    gb_ref[0] = p[:, 0:512].astype(BF16)
    z_ref[0] = (p[:, 512:1024] * p[:, 1024:1536]).astype(BF16)
    ql = p[:, 1536:1792]
    kvl = p[:, 1792:1920]
    kra = p[:, 1920:2048]
    krb = p[:, 2048:2176]
    cos = cos_ref[...]
    sin = sin_ref[...]
    qn = ql * lax.rsqrt(jnp.mean(ql * ql, axis=-1, keepdims=True) + EPS) * qg_ref[...]
    qq = _dot(qn.astype(BF16), wq_ref[...])
    kvn = kvl * lax.rsqrt(jnp.mean(kvl * kvl, axis=-1, keepdims=True) + EPS) * kvg_ref[...]
    kv = _dot(kvn.astype(BF16), wkv_ref[...])
    krope = kra * cos + krb * sin
    cs = cos * scale
    ss = sin * scale
    ones_lane = lax.broadcasted_iota(I32, cos.shape, 1) == MLA_V
    for hd in range(MLA_HEADS):
        a = qq[:, hd * 128:(hd + 1) * 128]
        b = qq[:, 1024 + hd * 128:1024 + (hd + 1) * 128]
        q_ref[0, hd] = (a * cs + b * ss).astype(BF16)
        k_ref[0, hd] = (kv[:, hd * 128:(hd + 1) * 128] + krope).astype(BF16)
        v_ref[0, hd] = jnp.where(ones_lane, 1.0, kv[:, 1024 + hd * 128:1024 + (hd + 1) * 128]).astype(BF16)


def ab_in(x, mod_l, g, win2, qg, wq2, kvg, wkv2, cos_t, sin_t, *, ctx, tm):
    bsz, n, d = x.shape
    tm = min(tm, n)
    scale = (MLA_NOPE + MLA_ROPE) ** -0.5 * math.log2(math.e)
    full = lambda a: pl.BlockSpec(a.shape, lambda b, i: (0,) * a.ndim)
    hshape = jax.ShapeDtypeStruct((bsz, MLA_HEADS, n, 128), BF16)
    hspec = pl.BlockSpec((1, MLA_HEADS, tm, 128), lambda b, i: (b, 0, i, 0))
    return pl.pallas_call(
        functools.partial(_ab_in_kernel, ctx=ctx, scale=scale),
        grid=(bsz, n // tm),
        in_specs=[
            pl.BlockSpec((1, tm, d), lambda b, i: (b, i, 0)),
            full(mod_l), full(g), full(win2), full(qg), full(wq2), full(kvg), full(wkv2),
            pl.BlockSpec((tm, 128), lambda b, i: (i, 0)),
            pl.BlockSpec((tm, 128), lambda b, i: (i, 0)),
        ],
        out_specs=[
            pl.BlockSpec((1, tm, 512), lambda b, i: (b, i, 0)),
            pl.BlockSpec((1, tm, 512), lambda b, i: (b, i, 0)),
            hspec, hspec, hspec,
        ],
        out_shape=[
            jax.ShapeDtypeStruct((bsz, n, 512), BF16),
            jax.ShapeDtypeStruct((bsz, n, 512), BF16),
            hshape, hshape, hshape,
        ],
        compiler_params=_cparams(("arbitrary", "arbitrary")),
        name="ab_in_ctx" if ctx else "ab_in",
    )(x, mod_l, g, win2, qg, wq2, kvg, wkv2, cos_t, sin_t)


def _mla_ctx_kernel(q_ref, kc_ref, vc_ref, o_ref):
    s = _dot_nt(q_ref[0, 0], kc_ref[0, 0])
    m = jnp.max(s, axis=-1, keepdims=True)
    acc = _dot(jnp.exp2(s - m).astype(BF16), vc_ref[0, 0])
    o_ref[0] = (acc * (1.0 / acc[:, MLA_V:MLA_V + 1])).astype(BF16)


def mla_attention_ctx(q, kc, vc):
    bsz, nh, lc, dk = q.shape
    kv = pl.BlockSpec((1, 1, lc, dk), lambda b, h: (b, h, 0, 0))
    return pl.pallas_call(
        _mla_ctx_kernel,
        grid=(bsz, nh),
        in_specs=[kv, kv, kv],
        out_specs=pl.BlockSpec((1, lc, dk), lambda b, h: (b, 0, h)),
        out_shape=jax.ShapeDtypeStruct((bsz, lc, nh * dk), BF16),
        compiler_params=_cparams(("arbitrary", "arbitrary")),
        name="mla_attention_ctx",
    )(q, kc, vc)


def _mla_flat_kernel(q_ref, k_ref, v_ref, kc_ref, vc_ref, o_ref, kall, vall, s_a, s_b, p_a, p_b,
                     *, tq, tk, n_tiles, n_chunks, spt):
    n_stages = n_tiles * n_chunks
    n_lat, n_ctx = k_ref.shape[2], kc_ref.shape[2]
    cp = math.gcd(n_lat, 2048)

    def copy_rows(c, carry):
        sl = pl.ds(pl.multiple_of(c * cp, cp), cp)
        kall[sl, :] = k_ref[0, 0, sl, :]
        vall[sl, :] = v_ref[0, 0, sl, :]
        return carry

    lax.fori_loop(0, n_lat // cp, copy_rows, 0)
    kall[n_lat:n_lat + n_ctx, :] = kc_ref[0, 0]
    vall[n_lat:n_lat + n_ctx, :] = vc_ref[0, 0]

    def nxt(c):
        i, j = c
        wrap = j + 1 == n_chunks
        return jnp.where(wrap, i + 1, i), jnp.where(wrap, 0, j + 1)

    def rows(idx, size):
        return pl.ds(pl.multiple_of(idx * size, size), size)

    def scores(c, s_buf):
        s = _dot_nt(q_ref[0, 0, rows(c[0], tq), :], kall[rows(c[1], tk), :])
        s_buf[...] = s
        return jnp.max(s, axis=-1, keepdims=True)

    def weights(c, s_buf, p_buf, mx, m):
        m_prev = jnp.where(c[1] == 0, -1e30, m)
        m_new = jnp.maximum(m_prev, mx)
        p_buf[...] = jnp.exp2(s_buf[...] - m_new).astype(BF16)
        return m_new, jnp.exp2(m_prev - m_new)

    def wsum(c, p_buf, alpha, acc):
        acc = alpha * acc + _dot(p_buf[...], vall[rows(c[1], tk), :])
        o_ref[0, pl.ds(pl.multiple_of(c[0] * tq, tq), tq), :] = (
            acc * (1.0 / acc[:, MLA_V:MLA_V + 1])).astype(BF16)
        return acc

    def trip(c_pv, c_w, c_s, s_cur, s_nxt, p_prev, p_cur, mx_cur, alpha_prev, m, acc):
        mx_nxt = scores(c_s, s_nxt) if c_s is not None else None
        acc = wsum(c_pv, p_prev, alpha_prev, acc)
        m, alpha_cur = weights(c_w, s_cur, p_cur, mx_cur, m)
        return mx_nxt, alpha_cur, m, acc

    def pair(c, mx_b, alpha_a, m, acc):
        mx_a, alpha_b, m, acc = trip(c[0], c[1], c[2], s_b, s_a, p_a, p_b, mx_b, alpha_a, m, acc)
        mx_b, alpha_a, m, acc = trip(c[1], c[2], c[3], s_a, s_b, p_b, p_a, mx_a, alpha_b, m, acc)
        return mx_b, alpha_a, m, acc

    def chain(c, k):
        out = [c]
        for _ in range(k):
            out.append(nxt(out[-1]))
        return out

    zero = jnp.int32(0)
    c0 = (zero, zero)
    mx_a = scores(c0, s_a)
    m, alpha_a = weights(c0, s_a, p_a, mx_a, jnp.zeros((tq, 1), F32))
    mx_b = scores(nxt(c0), s_b)
    acc = jnp.zeros((tq, LANES), F32)

    def run(n_st, carry):
        i, j, mx_b, alpha_a, m, acc = carry
        c = chain((i, j), n_st + 1)
        for u in range(0, n_st, 2):
            mx_b, alpha_a, m, acc = pair(c[u:u + 4], mx_b, alpha_a, m, acc)
        return c[n_st][0], c[n_st][1], mx_b, alpha_a, m, acc

    carry = run((n_stages - 4) % spt, (zero, zero, mx_b, alpha_a, m, acc))
    i, j, mx_b, alpha_a, m, acc = lax.fori_loop(0, (n_stages - 4) // spt, lambda _, c: run(spt, c), carry)
    c = chain((i, j), 3)
    mx_b, alpha_a, m, acc = pair(c[0:4], mx_b, alpha_a, m, acc)
    mx_a, alpha_b, m, acc = trip(c[2], c[3], None, s_b, s_a, p_a, p_b, mx_b, alpha_a, m, acc)
    wsum(c[3], p_b, alpha_b, acc)


def mla_attention_flat(q, k, v, kc, vc, *, tq, tk_max, spt):
    bsz, nh, n, dk = q.shape
    lc = kc.shape[2]
    nk = n + lc
    tq = min(tq, n)
    tk = max(t for t in range(256, tk_max + 1, 256) if nk % t == 0)
    n_tiles, n_chunks = n // tq, nk // tk
    assert spt % 2 == 0 and (n_tiles * n_chunks) % 2 == 0 and n_tiles * n_chunks >= 4
    head = lambda rows_: pl.BlockSpec((1, 1, rows_, dk), lambda b, h: (b, h, 0, 0))
    return pl.pallas_call(
        functools.partial(_mla_flat_kernel, tq=tq, tk=tk, n_tiles=n_tiles, n_chunks=n_chunks, spt=spt),
        grid=(bsz, nh),
        in_specs=[head(n), head(n), head(n), head(lc), head(lc)],
        out_specs=pl.BlockSpec((1, n, dk), lambda b, h: (b, 0, h)),
        out_shape=jax.ShapeDtypeStruct((bsz, n, nh * dk), BF16),
        scratch_shapes=([pltpu.VMEM((nk, dk), BF16)] * 2 + [pltpu.VMEM((tq, tk), F32)] * 2
                        + [pltpu.VMEM((tq, tk), BF16)] * 2),
        compiler_params=_cparams(("arbitrary", "arbitrary")),
        name="mla_attention",
    )(q, k, v, kc, vc)


def _mix_out_kernel(*refs, ctx, has_conv, tm, n_tiles):
    if has_conv:
        (gb_ref, z_ref, zp_ref, zn_ref, cw_ref, cb_ref, wc_ref,
         att_ref, wa_ref, x_ref, mod_ref, g2_ref, rw_ref, xs_ref, h2_ref, aff_ref) = refs
    else:
        att_ref, wa_ref, x_ref, mod_ref, g2_ref, rw_ref, xs_ref, h2_ref, aff_ref = refs
    y = _dot(att_ref[0], wa_ref[...])
    if has_conv:
        i = pl.program_id(1)
        z = z_ref[0].astype(F32)
        rows = lax.broadcasted_iota(I32, z.shape, 0)
        zprev_halo = jnp.where(i > 0, zp_ref[0, ROW_ALIGN - 1:ROW_ALIGN, :].astype(F32), 0.0)
        znext_halo = jnp.where(i < n_tiles - 1, zn_ref[0, 0:1, :].astype(F32), 0.0)
        zprev = jnp.where(rows == 0, zprev_halo, pltpu.roll(z, 1, 0))
        znext = jnp.where(rows == tm - 1, znext_halo, pltpu.roll(z, tm - 1, 0))
        cw = cw_ref[...]
        conv = gb_ref[0].astype(F32) * (zprev * cw[0:1] + z * cw[1:2] + znext * cw[2:3] + cb_ref[...])
        y = y + _dot(conv.astype(BF16), wc_ref[...])
    xs = x_ref[0] + _mod_row(mod_ref, ctx, 2) * y
    xs_ref[0] = xs
    h2 = _rms_mod(xs, g2_ref[...], _mod_row(mod_ref, ctx, 3), _mod_row(mod_ref, ctx, 4))
    h2_ref[0] = h2.astype(BF16)
    h2_hi = h2.astype(BF16)
    h2_lo = (h2 - h2_hi.astype(F32)).astype(BF16)
    hh = _dot(h2_hi, rw_ref[...])
    logits = hh[:, :LANES] + hh[:, LANES:] + _dot(h2_lo, rw_ref[:, :LANES])
    lane = lax.broadcasted_iota(I32, logits.shape, 1)
    logits = jnp.where(lane < N_EXPERTS, logits, -1e30)
    mx = jnp.max(logits, axis=-1, keepdims=True)
    ex = jnp.exp(logits - mx)
    aff = ex / jnp.sum(ex, axis=-1, keepdims=True)
    aff_ref[0] = aff[:, :N_EXPERTS]


def mix_out(att, wa, x, mod_l, g2, rw, conv=None, *, ctx, tm):
    bsz, n, d = x.shape
    tm = min(tm, n)
    n_tiles = n // tm
    full = lambda a: pl.BlockSpec(a.shape, lambda b, i: (0,) * a.ndim)
    row = lambda w: pl.BlockSpec((1, tm, w), lambda b, i: (b, i, 0))
    ins, in_specs = [], []
    if conv is not None:
        gb, z, cw, cb, wc = conv
        r8 = tm // ROW_ALIGN
        nb8 = n // ROW_ALIGN
        ins += [gb, z, z, z, cw, cb, wc]
        in_specs += [
            row(512), row(512),
            pl.BlockSpec((1, ROW_ALIGN, 512), lambda b, i: (b, jnp.maximum(i * r8 - 1, 0), 0)),
            pl.BlockSpec((1, ROW_ALIGN, 512), lambda b, i: (b, jnp.minimum((i + 1) * r8, nb8 - 1), 0)),
            full(cw), full(cb), full(wc),
        ]
    ins += [att, wa, x, mod_l, g2, rw]
    in_specs += [row(att.shape[-1]), full(wa), row(d), full(mod_l), full(g2), full(rw)]
    return pl.pallas_call(
        functools.partial(_mix_out_kernel, ctx=ctx, has_conv=conv is not None, tm=tm, n_tiles=n_tiles),
        grid=(bsz, n_tiles),
        in_specs=in_specs,
        out_specs=[row(d), row(d), row(N_EXPERTS)],
        out_shape=[
            jax.ShapeDtypeStruct((bsz, n, d), F32),
            jax.ShapeDtypeStruct((bsz, n, d), BF16),
            jax.ShapeDtypeStruct((bsz, n, N_EXPERTS), F32),
        ],
        compiler_params=_cparams(("arbitrary", "arbitrary")),
        name=("mix_out_ctx" if ctx else "mix_out") + ("_conv" if conv is not None else ""),
    )(*ins)


def _route_kernel(aff_ref, aff8_ref, ls_ref, us_ref, eye_ref, rank_ref, rankt_ref, oa_ref, nw_ref,
                  thr_s, need_s, eq_s, oa_s, *, n, cap, tb, win, chunk):
    n_chunks = (n // 8) // chunk
    nblk = n // tb
    k = pl.program_id(1)

    def count(pred_fn):
        def body(c, acc):
            kb = pltpu.bitcast(aff8_ref[0, pl.ds(pl.multiple_of(c * chunk, chunk), chunk), :], I32)
            return acc + jnp.sum(pred_fn(kb).astype(I32), axis=0, keepdims=True)
        acc = lax.fori_loop(0, n_chunks, body, jnp.zeros((1, LANES), I32))
        for sh in (64, 32, 16):
            acc = acc + pltpu.roll(acc, sh, 1)
        return acc

    @pl.when(k == 0)
    def _():
        def bit_body(i, thr):
            cand = thr | jnp.left_shift(jnp.int32(1), 30 - i)
            cnt = count(lambda kb: kb >= cand)
            return jnp.where(cnt >= cap, cand, thr)

        thr = lax.fori_loop(0, 31, bit_body, jnp.zeros((1, LANES), I32))
        thr_s[...] = thr[:, :N_EXPERTS]
        need_s[...] = (cap - count(lambda kb: kb > thr))[:, :N_EXPERTS].astype(F32)
        eq_s[...] = jnp.zeros((1, N_EXPERTS), F32)
        oa_s[...] = jnp.zeros((1, N_EXPERTS), F32)

    thr = thr_s[...]
    off = pl.multiple_of(k * tb, tb)
    kb = pltpu.bitcast(aff_ref[0, pl.ds(off, tb), :], I32)
    gt = kb > thr
    eq = kb == thr
    eqf = jnp.where(eq, 1.0, 0.0)
    eqrank = eq_s[...] + _dot(ls_ref[...], eqf.astype(BF16))
    sel = jnp.logical_or(gt, jnp.logical_and(eq, eqrank < need_s[...]))
    self_ = jnp.where(sel, 1.0, 0.0)
    selb = self_.astype(BF16)
    lrank = _dot(ls_ref[...], selb)
    gate_bits = pltpu.bitcast(aff_ref[0, pl.ds(off, tb), :].astype(BF16).astype(F32), I32)
    rank_ref[0] = gate_bits | jnp.where(sel, lrank.astype(I32) + 1, 0)
    lrank_t = _dot_tn(selb, us_ref[...])
    sel_t = _dot_tn(selb, eye_ref[...])
    rankt_ref[0] = jnp.where(sel_t > 0.5, lrank_t, -1.0)
    c = jnp.sum(self_, axis=0, keepdims=True)
    oa_run = oa_s[...]
    oa_ref[0, pl.ds(k, 1), :] = oa_run.astype(I32)
    nw_ref[0, pl.ds(k, 1), :] = jnp.ceil(c * (1.0 / win)).astype(I32)
    oa_new = oa_run + jnp.ceil(c * (1.0 / ROW_ALIGN)) * ROW_ALIGN
    oa_s[...] = oa_new
    eq_s[...] = eq_s[...] + jnp.sum(eqf, axis=0, keepdims=True)

    @pl.when(k == nblk - 1)
    def _():
        oa_ref[0, pl.ds(nblk, 1), :] = oa_new.astype(I32)


def route(aff, *, tb, win):
    bsz, n, ne = aff.shape
    cap = max(1, (EC_CAPACITY * n) // ne)
    nblk = n // tb
    assert ne * 8 == LANES
    aff8 = aff.reshape(bsz, n // 8, LANES)
    chunk = min(256, n // 8)
    ii = np.arange(tb)
    ls = jnp.asarray(ii[None, :] < ii[:, None], BF16)
    us = jnp.asarray(ii[:, None] < ii[None, :], BF16)
    eye = jnp.asarray(ii[:, None] == ii[None, :], BF16)
    full = lambda a: pl.BlockSpec(a.shape, lambda b, k: (0,) * a.ndim)
    return pl.pallas_call(
        functools.partial(_route_kernel, n=n, cap=cap, tb=tb, win=win, chunk=chunk),
        grid=(bsz, nblk),
        in_specs=[pl.BlockSpec((1, n, ne), lambda b, k: (b, 0, 0)),
                  pl.BlockSpec((1, n // 8, LANES), lambda b, k: (b, 0, 0)), full(ls), full(us), full(eye)],
        out_specs=[
            pl.BlockSpec((1, tb, ne), lambda b, k: (b, k, 0)),
            pl.BlockSpec((1, ne, tb), lambda b, k: (b, 0, k)),
            pl.BlockSpec((1, nblk + 1, ne), lambda b, k: (b, 0, 0)),
            pl.BlockSpec((1, nblk, ne), lambda b, k: (b, 0, 0)),
        ],
        out_shape=[
            jax.ShapeDtypeStruct((bsz, n, ne), I32),
            jax.ShapeDtypeStruct((bsz, ne, n), F32),
            jax.ShapeDtypeStruct((bsz, nblk + 1, ne), I32),
            jax.ShapeDtypeStruct((bsz, nblk, ne), I32),
        ],
        scratch_shapes=[pltpu.VMEM((1, ne), I32), pltpu.VMEM((1, ne), F32),
                        pltpu.VMEM((1, ne), F32), pltpu.VMEM((1, ne), F32)],
        compiler_params=_cparams(("arbitrary", "arbitrary")),
        name="route",
    )(aff, aff8, ls, us, eye)


def _dispatch_kernel(oa_ref, nw_ref, rankt_ref, h_ref, xg_ref, stack, sem,
                     *, nblk, n_steps, win, tb, group, tail):
    b = pl.program_id(0)
    k = pl.program_id(1)
    step = b * nblk + k
    slot = step % 2
    ne = N_EXPERTS

    def fill(sl, j):
        h = h_ref[0]
        for g0 in range(0, ne, group):
            pieces = []
            for e in range(g0, g0 + group):
                r = rankt_ref[0, e:e + 1, :]
                srow = lax.broadcasted_iota(I32, (win, tb), 0).astype(F32) + (j * win).astype(F32)
                pieces.append((r == srow).astype(BF16))
            oh_t = jnp.concatenate(pieces, axis=0)
            stack[sl, g0 * win:(g0 + group) * win, :] = _dot(oh_t, h).astype(BF16)

    def copies(sl, bb, kk, j):
        out = []
        for e in range(ne):
            off = pl.multiple_of(oa_ref[(bb * (nblk + 1) + kk) * ne + e] + j * win, ROW_ALIGN)
            out.append(pltpu.make_async_copy(
                stack.at[sl, pl.ds(e * win, win), :],
                xg_ref.at[bb, e, pl.ds(off, win), :],
                sem.at[sl]))
        return out

    @pl.when(k == 0)
    def _():
        stack[2, 0:win, :] = jnp.zeros((win, stack.shape[2]), BF16)
        cs = [pltpu.make_async_copy(stack.at[2, pl.ds(0, sz), :], xg_ref.at[b, e, pl.ds(off, sz), :], sem.at[2])
              for e in range(ne) for off, sz in tail]
        for c in cs:
            c.start()
        for c in cs:
            c.wait()

    fill(slot, jnp.int32(0))

    @pl.when(step > 0)
    def _():
        for c in copies(1 - slot, b, k, 0):
            c.wait()

    for c in copies(slot, b, k, 0):
        c.start()

    nws = [nw_ref[(b * nblk + k) * ne + e] for e in range(ne)]
    nwx = functools.reduce(jnp.maximum, nws)

    def extra(j, carry):
        fill(2, j)
        cs = copies(2, b, k, j)
        for e, c in enumerate(cs):
            pl.when(j < nws[e])(c.start)
        for e, c in enumerate(cs):
            pl.when(j < nws[e])(c.wait)
        return carry

    lax.fori_loop(1, jnp.maximum(nwx, 1), extra, 0)

    @pl.when(step == n_steps - 1)
    def _():
        for c in copies(slot, b, k, 0):
            c.wait()


def dispatch(oa, nw, rank_t, h2, *, tb, win, c_pad, tail):
    bsz, n, d = h2.shape
    ne = N_EXPERTS
    nblk = n // tb
    group = 4
    grid_spec = pltpu.PrefetchScalarGridSpec(
        num_scalar_prefetch=2,
        grid=(bsz, nblk),
        in_specs=[
            pl.BlockSpec((1, ne, tb), lambda b, k, *_: (b, 0, k)),
            pl.BlockSpec((1, tb, d), lambda b, k, *_: (b, k, 0)),
        ],
        out_specs=pl.BlockSpec(memory_space=pl.ANY),
        scratch_shapes=[
            pltpu.VMEM((3, ne * win, d), BF16),
            pltpu.SemaphoreType.DMA((3,)),
        ],
    )
    return pl.pallas_call(
        functools.partial(_dispatch_kernel, nblk=nblk, n_steps=bsz * nblk, win=win, tb=tb, group=group, tail=tail),
        grid_spec=grid_spec,
        out_shape=jax.ShapeDtypeStruct((bsz, ne, c_pad, d), BF16),
        compiler_params=_cparams(("arbitrary", "arbitrary")),
        name="dispatch",
    )(oa.reshape(-1), nw.reshape(-1), rank_t, h2)


def _ffn_kernel(tot_ref, x_ref, wg_ref, wu_ref, wd_ref, y_ref, wgb, wub, wdb, *, tm, nblk):
    e = pl.program_id(0)
    b = pl.program_id(1)
    i = pl.program_id(2)

    @pl.when(jnp.logical_and(b == 0, i == 0))
    def _():
        wgb[...] = wg_ref[0, 0].astype(BF16)
        wub[...] = wu_ref[0, 0].astype(BF16)
        wdb[...] = wd_ref[0, 0].astype(BF16)

    total = tot_ref[(b * (nblk + 1) + nblk) * N_EXPERTS + e]

    @pl.when(i * tm < total)
    def _():
        x = x_ref[0, 0]
        hid = (_silu(_dot(x, wgb[...])) * _dot(x, wub[...])).astype(BF16)
        y_ref[0, 0] = _dot(hid, wdb[...]).astype(BF16)

    @pl.when(i * tm >= total)
    def _():
        y_ref[0, 0] = jnp.zeros(y_ref.shape[2:], BF16)


def expert_ffn(oa, xg, w_gate, w_up, w_down, layer, *, tm, nblk):
    bsz, ne, c_pad, d = xg.shape
    f = w_gate.shape[-1]
    grid_spec = pltpu.PrefetchScalarGridSpec(
        num_scalar_prefetch=1,
        grid=(ne, bsz, c_pad // tm),
        in_specs=[
            pl.BlockSpec((1, 1, tm, d), lambda e, b, i, *_: (b, e, i, 0)),
            pl.BlockSpec((1, 1, d, f), lambda e, b, i, *_: (layer, e, 0, 0)),
            pl.BlockSpec((1, 1, d, f), lambda e, b, i, *_: (layer, e, 0, 0)),
            pl.BlockSpec((1, 1, f, d), lambda e, b, i, *_: (layer, e, 0, 0)),
        ],
        out_specs=pl.BlockSpec((1, 1, tm, d), lambda e, b, i, *_: (b, e, i, 0)),
        scratch_shapes=[pltpu.VMEM((d, f), BF16), pltpu.VMEM((d, f), BF16), pltpu.VMEM((f, d), BF16)],
    )
    return pl.pallas_call(
        functools.partial(_ffn_kernel, tm=tm, nblk=nblk),
        grid_spec=grid_spec,
        out_shape=jax.ShapeDtypeStruct(xg.shape, BF16),
        compiler_params=_cparams(("arbitrary", "arbitrary", "arbitrary")),
        name="expert_ffn",
    )(oa.reshape(-1), xg, w_gate, w_up, w_down)


def _combine_kernel(oa_ref, nw_ref, rank_ref, xs_ref, mod_ref, *rest,
                    ctx, nblk, n_steps, win, tb, final):
    if final:
        fg_ref, ye_ref, o_ref, stack, sem = rest
    else:
        ye_ref, o_ref, stack, sem = rest
    b = pl.program_id(0)
    k = pl.program_id(1)
    step = b * nblk + k
    slot = step % 2
    ne = N_EXPERTS

    def copies(sl, bb, kk, j):
        out = []
        for e in range(ne):
            off = pl.multiple_of(oa_ref[(bb * (nblk + 1) + kk) * ne + e] + j * win, ROW_ALIGN)
            out.append(pltpu.make_async_copy(
                ye_ref.at[bb, e, pl.ds(off, win), :],
                stack.at[sl, pl.ds(e * win, win), :],
                sem.at[sl]))
        return out

    @pl.when(step == 0)
    def _():
        stack[2] = jnp.zeros(stack.shape[1:], BF16)
        for c in copies(0, b, k, 0):
            c.start()

    @pl.when(step + 1 < n_steps)
    def _():
        nxt = step + 1
        for c in copies(1 - slot, nxt // nblk, nxt % nblk, 0):
            c.start()

    for c in copies(slot, b, k, 0):
        c.wait()

    def weighted_sum(j, sl):
        word = rank_ref[0]
        slot1 = lax.broadcasted_iota(I32, (tb, win), 1) + (j * win + 1)
        y = None
        for e0 in range(0, ne, 2):
            pieces = []
            for e in (e0, e0 + 1):
                wb = jnp.broadcast_to(word[:, e:e + 1], (tb, win))
                gate = pltpu.bitcast(wb & jnp.int32(-65536), F32)
                pieces.append(jnp.where((wb & 0xFFFF) == slot1, gate, 0.0).astype(BF16))
            part = _dot(jnp.concatenate(pieces, axis=1), stack[sl, e0 * win:(e0 + 2) * win, :])
            y = part if y is None else y + part
        return y

    y = weighted_sum(jnp.int32(0), slot)

    nws = [nw_ref[(b * nblk + k) * ne + e] for e in range(ne)]
    nwx = functools.reduce(jnp.maximum, nws)

    def extra(j, y):
        cs = copies(2, b, k, j)
        for e, c in enumerate(cs):
            pl.when(j < nws[e])(c.start)
        for e, c in enumerate(cs):
            pl.when(j < nws[e])(c.wait)
        return y + weighted_sum(j, 2)

    y = lax.fori_loop(1, jnp.maximum(nwx, 1), extra, y)
    out = xs_ref[0] + _mod_row(mod_ref, ctx, 5) * y
    if final:
        out = out * lax.rsqrt(jnp.mean(out * out, axis=-1, keepdims=True) + EPS) * fg_ref[...]
    o_ref[0] = out


def combine(oa, nw, rank, xs, mod_l, ye, final_g=None, *, ctx, tb, win):
    bsz, n, d = xs.shape
    ne = N_EXPERTS
    nblk = n // tb
    final = final_g is not None
    full = lambda a: pl.BlockSpec(a.shape, lambda b, k, *_: (0,) * a.ndim)
    ins = [rank, xs, mod_l]
    in_specs = [
        pl.BlockSpec((1, tb, ne), lambda b, k, *_: (b, k, 0)),
        pl.BlockSpec((1, tb, d), lambda b, k, *_: (b, k, 0)),
        full(mod_l),
    ]
    if final:
        ins.append(final_g)
        in_specs.append(full(final_g))
    ins.append(ye)
    in_specs.append(pl.BlockSpec(memory_space=pl.ANY))
    grid_spec = pltpu.PrefetchScalarGridSpec(
        num_scalar_prefetch=2,
        grid=(bsz, nblk),
        in_specs=in_specs,
        out_specs=pl.BlockSpec((1, tb, d), lambda b, k, *_: (b, k, 0)),
        scratch_shapes=[
            pltpu.VMEM((3, ne * win, d), BF16),
            pltpu.SemaphoreType.DMA((3,)),
        ],
    )
    return pl.pallas_call(
        functools.partial(_combine_kernel, ctx=ctx, nblk=nblk, n_steps=bsz * nblk, win=win, tb=tb, final=final),
        grid_spec=grid_spec,
        out_shape=jax.ShapeDtypeStruct(xs.shape, F32),
        compiler_params=_cparams(("arbitrary", "arbitrary")),
        name="combine_final" if final else ("combine_ctx" if ctx else "combine"),
    )(oa.reshape(-1), nw.reshape(-1), *ins)


def moe(xs1, h2, aff, mod_l, w_gate, w_up, w_down, layer, final_g=None, *, ctx):
    bsz, n, d = xs1.shape
    cap = max(1, (EC_CAPACITY * n) // N_EXPERTS)
    tb = min(512, n)
    win = LANES
    nblk = n // tb
    tm = 384 if n >= 4096 else 128
    need = cap + ROW_ALIGN * nblk + win
    c_pad = -(-need // tm) * tm
    rank, rank_t, oa, nw = route(aff, tb=tb, win=win)
    tail = tuple((off, min(win, c_pad - off)) for off in range(cap, c_pad, win))
    xg = dispatch(oa, nw, rank_t, h2, tb=tb, win=win, c_pad=c_pad, tail=tail)
    ye = expert_ffn(oa, xg, w_gate, w_up, w_down, layer, tm=tm, nblk=nblk)
    return combine(oa, nw, rank, xs1, mod_l, ye, final_g, ctx=ctx, tb=tb, win=win)


def _swa_in_kernel(x_ref, mod_ref, g_ref, w_ref, cos_ref, sin_ref, q_ref, k_ref, v_ref, *, ctx, scale):
    h = _rms_mod(x_ref[0], g_ref[...], _mod_row(mod_ref, ctx, 0), _mod_row(mod_ref, ctx, 1))
    p = _dot(h.astype(BF16), w_ref[...])
    cos = cos_ref[...]
    sin = sin_ref[...]
    cs = cos * scale
    ss = sin * scale
    lane = lax.broadcasted_iota(I32, cos.shape, 1)
    qd = SWA_HEAD_DIM // 4
    first = (lane % (2 * qd)) < qd

    def partner(x):
        return jnp.where(first, pltpu.roll(x, LANES - qd, 1), pltpu.roll(x, qd, 1))

    for j in range(8):
        a = p[:, j * 128:(j + 1) * 128]
        q_ref[0, :, j * 128:(j + 1) * 128] = (a * cs + partner(a) * ss).astype(BF16)
    low = lane < 64
    for pr in range(2):
        kx = p[:, 1024 + pr * 128:1024 + (pr + 1) * 128]
        kp = kx * cos + partner(kx) * sin
        vp = p[:, 1280 + pr * 128:1280 + (pr + 1) * 128]
        for src, dst in ((kp, k_ref), (vp, v_ref)):
            ev_lo = jnp.where(low, src, 0.0)
            od_hi = jnp.where(low, 0.0, src)
            ev_hi = pltpu.roll(ev_lo, 64, 1)
            od_lo = pltpu.roll(od_hi, 64, 1)
            base = pr * 512
            dst[0, :, base:base + 128] = ev_lo.astype(BF16)
            dst[0, :, base + 128:base + 256] = ev_hi.astype(BF16)
            dst[0, :, base + 256:base + 384] = od_lo.astype(BF16)
            dst[0, :, base + 384:base + 512] = od_hi.astype(BF16)


def swa_in(x, mod_l, g, w2, cos_t, sin_t, *, ctx, tm):
    bsz, n, d = x.shape
    tm = min(tm, n)
    full = lambda a: pl.BlockSpec(a.shape, lambda b, i: (0,) * a.ndim)
    row = lambda w: pl.BlockSpec((1, tm, w), lambda b, i: (b, i, 0))
    return pl.pallas_call(
        functools.partial(_swa_in_kernel, ctx=ctx, scale=SWA_HEAD_DIM ** -0.5 * math.log2(math.e)),
        grid=(bsz, n // tm),
        in_specs=[row(d), full(mod_l), full(g), full(w2),
                  pl.BlockSpec((tm, 128), lambda b, i: (i, 0)),
                  pl.BlockSpec((tm, 128), lambda b, i: (i, 0))],
        out_specs=[row(1024), row(1024), row(1024)],
        out_shape=[jax.ShapeDtypeStruct((bsz, n, 1024), BF16)] * 3,
        compiler_params=_cparams(("arbitrary", "arbitrary")),
        name="swa_in_ctx" if ctx else "swa_in",
    )(x, mod_l, g, w2, cos_t, sin_t)


def _swa_kernel(sink_ref, q_ref, km_ref, kp_ref, kn_ref, vm_ref, vp_ref, vn_ref, kc_ref, vc_ref,
                o_ref, kwin, vwin, *, tq, n_tiles):
    i = pl.program_id(1)
    blk = SWA_BLOCK
    kwin[0:blk] = kp_ref[0]
    kwin[blk:blk + tq] = km_ref[0]
    kwin[blk + tq:blk + tq + blk] = kn_ref[0]
    vwin[0:blk] = vp_ref[0]
    vwin[blk:blk + tq] = vm_ref[0]
    vwin[blk + tq:blk + tq + blk] = vn_ref[0]
    r = lax.broadcasted_iota(I32, (blk, 3 * blk), 0)
    s = lax.broadcasted_iota(I32, (blk, 3 * blk), 1)
    band = jnp.abs(r - (s - blk)) <= WINDOW
    low_lanes = lax.broadcasted_iota(I32, (blk, 128), 1) < 64

    def qblock(qb, carry):
        row0 = pl.multiple_of(qb * blk, blk)
        first = jnp.logical_and(i == 0, qb == 0)
        last = jnp.logical_and(i == n_tiles - 1, qb == tq // blk - 1)
        valid = jnp.logical_and(band, jnp.logical_and(jnp.logical_or(s >= blk, jnp.logical_not(first)),
                                                      jnp.logical_or(s < 2 * blk, jnp.logical_not(last))))
        bias = jnp.where(valid, 0.0, -1e30)
        for kvh in range(SWA_KV_HEADS):
            c0 = kvh * 256
            q4 = jnp.concatenate([q_ref[0, pl.ds(row0, blk), c0:c0 + 128],
                                  q_ref[0, pl.ds(row0, blk), c0 + 128:c0 + 256]], axis=0)
            kl = jnp.concatenate([kwin[pl.ds(row0, 3 * blk), c0:c0 + 128],
                                  kwin[pl.ds(row0, 3 * blk), c0 + 128:c0 + 256]], axis=0)
            vl = jnp.concatenate([vwin[pl.ds(row0, 3 * blk), c0:c0 + 128],
                                  vwin[pl.ds(row0, 3 * blk), c0 + 128:c0 + 256]], axis=0)
            s_loc = _dot_nt(q4, kl)
            s_ctx = _dot_nt(q4, kc_ref[0, kvh])
            lc = s_ctx.shape[1] // 2
            p_loc, p_ctx, inv = [], [], []
            for pp in range(2):
                pl_row, pc_row, inv_row = [], [], []
                for hf in range(2):
                    sk = sink_ref[kvh * 4 + pp * 2 + hf]
                    sl = s_loc[pp * blk:(pp + 1) * blk, hf * 3 * blk:(hf + 1) * 3 * blk] + bias
                    sc = s_ctx[pp * blk:(pp + 1) * blk, hf * lc:(hf + 1) * lc]
                    m = jnp.maximum(jnp.maximum(jnp.max(sl, axis=-1, keepdims=True),
                                                jnp.max(sc, axis=-1, keepdims=True)), sk)
                    el = jnp.exp2(sl - m)
                    ec = jnp.exp2(sc - m)
                    den = (jnp.sum(el, axis=-1, keepdims=True) + jnp.sum(ec, axis=-1, keepdims=True)
                           + jnp.exp2(sk - m))
                    pl_row.append(el.astype(BF16))
                    pc_row.append(ec.astype(BF16))
                    inv_row.append(1.0 / den)
                p_loc.append(jnp.concatenate(pl_row, axis=1))
                p_ctx.append(jnp.concatenate(pc_row, axis=1))
                inv.append(jnp.where(low_lanes, inv_row[0], inv_row[1]))
            o4 = _dot(jnp.concatenate(p_loc, axis=0), vl) + _dot(jnp.concatenate(p_ctx, axis=0), vc_ref[0, kvh])
            for pp in range(2):
                pair = kvh * 2 + pp
                o_ref[0, pl.ds(row0, blk), pair * 128:(pair + 1) * 128] = (
                    o4[pp * blk:(pp + 1) * blk] * inv[pp]).astype(BF16)
        return carry

    lax.fori_loop(0, tq // blk, qblock, 0)


def swa_attention(q, k2, v2, kc2, vc2, sink, *, tq):
    bsz, n, _ = q.shape
    lc = kc2.shape[1]
    tq = min(tq, n)
    n_tiles = n // tq
    rb = tq // SWA_BLOCK
    nb = n // SWA_BLOCK
    main = pl.BlockSpec((1, tq, 1024), lambda b, i, *_: (b, i, 0))
    prev = pl.BlockSpec((1, SWA_BLOCK, 1024), lambda b, i, *_: (b, jnp.maximum(i * rb - 1, 0), 0))
    nxt = pl.BlockSpec((1, SWA_BLOCK, 1024), lambda b, i, *_: (b, jnp.minimum((i + 1) * rb, nb - 1), 0))
    stack = lambda a: a.reshape(bsz, lc, SWA_KV_HEADS, 2, 128).transpose(0, 2, 3, 1, 4).reshape(
        bsz, SWA_KV_HEADS, 2 * lc, 128)
    kc2, vc2 = stack(kc2), stack(vc2)
    sink = sink * math.log2(math.e)
    cspec = pl.BlockSpec((1, SWA_KV_HEADS, 2 * lc, 128), lambda b, i, *_: (b, 0, 0, 0))
    grid_spec = pltpu.PrefetchScalarGridSpec(
        num_scalar_prefetch=1,
        grid=(bsz, n_tiles),
        in_specs=[main, main, prev, nxt, main, prev, nxt, cspec, cspec],
        out_specs=main,
        scratch_shapes=[pltpu.VMEM((tq + 2 * SWA_BLOCK, 1024), BF16)] * 2,
    )
    return pl.pallas_call(
        functools.partial(_swa_kernel, tq=tq, n_tiles=n_tiles),
        grid_spec=grid_spec,
        out_shape=jax.ShapeDtypeStruct((bsz, n, 1024), BF16),
        compiler_params=_cparams(("arbitrary", "arbitrary")),
        name="swa_attention",
    )(sink, q, k2, k2, k2, v2, v2, v2, kc2, vc2)


def _prep_ab(w_in, w_uq, w_ukv, w_out):
    d = w_in.shape[0]
    perm = _rope_perm(MLA_ROPE)
    o = 3 * CONV_DIM
    kr = w_in[:, o + MLA_Q_LORA + MLA_KV_LORA:]
    z64 = jnp.zeros((d, 64), F32)
    z32 = jnp.zeros((d, 32), F32)
    win2 = jnp.concatenate([w_in[:, :o + MLA_Q_LORA + MLA_KV_LORA], z64, kr, z32, z64, kr[:, perm], z32],
                           axis=1).astype(BF16)
    dq = MLA_NOPE + MLA_ROPE
    wq = w_uq.reshape(MLA_Q_LORA, MLA_HEADS, dq)
    zq = jnp.zeros((MLA_Q_LORA, MLA_HEADS, 32), F32)
    zq64 = jnp.zeros((MLA_Q_LORA, MLA_HEADS, 64), F32)
    qa = jnp.concatenate([wq, zq], axis=-1).reshape(MLA_Q_LORA, MLA_HEADS * 128)
    qb = jnp.concatenate([zq64, wq[:, :, MLA_NOPE:][:, :, perm], zq], axis=-1).reshape(MLA_Q_LORA, MLA_HEADS * 128)
    wq2 = jnp.concatenate([qa, qb], axis=1).astype(BF16)
    wkv = w_ukv.reshape(MLA_KV_LORA, MLA_HEADS, MLA_NOPE + MLA_V)
    zk = jnp.zeros((MLA_KV_LORA, MLA_HEADS, 64), F32)
    ka = jnp.concatenate([wkv[:, :, :MLA_NOPE], zk], axis=-1).reshape(MLA_KV_LORA, MLA_HEADS * 128)
    va = jnp.concatenate([wkv[:, :, MLA_NOPE:], zk], axis=-1).reshape(MLA_KV_LORA, MLA_HEADS * 128)
    wkv2 = jnp.concatenate([ka, va], axis=1).astype(BF16)
    wc = w_out[:CONV_DIM].astype(BF16)
    wa = w_out[CONV_DIM:].reshape(MLA_HEADS, MLA_V, d)
    wa = jnp.concatenate([wa, jnp.zeros_like(wa)], axis=1).reshape(MLA_HEADS * 128, d).astype(BF16)
    return win2, wq2, wkv2, wc, wa


def _prep_swa(w_qkv):
    return w_qkv.astype(BF16)


def _pad_lanes(t, left, width=LANES, fill=0.0):
    n, w = t.shape
    return jnp.concatenate([jnp.full((n, left), fill, F32), t, jnp.zeros((n, width - left - w), F32)], axis=1)


def kernel(x, c, ctx, c_ctx, mod_w, mod_b, norm1_g, norm2_g, ab_w_in, conv_w, conv_b, mla_q_norm_g, mla_w_uq,
           mla_kv_norm_g, mla_w_ukv, ab_w_out, swa_w_qkv, swa_sink, swa_w_out, router_w, exp_w_gate, exp_w_up,
           exp_w_down, final_g):
    bsz, n, d = x.shape
    lc = ctx.shape[1]
    depth = mod_w.shape[0]
    assert bsz <= 2 and d == 1024

    cs = jnp.concatenate([c, c_ctx[None, :], jnp.zeros((8 - bsz - 1, d), F32)], axis=0)
    mod = modulation(cs, mod_w, mod_b)

    cos_m, sin_m = _rope_tables(n, MLA_ROPE)
    cos_mla = _pad_lanes(cos_m, 64, fill=1.0)
    sin_mla = _pad_lanes(sin_m, 64)
    cos_mla_c = jnp.concatenate([jnp.ones((lc, 96), F32), jnp.zeros((lc, 32), F32)], axis=1)
    zeros_c = jnp.zeros((lc, 128), F32)
    cos_s, sin_s = _rope_tables(n, SWA_HEAD_DIM)
    cos_swa = jnp.concatenate([cos_s, cos_s], axis=1)
    sin_swa = jnp.concatenate([sin_s, sin_s], axis=1)
    ones_c = jnp.ones((lc, 128), F32)

    row2 = lambda v: v.reshape(1, -1)
    xs, xc = x, ctx
    for layer in range(depth):
        need_ctx = layer < depth - 1
        last = layer == depth - 1
        mod_l = mod[layer]
        g1 = row2(norm1_g[layer])
        g2 = row2(norm2_g[layer])
        rw_f = jnp.concatenate([router_w[layer], jnp.zeros((d, LANES - N_EXPERTS), F32)], axis=1)
        rw_hi = rw_f.astype(BF16)
        rw = jnp.concatenate([rw_hi, (rw_f - rw_hi.astype(F32)).astype(BF16)], axis=1)
        wg, wu, wd = exp_w_gate, exp_w_up, exp_w_down
        if layer % 2 == 0:
            e = layer // 2
            win2, wq2, wkv2, wc, wa = _prep_ab(ab_w_in[e], mla_w_uq[e], mla_w_ukv[e], ab_w_out[e])
            qg, kvg = row2(mla_q_norm_g[e]), row2(mla_kv_norm_g[e])
            cw, cb = conv_w[e], row2(conv_b[e])
            gb, z, q, k, v = ab_in(xs, mod_l, g1, win2, qg, wq2, kvg, wkv2, cos_mla, sin_mla, ctx=False, tm=512)
            gbc, zc, qc, kc, vc = ab_in(xc, mod_l, g1, win2, qg, wq2, kvg, wkv2, cos_mla_c, zeros_c, ctx=True, tm=256)
            att = mla_attention_flat(q, k, v, kc, vc, tq=512, tk_max=1280, spt=8)
            xs1, h2, aff = mix_out(att, wa, xs, mod_l, g2, rw, (gb, z, cw, cb, wc), ctx=False, tm=512)
            if need_ctx:
                att_c = mla_attention_ctx(qc, kc, vc)
                xc1, hc2, affc = mix_out(att_c, wa, xc, mod_l, g2, rw, (gbc, zc, cw, cb, wc), ctx=True, tm=256)
        else:
            o = layer // 2
            w2 = _prep_swa(swa_w_qkv[o])
            wo = swa_w_out[o].astype(BF16)
            q, k2, v2 = swa_in(xs, mod_l, g1, w2, cos_swa, sin_swa, ctx=False, tm=512)
            qc, kc2, vc2 = swa_in(xc, mod_l, g1, w2, ones_c, zeros_c, ctx=True, tm=256)
            att = swa_attention(q, k2, v2, kc2, vc2, swa_sink[o], tq=512)
            xs1, h2, aff = mix_out(att, wo, xs, mod_l, g2, rw, ctx=False, tm=512)
            if need_ctx:
                raise NotImplementedError("context self-attention for windowed layers below the last")
        xs = moe(xs1, h2, aff, mod_l, wg, wu, wd, layer, final_g=row2(final_g) if last else None, ctx=False)
        if need_ctx:
            xc = moe(xc1, hc2, affc, mod_l, wg, wu, wd, layer, ctx=True)
    return xs
```

```python
import functools
import math

import jax
import jax.numpy as jnp
import numpy as np
from jax import lax
from jax.experimental import pallas as pl
from jax.experimental.pallas import tpu as pltpu

F32 = jnp.float32
BF16 = jnp.bfloat16
I32 = jnp.int32

GRID_W = 64
EPS = 1e-6
ROPE_THETA = 10000.0
CONV_DIM = 512
MLA_HEADS = 8
MLA_Q_LORA = 256
MLA_KV_LORA = 128
MLA_NOPE = 64
MLA_ROPE = 32
MLA_V = 64
SWA_HEADS = 16
SWA_KV_HEADS = 4
SWA_HEAD_DIM = 64
WINDOW = 128
SWA_BLOCK = 128
N_EXPERTS = 16
EC_CAPACITY = 2
N_MOD = 6

LANES = 128
ROW_ALIGN = 16
VMEM_LIMIT = 56 * 1024 * 1024


def _cparams(sem, vmem=VMEM_LIMIT):
    return pltpu.CompilerParams(dimension_semantics=sem, vmem_limit_bytes=vmem)


def _dot(a, b):
    return jnp.dot(a, b, preferred_element_type=F32)


def _dot_nt(a, b):
    return lax.dot_general(a, b, (((1,), (1,)), ((), ())), preferred_element_type=F32)


def _dot_tn(a, b):
    return lax.dot_general(a, b, (((0,), (0,)), ((), ())), preferred_element_type=F32)


def _dot_hi(a, b):
    return jnp.dot(a, b, preferred_element_type=F32, precision=lax.Precision.HIGHEST)


def _silu(x):
    return x * (1.0 / (1.0 + jnp.exp(-x)))


def _rms_mod(x, g, shift, scale):
    y = x * lax.rsqrt(jnp.mean(x * x, axis=-1, keepdims=True) + EPS)
    return (y * g) * (1.0 + scale) + shift


def _mod_row(mod_ref, ctx, which):
    r = 2 if ctx else pl.program_id(0)
    return mod_ref[pl.ds(r, 1), which * 1024:(which + 1) * 1024]


def _mod_kernel(cs_ref, w_ref, b_ref, o_ref):
    s = _silu(cs_ref[...])
    o_ref[0] = _dot_hi(s, w_ref[0]) + b_ref[0]


def modulation(cs, mod_w, mod_b):
    depth, d, n6 = mod_w.shape
    tn = 1536
    return pl.pallas_call(
        _mod_kernel,
        grid=(depth, n6 // tn),
        in_specs=[
            pl.BlockSpec((8, d), lambda l, j: (0, 0)),
            pl.BlockSpec((1, d, tn), lambda l, j: (l, 0, j)),
            pl.BlockSpec((1, 1, tn), lambda l, j: (l, 0, j)),
        ],
        out_specs=pl.BlockSpec((1, 8, tn), lambda l, j: (l, 0, j)),
        out_shape=jax.ShapeDtypeStruct((depth, 8, n6), F32),
        compiler_params=_cparams(("arbitrary", "arbitrary")),
        name="modulation",
    )(cs, mod_w, mod_b.reshape(depth, 1, n6))


def _rope_tables(n, dim):
    n_rows = n // GRID_W
    half = dim // 2
    qd = dim // 4
    freqs = ROPE_THETA ** (-jnp.arange(0, half, 2, dtype=F32) / half)
    ang_r = jnp.arange(n_rows, dtype=F32)[:, None] * freqs
    ang_c = jnp.arange(GRID_W, dtype=F32)[:, None] * freqs

    def table(fr, fc, sign):
        r = jnp.broadcast_to(fr[:, None, :], (n_rows, GRID_W, qd))
        c = jnp.broadcast_to(fc[None, :, :], (n_rows, GRID_W, qd))
        return jnp.concatenate([sign * r, r, sign * c, c], axis=-1).reshape(n, dim)

    return table(jnp.cos(ang_r), jnp.cos(ang_c), 1.0), table(jnp.sin(ang_r), jnp.sin(ang_c), -1.0)


def _rope_perm(dim):
    qd = dim // 4
    ch = np.arange(dim)
    pair = (ch // qd) % 2
    return np.where(pair == 0, ch + qd, ch - qd)


def _ab_in_kernel(x_ref, mod_ref, g_ref, win_ref, qg_ref, wq_ref, kvg_ref, wkv_ref, cos_ref, sin_ref,
                  gb_ref, z_ref, q_ref, k_ref, v_ref, *, ctx, scale):
    x = x_ref[0]
    h = _rms_mod(x, g_ref[...], _mod_row(mod_ref, ctx, 0), _mod_row(mod_ref, ctx, 1))
    p = _dot(h.astype(BF16), win_ref[...])
    gb_ref[0] = p[:, 0:512].astype(BF16)
    z_ref[0] = (p[:, 512:1024] * p[:, 1024:1536]).astype(BF16)
    ql = p[:, 1536:1792]
    kvl = p[:, 1792:1920]
    kra = p[:, 1920:2048]
    krb = p[:, 2048:2176]
    cos = cos_ref[...]
    sin = sin_ref[...]
    qn = ql * lax.rsqrt(jnp.mean(ql * ql, axis=-1, keepdims=True) + EPS) * qg_ref[...]
    qq = _dot(qn.astype(BF16), wq_ref[...])
    kvn = kvl * lax.rsqrt(jnp.mean(kvl * kvl, axis=-1, keepdims=True) + EPS) * kvg_ref[...]
    kv = _dot(kvn.astype(BF16), wkv_ref[...])
    krope = kra * cos + krb * sin
    cs = cos * scale
    ss = sin * scale
    ones_lane = lax.broadcasted_iota(I32, cos.shape, 1) == MLA_V
    for hd in range(MLA_HEADS):
        a = qq[:, hd * 128:(hd + 1) * 128]
        b = qq[:, 1024 + hd * 128:1024 + (hd + 1) * 128]
        q_ref[0, hd] = (a * cs + b * ss).astype(BF16)
        k_ref[0, hd] = (kv[:, hd * 128:(hd + 1) * 128] + krope).astype(BF16)
        v_ref[0, hd] = jnp.where(ones_lane, 1.0, kv[:, 1024 + hd * 128:1024 + (hd + 1) * 128]).astype(BF16)


def ab_in(x, mod_l, g, win2, qg, wq2, kvg, wkv2, cos_t, sin_t, *, ctx, tm):
    bsz, n, d = x.shape
    tm = min(tm, n)
    scale = (MLA_NOPE + MLA_ROPE) ** -0.5 * math.log2(math.e)
    full = lambda a: pl.BlockSpec(a.shape, lambda b, i: (0,) * a.ndim)
    hshape = jax.ShapeDtypeStruct((bsz, MLA_HEADS, n, 128), BF16)
    hspec = pl.BlockSpec((1, MLA_HEADS, tm, 128), lambda b, i: (b, 0, i, 0))
    return pl.pallas_call(
        functools.partial(_ab_in_kernel, ctx=ctx, scale=scale),
        grid=(bsz, n // tm),
        in_specs=[
            pl.BlockSpec((1, tm, d), lambda b, i: (b, i, 0)),
            full(mod_l), full(g), full(win2), full(qg), full(wq2), full(kvg), full(wkv2),
            pl.BlockSpec((tm, 128), lambda b, i: (i, 0)),
            pl.BlockSpec((tm, 128), lambda b, i: (i, 0)),
        ],
        out_specs=[
            pl.BlockSpec((1, tm, 512), lambda b, i: (b, i, 0)),
            pl.BlockSpec((1, tm, 512), lambda b, i: (b, i, 0)),
            hspec, hspec, hspec,
        ],
        out_shape=[
            jax.ShapeDtypeStruct((bsz, n, 512), BF16),
            jax.ShapeDtypeStruct((bsz, n, 512), BF16),
            hshape, hshape, hshape,
        ],
        compiler_params=_cparams(("arbitrary", "arbitrary")),
        name="ab_in_ctx" if ctx else "ab_in",
    )(x, mod_l, g, win2, qg, wq2, kvg, wkv2, cos_t, sin_t)


def _mla_ctx_kernel(q_ref, kc_ref, vc_ref, o_ref):
    s = _dot_nt(q_ref[0, 0], kc_ref[0, 0])
    m = jnp.max(s, axis=-1, keepdims=True)
    acc = _dot(jnp.exp2(s - m).astype(BF16), vc_ref[0, 0])
    o_ref[0] = (acc * (1.0 / acc[:, MLA_V:MLA_V + 1])).astype(BF16)


def mla_attention_ctx(q, kc, vc):
    bsz, nh, lc, dk = q.shape
    kv = pl.BlockSpec((1, 1, lc, dk), lambda b, h: (b, h, 0, 0))
    return pl.pallas_call(
        _mla_ctx_kernel,
        grid=(bsz, nh),
        in_specs=[kv, kv, kv],
        out_specs=pl.BlockSpec((1, lc, dk), lambda b, h: (b, 0, h)),
        out_shape=jax.ShapeDtypeStruct((bsz, lc, nh * dk), BF16),
        compiler_params=_cparams(("arbitrary", "arbitrary")),
        name="mla_attention_ctx",
    )(q, kc, vc)


def _mla_flat_kernel(q_ref, k_ref, v_ref, kc_ref, vc_ref, o_ref, kall, vall, s_a, s_b, p_a, p_b,
                     *, tq, tk, n_tiles, n_chunks, spt):
    n_stages = n_tiles * n_chunks
    n_lat, n_ctx = k_ref.shape[2], kc_ref.shape[2]
    cp = math.gcd(n_lat, 2048)

    def copy_rows(c, carry):
        sl = pl.ds(pl.multiple_of(c * cp, cp), cp)
        kall[sl, :] = k_ref[0, 0, sl, :]
        vall[sl, :] = v_ref[0, 0, sl, :]
        return carry

    lax.fori_loop(0, n_lat // cp, copy_rows, 0)
    kall[n_lat:n_lat + n_ctx, :] = kc_ref[0, 0]
    vall[n_lat:n_lat + n_ctx, :] = vc_ref[0, 0]

    def nxt(c):
        i, j = c
        wrap = j + 1 == n_chunks
        return jnp.where(wrap, i + 1, i), jnp.where(wrap, 0, j + 1)

    def rows(idx, size):
        return pl.ds(pl.multiple_of(idx * size, size), size)

    def scores(c, s_buf):
        s = _dot_nt(q_ref[0, 0, rows(c[0], tq), :], kall[rows(c[1], tk), :])
        s_buf[...] = s
        return jnp.max(s, axis=-1, keepdims=True)

    def weights(c, s_buf, p_buf, mx, m):
        m_prev = jnp.where(c[1] == 0, -1e30, m)
        m_new = jnp.maximum(m_prev, mx)
        p_buf[...] = jnp.exp2(s_buf[...] - m_new).astype(BF16)
        return m_new, jnp.exp2(m_prev - m_new)

    def wsum(c, p_buf, alpha, acc):
        acc = alpha * acc + _dot(p_buf[...], vall[rows(c[1], tk), :])
        o_ref[0, pl.ds(pl.multiple_of(c[0] * tq, tq), tq), :] = (
            acc * (1.0 / acc[:, MLA_V:MLA_V + 1])).astype(BF16)
        return acc

    def trip(c_pv, c_w, c_s, s_cur, s_nxt, p_prev, p_cur, mx_cur, alpha_prev, m, acc):
        mx_nxt = scores(c_s, s_nxt) if c_s is not None else None
        acc = wsum(c_pv, p_prev, alpha_prev, acc)
        m, alpha_cur = weights(c_w, s_cur, p_cur, mx_cur, m)
        return mx_nxt, alpha_cur, m, acc

    def pair(c, mx_b, alpha_a, m, acc):
        mx_a, alpha_b, m, acc = trip(c[0], c[1], c[2], s_b, s_a, p_a, p_b, mx_b, alpha_a, m, acc)
        mx_b, alpha_a, m, acc = trip(c[1], c[2], c[3], s_a, s_b, p_b, p_a, mx_a, alpha_b, m, acc)
        return mx_b, alpha_a, m, acc

    def chain(c, k):
        out = [c]
        for _ in range(k):
            out.append(nxt(out[-1]))
        return out

    zero = jnp.int32(0)
    c0 = (zero, zero)
    mx_a = scores(c0, s_a)
    m, alpha_a = weights(c0, s_a, p_a, mx_a, jnp.zeros((tq, 1), F32))
    mx_b = scores(nxt(c0), s_b)
    acc = jnp.zeros((tq, LANES), F32)

    def run(n_st, carry):
        i, j, mx_b, alpha_a, m, acc = carry
        c = chain((i, j), n_st + 1)
        for u in range(0, n_st, 2):
            mx_b, alpha_a, m, acc = pair(c[u:u + 4], mx_b, alpha_a, m, acc)
        return c[n_st][0], c[n_st][1], mx_b, alpha_a, m, acc

    carry = run((n_stages - 4) % spt, (zero, zero, mx_b, alpha_a, m, acc))
    i, j, mx_b, alpha_a, m, acc = lax.fori_loop(0, (n_stages - 4) // spt, lambda _, c: run(spt, c), carry)
    c = chain((i, j), 3)
    mx_b, alpha_a, m, acc = pair(c[0:4], mx_b, alpha_a, m, acc)
    mx_a, alpha_b, m, acc = trip(c[2], c[3], None, s_b, s_a, p_a, p_b, mx_b, alpha_a, m, acc)
    wsum(c[3], p_b, alpha_b, acc)


def mla_attention_flat(q, k, v, kc, vc, *, tq, tk_max, spt):
    bsz, nh, n, dk = q.shape
    lc = kc.shape[2]
    nk = n + lc
    tq = min(tq, n)
    tk = max(t for t in range(256, tk_max + 1, 256) if nk % t == 0)
    n_tiles, n_chunks = n // tq, nk // tk
    assert spt % 2 == 0 and (n_tiles * n_chunks) % 2 == 0 and n_tiles * n_chunks >= 4
    head = lambda rows_: pl.BlockSpec((1, 1, rows_, dk), lambda b, h: (b, h, 0, 0))
    return pl.pallas_call(
        functools.partial(_mla_flat_kernel, tq=tq, tk=tk, n_tiles=n_tiles, n_chunks=n_chunks, spt=spt),
        grid=(bsz, nh),
        in_specs=[head(n), head(n), head(n), head(lc), head(lc)],
        out_specs=pl.BlockSpec((1, n, dk), lambda b, h: (b, 0, h)),
        out_shape=jax.ShapeDtypeStruct((bsz, n, nh * dk), BF16),
        scratch_shapes=([pltpu.VMEM((nk, dk), BF16)] * 2 + [pltpu.VMEM((tq, tk), F32)] * 2
                        + [pltpu.VMEM((tq, tk), BF16)] * 2),
        compiler_params=_cparams(("arbitrary", "arbitrary")),
        name="mla_attention",
    )(q, k, v, kc, vc)


def _mix_out_kernel(*refs, ctx, has_conv, tm, n_tiles):
    if has_conv:
        (gb_ref, z_ref, zp_ref, zn_ref, cw_ref, cb_ref, wc_ref,
         att_ref, wa_ref, x_ref, mod_ref, g2_ref, rw_ref, xs_ref, h2_ref, aff_ref) = refs
    else:
        att_ref, wa_ref, x_ref, mod_ref, g2_ref, rw_ref, xs_ref, h2_ref, aff_ref = refs
    y = _dot(att_ref[0], wa_ref[...])
    if has_conv:
        i = pl.program_id(1)
        z = z_ref[0].astype(F32)
        rows = lax.broadcasted_iota(I32, z.shape, 0)
        zprev_halo = jnp.where(i > 0, zp_ref[0, ROW_ALIGN - 1:ROW_ALIGN, :].astype(F32), 0.0)
        znext_halo = jnp.where(i < n_tiles - 1, zn_ref[0, 0:1, :].astype(F32), 0.0)
        zprev = jnp.where(rows == 0, zprev_halo, pltpu.roll(z, 1, 0))
        znext = jnp.where(rows == tm - 1, znext_halo, pltpu.roll(z, tm - 1, 0))
        cw = cw_ref[...]
        conv = gb_ref[0].astype(F32) * (zprev * cw[0:1] + z * cw[1:2] + znext * cw[2:3] + cb_ref[...])
        y = y + _dot(conv.astype(BF16), wc_ref[...])
    xs = x_ref[0] + _mod_row(mod_ref, ctx, 2) * y
    xs_ref[0] = xs
    h2 = _rms_mod(xs, g2_ref[...], _mod_row(mod_ref, ctx, 3), _mod_row(mod_ref, ctx, 4))
    h2_ref[0] = h2.astype(BF16)
    h2_hi = h2.astype(BF16)
    h2_lo = (h2 - h2_hi.astype(F32)).astype(BF16)
    hh = _dot(h2_hi, rw_ref[...])
    logits = hh[:, :LANES] + hh[:, LANES:] + _dot(h2_lo, rw_ref[:, :LANES])
    lane = lax.broadcasted_iota(I32, logits.shape, 1)
    logits = jnp.where(lane < N_EXPERTS, logits, -1e30)
    mx = jnp.max(logits, axis=-1, keepdims=True)
    ex = jnp.exp(logits - mx)
    aff = ex / jnp.sum(ex, axis=-1, keepdims=True)
    aff_ref[0] = aff[:, :N_EXPERTS]


def mix_out(att, wa, x, mod_l, g2, rw, conv=None, *, ctx, tm):
    bsz, n, d = x.shape
    tm = min(tm, n)
    n_tiles = n // tm
    full = lambda a: pl.BlockSpec(a.shape, lambda b, i: (0,) * a.ndim)
    row = lambda w: pl.BlockSpec((1, tm, w), lambda b, i: (b, i, 0))
    ins, in_specs = [], []
    if conv is not None:
        gb, z, cw, cb, wc = conv
        r8 = tm // ROW_ALIGN
        nb8 = n // ROW_ALIGN
        ins += [gb, z, z, z, cw, cb, wc]
        in_specs += [
            row(512), row(512),
            pl.BlockSpec((1, ROW_ALIGN, 512), lambda b, i: (b, jnp.maximum(i * r8 - 1, 0), 0)),
            pl.BlockSpec((1, ROW_ALIGN, 512), lambda b, i: (b, jnp.minimum((i + 1) * r8, nb8 - 1), 0)),
            full(cw), full(cb), full(wc),
        ]
    ins += [att, wa, x, mod_l, g2, rw]
    in_specs += [row(att.shape[-1]), full(wa), row(d), full(mod_l), full(g2), full(rw)]
    return pl.pallas_call(
        functools.partial(_mix_out_kernel, ctx=ctx, has_conv=conv is not None, tm=tm, n_tiles=n_tiles),
        grid=(bsz, n_tiles),
        in_specs=in_specs,
        out_specs=[row(d), row(d), row(N_EXPERTS)],
        out_shape=[
            jax.ShapeDtypeStruct((bsz, n, d), F32),
            jax.ShapeDtypeStruct((bsz, n, d), BF16),
            jax.ShapeDtypeStruct((bsz, n, N_EXPERTS), F32),
        ],
        compiler_params=_cparams(("arbitrary", "arbitrary")),
        name=("mix_out_ctx" if ctx else "mix_out") + ("_conv" if conv is not None else ""),
    )(*ins)


def _route_kernel(aff_ref, aff8_ref, ls_ref, us_ref, eye_ref, rank_ref, rankt_ref, oa_ref, nwd_ref, nwc_ref,
                  thr_s, need_s, eq_s, oa_s, *, n, cap, tb, win_d, win_c, chunk):
    n_chunks = (n // 8) // chunk
    nblk = n // tb
    k = pl.program_id(1)

    def count(pred_fn):
        def body(c, acc):
            kb = pltpu.bitcast(aff8_ref[0, pl.ds(pl.multiple_of(c * chunk, chunk), chunk), :], I32)
            return acc + jnp.sum(pred_fn(kb).astype(I32), axis=0, keepdims=True)
        acc = lax.fori_loop(0, n_chunks, body, jnp.zeros((1, LANES), I32))
        for sh in (64, 32, 16):
            acc = acc + pltpu.roll(acc, sh, 1)
        return acc

    @pl.when(k == 0)
    def _():
        def bit_body(i, thr):
            cand = thr | jnp.left_shift(jnp.int32(1), 30 - i)
            cnt = count(lambda kb: kb >= cand)
            return jnp.where(cnt >= cap, cand, thr)

        thr = lax.fori_loop(0, 31, bit_body, jnp.zeros((1, LANES), I32))
        thr_s[...] = thr[:, :N_EXPERTS]
        need_s[...] = (cap - count(lambda kb: kb > thr))[:, :N_EXPERTS].astype(F32)
        eq_s[...] = jnp.zeros((1, N_EXPERTS), F32)
        oa_s[...] = jnp.zeros((1, N_EXPERTS), F32)

    thr = thr_s[...]
    off = pl.multiple_of(k * tb, tb)
    kb = pltpu.bitcast(aff_ref[0, pl.ds(off, tb), :], I32)
    gt = kb > thr
    eq = kb == thr
    eqf = jnp.where(eq, 1.0, 0.0)
    eqrank = eq_s[...] + _dot(ls_ref[...], eqf.astype(BF16))
    sel = jnp.logical_or(gt, jnp.logical_and(eq, eqrank < need_s[...]))
    self_ = jnp.where(sel, 1.0, 0.0)
    selb = self_.astype(BF16)
    lrank = _dot(ls_ref[...], selb)
    gate_bits = pltpu.bitcast(aff_ref[0, pl.ds(off, tb), :].astype(BF16).astype(F32), I32)
    rank_ref[0] = gate_bits | jnp.where(sel, lrank.astype(I32) + 1, 0)
    lrank_t = _dot_tn(selb, us_ref[...])
    sel_t = _dot_tn(selb, eye_ref[...])
    rankt_ref[0] = jnp.where(sel_t > 0.5, lrank_t, -1.0)
    c = jnp.sum(self_, axis=0, keepdims=True)
    oa_run = oa_s[...]
    oa_ref[0, pl.ds(k, 1), :] = oa_run.astype(I32)
    nwd_ref[0, pl.ds(k, 1), :] = jnp.ceil(c * (1.0 / win_d)).astype(I32)
    nwc_ref[0, pl.ds(k, 1), :] = jnp.ceil(c * (1.0 / win_c)).astype(I32)
    oa_new = oa_run + jnp.ceil(c * (1.0 / ROW_ALIGN)) * ROW_ALIGN
    oa_s[...] = oa_new
    eq_s[...] = eq_s[...] + jnp.sum(eqf, axis=0, keepdims=True)

    @pl.when(k == nblk - 1)
    def _():
        oa_ref[0, pl.ds(nblk, 1), :] = oa_new.astype(I32)


def route(aff, *, tb, win_d, win_c):
    bsz, n, ne = aff.shape
    cap = max(1, (EC_CAPACITY * n) // ne)
    nblk = n // tb
    assert ne * 8 == LANES
    aff8 = aff.reshape(bsz, n // 8, LANES)
    chunk = min(256, n // 8)
    ii = np.arange(tb)
    ls = jnp.asarray(ii[None, :] < ii[:, None], BF16)
    us = jnp.asarray(ii[:, None] < ii[None, :], BF16)
    eye = jnp.asarray(ii[:, None] == ii[None, :], BF16)
    full = lambda a: pl.BlockSpec(a.shape, lambda b, k: (0,) * a.ndim)
    return pl.pallas_call(
        functools.partial(_route_kernel, n=n, cap=cap, tb=tb, win_d=win_d, win_c=win_c, chunk=chunk),
        grid=(bsz, nblk),
        in_specs=[pl.BlockSpec((1, n, ne), lambda b, k: (b, 0, 0)),
                  pl.BlockSpec((1, n // 8, LANES), lambda b, k: (b, 0, 0)), full(ls), full(us), full(eye)],
        out_specs=[
            pl.BlockSpec((1, tb, ne), lambda b, k: (b, k, 0)),
            pl.BlockSpec((1, ne, tb), lambda b, k: (b, 0, k)),
            pl.BlockSpec((1, nblk + 1, ne), lambda b, k: (b, 0, 0)),
            pl.BlockSpec((1, nblk, ne), lambda b, k: (b, 0, 0)),
            pl.BlockSpec((1, nblk, ne), lambda b, k: (b, 0, 0)),
        ],
        out_shape=[
            jax.ShapeDtypeStruct((bsz, n, ne), I32),
            jax.ShapeDtypeStruct((bsz, ne, n), F32),
            jax.ShapeDtypeStruct((bsz, nblk + 1, ne), I32),
            jax.ShapeDtypeStruct((bsz, nblk, ne), I32),
            jax.ShapeDtypeStruct((bsz, nblk, ne), I32),
        ],
        scratch_shapes=[pltpu.VMEM((1, ne), I32), pltpu.VMEM((1, ne), F32),
                        pltpu.VMEM((1, ne), F32), pltpu.VMEM((1, ne), F32)],
        compiler_params=_cparams(("arbitrary", "arbitrary")),
        name="route",
    )(aff, aff8, ls, us, eye)


def _dispatch_kernel(oa_ref, nw_ref, rankt_ref, h_ref, xg_ref, stack, sem,
                     *, nblk, n_steps, win, tb, group, tail):
    b = pl.program_id(0)
    k = pl.program_id(1)
    step = b * nblk + k
    slot = step % 2
    ne = N_EXPERTS

    def fill(sl, j):
        h = h_ref[0]
        for g0 in range(0, ne, group):
            pieces = []
            for e in range(g0, g0 + group):
                r = rankt_ref[0, e:e + 1, :]
                srow = lax.broadcasted_iota(I32, (win, tb), 0).astype(F32) + (j * win).astype(F32)
                pieces.append((r == srow).astype(BF16))
            oh_t = jnp.concatenate(pieces, axis=0)
            stack[sl, g0 * win:(g0 + group) * win, :] = _dot(oh_t, h).astype(BF16)

    def copies(sl, bb, kk, j):
        out = []
        for e in range(ne):
            off = pl.multiple_of(oa_ref[(bb * (nblk + 1) + kk) * ne + e] + j * win, ROW_ALIGN)
            out.append(pltpu.make_async_copy(
                stack.at[sl, pl.ds(e * win, win), :],
                xg_ref.at[bb, e, pl.ds(off, win), :],
                sem.at[sl]))
        return out

    @pl.when(k == 0)
    def _():
        stack[2, 0:win, :] = jnp.zeros((win, stack.shape[2]), BF16)
        cs = [pltpu.make_async_copy(stack.at[2, pl.ds(0, sz), :], xg_ref.at[b, e, pl.ds(off, sz), :], sem.at[2])
              for e in range(ne) for off, sz in tail]
        for c in cs:
            c.start()
        for c in cs:
            c.wait()

    fill(slot, jnp.int32(0))

    @pl.when(step > 0)
    def _():
        for c in copies(1 - slot, b, k, 0):
            c.wait()

    for c in copies(slot, b, k, 0):
        c.start()

    nws = [nw_ref[(b * nblk + k) * ne + e] for e in range(ne)]
    nwx = functools.reduce(jnp.maximum, nws)

    def extra(j, carry):
        fill(2, j)
        cs = copies(2, b, k, j)
        for e, c in enumerate(cs):
            pl.when(j < nws[e])(c.start)
        for e, c in enumerate(cs):
            pl.when(j < nws[e])(c.wait)
        return carry

    lax.fori_loop(1, jnp.maximum(nwx, 1), extra, 0)

    @pl.when(step == n_steps - 1)
    def _():
        for c in copies(slot, b, k, 0):
            c.wait()


def dispatch(oa, nw, rank_t, h2, *, tb, win, c_pad, tail):
    bsz, n, d = h2.shape
    ne = N_EXPERTS
    nblk = n // tb
    group = 4
    grid_spec = pltpu.PrefetchScalarGridSpec(
        num_scalar_prefetch=2,
        grid=(bsz, nblk),
        in_specs=[
            pl.BlockSpec((1, ne, tb), lambda b, k, *_: (b, 0, k)),
            pl.BlockSpec((1, tb, d), lambda b, k, *_: (b, k, 0)),
        ],
        out_specs=pl.BlockSpec(memory_space=pl.ANY),
        scratch_shapes=[
            pltpu.VMEM((3, ne * win, d), BF16),
            pltpu.SemaphoreType.DMA((3,)),
        ],
    )
    return pl.pallas_call(
        functools.partial(_dispatch_kernel, nblk=nblk, n_steps=bsz * nblk, win=win, tb=tb, group=group, tail=tail),
        grid_spec=grid_spec,
        out_shape=jax.ShapeDtypeStruct((bsz, ne, c_pad, d), BF16),
        compiler_params=_cparams(("arbitrary", "arbitrary")),
        name="dispatch",
    )(oa.reshape(-1), nw.reshape(-1), rank_t, h2)


def _ffn_kernel(tot_ref, x_ref, wg_ref, wu_ref, wd_ref, y_ref, wgb, wub, wdb, *, tm, nblk):
    e = pl.program_id(0)
    b = pl.program_id(1)
    i = pl.program_id(2)

    @pl.when(jnp.logical_and(b == 0, i == 0))
    def _():
        wgb[...] = wg_ref[0, 0].astype(BF16)
        wub[...] = wu_ref[0, 0].astype(BF16)
        wdb[...] = wd_ref[0, 0].astype(BF16)

    total = tot_ref[(b * (nblk + 1) + nblk) * N_EXPERTS + e]

    @pl.when(i * tm < total)
    def _():
        x = x_ref[0, 0]
        hid = (_silu(_dot(x, wgb[...])) * _dot(x, wub[...])).astype(BF16)
        y_ref[0, 0] = _dot(hid, wdb[...]).astype(BF16)

    @pl.when(i * tm >= total)
    def _():
        y_ref[0, 0] = jnp.zeros(y_ref.shape[2:], BF16)


def expert_ffn(oa, xg, w_gate, w_up, w_down, layer, *, tm, nblk):
    bsz, ne, c_pad, d = xg.shape
    f = w_gate.shape[-1]
    grid_spec = pltpu.PrefetchScalarGridSpec(
        num_scalar_prefetch=1,
        grid=(ne, bsz, c_pad // tm),
        in_specs=[
            pl.BlockSpec((1, 1, tm, d), lambda e, b, i, *_: (b, e, i, 0)),
            pl.BlockSpec((1, 1, d, f), lambda e, b, i, *_: (layer, e, 0, 0)),
            pl.BlockSpec((1, 1, d, f), lambda e, b, i, *_: (layer, e, 0, 0)),
            pl.BlockSpec((1, 1, f, d), lambda e, b, i, *_: (layer, e, 0, 0)),
        ],
        out_specs=pl.BlockSpec((1, 1, tm, d), lambda e, b, i, *_: (b, e, i, 0)),
        scratch_shapes=[pltpu.VMEM((d, f), BF16), pltpu.VMEM((d, f), BF16), pltpu.VMEM((f, d), BF16)],
    )
    return pl.pallas_call(
        functools.partial(_ffn_kernel, tm=tm, nblk=nblk),
        grid_spec=grid_spec,
        out_shape=jax.ShapeDtypeStruct(xg.shape, BF16),
        compiler_params=_cparams(("arbitrary", "arbitrary", "arbitrary")),
        name="expert_ffn",
    )(oa.reshape(-1), xg, w_gate, w_up, w_down)


def _combine_kernel(oa_ref, nw_ref, rank_ref, xs_ref, mod_ref, *rest,
                    ctx, nblk, n_steps, win, tb, final):
    if final:
        fg_ref, ye_ref, o_ref, stack, sem = rest
    else:
        ye_ref, o_ref, stack, sem = rest
    b = pl.program_id(0)
    k = pl.program_id(1)
    step = b * nblk + k
    slot = step % 2
    ne = N_EXPERTS

    def copies(sl, bb, kk, j):
        out = []
        for e in range(ne):
            off = pl.multiple_of(oa_ref[(bb * (nblk + 1) + kk) * ne + e] + j * win, ROW_ALIGN)
            out.append(pltpu.make_async_copy(
                ye_ref.at[bb, e, pl.ds(off, win), :],
                stack.at[sl, pl.ds(e * win, win), :],
                sem.at[sl]))
        return out

    @pl.when(step == 0)
    def _():
        stack[2] = jnp.zeros(stack.shape[1:], BF16)
        for c in copies(0, b, k, 0):
            c.start()

    @pl.when(step + 1 < n_steps)
    def _():
        nxt = step + 1
        for c in copies(1 - slot, nxt // nblk, nxt % nblk, 0):
            c.start()

    for c in copies(slot, b, k, 0):
        c.wait()

    def weighted_sum(j, sl):
        word = rank_ref[0]
        slot1 = lax.broadcasted_iota(I32, (tb, win), 1) + (j * win + 1)
        y = None
        for e0 in range(0, ne, 2):
            pieces = []
            for e in (e0, e0 + 1):
                wb = jnp.broadcast_to(word[:, e:e + 1], (tb, win))
                gate = pltpu.bitcast(wb & jnp.int32(-65536), F32)
                pieces.append(jnp.where((wb & 0xFFFF) == slot1, gate, 0.0).astype(BF16))
            part = _dot(jnp.concatenate(pieces, axis=1), stack[sl, e0 * win:(e0 + 2) * win, :])
            y = part if y is None else y + part
        return y

    y = weighted_sum(jnp.int32(0), slot)

    nws = [nw_ref[(b * nblk + k) * ne + e] for e in range(ne)]
    nwx = functools.reduce(jnp.maximum, nws)

    def extra(j, y):
        cs = copies(2, b, k, j)
        for e, c in enumerate(cs):
            pl.when(j < nws[e])(c.start)
        for e, c in enumerate(cs):
            pl.when(j < nws[e])(c.wait)
        return y + weighted_sum(j, 2)

    y = lax.fori_loop(1, jnp.maximum(nwx, 1), extra, y)
    out = xs_ref[0] + _mod_row(mod_ref, ctx, 5) * y
    if final:
        out = out * lax.rsqrt(jnp.mean(out * out, axis=-1, keepdims=True) + EPS) * fg_ref[...]
    o_ref[0] = out


def combine(oa, nw, rank, xs, mod_l, ye, final_g=None, *, ctx, tb, win):
    bsz, n, d = xs.shape
    ne = N_EXPERTS
    nblk = n // tb
    final = final_g is not None
    full = lambda a: pl.BlockSpec(a.shape, lambda b, k, *_: (0,) * a.ndim)
    ins = [rank, xs, mod_l]
    in_specs = [
        pl.BlockSpec((1, tb, ne), lambda b, k, *_: (b, k, 0)),
        pl.BlockSpec((1, tb, d), lambda b, k, *_: (b, k, 0)),
        full(mod_l),
    ]
    if final:
        ins.append(final_g)
        in_specs.append(full(final_g))
    ins.append(ye)
    in_specs.append(pl.BlockSpec(memory_space=pl.ANY))
    grid_spec = pltpu.PrefetchScalarGridSpec(
        num_scalar_prefetch=2,
        grid=(bsz, nblk),
        in_specs=in_specs,
        out_specs=pl.BlockSpec((1, tb, d), lambda b, k, *_: (b, k, 0)),
        scratch_shapes=[
            pltpu.VMEM((3, ne * win, d), BF16),
            pltpu.SemaphoreType.DMA((3,)),
        ],
    )
    return pl.pallas_call(
        functools.partial(_combine_kernel, ctx=ctx, nblk=nblk, n_steps=bsz * nblk, win=win, tb=tb, final=final),
        grid_spec=grid_spec,
        out_shape=jax.ShapeDtypeStruct(xs.shape, F32),
        compiler_params=_cparams(("arbitrary", "arbitrary")),
        name="combine_final" if final else ("combine_ctx" if ctx else "combine"),
    )(oa.reshape(-1), nw.reshape(-1), *ins)


def moe(xs1, h2, aff, mod_l, w_gate, w_up, w_down, layer, final_g=None, *, ctx):
    bsz, n, d = xs1.shape
    cap = max(1, (EC_CAPACITY * n) // N_EXPERTS)
    tb = min(512, n)
    win_c = LANES
    win_d = 96 if tb == 512 else LANES
    nblk = n // tb
    tm = 384 if n >= 4096 else 128
    need = cap + ROW_ALIGN * nblk + win_c
    c_pad = -(-need // tm) * tm
    rank, rank_t, oa, nw_d, nw_c = route(aff, tb=tb, win_d=win_d, win_c=win_c)
    tail = tuple((off, min(win_d, c_pad - off)) for off in range(cap, c_pad, win_d))
    xg = dispatch(oa, nw_d, rank_t, h2, tb=tb, win=win_d, c_pad=c_pad, tail=tail)
    ye = expert_ffn(oa, xg, w_gate, w_up, w_down, layer, tm=tm, nblk=nblk)
    return combine(oa, nw_c, rank, xs1, mod_l, ye, final_g, ctx=ctx, tb=tb, win=win_c)


def _swa_in_kernel(x_ref, mod_ref, g_ref, w_ref, cos_ref, sin_ref, q_ref, k_ref, v_ref, *, ctx, scale):
    h = _rms_mod(x_ref[0], g_ref[...], _mod_row(mod_ref, ctx, 0), _mod_row(mod_ref, ctx, 1))
    p = _dot(h.astype(BF16), w_ref[...])
    cos = cos_ref[...]
    sin = sin_ref[...]
    cs = cos * scale
    ss = sin * scale
    lane = lax.broadcasted_iota(I32, cos.shape, 1)
    qd = SWA_HEAD_DIM // 4
    first = (lane % (2 * qd)) < qd

    def partner(x):
        return jnp.where(first, pltpu.roll(x, LANES - qd, 1), pltpu.roll(x, qd, 1))

    for j in range(8):
        a = p[:, j * 128:(j + 1) * 128]
        q_ref[0, :, j * 128:(j + 1) * 128] = (a * cs + partner(a) * ss).astype(BF16)
    low = lane < 64
    for pr in range(2):
        kx = p[:, 1024 + pr * 128:1024 + (pr + 1) * 128]
        kp = kx * cos + partner(kx) * sin
        vp = p[:, 1280 + pr * 128:1280 + (pr + 1) * 128]
        for src, dst in ((kp, k_ref), (vp, v_ref)):
            ev_lo = jnp.where(low, src, 0.0)
            od_hi = jnp.where(low, 0.0, src)
            ev_hi = pltpu.roll(ev_lo, 64, 1)
            od_lo = pltpu.roll(od_hi, 64, 1)
            base = pr * 512
            dst[0, :, base:base + 128] = ev_lo.astype(BF16)
            dst[0, :, base + 128:base + 256] = ev_hi.astype(BF16)
            dst[0, :, base + 256:base + 384] = od_lo.astype(BF16)
            dst[0, :, base + 384:base + 512] = od_hi.astype(BF16)


def swa_in(x, mod_l, g, w2, cos_t, sin_t, *, ctx, tm):
    bsz, n, d = x.shape
    tm = min(tm, n)
    full = lambda a: pl.BlockSpec(a.shape, lambda b, i: (0,) * a.ndim)
    row = lambda w: pl.BlockSpec((1, tm, w), lambda b, i: (b, i, 0))
    return pl.pallas_call(
        functools.partial(_swa_in_kernel, ctx=ctx, scale=SWA_HEAD_DIM ** -0.5 * math.log2(math.e)),
        grid=(bsz, n // tm),
        in_specs=[row(d), full(mod_l), full(g), full(w2),
                  pl.BlockSpec((tm, 128), lambda b, i: (i, 0)),
                  pl.BlockSpec((tm, 128), lambda b, i: (i, 0))],
        out_specs=[row(1024), row(1024), row(1024)],
        out_shape=[jax.ShapeDtypeStruct((bsz, n, 1024), BF16)] * 3,
        compiler_params=_cparams(("arbitrary", "arbitrary")),
        name="swa_in_ctx" if ctx else "swa_in",
    )(x, mod_l, g, w2, cos_t, sin_t)


def _swa_kernel(sink_ref, q_ref, km_ref, kp_ref, kn_ref, vm_ref, vp_ref, vn_ref, kc_ref, vc_ref,
                o_ref, kwin, vwin, *, tq, n_tiles):
    i = pl.program_id(1)
    blk = SWA_BLOCK
    kwin[0:blk] = kp_ref[0]
    kwin[blk:blk + tq] = km_ref[0]
    kwin[blk + tq:blk + tq + blk] = kn_ref[0]
    vwin[0:blk] = vp_ref[0]
    vwin[blk:blk + tq] = vm_ref[0]
    vwin[blk + tq:blk + tq + blk] = vn_ref[0]
    r = lax.broadcasted_iota(I32, (blk, 3 * blk), 0)
    s = lax.broadcasted_iota(I32, (blk, 3 * blk), 1)
    band = jnp.abs(r - (s - blk)) <= WINDOW
    low_lanes = lax.broadcasted_iota(I32, (blk, 128), 1) < 64

    def qblock(qb, carry):
        row0 = pl.multiple_of(qb * blk, blk)
        first = jnp.logical_and(i == 0, qb == 0)
        last = jnp.logical_and(i == n_tiles - 1, qb == tq // blk - 1)
        valid = jnp.logical_and(band, jnp.logical_and(jnp.logical_or(s >= blk, jnp.logical_not(first)),
                                                      jnp.logical_or(s < 2 * blk, jnp.logical_not(last))))
        bias = jnp.where(valid, 0.0, -1e30)
        for kvh in range(SWA_KV_HEADS):
            c0 = kvh * 256
            q4 = jnp.concatenate([q_ref[0, pl.ds(row0, blk), c0:c0 + 128],
                                  q_ref[0, pl.ds(row0, blk), c0 + 128:c0 + 256]], axis=0)
            kl = jnp.concatenate([kwin[pl.ds(row0, 3 * blk), c0:c0 + 128],
                                  kwin[pl.ds(row0, 3 * blk), c0 + 128:c0 + 256]], axis=0)
            vl = jnp.concatenate([vwin[pl.ds(row0, 3 * blk), c0:c0 + 128],
                                  vwin[pl.ds(row0, 3 * blk), c0 + 128:c0 + 256]], axis=0)
            s_loc = _dot_nt(q4, kl)
            s_ctx = _dot_nt(q4, kc_ref[0, kvh])
            lc = s_ctx.shape[1] // 2
            p_loc, p_ctx, inv = [], [], []
            for pp in range(2):
                pl_row, pc_row, inv_row = [], [], []
                for hf in range(2):
                    sk = sink_ref[kvh * 4 + pp * 2 + hf]
                    sl = s_loc[pp * blk:(pp + 1) * blk, hf * 3 * blk:(hf + 1) * 3 * blk] + bias
                    sc = s_ctx[pp * blk:(pp + 1) * blk, hf * lc:(hf + 1) * lc]
                    m = jnp.maximum(jnp.maximum(jnp.max(sl, axis=-1, keepdims=True),
                                                jnp.max(sc, axis=-1, keepdims=True)), sk)
                    el = jnp.exp2(sl - m)
                    ec = jnp.exp2(sc - m)
                    den = (jnp.sum(el, axis=-1, keepdims=True) + jnp.sum(ec, axis=-1, keepdims=True)
                           + jnp.exp2(sk - m))
                    pl_row.append(el.astype(BF16))
                    pc_row.append(ec.astype(BF16))
                    inv_row.append(1.0 / den)
                p_loc.append(jnp.concatenate(pl_row, axis=1))
                p_ctx.append(jnp.concatenate(pc_row, axis=1))
                inv.append(jnp.where(low_lanes, inv_row[0], inv_row[1]))
            o4 = _dot(jnp.concatenate(p_loc, axis=0), vl) + _dot(jnp.concatenate(p_ctx, axis=0), vc_ref[0, kvh])
            for pp in range(2):
                pair = kvh * 2 + pp
                o_ref[0, pl.ds(row0, blk), pair * 128:(pair + 1) * 128] = (
                    o4[pp * blk:(pp + 1) * blk] * inv[pp]).astype(BF16)
        return carry

    lax.fori_loop(0, tq // blk, qblock, 0)


def swa_attention(q, k2, v2, kc2, vc2, sink, *, tq):
    bsz, n, _ = q.shape
    lc = kc2.shape[1]
    tq = min(tq, n)
    n_tiles = n // tq
    rb = tq // SWA_BLOCK
    nb = n // SWA_BLOCK
    main = pl.BlockSpec((1, tq, 1024), lambda b, i, *_: (b, i, 0))
    prev = pl.BlockSpec((1, SWA_BLOCK, 1024), lambda b, i, *_: (b, jnp.maximum(i * rb - 1, 0), 0))
    nxt = pl.BlockSpec((1, SWA_BLOCK, 1024), lambda b, i, *_: (b, jnp.minimum((i + 1) * rb, nb - 1), 0))
    stack = lambda a: a.reshape(bsz, lc, SWA_KV_HEADS, 2, 128).transpose(0, 2, 3, 1, 4).reshape(
        bsz, SWA_KV_HEADS, 2 * lc, 128)
    kc2, vc2 = stack(kc2), stack(vc2)
    sink = sink * math.log2(math.e)
    cspec = pl.BlockSpec((1, SWA_KV_HEADS, 2 * lc, 128), lambda b, i, *_: (b, 0, 0, 0))
    grid_spec = pltpu.PrefetchScalarGridSpec(
        num_scalar_prefetch=1,
        grid=(bsz, n_tiles),
        in_specs=[main, main, prev, nxt, main, prev, nxt, cspec, cspec],
        out_specs=main,
        scratch_shapes=[pltpu.VMEM((tq + 2 * SWA_BLOCK, 1024), BF16)] * 2,
    )
    return pl.pallas_call(
        functools.partial(_swa_kernel, tq=tq, n_tiles=n_tiles),
        grid_spec=grid_spec,
        out_shape=jax.ShapeDtypeStruct((bsz, n, 1024), BF16),
        compiler_params=_cparams(("arbitrary", "arbitrary")),
        name="swa_attention",
    )(sink, q, k2, k2, k2, v2, v2, v2, kc2, vc2)


def _prep_ab(w_in, w_uq, w_ukv, w_out):
    d = w_in.shape[0]
    perm = _rope_perm(MLA_ROPE)
    o = 3 * CONV_DIM + MLA_Q_LORA + MLA_KV_LORA
    dq = MLA_NOPE + MLA_ROPE
    kr = w_in[:, o:]
    rot_lanes = lambda t: jnp.pad(t, ((0, 0),) * (t.ndim - 1) + ((MLA_NOPE, LANES - dq),))
    heads = lambda t: t.reshape(t.shape[0], MLA_HEADS * LANES)
    win2 = jnp.concatenate([w_in[:, :o], rot_lanes(kr), rot_lanes(kr[:, perm])], axis=1).astype(BF16)
    wq = w_uq.reshape(MLA_Q_LORA, MLA_HEADS, dq)
    qa = heads(jnp.pad(wq, ((0, 0), (0, 0), (0, LANES - dq))))
    qb = heads(rot_lanes(wq[:, :, MLA_NOPE:][:, :, perm]))
    wq2 = jnp.concatenate([qa, qb], axis=1).astype(BF16)
    wkv = w_ukv.reshape(MLA_KV_LORA, MLA_HEADS, MLA_NOPE + MLA_V)
    half_lanes = lambda t: heads(jnp.pad(t, ((0, 0), (0, 0), (0, LANES - t.shape[-1]))))
    wkv2 = jnp.concatenate([half_lanes(wkv[:, :, :MLA_NOPE]), half_lanes(wkv[:, :, MLA_NOPE:])], axis=1).astype(BF16)
    wc = w_out[:CONV_DIM].astype(BF16)
    wa = jnp.pad(w_out[CONV_DIM:].reshape(MLA_HEADS, MLA_V, d), ((0, 0), (0, LANES - MLA_V), (0, 0)))
    return win2, wq2, wkv2, wc, wa.reshape(MLA_HEADS * LANES, d).astype(BF16)


def _prep_swa(w_qkv):
    return w_qkv.astype(BF16)


def _pad_lanes(t, left, width=LANES, fill=0.0):
    n, w = t.shape
    return jnp.concatenate([jnp.full((n, left), fill, F32), t, jnp.zeros((n, width - left - w), F32)], axis=1)


def kernel(x, c, ctx, c_ctx, mod_w, mod_b, norm1_g, norm2_g, ab_w_in, conv_w, conv_b, mla_q_norm_g, mla_w_uq,
           mla_kv_norm_g, mla_w_ukv, ab_w_out, swa_w_qkv, swa_sink, swa_w_out, router_w, exp_w_gate, exp_w_up,
           exp_w_down, final_g):
    bsz, n, d = x.shape
    lc = ctx.shape[1]
    depth = mod_w.shape[0]
    assert bsz <= 2 and d == 1024

    cs = jnp.concatenate([c, c_ctx[None, :], jnp.zeros((8 - bsz - 1, d), F32)], axis=0)
    mod = modulation(cs, mod_w, mod_b)

    cos_m, sin_m = _rope_tables(n, MLA_ROPE)
    cos_mla = _pad_lanes(cos_m, 64, fill=1.0)
    sin_mla = _pad_lanes(sin_m, 64)
    cos_mla_c = jnp.concatenate([jnp.ones((lc, 96), F32), jnp.zeros((lc, 32), F32)], axis=1)
    zeros_c = jnp.zeros((lc, 128), F32)
    cos_s, sin_s = _rope_tables(n, SWA_HEAD_DIM)
    cos_swa = jnp.concatenate([cos_s, cos_s], axis=1)
    sin_swa = jnp.concatenate([sin_s, sin_s], axis=1)
    ones_c = jnp.ones((lc, 128), F32)

    row2 = lambda v: v.reshape(1, -1)
    xs, xc = x, ctx
    for layer in range(depth):
        need_ctx = layer < depth - 1
        last = layer == depth - 1
        mod_l = mod[layer]
        g1 = row2(norm1_g[layer])
        g2 = row2(norm2_g[layer])
        rw_f = jnp.concatenate([router_w[layer], jnp.zeros((d, LANES - N_EXPERTS), F32)], axis=1)
        rw_hi = rw_f.astype(BF16)
        rw = jnp.concatenate([rw_hi, (rw_f - rw_hi.astype(F32)).astype(BF16)], axis=1)
        wg, wu, wd = exp_w_gate, exp_w_up, exp_w_down
        if layer % 2 == 0:
            e = layer // 2
            win2, wq2, wkv2, wc, wa = _prep_ab(ab_w_in[e], mla_w_uq[e], mla_w_ukv[e], ab_w_out[e])
            qg, kvg = row2(mla_q_norm_g[e]), row2(mla_kv_norm_g[e])
            cw, cb = conv_w[e], row2(conv_b[e])
            gb, z, q, k, v = ab_in(xs, mod_l, g1, win2, qg, wq2, kvg, wkv2, cos_mla, sin_mla, ctx=False, tm=512)
            gbc, zc, qc, kc, vc = ab_in(xc, mod_l, g1, win2, qg, wq2, kvg, wkv2, cos_mla_c, zeros_c, ctx=True, tm=256)
            att = mla_attention_flat(q, k, v, kc, vc, tq=512, tk_max=1280, spt=8)
            xs1, h2, aff = mix_out(att, wa, xs, mod_l, g2, rw, (gb, z, cw, cb, wc), ctx=False, tm=512)
            if need_ctx:
                att_c = mla_attention_ctx(qc, kc, vc)
                xc1, hc2, affc = mix_out(att_c, wa, xc, mod_l, g2, rw, (gbc, zc, cw, cb, wc), ctx=True, tm=256)
        else:
            o = layer // 2
            w2 = _prep_swa(swa_w_qkv[o])
            wo = swa_w_out[o].astype(BF16)
            q, k2, v2 = swa_in(xs, mod_l, g1, w2, cos_swa, sin_swa, ctx=False, tm=512)
            qc, kc2, vc2 = swa_in(xc, mod_l, g1, w2, ones_c, zeros_c, ctx=True, tm=256)
            att = swa_attention(q, k2, v2, kc2, vc2, swa_sink[o], tq=512)
            xs1, h2, aff = mix_out(att, wo, xs, mod_l, g2, rw, ctx=False, tm=512)
            if need_ctx:
                raise NotImplementedError("context self-attention for windowed layers below the last")
        xs = moe(xs1, h2, aff, mod_l, wg, wu, wd, layer, final_g=row2(final_g) if last else None, ctx=False)
        if need_ctx:
            xc = moe(xc1, hc2, affc, mod_l, wg, wu, wd, layer, ctx=True)
    return xs
```

```python
import functools
import math

import jax
import jax.numpy as jnp
import numpy as np
from jax import lax
from jax.experimental import pallas as pl
from jax.experimental.pallas import tpu as pltpu

F32 = jnp.float32
BF16 = jnp.bfloat16
I32 = jnp.int32

GRID_W = 64
EPS = 1e-6
ROPE_THETA = 10000.0
CONV_DIM = 512
MLA_HEADS = 8
MLA_Q_LORA = 256
MLA_KV_LORA = 128
MLA_NOPE = 64
MLA_ROPE = 32
MLA_V = 64
SWA_HEADS = 16
SWA_KV_HEADS = 4
SWA_HEAD_DIM = 64
WINDOW = 128
SWA_BLOCK = 128
N_EXPERTS = 16
EC_CAPACITY = 2
N_MOD = 6

LANES = 128
ROW_ALIGN = 16
VMEM_LIMIT = 56 * 1024 * 1024


def _cparams(sem, vmem=VMEM_LIMIT):
    return pltpu.CompilerParams(dimension_semantics=sem, vmem_limit_bytes=vmem)


def _dot(a, b):
    return jnp.dot(a, b, preferred_element_type=F32)


def _dot_nt(a, b):
    return lax.dot_general(a, b, (((1,), (1,)), ((), ())), preferred_element_type=F32)


def _dot_tn(a, b):
    return lax.dot_general(a, b, (((0,), (0,)), ((), ())), preferred_element_type=F32)


def _dot_hi(a, b):
    return jnp.dot(a, b, preferred_element_type=F32, precision=lax.Precision.HIGHEST)


def _silu(x):
    return x * (1.0 / (1.0 + jnp.exp(-x)))


def _rms_mod(x, g, shift, scale):
    y = x * lax.rsqrt(jnp.mean(x * x, axis=-1, keepdims=True) + EPS)
    return (y * g) * (1.0 + scale) + shift


def _mod_row(mod_ref, ctx, which):
    r = 2 if ctx else pl.program_id(0)
    return mod_ref[pl.ds(r, 1), which * 1024:(which + 1) * 1024]


def _mod_kernel(cs_ref, w_ref, b_ref, o_ref):
    s = _silu(cs_ref[...])
    o_ref[0] = _dot_hi(s, w_ref[0]) + b_ref[0]


def modulation(cs, mod_w, mod_b):
    depth, d, n6 = mod_w.shape
    tn = 1536
    return pl.pallas_call(
        _mod_kernel,
        grid=(depth, n6 // tn),
        in_specs=[
            pl.BlockSpec((8, d), lambda l, j: (0, 0)),
            pl.BlockSpec((1, d, tn), lambda l, j: (l, 0, j)),
            pl.BlockSpec((1, 1, tn), lambda l, j: (l, 0, j)),
        ],
        out_specs=pl.BlockSpec((1, 8, tn), lambda l, j: (l, 0, j)),
        out_shape=jax.ShapeDtypeStruct((depth, 8, n6), F32),
        compiler_params=_cparams(("arbitrary", "arbitrary")),
        name="modulation",
    )(cs, mod_w, mod_b.reshape(depth, 1, n6))


def _rope_tables(n, dim):
    n_rows = n // GRID_W
    half = dim // 2
    qd = dim // 4
    freqs = ROPE_THETA ** (-jnp.arange(0, half, 2, dtype=F32) / half)
    ang_r = jnp.arange(n_rows, dtype=F32)[:, None] * freqs
    ang_c = jnp.arange(GRID_W, dtype=F32)[:, None] * freqs

    def table(fr, fc, sign):
        r = jnp.broadcast_to(fr[:, None, :], (n_rows, GRID_W, qd))
        c = jnp.broadcast_to(fc[None, :, :], (n_rows, GRID_W, qd))
        return jnp.concatenate([sign * r, r, sign * c, c], axis=-1).reshape(n, dim)

    return table(jnp.cos(ang_r), jnp.cos(ang_c), 1.0), table(jnp.sin(ang_r), jnp.sin(ang_c), -1.0)


def _rope_perm(dim):
    qd = dim // 4
    ch = np.arange(dim)
    pair = (ch // qd) % 2
    return np.where(pair == 0, ch + qd, ch - qd)


def _ab_in_kernel(x_ref, mod_ref, g_ref, win_ref, qg_ref, wq_ref, kvg_ref, wkv_ref, cos_ref, sin_ref,
                  gb_ref, z_ref, q_ref, k_ref, v_ref, *, ctx, scale):
    x = x_ref[0]
    h = _rms_mod(x, g_ref[...], _mod_row(mod_ref, ctx, 0), _mod_row(mod_ref, ctx, 1))
    p = _dot(h.astype(BF16), win_ref[...])
    gb_ref[0] = p[:, 0:512].astype(BF16)
    z_ref[0] = (p[:, 512:1024] * p[:, 1024:1536]).astype(BF16)
    ql = p[:, 1536:1792]
    kvl = p[:, 1792:1920]
    kra = p[:, 1920:2048]
    krb = p[:, 2048:2176]
    cos = cos_ref[...]
    sin = sin_ref[...]
    qn = ql * lax.rsqrt(jnp.mean(ql * ql, axis=-1, keepdims=True) + EPS) * qg_ref[...]
    qq = _dot(qn.astype(BF16), wq_ref[...])
    kvn = kvl * lax.rsqrt(jnp.mean(kvl * kvl, axis=-1, keepdims=True) + EPS) * kvg_ref[...]
    kv = _dot(kvn.astype(BF16), wkv_ref[...])
    krope = kra * cos + krb * sin
    cs = cos * scale
    ss = sin * scale
    ones_lane = lax.broadcasted_iota(I32, cos.shape, 1) == MLA_V
    for hd in range(MLA_HEADS):
        a = qq[:, hd * 128:(hd + 1) * 128]
        b = qq[:, 1024 + hd * 128:1024 + (hd + 1) * 128]
        q_ref[0, hd] = (a * cs + b * ss).astype(BF16)
        k_ref[0, hd] = (kv[:, hd * 128:(hd + 1) * 128] + krope).astype(BF16)
        v_ref[0, hd] = jnp.where(ones_lane, 1.0, kv[:, 1024 + hd * 128:1024 + (hd + 1) * 128]).astype(BF16)


def ab_in(x, mod_l, g, win2, qg, wq2, kvg, wkv2, cos_t, sin_t, *, ctx, tm):
    bsz, n, d = x.shape
    tm = min(tm, n)
    scale = (MLA_NOPE + MLA_ROPE) ** -0.5 * math.log2(math.e)
    full = lambda a: pl.BlockSpec(a.shape, lambda b, i: (0,) * a.ndim)
    hshape = jax.ShapeDtypeStruct((bsz, MLA_HEADS, n, 128), BF16)
    hspec = pl.BlockSpec((1, MLA_HEADS, tm, 128), lambda b, i: (b, 0, i, 0))
    return pl.pallas_call(
        functools.partial(_ab_in_kernel, ctx=ctx, scale=scale),
        grid=(bsz, n // tm),
        in_specs=[
            pl.BlockSpec((1, tm, d), lambda b, i: (b, i, 0)),
            full(mod_l), full(g), full(win2), full(qg), full(wq2), full(kvg), full(wkv2),
            pl.BlockSpec((tm, 128), lambda b, i: (i, 0)),
            pl.BlockSpec((tm, 128), lambda b, i: (i, 0)),
        ],
        out_specs=[
            pl.BlockSpec((1, tm, 512), lambda b, i: (b, i, 0)),
            pl.BlockSpec((1, tm, 512), lambda b, i: (b, i, 0)),
            hspec, hspec, hspec,
        ],
        out_shape=[
            jax.ShapeDtypeStruct((bsz, n, 512), BF16),
            jax.ShapeDtypeStruct((bsz, n, 512), BF16),
            hshape, hshape, hshape,
        ],
        compiler_params=_cparams(("arbitrary", "arbitrary")),
        name="ab_in_ctx" if ctx else "ab_in",
    )(x, mod_l, g, win2, qg, wq2, kvg, wkv2, cos_t, sin_t)


def _mla_ctx_kernel(q_ref, kc_ref, vc_ref, o_ref):
    s = _dot_nt(q_ref[0, 0], kc_ref[0, 0])
    m = jnp.max(s, axis=-1, keepdims=True)
    acc = _dot(jnp.exp2(s - m).astype(BF16), vc_ref[0, 0])
    o_ref[0] = (acc * (1.0 / acc[:, MLA_V:MLA_V + 1])).astype(BF16)


def mla_attention_ctx(q, kc, vc):
    bsz, nh, lc, dk = q.shape
    kv = pl.BlockSpec((1, 1, lc, dk), lambda b, h: (b, h, 0, 0))
    return pl.pallas_call(
        _mla_ctx_kernel,
        grid=(bsz, nh),
        in_specs=[kv, kv, kv],
        out_specs=pl.BlockSpec((1, lc, dk), lambda b, h: (b, 0, h)),
        out_shape=jax.ShapeDtypeStruct((bsz, lc, nh * dk), BF16),
        compiler_params=_cparams(("arbitrary", "arbitrary")),
        name="mla_attention_ctx",
    )(q, kc, vc)


def _mla_flat_kernel(q_ref, k_ref, v_ref, kc_ref, vc_ref, o_ref, kall, vall, s_a, s_b, p_a, p_b,
                     *, tq, tk, n_tiles, n_chunks, spt):
    n_stages = n_tiles * n_chunks
    n_lat, n_ctx = k_ref.shape[2], kc_ref.shape[2]
    cp = math.gcd(n_lat, 2048)

    def copy_rows(c, carry):
        sl = pl.ds(pl.multiple_of(c * cp, cp), cp)
        kall[sl, :] = k_ref[0, 0, sl, :]
        vall[sl, :] = v_ref[0, 0, sl, :]
        return carry

    lax.fori_loop(0, n_lat // cp, copy_rows, 0)
    kall[n_lat:n_lat + n_ctx, :] = kc_ref[0, 0]
    vall[n_lat:n_lat + n_ctx, :] = vc_ref[0, 0]

    def nxt(c):
        i, j = c
        wrap = j + 1 == n_chunks
        return jnp.where(wrap, i + 1, i), jnp.where(wrap, 0, j + 1)

    def rows(idx, size):
        return pl.ds(pl.multiple_of(idx * size, size), size)

    def scores(c, s_buf):
        s = _dot_nt(q_ref[0, 0, rows(c[0], tq), :], kall[rows(c[1], tk), :])
        s_buf[...] = s
        return jnp.max(s, axis=-1, keepdims=True)

    def weights(c, s_buf, p_buf, mx, m):
        m_prev = jnp.where(c[1] == 0, -1e30, m)
        m_new = jnp.maximum(m_prev, mx)
        p_buf[...] = jnp.exp2(s_buf[...] - m_new).astype(BF16)
        return m_new, jnp.exp2(m_prev - m_new)

    def wsum(c, p_buf, alpha, acc):
        acc = alpha * acc + _dot(p_buf[...], vall[rows(c[1], tk), :])
        o_ref[0, pl.ds(pl.multiple_of(c[0] * tq, tq), tq), :] = (
            acc * (1.0 / acc[:, MLA_V:MLA_V + 1])).astype(BF16)
        return acc

    def trip(c_pv, c_w, c_s, s_cur, s_nxt, p_prev, p_cur, mx_cur, alpha_prev, m, acc):
        mx_nxt = scores(c_s, s_nxt) if c_s is not None else None
        acc = wsum(c_pv, p_prev, alpha_prev, acc)
        m, alpha_cur = weights(c_w, s_cur, p_cur, mx_cur, m)
        return mx_nxt, alpha_cur, m, acc

    def pair(c, mx_b, alpha_a, m, acc):
        mx_a, alpha_b, m, acc = trip(c[0], c[1], c[2], s_b, s_a, p_a, p_b, mx_b, alpha_a, m, acc)
        mx_b, alpha_a, m, acc = trip(c[1], c[2], c[3], s_a, s_b, p_b, p_a, mx_a, alpha_b, m, acc)
        return mx_b, alpha_a, m, acc

    def chain(c, k):
        out = [c]
        for _ in range(k):
            out.append(nxt(out[-1]))
        return out

    zero = jnp.int32(0)
    c0 = (zero, zero)
    mx_a = scores(c0, s_a)
    m, alpha_a = weights(c0, s_a, p_a, mx_a, jnp.zeros((tq, 1), F32))
    mx_b = scores(nxt(c0), s_b)
    acc = jnp.zeros((tq, LANES), F32)

    def run(n_st, carry):
        i, j, mx_b, alpha_a, m, acc = carry
        c = chain((i, j), n_st + 1)
        for u in range(0, n_st, 2):
            mx_b, alpha_a, m, acc = pair(c[u:u + 4], mx_b, alpha_a, m, acc)
        return c[n_st][0], c[n_st][1], mx_b, alpha_a, m, acc

    carry = run((n_stages - 4) % spt, (zero, zero, mx_b, alpha_a, m, acc))
    i, j, mx_b, alpha_a, m, acc = lax.fori_loop(0, (n_stages - 4) // spt, lambda _, c: run(spt, c), carry)
    c = chain((i, j), 3)
    mx_b, alpha_a, m, acc = pair(c[0:4], mx_b, alpha_a, m, acc)
    mx_a, alpha_b, m, acc = trip(c[2], c[3], None, s_b, s_a, p_a, p_b, mx_b, alpha_a, m, acc)
    wsum(c[3], p_b, alpha_b, acc)


def mla_attention_flat(q, k, v, kc, vc, *, tq, tk_max, spt):
    bsz, nh, n, dk = q.shape
    lc = kc.shape[2]
    nk = n + lc
    tq = min(tq, n)
    tk = max(t for t in range(256, tk_max + 1, 256) if nk % t == 0)
    n_tiles, n_chunks = n // tq, nk // tk
    assert spt % 2 == 0 and (n_tiles * n_chunks) % 2 == 0 and n_tiles * n_chunks >= 4
    head = lambda rows_: pl.BlockSpec((1, 1, rows_, dk), lambda b, h: (b, h, 0, 0))
    return pl.pallas_call(
        functools.partial(_mla_flat_kernel, tq=tq, tk=tk, n_tiles=n_tiles, n_chunks=n_chunks, spt=spt),
        grid=(bsz, nh),
        in_specs=[head(n), head(n), head(n), head(lc), head(lc)],
        out_specs=pl.BlockSpec((1, n, dk), lambda b, h: (b, 0, h)),
        out_shape=jax.ShapeDtypeStruct((bsz, n, nh * dk), BF16),
        scratch_shapes=([pltpu.VMEM((nk, dk), BF16)] * 2 + [pltpu.VMEM((tq, tk), F32)] * 2
                        + [pltpu.VMEM((tq, tk), BF16)] * 2),
        compiler_params=_cparams(("arbitrary", "arbitrary")),
        name="mla_attention",
    )(q, k, v, kc, vc)


def _mix_out_kernel(*refs, ctx, has_conv, tm, n_tiles):
    if has_conv:
        (gb_ref, z_ref, zp_ref, zn_ref, cw_ref, cb_ref, wc_ref,
         att_ref, wa_ref, x_ref, mod_ref, g2_ref, rw_ref, xs_ref, h2_ref, aff_ref) = refs
    else:
        att_ref, wa_ref, x_ref, mod_ref, g2_ref, rw_ref, xs_ref, h2_ref, aff_ref = refs
    y = _dot(att_ref[0], wa_ref[...])
    if has_conv:
        i = pl.program_id(1)
        z = z_ref[0].astype(F32)
        rows = lax.broadcasted_iota(I32, z.shape, 0)
        zprev_halo = jnp.where(i > 0, zp_ref[0, ROW_ALIGN - 1:ROW_ALIGN, :].astype(F32), 0.0)
        znext_halo = jnp.where(i < n_tiles - 1, zn_ref[0, 0:1, :].astype(F32), 0.0)
        zprev = jnp.where(rows == 0, zprev_halo, pltpu.roll(z, 1, 0))
        znext = jnp.where(rows == tm - 1, znext_halo, pltpu.roll(z, tm - 1, 0))
        cw = cw_ref[...]
        conv = gb_ref[0].astype(F32) * (zprev * cw[0:1] + z * cw[1:2] + znext * cw[2:3] + cb_ref[...])
        y = y + _dot(conv.astype(BF16), wc_ref[...])
    xs = x_ref[0] + _mod_row(mod_ref, ctx, 2) * y
    xs_ref[0] = xs
    h2 = _rms_mod(xs, g2_ref[...], _mod_row(mod_ref, ctx, 3), _mod_row(mod_ref, ctx, 4))
    h2_ref[0] = h2.astype(BF16)
    h2_hi = h2.astype(BF16)
    h2_lo = (h2 - h2_hi.astype(F32)).astype(BF16)
    hh = _dot(h2_hi, rw_ref[...])
    logits = hh[:, :LANES] + hh[:, LANES:] + _dot(h2_lo, rw_ref[:, :LANES])
    lane = lax.broadcasted_iota(I32, logits.shape, 1)
    logits = jnp.where(lane < N_EXPERTS, logits, -1e30)
    mx = jnp.max(logits, axis=-1, keepdims=True)
    ex = jnp.exp(logits - mx)
    aff = ex / jnp.sum(ex, axis=-1, keepdims=True)
    aff_ref[0] = aff[:, :N_EXPERTS]


def mix_out(att, wa, x, mod_l, g2, rw, conv=None, *, ctx, tm):
    bsz, n, d = x.shape
    tm = min(tm, n)
    n_tiles = n // tm
    full = lambda a: pl.BlockSpec(a.shape, lambda b, i: (0,) * a.ndim)
    row = lambda w: pl.BlockSpec((1, tm, w), lambda b, i: (b, i, 0))
    ins, in_specs = [], []
    if conv is not None:
        gb, z, cw, cb, wc = conv
        r8 = tm // ROW_ALIGN
        nb8 = n // ROW_ALIGN
        ins += [gb, z, z, z, cw, cb, wc]
        in_specs += [
            row(512), row(512),
            pl.BlockSpec((1, ROW_ALIGN, 512), lambda b, i: (b, jnp.maximum(i * r8 - 1, 0), 0)),
            pl.BlockSpec((1, ROW_ALIGN, 512), lambda b, i: (b, jnp.minimum((i + 1) * r8, nb8 - 1), 0)),
            full(cw), full(cb), full(wc),
        ]
    ins += [att, wa, x, mod_l, g2, rw]
    in_specs += [row(att.shape[-1]), full(wa), row(d), full(mod_l), full(g2), full(rw)]
    return pl.pallas_call(
        functools.partial(_mix_out_kernel, ctx=ctx, has_conv=conv is not None, tm=tm, n_tiles=n_tiles),
        grid=(bsz, n_tiles),
        in_specs=in_specs,
        out_specs=[row(d), row(d), row(N_EXPERTS)],
        out_shape=[
            jax.ShapeDtypeStruct((bsz, n, d), F32),
            jax.ShapeDtypeStruct((bsz, n, d), BF16),
            jax.ShapeDtypeStruct((bsz, n, N_EXPERTS), F32),
        ],
        compiler_params=_cparams(("arbitrary", "arbitrary")),
        name=("mix_out_ctx" if ctx else "mix_out") + ("_conv" if conv is not None else ""),
    )(*ins)


def _route_kernel(aff_ref, aff8_ref, ls_ref, us_ref, eye_ref, rank_ref, rankt_ref, oa_ref, nwd_ref, nwc_ref,
                  thr_s, need_s, eq_s, oa_s, *, n, cap, tb, win_d, win_c, chunk):
    n_chunks = (n // 8) // chunk
    nblk = n // tb
    k = pl.program_id(1)

    def count(pred_fn):
        def body(c, acc):
            kb = pltpu.bitcast(aff8_ref[0, pl.ds(pl.multiple_of(c * chunk, chunk), chunk), :], I32)
            return acc + jnp.sum(pred_fn(kb).astype(I32), axis=0, keepdims=True)
        acc = lax.fori_loop(0, n_chunks, body, jnp.zeros((1, LANES), I32))
        for sh in (64, 32, 16):
            acc = acc + pltpu.roll(acc, sh, 1)
        return acc

    @pl.when(k == 0)
    def _():
        def bit_body(i, thr):
            cand = thr | jnp.left_shift(jnp.int32(1), 30 - i)
            cnt = count(lambda kb: kb >= cand)
            return jnp.where(cnt >= cap, cand, thr)

        thr = lax.fori_loop(0, 31, bit_body, jnp.zeros((1, LANES), I32))
        thr_s[...] = thr[:, :N_EXPERTS]
        need_s[...] = (cap - count(lambda kb: kb > thr))[:, :N_EXPERTS].astype(F32)
        eq_s[...] = jnp.zeros((1, N_EXPERTS), F32)
        oa_s[...] = jnp.zeros((1, N_EXPERTS), F32)

    thr = thr_s[...]
    off = pl.multiple_of(k * tb, tb)
    kb = pltpu.bitcast(aff_ref[0, pl.ds(off, tb), :], I32)
    gt = kb > thr
    eq = kb == thr
    eqf = jnp.where(eq, 1.0, 0.0)
    eqrank = eq_s[...] + _dot(ls_ref[...], eqf.astype(BF16))
    sel = jnp.logical_or(gt, jnp.logical_and(eq, eqrank < need_s[...]))
    self_ = jnp.where(sel, 1.0, 0.0)
    selb = self_.astype(BF16)
    lrank = _dot(ls_ref[...], selb)
    gate_bits = pltpu.bitcast(aff_ref[0, pl.ds(off, tb), :].astype(BF16).astype(F32), I32)
    rank_ref[0] = gate_bits | jnp.where(sel, lrank.astype(I32) + 1, 0)
    lrank_t = _dot_tn(selb, us_ref[...])
    sel_t = _dot_tn(selb, eye_ref[...])
    rankt_ref[0] = jnp.where(sel_t > 0.5, lrank_t, -1.0)
    c = jnp.sum(self_, axis=0, keepdims=True)
    oa_run = oa_s[...]
    oa_ref[0, pl.ds(k, 1), :] = oa_run.astype(I32)
    nwd_ref[0, pl.ds(k, 1), :] = jnp.ceil(c * (1.0 / win_d)).astype(I32)
    nwc_ref[0, pl.ds(k, 1), :] = jnp.ceil(c * (1.0 / win_c)).astype(I32)
    oa_new = oa_run + jnp.ceil(c * (1.0 / ROW_ALIGN)) * ROW_ALIGN
    oa_s[...] = oa_new
    eq_s[...] = eq_s[...] + jnp.sum(eqf, axis=0, keepdims=True)

    @pl.when(k == nblk - 1)
    def _():
        oa_ref[0, pl.ds(nblk, 1), :] = oa_new.astype(I32)


def route(aff, *, tb, win_d, win_c):
    bsz, n, ne = aff.shape
    cap = max(1, (EC_CAPACITY * n) // ne)
    nblk = n // tb
    assert ne * 8 == LANES
    aff8 = aff.reshape(bsz, n // 8, LANES)
    chunk = min(256, n // 8)
    ii = np.arange(tb)
    ls = jnp.asarray(ii[None, :] < ii[:, None], BF16)
    us = jnp.asarray(ii[:, None] < ii[None, :], BF16)
    eye = jnp.asarray(ii[:, None] == ii[None, :], BF16)
    full = lambda a: pl.BlockSpec(a.shape, lambda b, k: (0,) * a.ndim)
    return pl.pallas_call(
        functools.partial(_route_kernel, n=n, cap=cap, tb=tb, win_d=win_d, win_c=win_c, chunk=chunk),
        grid=(bsz, nblk),
        in_specs=[pl.BlockSpec((1, n, ne), lambda b, k: (b, 0, 0)),
                  pl.BlockSpec((1, n // 8, LANES), lambda b, k: (b, 0, 0)), full(ls), full(us), full(eye)],
        out_specs=[
            pl.BlockSpec((1, tb, ne), lambda b, k: (b, k, 0)),
            pl.BlockSpec((1, ne, tb), lambda b, k: (b, 0, k)),
            pl.BlockSpec((1, nblk + 1, ne), lambda b, k: (b, 0, 0)),
            pl.BlockSpec((1, nblk, ne), lambda b, k: (b, 0, 0)),
            pl.BlockSpec((1, nblk, ne), lambda b, k: (b, 0, 0)),
        ],
        out_shape=[
            jax.ShapeDtypeStruct((bsz, n, ne), I32),
            jax.ShapeDtypeStruct((bsz, ne, n), F32),
            jax.ShapeDtypeStruct((bsz, nblk + 1, ne), I32),
            jax.ShapeDtypeStruct((bsz, nblk, ne), I32),
            jax.ShapeDtypeStruct((bsz, nblk, ne), I32),
        ],
        scratch_shapes=[pltpu.VMEM((1, ne), I32), pltpu.VMEM((1, ne), F32),
                        pltpu.VMEM((1, ne), F32), pltpu.VMEM((1, ne), F32)],
        compiler_params=_cparams(("arbitrary", "arbitrary")),
        name="route",
    )(aff, aff8, ls, us, eye)


def _dispatch_kernel(oa_ref, nw_ref, rankt_ref, h_ref, *rest, nblk, n_steps, win, tb, group, tail, base):
    xg_ref, stack, sem = rest[-3:]
    b = pl.program_id(0)
    k = pl.program_id(1)
    step = b * nblk + k
    slot = step % 2
    ne = N_EXPERTS

    def fill(sl, j):
        h = h_ref[0]
        for g0 in range(0, ne, group):
            pieces = []
            for e in range(g0, g0 + group):
                r = rankt_ref[0, e:e + 1, :]
                srow = lax.broadcasted_iota(I32, (win, tb), 0).astype(F32) + (j * win).astype(F32)
                pieces.append((r == srow).astype(BF16))
            oh_t = jnp.concatenate(pieces, axis=0)
            stack[sl, g0 * win:(g0 + group) * win, :] = _dot(oh_t, h).astype(BF16)

    def copies(sl, bb, kk, j):
        out = []
        for e in range(ne):
            off = pl.multiple_of(base + oa_ref[(bb * (nblk + 1) + kk) * ne + e] + j * win, ROW_ALIGN)
            out.append(pltpu.make_async_copy(
                stack.at[sl, pl.ds(e * win, win), :],
                xg_ref.at[bb, e, pl.ds(off, win), :],
                sem.at[sl]))
        return out

    @pl.when(k == 0)
    def _():
        stack[2, 0:win, :] = jnp.zeros((win, stack.shape[2]), BF16)
        cs = [pltpu.make_async_copy(stack.at[2, pl.ds(0, sz), :], xg_ref.at[b, e, pl.ds(base + off, sz), :],
                                    sem.at[2])
              for e in range(ne) for off, sz in tail]
        for c in cs:
            c.start()
        for c in cs:
            c.wait()

    fill(slot, jnp.int32(0))

    @pl.when(step > 0)
    def _():
        for c in copies(1 - slot, b, k, 0):
            c.wait()

    for c in copies(slot, b, k, 0):
        c.start()

    nws = [nw_ref[(b * nblk + k) * ne + e] for e in range(ne)]
    nwx = functools.reduce(jnp.maximum, nws)

    def extra(j, carry):
        fill(2, j)
        cs = copies(2, b, k, j)
        for e, c in enumerate(cs):
            pl.when(j < nws[e])(c.start)
        for e, c in enumerate(cs):
            pl.when(j < nws[e])(c.wait)
        return carry

    lax.fori_loop(1, jnp.maximum(nwx, 1), extra, 0)

    @pl.when(step == n_steps - 1)
    def _():
        for c in copies(slot, b, k, 0):
            c.wait()


def dispatch(oa, nw, rank_t, h2, *, tb, win, rows, tail, base=0, into=None):
    bsz, n, d = h2.shape
    ne = N_EXPERTS
    nblk = n // tb
    group = 4
    ins = [oa.reshape(-1), nw.reshape(-1), rank_t, h2]
    in_specs = [
        pl.BlockSpec((1, ne, tb), lambda b, k, *_: (b, 0, k)),
        pl.BlockSpec((1, tb, d), lambda b, k, *_: (b, k, 0)),
    ]
    aliases = {}
    if into is not None:
        assert into.shape == (bsz, ne, rows, d)
        aliases = {len(ins): 0}
        ins.append(into)
        in_specs.append(pl.BlockSpec(memory_space=pl.ANY))
    grid_spec = pltpu.PrefetchScalarGridSpec(
        num_scalar_prefetch=2,
        grid=(bsz, nblk),
        in_specs=in_specs,
        out_specs=pl.BlockSpec(memory_space=pl.ANY),
        scratch_shapes=[
            pltpu.VMEM((3, ne * win, d), BF16),
            pltpu.SemaphoreType.DMA((3,)),
        ],
    )
    return pl.pallas_call(
        functools.partial(_dispatch_kernel, nblk=nblk, n_steps=bsz * nblk, win=win, tb=tb, group=group, tail=tail,
                          base=base),
        grid_spec=grid_spec,
        out_shape=jax.ShapeDtypeStruct((bsz, ne, rows, d), BF16),
        input_output_aliases=aliases,
        compiler_params=_cparams(("arbitrary", "arbitrary")),
        name="dispatch",
    )(*ins)


def _ffn_kernel(*refs, tm, sets):
    tot_refs = refs[:len(sets)]
    x_ref, wg_ref, wu_ref, wd_ref, y_ref, wgb, wub, wdb = refs[len(sets):]
    e = pl.program_id(0)
    b = pl.program_id(1)
    i = pl.program_id(2)

    @pl.when(jnp.logical_and(b == 0, i == 0))
    def _():
        wgb[...] = wg_ref[0, 0].astype(BF16)
        wub[...] = wu_ref[0, 0].astype(BF16)
        wdb[...] = wd_ref[0, 0].astype(BF16)

    live = jnp.bool_(False)
    for tot_ref, (tile0, nblk) in zip(tot_refs, sets):
        total = tot_ref[(b * (nblk + 1) + nblk) * N_EXPERTS + e]
        live = jnp.logical_or(live, jnp.logical_and(i >= tile0, (i - tile0) * tm < total))

    @pl.when(live)
    def _():
        x = x_ref[0, 0]
        hid = (_silu(_dot(x, wgb[...])) * _dot(x, wub[...])).astype(BF16)
        y_ref[0, 0] = _dot(hid, wdb[...]).astype(BF16)

    @pl.when(jnp.logical_not(live))
    def _():
        y_ref[0, 0] = jnp.zeros(y_ref.shape[2:], BF16)


def expert_ffn(oas, xg, w_gate, w_up, w_down, layer, *, tm, sets):
    bsz, ne, c_pad, d = xg.shape
    f = w_gate.shape[-1]
    grid_spec = pltpu.PrefetchScalarGridSpec(
        num_scalar_prefetch=len(oas),
        grid=(ne, bsz, c_pad // tm),
        in_specs=[
            pl.BlockSpec((1, 1, tm, d), lambda e, b, i, *_: (b, e, i, 0)),
            pl.BlockSpec((1, 1, d, f), lambda e, b, i, *_: (layer, e, 0, 0)),
            pl.BlockSpec((1, 1, d, f), lambda e, b, i, *_: (layer, e, 0, 0)),
            pl.BlockSpec((1, 1, f, d), lambda e, b, i, *_: (layer, e, 0, 0)),
        ],
        out_specs=pl.BlockSpec((1, 1, tm, d), lambda e, b, i, *_: (b, e, i, 0)),
        scratch_shapes=[pltpu.VMEM((d, f), BF16), pltpu.VMEM((d, f), BF16), pltpu.VMEM((f, d), BF16)],
    )
    return pl.pallas_call(
        functools.partial(_ffn_kernel, tm=tm, sets=sets),
        grid_spec=grid_spec,
        out_shape=jax.ShapeDtypeStruct(xg.shape, BF16),
        compiler_params=_cparams(("arbitrary", "arbitrary", "arbitrary")),
        name="expert_ffn",
    )(*[oa.reshape(-1) for oa in oas], xg, w_gate, w_up, w_down)


def _combine_kernel(oa_ref, nw_ref, rank_ref, xs_ref, mod_ref, *rest,
                    ctx, nblk, n_steps, win, tb, final, base):
    if final:
        fg_ref, ye_ref, o_ref, stack, sem = rest
    else:
        ye_ref, o_ref, stack, sem = rest
    b = pl.program_id(0)
    k = pl.program_id(1)
    step = b * nblk + k
    slot = step % 2
    ne = N_EXPERTS

    def copies(sl, bb, kk, j):
        out = []
        for e in range(ne):
            off = pl.multiple_of(base + oa_ref[(bb * (nblk + 1) + kk) * ne + e] + j * win, ROW_ALIGN)
            out.append(pltpu.make_async_copy(
                ye_ref.at[bb, e, pl.ds(off, win), :],
                stack.at[sl, pl.ds(e * win, win), :],
                sem.at[sl]))
        return out

    @pl.when(step == 0)
    def _():
        stack[2] = jnp.zeros(stack.shape[1:], BF16)
        for c in copies(0, b, k, 0):
            c.start()

    @pl.when(step + 1 < n_steps)
    def _():
        nxt = step + 1
        for c in copies(1 - slot, nxt // nblk, nxt % nblk, 0):
            c.start()

    for c in copies(slot, b, k, 0):
        c.wait()

    def weighted_sum(j, sl):
        word = rank_ref[0]
        slot1 = lax.broadcasted_iota(I32, (tb, win), 1) + (j * win + 1)
        y = None
        for e0 in range(0, ne, 2):
            pieces = []
            for e in (e0, e0 + 1):
                wb = jnp.broadcast_to(word[:, e:e + 1], (tb, win))
                gate = pltpu.bitcast(wb & jnp.int32(-65536), F32)
                pieces.append(jnp.where((wb & 0xFFFF) == slot1, gate, 0.0).astype(BF16))
            part = _dot(jnp.concatenate(pieces, axis=1), stack[sl, e0 * win:(e0 + 2) * win, :])
            y = part if y is None else y + part
        return y

    y = weighted_sum(jnp.int32(0), slot)

    nws = [nw_ref[(b * nblk + k) * ne + e] for e in range(ne)]
    nwx = functools.reduce(jnp.maximum, nws)

    def extra(j, y):
        cs = copies(2, b, k, j)
        for e, c in enumerate(cs):
            pl.when(j < nws[e])(c.start)
        for e, c in enumerate(cs):
            pl.when(j < nws[e])(c.wait)
        return y + weighted_sum(j, 2)

    y = lax.fori_loop(1, jnp.maximum(nwx, 1), extra, y)
    out = xs_ref[0] + _mod_row(mod_ref, ctx, 5) * y
    if final:
        out = out * lax.rsqrt(jnp.mean(out * out, axis=-1, keepdims=True) + EPS) * fg_ref[...]
    o_ref[0] = out


def combine(oa, nw, rank, xs, mod_l, ye, final_g=None, *, ctx, tb, win, base=0):
    bsz, n, d = xs.shape
    ne = N_EXPERTS
    nblk = n // tb
    final = final_g is not None
    full = lambda a: pl.BlockSpec(a.shape, lambda b, k, *_: (0,) * a.ndim)
    ins = [rank, xs, mod_l]
    in_specs = [
        pl.BlockSpec((1, tb, ne), lambda b, k, *_: (b, k, 0)),
        pl.BlockSpec((1, tb, d), lambda b, k, *_: (b, k, 0)),
        full(mod_l),
    ]
    if final:
        ins.append(final_g)
        in_specs.append(full(final_g))
    ins.append(ye)
    in_specs.append(pl.BlockSpec(memory_space=pl.ANY))
    grid_spec = pltpu.PrefetchScalarGridSpec(
        num_scalar_prefetch=2,
        grid=(bsz, nblk),
        in_specs=in_specs,
        out_specs=pl.BlockSpec((1, tb, d), lambda b, k, *_: (b, k, 0)),
        scratch_shapes=[
            pltpu.VMEM((3, ne * win, d), BF16),
            pltpu.SemaphoreType.DMA((3,)),
        ],
    )
    return pl.pallas_call(
        functools.partial(_combine_kernel, ctx=ctx, nblk=nblk, n_steps=bsz * nblk, win=win, tb=tb, final=final,
                          base=base),
        grid_spec=grid_spec,
        out_shape=jax.ShapeDtypeStruct(xs.shape, F32),
        compiler_params=_cparams(("arbitrary", "arbitrary")),
        name="combine_final" if final else ("combine_ctx" if ctx else "combine"),
    )(oa.reshape(-1), nw.reshape(-1), *ins)


def moe(token_sets, mod_l, w_gate, w_up, w_down, layer, final_g=None):
    tm = 384 if token_sets[0][0].shape[1] >= 4096 else 128
    plans, base = [], 0
    for xs1, h2, aff, is_ctx in token_sets:
        n = xs1.shape[1]
        cap = max(1, (EC_CAPACITY * n) // N_EXPERTS)
        tb = min(512, n)
        win_c = LANES
        win_d = 96 if tb == 512 else LANES
        nblk = n // tb
        c_pad = -(-(cap + ROW_ALIGN * nblk + win_c) // tm) * tm
        plans.append(dict(cap=cap, tb=tb, win_c=win_c, win_d=win_d, nblk=nblk, c_pad=c_pad, base=base))
        base += c_pad
    rows = base
    xg, routed = None, []
    for s, ((xs1, h2, aff, is_ctx), p) in enumerate(zip(token_sets, plans)):
        rank, rank_t, oa, nw_d, nw_c = route(aff, tb=p["tb"], win_d=p["win_d"], win_c=p["win_c"])
        end = rows if s == 0 else p["c_pad"]
        tail = tuple((off, min(p["win_d"], end - off)) for off in range(p["cap"], end, p["win_d"]))
        xg = dispatch(oa, nw_d, rank_t, h2, tb=p["tb"], win=p["win_d"], rows=rows, tail=tail, base=p["base"], into=xg)
        routed.append((rank, oa, nw_c))
    ye = expert_ffn([r[1] for r in routed], xg, w_gate, w_up, w_down, layer, tm=tm,
                    sets=tuple((p["base"] // tm, p["nblk"]) for p in plans))
    outs = []
    for s, ((xs1, h2, aff, is_ctx), p, (rank, oa, nw_c)) in enumerate(zip(token_sets, plans, routed)):
        outs.append(combine(oa, nw_c, rank, xs1, mod_l, ye, final_g if s == 0 else None, ctx=is_ctx, tb=p["tb"],
                            win=p["win_c"], base=p["base"]))
    return outs


def _swa_in_kernel(x_ref, mod_ref, g_ref, w_ref, cos_ref, sin_ref, q_ref, k_ref, v_ref, *, ctx, scale):
    h = _rms_mod(x_ref[0], g_ref[...], _mod_row(mod_ref, ctx, 0), _mod_row(mod_ref, ctx, 1))
    p = _dot(h.astype(BF16), w_ref[...])
    cos = cos_ref[...]
    sin = sin_ref[...]
    cs = cos * scale
    ss = sin * scale
    lane = lax.broadcasted_iota(I32, cos.shape, 1)
    qd = SWA_HEAD_DIM // 4
    first = (lane % (2 * qd)) < qd

    def partner(x):
        return jnp.where(first, pltpu.roll(x, LANES - qd, 1), pltpu.roll(x, qd, 1))

    for j in range(8):
        a = p[:, j * 128:(j + 1) * 128]
        q_ref[0, :, j * 128:(j + 1) * 128] = (a * cs + partner(a) * ss).astype(BF16)
    low = lane < 64
    for pr in range(2):
        kx = p[:, 1024 + pr * 128:1024 + (pr + 1) * 128]
        kp = kx * cos + partner(kx) * sin
        vp = p[:, 1280 + pr * 128:1280 + (pr + 1) * 128]
        for src, dst in ((kp, k_ref), (vp, v_ref)):
            ev_lo = jnp.where(low, src, 0.0)
            od_hi = jnp.where(low, 0.0, src)
            ev_hi = pltpu.roll(ev_lo, 64, 1)
            od_lo = pltpu.roll(od_hi, 64, 1)
            base = pr * 512
            dst[0, :, base:base + 128] = ev_lo.astype(BF16)
            dst[0, :, base + 128:base + 256] = ev_hi.astype(BF16)
            dst[0, :, base + 256:base + 384] = od_lo.astype(BF16)
            dst[0, :, base + 384:base + 512] = od_hi.astype(BF16)


def swa_in(x, mod_l, g, w2, cos_t, sin_t, *, ctx, tm):
    bsz, n, d = x.shape
    tm = min(tm, n)
    full = lambda a: pl.BlockSpec(a.shape, lambda b, i: (0,) * a.ndim)
    row = lambda w: pl.BlockSpec((1, tm, w), lambda b, i: (b, i, 0))
    return pl.pallas_call(
        functools.partial(_swa_in_kernel, ctx=ctx, scale=SWA_HEAD_DIM ** -0.5 * math.log2(math.e)),
        grid=(bsz, n // tm),
        in_specs=[row(d), full(mod_l), full(g), full(w2),
                  pl.BlockSpec((tm, 128), lambda b, i: (i, 0)),
                  pl.BlockSpec((tm, 128), lambda b, i: (i, 0))],
        out_specs=[row(1024), row(1024), row(1024)],
        out_shape=[jax.ShapeDtypeStruct((bsz, n, 1024), BF16)] * 3,
        compiler_params=_cparams(("arbitrary", "arbitrary")),
        name="swa_in_ctx" if ctx else "swa_in",
    )(x, mod_l, g, w2, cos_t, sin_t)


def _swa_kernel(sink_ref, q_ref, km_ref, kp_ref, kn_ref, vm_ref, vp_ref, vn_ref, kc_ref, vc_ref,
                o_ref, kwin, vwin, *, tq, n_tiles):
    i = pl.program_id(1)
    blk = SWA_BLOCK
    kwin[0:blk] = kp_ref[0]
    kwin[blk:blk + tq] = km_ref[0]
    kwin[blk + tq:blk + tq + blk] = kn_ref[0]
    vwin[0:blk] = vp_ref[0]
    vwin[blk:blk + tq] = vm_ref[0]
    vwin[blk + tq:blk + tq + blk] = vn_ref[0]
    r = lax.broadcasted_iota(I32, (blk, 3 * blk), 0)
    s = lax.broadcasted_iota(I32, (blk, 3 * blk), 1)
    band = jnp.abs(r - (s - blk)) <= WINDOW
    low_lanes = lax.broadcasted_iota(I32, (blk, 128), 1) < 64

    def qblock(qb, carry):
        row0 = pl.multiple_of(qb * blk, blk)
        first = jnp.logical_and(i == 0, qb == 0)
        last = jnp.logical_and(i == n_tiles - 1, qb == tq // blk - 1)
        valid = jnp.logical_and(band, jnp.logical_and(jnp.logical_or(s >= blk, jnp.logical_not(first)),
                                                      jnp.logical_or(s < 2 * blk, jnp.logical_not(last))))
        bias = jnp.where(valid, 0.0, -1e30)
        for kvh in range(SWA_KV_HEADS):
            c0 = kvh * 256
            q4 = jnp.concatenate([q_ref[0, pl.ds(row0, blk), c0:c0 + 128],
                                  q_ref[0, pl.ds(row0, blk), c0 + 128:c0 + 256]], axis=0)
            kl = jnp.concatenate([kwin[pl.ds(row0, 3 * blk), c0:c0 + 128],
                                  kwin[pl.ds(row0, 3 * blk), c0 + 128:c0 + 256]], axis=0)
            vl = jnp.concatenate([vwin[pl.ds(row0, 3 * blk), c0:c0 + 128],
                                  vwin[pl.ds(row0, 3 * blk), c0 + 128:c0 + 256]], axis=0)
            s_loc = _dot_nt(q4, kl)
            s_ctx = _dot_nt(q4, kc_ref[0, kvh])
            lc = s_ctx.shape[1] // 2
            p_loc, p_ctx, inv = [], [], []
            for pp in range(2):
                pl_row, pc_row, inv_row = [], [], []
                for hf in range(2):
                    sk = sink_ref[kvh * 4 + pp * 2 + hf]
                    sl = s_loc[pp * blk:(pp + 1) * blk, hf * 3 * blk:(hf + 1) * 3 * blk] + bias
                    sc = s_ctx[pp * blk:(pp + 1) * blk, hf * lc:(hf + 1) * lc]
                    m = jnp.maximum(jnp.maximum(jnp.max(sl, axis=-1, keepdims=True),
                                                jnp.max(sc, axis=-1, keepdims=True)), sk)
                    el = jnp.exp2(sl - m)
                    ec = jnp.exp2(sc - m)
                    den = (jnp.sum(el, axis=-1, keepdims=True) + jnp.sum(ec, axis=-1, keepdims=True)
                           + jnp.exp2(sk - m))
                    pl_row.append(el.astype(BF16))
                    pc_row.append(ec.astype(BF16))
                    inv_row.append(1.0 / den)
                p_loc.append(jnp.concatenate(pl_row, axis=1))
                p_ctx.append(jnp.concatenate(pc_row, axis=1))
                inv.append(jnp.where(low_lanes, inv_row[0], inv_row[1]))
            o4 = _dot(jnp.concatenate(p_loc, axis=0), vl) + _dot(jnp.concatenate(p_ctx, axis=0), vc_ref[0, kvh])
            for pp in range(2):
                pair = kvh * 2 + pp
                o_ref[0, pl.ds(row0, blk), pair * 128:(pair + 1) * 128] = (
                    o4[pp * blk:(pp + 1) * blk] * inv[pp]).astype(BF16)
        return carry

    lax.fori_loop(0, tq // blk, qblock, 0)


def swa_attention(q, k2, v2, kc2, vc2, sink, *, tq):
    bsz, n, _ = q.shape
    lc = kc2.shape[1]
    tq = min(tq, n)
    n_tiles = n // tq
    rb = tq // SWA_BLOCK
    nb = n // SWA_BLOCK
    main = pl.BlockSpec((1, tq, 1024), lambda b, i, *_: (b, i, 0))
    prev = pl.BlockSpec((1, SWA_BLOCK, 1024), lambda b, i, *_: (b, jnp.maximum(i * rb - 1, 0), 0))
    nxt = pl.BlockSpec((1, SWA_BLOCK, 1024), lambda b, i, *_: (b, jnp.minimum((i + 1) * rb, nb - 1), 0))
    stack = lambda a: a.reshape(bsz, lc, SWA_KV_HEADS, 2, 128).transpose(0, 2, 3, 1, 4).reshape(
        bsz, SWA_KV_HEADS, 2 * lc, 128)
    kc2, vc2 = stack(kc2), stack(vc2)
    sink = sink * math.log2(math.e)
    cspec = pl.BlockSpec((1, SWA_KV_HEADS, 2 * lc, 128), lambda b, i, *_: (b, 0, 0, 0))
    grid_spec = pltpu.PrefetchScalarGridSpec(
        num_scalar_prefetch=1,
        grid=(bsz, n_tiles),
        in_specs=[main, main, prev, nxt, main, prev, nxt, cspec, cspec],
        out_specs=main,
        scratch_shapes=[pltpu.VMEM((tq + 2 * SWA_BLOCK, 1024), BF16)] * 2,
    )
    return pl.pallas_call(
        functools.partial(_swa_kernel, tq=tq, n_tiles=n_tiles),
        grid_spec=grid_spec,
        out_shape=jax.ShapeDtypeStruct((bsz, n, 1024), BF16),
        compiler_params=_cparams(("arbitrary", "arbitrary")),
        name="swa_attention",
    )(sink, q, k2, k2, k2, v2, v2, v2, kc2, vc2)


def _prep_ab(w_in, w_uq, w_ukv, w_out):
    d = w_in.shape[0]
    perm = _rope_perm(MLA_ROPE)
    o = 3 * CONV_DIM + MLA_Q_LORA + MLA_KV_LORA
    dq = MLA_NOPE + MLA_ROPE
    kr = w_in[:, o:]
    rot_lanes = lambda t: jnp.pad(t, ((0, 0),) * (t.ndim - 1) + ((MLA_NOPE, LANES - dq),))
    heads = lambda t: t.reshape(t.shape[0], MLA_HEADS * LANES)
    win2 = jnp.concatenate([w_in[:, :o], rot_lanes(kr), rot_lanes(kr[:, perm])], axis=1).astype(BF16)
    wq = w_uq.reshape(MLA_Q_LORA, MLA_HEADS, dq)
    qa = heads(jnp.pad(wq, ((0, 0), (0, 0), (0, LANES - dq))))
    qb = heads(rot_lanes(wq[:, :, MLA_NOPE:][:, :, perm]))
    wq2 = jnp.concatenate([qa, qb], axis=1).astype(BF16)
    wkv = w_ukv.reshape(MLA_KV_LORA, MLA_HEADS, MLA_NOPE + MLA_V)
    half_lanes = lambda t: heads(jnp.pad(t, ((0, 0), (0, 0), (0, LANES - t.shape[-1]))))
    wkv2 = jnp.concatenate([half_lanes(wkv[:, :, :MLA_NOPE]), half_lanes(wkv[:, :, MLA_NOPE:])], axis=1).astype(BF16)
    wc = w_out[:CONV_DIM].astype(BF16)
    wa = jnp.pad(w_out[CONV_DIM:].reshape(MLA_HEADS, MLA_V, d), ((0, 0), (0, LANES - MLA_V), (0, 0)))
    return win2, wq2, wkv2, wc, wa.reshape(MLA_HEADS * LANES, d).astype(BF16)


def _prep_swa(w_qkv):
    return w_qkv.astype(BF16)


def _pad_lanes(t, left, width=LANES, fill=0.0):
    n, w = t.shape
    return jnp.concatenate([jnp.full((n, left), fill, F32), t, jnp.zeros((n, width - left - w), F32)], axis=1)


def kernel(x, c, ctx, c_ctx, mod_w, mod_b, norm1_g, norm2_g, ab_w_in, conv_w, conv_b, mla_q_norm_g, mla_w_uq,
           mla_kv_norm_g, mla_w_ukv, ab_w_out, swa_w_qkv, swa_sink, swa_w_out, router_w, exp_w_gate, exp_w_up,
           exp_w_down, final_g):
    bsz, n, d = x.shape
    lc = ctx.shape[1]
    depth = mod_w.shape[0]
    assert bsz <= 2 and d == 1024

    cs = jnp.concatenate([c, c_ctx[None, :], jnp.zeros((8 - bsz - 1, d), F32)], axis=0)
    mod = modulation(cs, mod_w, mod_b)

    cos_m, sin_m = _rope_tables(n, MLA_ROPE)
    cos_mla = _pad_lanes(cos_m, 64, fill=1.0)
    sin_mla = _pad_lanes(sin_m, 64)
    cos_mla_c = jnp.concatenate([jnp.ones((lc, 96), F32), jnp.zeros((lc, 32), F32)], axis=1)
    zeros_c = jnp.zeros((lc, 128), F32)
    cos_s, sin_s = _rope_tables(n, SWA_HEAD_DIM)
    cos_swa = jnp.concatenate([cos_s, cos_s], axis=1)
    sin_swa = jnp.concatenate([sin_s, sin_s], axis=1)
    ones_c = jnp.ones((lc, 128), F32)

    row2 = lambda v: v.reshape(1, -1)
    xs, xc = x, ctx
    for layer in range(depth):
        need_ctx = layer < depth - 1
        last = layer == depth - 1
        mod_l = mod[layer]
        g1 = row2(norm1_g[layer])
        g2 = row2(norm2_g[layer])
        rw_f = jnp.concatenate([router_w[layer], jnp.zeros((d, LANES - N_EXPERTS), F32)], axis=1)
        rw_hi = rw_f.astype(BF16)
        rw = jnp.concatenate([rw_hi, (rw_f - rw_hi.astype(F32)).astype(BF16)], axis=1)
        wg, wu, wd = exp_w_gate, exp_w_up, exp_w_down
        if layer % 2 == 0:
            e = layer // 2
            win2, wq2, wkv2, wc, wa = _prep_ab(ab_w_in[e], mla_w_uq[e], mla_w_ukv[e], ab_w_out[e])
            qg, kvg = row2(mla_q_norm_g[e]), row2(mla_kv_norm_g[e])
            cw, cb = conv_w[e], row2(conv_b[e])
            gb, z, q, k, v = ab_in(xs, mod_l, g1, win2, qg, wq2, kvg, wkv2, cos_mla, sin_mla, ctx=False, tm=512)
            gbc, zc, qc, kc, vc = ab_in(xc, mod_l, g1, win2, qg, wq2, kvg, wkv2, cos_mla_c, zeros_c, ctx=True, tm=256)
            att = mla_attention_flat(q, k, v, kc, vc, tq=512, tk_max=1280, spt=12)
            xs1, h2, aff = mix_out(att, wa, xs, mod_l, g2, rw, (gb, z, cw, cb, wc), ctx=False, tm=512)
            if need_ctx:
                att_c = mla_attention_ctx(qc, kc, vc)
                xc1, hc2, affc = mix_out(att_c, wa, xc, mod_l, g2, rw, (gbc, zc, cw, cb, wc), ctx=True, tm=256)
        else:
            o = layer // 2
            w2 = _prep_swa(swa_w_qkv[o])
            wo = swa_w_out[o].astype(BF16)
            q, k2, v2 = swa_in(xs, mod_l, g1, w2, cos_swa, sin_swa, ctx=False, tm=512)
            qc, kc2, vc2 = swa_in(xc, mod_l, g1, w2, ones_c, zeros_c, ctx=True, tm=256)
            att = swa_attention(q, k2, v2, kc2, vc2, swa_sink[o], tq=512)
            xs1, h2, aff = mix_out(att, wo, xs, mod_l, g2, rw, ctx=False, tm=512)
            if need_ctx:
                raise NotImplementedError("context self-attention for windowed layers below the last")
        token_sets = [(xs1, h2, aff, False)] + ([(xc1, hc2, affc, True)] if need_ctx else [])
        outs = moe(token_sets, mod_l, wg, wu, wd, layer, final_g=row2(final_g) if last else None)
        xs = outs[0]
        if need_ctx:
            xc = outs[1]
    return xs
```

```python
import functools
import math

import jax
import jax.numpy as jnp
import numpy as np
from jax import lax
from jax.experimental import pallas as pl
from jax.experimental.pallas import tpu as pltpu

F32 = jnp.float32
BF16 = jnp.bfloat16
I32 = jnp.int32

GRID_W = 64
EPS = 1e-6
ROPE_THETA = 10000.0
CONV_DIM = 512
MLA_HEADS = 8
MLA_Q_LORA = 256
MLA_KV_LORA = 128
MLA_NOPE = 64
MLA_ROPE = 32
MLA_V = 64
SWA_HEADS = 16
SWA_KV_HEADS = 4
SWA_HEAD_DIM = 64
WINDOW = 128
SWA_BLOCK = 128
N_EXPERTS = 16
EC_CAPACITY = 2
N_MOD = 6

LANES = 128
ROW_ALIGN = 16
VMEM_LIMIT = 56 * 1024 * 1024


def _cparams(sem, vmem=VMEM_LIMIT):
    return pltpu.CompilerParams(dimension_semantics=sem, vmem_limit_bytes=vmem)


def _dot(a, b):
    return jnp.dot(a, b, preferred_element_type=F32)


def _dot_nt(a, b):
    return lax.dot_general(a, b, (((1,), (1,)), ((), ())), preferred_element_type=F32)


def _dot_tn(a, b):
    return lax.dot_general(a, b, (((0,), (0,)), ((), ())), preferred_element_type=F32)


def _dot_hi(a, b):
    return jnp.dot(a, b, preferred_element_type=F32, precision=lax.Precision.HIGHEST)


def _silu(x):
    return x * (1.0 / (1.0 + jnp.exp(-x)))


def _rms_mod(x, g, shift, scale):
    y = x * lax.rsqrt(jnp.mean(x * x, axis=-1, keepdims=True) + EPS)
    return (y * g) * (1.0 + scale) + shift


def _mod_row(mod_ref, ctx, which):
    r = 2 if ctx else pl.program_id(0)
    return mod_ref[pl.ds(r, 1), which * 1024:(which + 1) * 1024]


def _mod_kernel(cs_ref, w_ref, b_ref, o_ref):
    s = _silu(cs_ref[...])
    o_ref[0] = _dot_hi(s, w_ref[0]) + b_ref[0]


def modulation(cs, mod_w, mod_b):
    depth, d, n6 = mod_w.shape
    tn = 1536
    return pl.pallas_call(
        _mod_kernel,
        grid=(depth, n6 // tn),
        in_specs=[
            pl.BlockSpec((8, d), lambda l, j: (0, 0)),
            pl.BlockSpec((1, d, tn), lambda l, j: (l, 0, j)),
            pl.BlockSpec((1, 1, tn), lambda l, j: (l, 0, j)),
        ],
        out_specs=pl.BlockSpec((1, 8, tn), lambda l, j: (l, 0, j)),
        out_shape=jax.ShapeDtypeStruct((depth, 8, n6), F32),
        compiler_params=_cparams(("arbitrary", "arbitrary")),
        name="modulation",
    )(cs, mod_w, mod_b.reshape(depth, 1, n6))


def _rope_tables(n, dim):
    n_rows = n // GRID_W
    half = dim // 2
    qd = dim // 4
    freqs = ROPE_THETA ** (-jnp.arange(0, half, 2, dtype=F32) / half)
    ang_r = jnp.arange(n_rows, dtype=F32)[:, None] * freqs
    ang_c = jnp.arange(GRID_W, dtype=F32)[:, None] * freqs

    def table(fr, fc, sign):
        r = jnp.broadcast_to(fr[:, None, :], (n_rows, GRID_W, qd))
        c = jnp.broadcast_to(fc[None, :, :], (n_rows, GRID_W, qd))
        return jnp.concatenate([sign * r, r, sign * c, c], axis=-1).reshape(n, dim)

    return table(jnp.cos(ang_r), jnp.cos(ang_c), 1.0), table(jnp.sin(ang_r), jnp.sin(ang_c), -1.0)


def _rope_perm(dim):
    qd = dim // 4
    ch = np.arange(dim)
    pair = (ch // qd) % 2
    return np.where(pair == 0, ch + qd, ch - qd)


def _ab_in_kernel(x_ref, mod_ref, g_ref, win_ref, qg_ref, wq_ref, kvg_ref, wkv_ref, cos_ref, sin_ref,
                  gb_ref, z_ref, q_ref, k_ref, v_ref, *, ctx, scale):
    x = x_ref[0]
    h = _rms_mod(x, g_ref[...], _mod_row(mod_ref, ctx, 0), _mod_row(mod_ref, ctx, 1))
    p = _dot(h.astype(BF16), win_ref[...])
    gb_ref[0] = p[:, 0:512].astype(BF16)
    z_ref[0] = (p[:, 512:1024] * p[:, 1024:1536]).astype(BF16)
    ql = p[:, 1536:1792]
    kvl = p[:, 1792:1920]
    kra = p[:, 1920:2048]
    krb = p[:, 2048:2176]
    cos = cos_ref[...]
    sin = sin_ref[...]
    qn = ql * lax.rsqrt(jnp.mean(ql * ql, axis=-1, keepdims=True) + EPS) * qg_ref[...]
    qq = _dot(qn.astype(BF16), wq_ref[...])
    kvn = kvl * lax.rsqrt(jnp.mean(kvl * kvl, axis=-1, keepdims=True) + EPS) * kvg_ref[...]
    kv = _dot(kvn.astype(BF16), wkv_ref[...])
    krope = kra * cos + krb * sin
    cs = cos * scale
    ss = sin * scale
    ones_lane = lax.broadcasted_iota(I32, cos.shape, 1) == MLA_V
    for hd in range(MLA_HEADS):
        a = qq[:, hd * 128:(hd + 1) * 128]
        b = qq[:, 1024 + hd * 128:1024 + (hd + 1) * 128]
        q_ref[0, hd] = (a * cs + b * ss).astype(BF16)
        k_ref[0, hd] = (kv[:, hd * 128:(hd + 1) * 128] + krope).astype(BF16)
        v_ref[0, hd] = jnp.where(ones_lane, 1.0, kv[:, 1024 + hd * 128:1024 + (hd + 1) * 128]).astype(BF16)


def ab_in(x, mod_l, g, win2, qg, wq2, kvg, wkv2, cos_t, sin_t, *, ctx, tm):
    bsz, n, d = x.shape
    tm = min(tm, n)
    scale = (MLA_NOPE + MLA_ROPE) ** -0.5 * math.log2(math.e)
    full = lambda a: pl.BlockSpec(a.shape, lambda b, i: (0,) * a.ndim)
    hshape = jax.ShapeDtypeStruct((bsz, MLA_HEADS, n, 128), BF16)
    hspec = pl.BlockSpec((1, MLA_HEADS, tm, 128), lambda b, i: (b, 0, i, 0))
    return pl.pallas_call(
        functools.partial(_ab_in_kernel, ctx=ctx, scale=scale),
        grid=(bsz, n // tm),
        in_specs=[
            pl.BlockSpec((1, tm, d), lambda b, i: (b, i, 0)),
            full(mod_l), full(g), full(win2), full(qg), full(wq2), full(kvg), full(wkv2),
            pl.BlockSpec((tm, 128), lambda b, i: (i, 0)),
            pl.BlockSpec((tm, 128), lambda b, i: (i, 0)),
        ],
        out_specs=[
            pl.BlockSpec((1, tm, 512), lambda b, i: (b, i, 0)),
            pl.BlockSpec((1, tm, 512), lambda b, i: (b, i, 0)),
            hspec, hspec, hspec,
        ],
        out_shape=[
            jax.ShapeDtypeStruct((bsz, n, 512), BF16),
            jax.ShapeDtypeStruct((bsz, n, 512), BF16),
            hshape, hshape, hshape,
        ],
        compiler_params=_cparams(("arbitrary", "arbitrary")),
        name="ab_in_ctx" if ctx else "ab_in",
    )(x, mod_l, g, win2, qg, wq2, kvg, wkv2, cos_t, sin_t)


def _mla_ctx_kernel(q_ref, kc_ref, vc_ref, o_ref):
    s = _dot_nt(q_ref[0, 0], kc_ref[0, 0])
    m = jnp.max(s, axis=-1, keepdims=True)
    acc = _dot(jnp.exp2(s - m).astype(BF16), vc_ref[0, 0])
    o_ref[0] = (acc * (1.0 / acc[:, MLA_V:MLA_V + 1])).astype(BF16)


def mla_attention_ctx(q, kc, vc):
    bsz, nh, lc, dk = q.shape
    kv = pl.BlockSpec((1, 1, lc, dk), lambda b, h: (b, h, 0, 0))
    return pl.pallas_call(
        _mla_ctx_kernel,
        grid=(bsz, nh),
        in_specs=[kv, kv, kv],
        out_specs=pl.BlockSpec((1, lc, dk), lambda b, h: (b, 0, h)),
        out_shape=jax.ShapeDtypeStruct((bsz, lc, nh * dk), BF16),
        compiler_params=_cparams(("arbitrary", "arbitrary")),
        name="mla_attention_ctx",
    )(q, kc, vc)


def _mla_flat_kernel(q_ref, k_ref, v_ref, kc_ref, vc_ref, o_ref, kall, vall, s_a, s_b, p_a, p_b,
                     *, tq, tk, n_tiles, n_chunks, spt):
    n_stages = n_tiles * n_chunks
    n_lat, n_ctx = k_ref.shape[2], kc_ref.shape[2]
    cp = math.gcd(n_lat, 2048)

    def copy_rows(c, carry):
        sl = pl.ds(pl.multiple_of(c * cp, cp), cp)
        kall[sl, :] = k_ref[0, 0, sl, :]
        vall[sl, :] = v_ref[0, 0, sl, :]
        return carry

    lax.fori_loop(0, n_lat // cp, copy_rows, 0)
    kall[n_lat:n_lat + n_ctx, :] = kc_ref[0, 0]
    vall[n_lat:n_lat + n_ctx, :] = vc_ref[0, 0]

    def nxt(c):
        i, j = c
        wrap = j + 1 == n_chunks
        return jnp.where(wrap, i + 1, i), jnp.where(wrap, 0, j + 1)

    def rows(idx, size):
        return pl.ds(pl.multiple_of(idx * size, size), size)

    def scores(c, s_buf):
        s = _dot_nt(q_ref[0, 0, rows(c[0], tq), :], kall[rows(c[1], tk), :])
        s_buf[...] = s
        return jnp.max(s, axis=-1, keepdims=True)

    def weights(c, s_buf, p_buf, mx, m):
        m_prev = jnp.where(c[1] == 0, -1e30, m)
        m_new = jnp.maximum(m_prev, mx)
        p_buf[...] = jnp.exp2(s_buf[...] - m_new).astype(BF16)
        return m_new, jnp.exp2(m_prev - m_new)

    def wsum(c, p_buf, alpha, acc):
        acc = alpha * acc + _dot(p_buf[...], vall[rows(c[1], tk), :])
        o_ref[0, pl.ds(pl.multiple_of(c[0] * tq, tq), tq), :] = (
            acc * (1.0 / acc[:, MLA_V:MLA_V + 1])).astype(BF16)
        return acc

    def trip(c_pv, c_w, c_s, s_cur, s_nxt, p_prev, p_cur, mx_cur, alpha_prev, m, acc):
        mx_nxt = scores(c_s, s_nxt) if c_s is not None else None
        acc = wsum(c_pv, p_prev, alpha_prev, acc)
        m, alpha_cur = weights(c_w, s_cur, p_cur, mx_cur, m)
        return mx_nxt, alpha_cur, m, acc

    def pair(c, mx_b, alpha_a, m, acc):
        mx_a, alpha_b, m, acc = trip(c[0], c[1], c[2], s_b, s_a, p_a, p_b, mx_b, alpha_a, m, acc)
        mx_b, alpha_a, m, acc = trip(c[1], c[2], c[3], s_a, s_b, p_b, p_a, mx_a, alpha_b, m, acc)
        return mx_b, alpha_a, m, acc

    def chain(c, k):
        out = [c]
        for _ in range(k):
            out.append(nxt(out[-1]))
        return out

    zero = jnp.int32(0)
    c0 = (zero, zero)
    mx_a = scores(c0, s_a)
    m, alpha_a = weights(c0, s_a, p_a, mx_a, jnp.zeros((tq, 1), F32))
    mx_b = scores(nxt(c0), s_b)
    acc = jnp.zeros((tq, LANES), F32)

    def run(n_st, carry):
        i, j, mx_b, alpha_a, m, acc = carry
        c = chain((i, j), n_st + 1)
        for u in range(0, n_st, 2):
            mx_b, alpha_a, m, acc = pair(c[u:u + 4], mx_b, alpha_a, m, acc)
        return c[n_st][0], c[n_st][1], mx_b, alpha_a, m, acc

    carry = run((n_stages - 4) % spt, (zero, zero, mx_b, alpha_a, m, acc))
    i, j, mx_b, alpha_a, m, acc = lax.fori_loop(0, (n_stages - 4) // spt, lambda _, c: run(spt, c), carry)
    c = chain((i, j), 3)
    mx_b, alpha_a, m, acc = pair(c[0:4], mx_b, alpha_a, m, acc)
    mx_a, alpha_b, m, acc = trip(c[2], c[3], None, s_b, s_a, p_a, p_b, mx_b, alpha_a, m, acc)
    wsum(c[3], p_b, alpha_b, acc)


def mla_attention_flat(q, k, v, kc, vc, *, tq, tk_max, spt):
    bsz, nh, n, dk = q.shape
    lc = kc.shape[2]
    nk = n + lc
    tq = min(tq, n)
    tk = max(t for t in range(256, tk_max + 1, 256) if nk % t == 0)
    n_tiles, n_chunks = n // tq, nk // tk
    assert spt % 2 == 0 and (n_tiles * n_chunks) % 2 == 0 and n_tiles * n_chunks >= 4
    head = lambda rows_: pl.BlockSpec((1, 1, rows_, dk), lambda b, h: (b, h, 0, 0))
    return pl.pallas_call(
        functools.partial(_mla_flat_kernel, tq=tq, tk=tk, n_tiles=n_tiles, n_chunks=n_chunks, spt=spt),
        grid=(bsz, nh),
        in_specs=[head(n), head(n), head(n), head(lc), head(lc)],
        out_specs=pl.BlockSpec((1, n, dk), lambda b, h: (b, 0, h)),
        out_shape=jax.ShapeDtypeStruct((bsz, n, nh * dk), BF16),
        scratch_shapes=([pltpu.VMEM((nk, dk), BF16)] * 2 + [pltpu.VMEM((tq, tk), F32)] * 2
                        + [pltpu.VMEM((tq, tk), BF16)] * 2),
        compiler_params=_cparams(("arbitrary", "arbitrary")),
        name="mla_attention",
    )(q, k, v, kc, vc)


def _mix_out_kernel(*refs, ctx, has_conv, tm, n_tiles):
    if has_conv:
        (gb_ref, z_ref, zp_ref, zn_ref, cw_ref, cb_ref, wc_ref,
         att_ref, wa_ref, x_ref, mod_ref, g2_ref, rw_ref, xs_ref, h2_ref, aff_ref) = refs
    else:
        att_ref, wa_ref, x_ref, mod_ref, g2_ref, rw_ref, xs_ref, h2_ref, aff_ref = refs
    y = _dot(att_ref[0], wa_ref[...])
    if has_conv:
        i = pl.program_id(1)
        z = z_ref[0].astype(F32)
        rows = lax.broadcasted_iota(I32, z.shape, 0)
        zprev_halo = jnp.where(i > 0, zp_ref[0, ROW_ALIGN - 1:ROW_ALIGN, :].astype(F32), 0.0)
        znext_halo = jnp.where(i < n_tiles - 1, zn_ref[0, 0:1, :].astype(F32), 0.0)
        zprev = jnp.where(rows == 0, zprev_halo, pltpu.roll(z, 1, 0))
        znext = jnp.where(rows == tm - 1, znext_halo, pltpu.roll(z, tm - 1, 0))
        cw = cw_ref[...]
        conv = gb_ref[0].astype(F32) * (zprev * cw[0:1] + z * cw[1:2] + znext * cw[2:3] + cb_ref[...])
        y = y + _dot(conv.astype(BF16), wc_ref[...])
    xs = x_ref[0] + _mod_row(mod_ref, ctx, 2) * y
    xs_ref[0] = xs
    h2 = _rms_mod(xs, g2_ref[...], _mod_row(mod_ref, ctx, 3), _mod_row(mod_ref, ctx, 4))
    h2_ref[0] = h2.astype(BF16)
    h2_hi = h2.astype(BF16)
    h2_lo = (h2 - h2_hi.astype(F32)).astype(BF16)
    hh = _dot(h2_hi, rw_ref[...])
    logits = hh[:, :LANES] + hh[:, LANES:] + _dot(h2_lo, rw_ref[:, :LANES])
    lane = lax.broadcasted_iota(I32, logits.shape, 1)
    logits = jnp.where(lane < N_EXPERTS, logits, -1e30)
    mx = jnp.max(logits, axis=-1, keepdims=True)
    ex = jnp.exp(logits - mx)
    aff = ex / jnp.sum(ex, axis=-1, keepdims=True)
    aff_ref[0] = aff[:, :N_EXPERTS]


def mix_out(att, wa, x, mod_l, g2, rw, conv=None, *, ctx, tm):
    bsz, n, d = x.shape
    tm = min(tm, n)
    n_tiles = n // tm
    full = lambda a: pl.BlockSpec(a.shape, lambda b, i: (0,) * a.ndim)
    row = lambda w: pl.BlockSpec((1, tm, w), lambda b, i: (b, i, 0))
    ins, in_specs = [], []
    if conv is not None:
        gb, z, cw, cb, wc = conv
        r8 = tm // ROW_ALIGN
        nb8 = n // ROW_ALIGN
        ins += [gb, z, z, z, cw, cb, wc]
        in_specs += [
            row(512), row(512),
            pl.BlockSpec((1, ROW_ALIGN, 512), lambda b, i: (b, jnp.maximum(i * r8 - 1, 0), 0)),
            pl.BlockSpec((1, ROW_ALIGN, 512), lambda b, i: (b, jnp.minimum((i + 1) * r8, nb8 - 1), 0)),
            full(cw), full(cb), full(wc),
        ]
    ins += [att, wa, x, mod_l, g2, rw]
    in_specs += [row(att.shape[-1]), full(wa), row(d), full(mod_l), full(g2), full(rw)]
    return pl.pallas_call(
        functools.partial(_mix_out_kernel, ctx=ctx, has_conv=conv is not None, tm=tm, n_tiles=n_tiles),
        grid=(bsz, n_tiles),
        in_specs=in_specs,
        out_specs=[row(d), row(d), row(N_EXPERTS)],
        out_shape=[
            jax.ShapeDtypeStruct((bsz, n, d), F32),
            jax.ShapeDtypeStruct((bsz, n, d), BF16),
            jax.ShapeDtypeStruct((bsz, n, N_EXPERTS), F32),
        ],
        compiler_params=_cparams(("arbitrary", "arbitrary")),
        name=("mix_out_ctx" if ctx else "mix_out") + ("_conv" if conv is not None else ""),
    )(*ins)


def _route_kernel(aff_ref, aff8_ref, ls_ref, us_ref, eye_ref, rank_ref, rankt_ref, oa_ref, nwd_ref, nwc_ref,
                  thr_s, need_s, eq_s, oa_s, *, n, cap, tb, win_d, win_c, chunk):
    n_chunks = (n // 8) // chunk
    nblk = n // tb
    k = pl.program_id(1)

    def count(pred_fn):
        def body(c, acc):
            kb = pltpu.bitcast(aff8_ref[0, pl.ds(pl.multiple_of(c * chunk, chunk), chunk), :], I32)
            return acc + jnp.sum(pred_fn(kb).astype(I32), axis=0, keepdims=True)
        acc = lax.fori_loop(0, n_chunks, body, jnp.zeros((1, LANES), I32))
        for sh in (64, 32, 16):
            acc = acc + pltpu.roll(acc, sh, 1)
        return acc

    @pl.when(k == 0)
    def _():
        def bit_body(i, thr):
            cand = thr | jnp.left_shift(jnp.int32(1), 30 - i)
            cnt = count(lambda kb: kb >= cand)
            return jnp.where(cnt >= cap, cand, thr)

        thr = lax.fori_loop(0, 31, bit_body, jnp.zeros((1, LANES), I32))
        thr_s[...] = thr[:, :N_EXPERTS]
        need_s[...] = (cap - count(lambda kb: kb > thr))[:, :N_EXPERTS].astype(F32)
        eq_s[...] = jnp.zeros((1, N_EXPERTS), F32)
        oa_s[...] = jnp.zeros((1, N_EXPERTS), F32)

    thr = thr_s[...]
    off = pl.multiple_of(k * tb, tb)
    kb = pltpu.bitcast(aff_ref[0, pl.ds(off, tb), :], I32)
    gt = kb > thr
    eq = kb == thr
    eqf = jnp.where(eq, 1.0, 0.0)
    eqrank = eq_s[...] + _dot(ls_ref[...], eqf.astype(BF16))
    sel = jnp.logical_or(gt, jnp.logical_and(eq, eqrank < need_s[...]))
    self_ = jnp.where(sel, 1.0, 0.0)
    selb = self_.astype(BF16)
    lrank = _dot(ls_ref[...], selb)
    gate_bits = pltpu.bitcast(aff_ref[0, pl.ds(off, tb), :].astype(BF16).astype(F32), I32)
    rank_ref[0] = gate_bits | jnp.where(sel, lrank.astype(I32) + 1, 0)
    lrank_t = _dot_tn(selb, us_ref[...])
    sel_t = _dot_tn(selb, eye_ref[...])
    rankt_ref[0] = jnp.where(sel_t > 0.5, lrank_t, -1.0)
    c = jnp.sum(self_, axis=0, keepdims=True)
    oa_run = oa_s[...]
    oa_ref[0, pl.ds(k, 1), :] = oa_run.astype(I32)
    nwd_ref[0, pl.ds(k, 1), :] = jnp.ceil(c * (1.0 / win_d)).astype(I32)
    nwc_ref[0, pl.ds(k, 1), :] = jnp.ceil(c * (1.0 / win_c)).astype(I32)
    oa_new = oa_run + jnp.ceil(c * (1.0 / ROW_ALIGN)) * ROW_ALIGN
    oa_s[...] = oa_new
    eq_s[...] = eq_s[...] + jnp.sum(eqf, axis=0, keepdims=True)

    @pl.when(k == nblk - 1)
    def _():
        oa_ref[0, pl.ds(nblk, 1), :] = oa_new.astype(I32)


def route(aff, *, tb, win_d, win_c):
    bsz, n, ne = aff.shape
    cap = max(1, (EC_CAPACITY * n) // ne)
    nblk = n // tb
    assert ne * 8 == LANES
    aff8 = aff.reshape(bsz, n // 8, LANES)
    chunk = min(256, n // 8)
    ii = np.arange(tb)
    ls = jnp.asarray(ii[None, :] < ii[:, None], BF16)
    us = jnp.asarray(ii[:, None] < ii[None, :], BF16)
    eye = jnp.asarray(ii[:, None] == ii[None, :], BF16)
    full = lambda a: pl.BlockSpec(a.shape, lambda b, k: (0,) * a.ndim)
    return pl.pallas_call(
        functools.partial(_route_kernel, n=n, cap=cap, tb=tb, win_d=win_d, win_c=win_c, chunk=chunk),
        grid=(bsz, nblk),
        in_specs=[pl.BlockSpec((1, n, ne), lambda b, k: (b, 0, 0)),
                  pl.BlockSpec((1, n // 8, LANES), lambda b, k: (b, 0, 0)), full(ls), full(us), full(eye)],
        out_specs=[
            pl.BlockSpec((1, tb, ne), lambda b, k: (b, k, 0)),
            pl.BlockSpec((1, ne, tb), lambda b, k: (b, 0, k)),
            pl.BlockSpec((1, nblk + 1, ne), lambda b, k: (b, 0, 0)),
            pl.BlockSpec((1, nblk, ne), lambda b, k: (b, 0, 0)),
            pl.BlockSpec((1, nblk, ne), lambda b, k: (b, 0, 0)),
        ],
        out_shape=[
            jax.ShapeDtypeStruct((bsz, n, ne), I32),
            jax.ShapeDtypeStruct((bsz, ne, n), F32),
            jax.ShapeDtypeStruct((bsz, nblk + 1, ne), I32),
            jax.ShapeDtypeStruct((bsz, nblk, ne), I32),
            jax.ShapeDtypeStruct((bsz, nblk, ne), I32),
        ],
        scratch_shapes=[pltpu.VMEM((1, ne), I32), pltpu.VMEM((1, ne), F32),
                        pltpu.VMEM((1, ne), F32), pltpu.VMEM((1, ne), F32)],
        compiler_params=_cparams(("arbitrary", "arbitrary")),
        name="route",
    )(aff, aff8, ls, us, eye)


def _dispatch_kernel(oa_ref, nw_ref, rankt_ref, h_ref, *rest, nblk, n_steps, win, tb, group, tail, base):
    xg_ref, stack, sem = rest[-3:]
    b = pl.program_id(0)
    k = pl.program_id(1)
    step = b * nblk + k
    slot = step % 2
    ne = N_EXPERTS

    def fill(sl, j):
        h = h_ref[0]
        for g0 in range(0, ne, group):
            pieces = []
            for e in range(g0, g0 + group):
                r = rankt_ref[0, e:e + 1, :]
                srow = lax.broadcasted_iota(I32, (win, tb), 0).astype(F32) + (j * win).astype(F32)
                pieces.append((r == srow).astype(BF16))
            oh_t = jnp.concatenate(pieces, axis=0)
            stack[sl, g0 * win:(g0 + group) * win, :] = _dot(oh_t, h).astype(BF16)

    def copies(sl, bb, kk, j):
        out = []
        for e in range(ne):
            off = pl.multiple_of(base + oa_ref[(bb * (nblk + 1) + kk) * ne + e] + j * win, ROW_ALIGN)
            out.append(pltpu.make_async_copy(
                stack.at[sl, pl.ds(e * win, win), :],
                xg_ref.at[bb, e, pl.ds(off, win), :],
                sem.at[sl]))
        return out

    @pl.when(k == 0)
    def _():
        stack[2, 0:win, :] = jnp.zeros((win, stack.shape[2]), BF16)
        cs = [pltpu.make_async_copy(stack.at[2, pl.ds(0, sz), :], xg_ref.at[b, e, pl.ds(base + off, sz), :],
                                    sem.at[2])
              for e in range(ne) for off, sz in tail]
        for c in cs:
            c.start()
        for c in cs:
            c.wait()

    fill(slot, jnp.int32(0))

    @pl.when(step > 0)
    def _():
        for c in copies(1 - slot, b, k, 0):
            c.wait()

    for c in copies(slot, b, k, 0):
        c.start()

    nws = [nw_ref[(b * nblk + k) * ne + e] for e in range(ne)]
    nwx = functools.reduce(jnp.maximum, nws)

    def extra(j, carry):
        fill(2, j)
        cs = copies(2, b, k, j)
        for e, c in enumerate(cs):
            pl.when(j < nws[e])(c.start)
        for e, c in enumerate(cs):
            pl.when(j < nws[e])(c.wait)
        return carry

    lax.fori_loop(1, jnp.maximum(nwx, 1), extra, 0)

    @pl.when(step == n_steps - 1)
    def _():
        for c in copies(slot, b, k, 0):
            c.wait()


def dispatch(oa, nw, rank_t, h2, *, tb, win, rows, tail, base=0, into=None):
    bsz, n, d = h2.shape
    ne = N_EXPERTS
    nblk = n // tb
    group = 4
    ins = [oa.reshape(-1), nw.reshape(-1), rank_t, h2]
    in_specs = [
        pl.BlockSpec((1, ne, tb), lambda b, k, *_: (b, 0, k)),
        pl.BlockSpec((1, tb, d), lambda b, k, *_: (b, k, 0)),
    ]
    aliases = {}
    if into is not None:
        assert into.shape == (bsz, ne, rows, d)
        aliases = {len(ins): 0}
        ins.append(into)
        in_specs.append(pl.BlockSpec(memory_space=pl.ANY))
    grid_spec = pltpu.PrefetchScalarGridSpec(
        num_scalar_prefetch=2,
        grid=(bsz, nblk),
        in_specs=in_specs,
        out_specs=pl.BlockSpec(memory_space=pl.ANY),
        scratch_shapes=[
            pltpu.VMEM((3, ne * win, d), BF16),
            pltpu.SemaphoreType.DMA((3,)),
        ],
    )
    return pl.pallas_call(
        functools.partial(_dispatch_kernel, nblk=nblk, n_steps=bsz * nblk, win=win, tb=tb, group=group, tail=tail,
                          base=base),
        grid_spec=grid_spec,
        out_shape=jax.ShapeDtypeStruct((bsz, ne, rows, d), BF16),
        input_output_aliases=aliases,
        compiler_params=_cparams(("arbitrary", "arbitrary")),
        name="dispatch",
    )(*ins)


def _ffn_kernel(tot_ref, x_ref, *rest, tm, nblk, n_tiles, has_small):
    if has_small:
        xs_ref, wg_ref, wu_ref, wd_ref, y_ref, ys_ref, wgb, wub, wdb = rest
    else:
        wg_ref, wu_ref, wd_ref, y_ref, wgb, wub, wdb = rest
    e = pl.program_id(0)
    b = pl.program_id(1)
    i = pl.program_id(2)

    @pl.when(jnp.logical_and(b == 0, i == 0))
    def _():
        wgb[...] = wg_ref[0, 0].astype(BF16)
        wub[...] = wu_ref[0, 0].astype(BF16)
        wdb[...] = wd_ref[0, 0].astype(BF16)

    def ffn(x):
        hid = (_silu(_dot(x, wgb[...])) * _dot(x, wub[...])).astype(BF16)
        return _dot(hid, wdb[...]).astype(BF16)

    total = tot_ref[(b * (nblk + 1) + nblk) * N_EXPERTS + e]

    @pl.when(jnp.logical_and(i < n_tiles, i * tm < total))
    def _():
        y_ref[0, 0] = ffn(x_ref[0, 0])

    @pl.when(jnp.logical_and(i < n_tiles, i * tm >= total))
    def _():
        y_ref[0, 0] = jnp.zeros(y_ref.shape[2:], BF16)

    if has_small:
        @pl.when(i == n_tiles)
        def _():
            ys_ref[0, 0] = ffn(xs_ref[0, 0])


def expert_ffn(oa, xg, w_gate, w_up, w_down, layer, xg_small=None, *, tm, nblk):
    bsz, ne, c_pad, d = xg.shape
    f = w_gate.shape[-1]
    n_tiles = c_pad // tm
    has_small = xg_small is not None
    tile = pl.BlockSpec((1, 1, tm, d), lambda e, b, i, *_: (b, e, jnp.minimum(i, n_tiles - 1), 0))
    wspec = lambda r, c: pl.BlockSpec((1, 1, r, c), lambda e, b, i, *_: (layer, e, 0, 0))
    ins, in_specs, out_specs, out_shape = [xg], [tile], [tile], [jax.ShapeDtypeStruct(xg.shape, BF16)]
    if has_small:
        small = pl.BlockSpec((1, 1) + xg_small.shape[2:], lambda e, b, i, *_: (b, e, 0, 0))
        ins.append(xg_small)
        in_specs.append(small)
        out_specs.append(small)
        out_shape.append(jax.ShapeDtypeStruct(xg_small.shape, BF16))
    grid_spec = pltpu.PrefetchScalarGridSpec(
        num_scalar_prefetch=1,
        grid=(ne, bsz, n_tiles + int(has_small)),
        in_specs=in_specs + [wspec(d, f), wspec(d, f), wspec(f, d)],
        out_specs=out_specs,
        scratch_shapes=[pltpu.VMEM((d, f), BF16), pltpu.VMEM((d, f), BF16), pltpu.VMEM((f, d), BF16)],
    )
    return pl.pallas_call(
        functools.partial(_ffn_kernel, tm=tm, nblk=nblk, n_tiles=n_tiles, has_small=has_small),
        grid_spec=grid_spec,
        out_shape=out_shape,
        compiler_params=_cparams(("arbitrary", "arbitrary", "arbitrary")),
        name="expert_ffn",
    )(oa.reshape(-1), *ins, w_gate, w_up, w_down)


def _combine_kernel(oa_ref, nw_ref, rank_ref, xs_ref, mod_ref, *rest,
                    ctx, nblk, n_steps, win, tb, final, base):
    if final:
        fg_ref, ye_ref, o_ref, stack, sem = rest
    else:
        ye_ref, o_ref, stack, sem = rest
    b = pl.program_id(0)
    k = pl.program_id(1)
    step = b * nblk + k
    slot = step % 2
    ne = N_EXPERTS

    def copies(sl, bb, kk, j):
        out = []
        for e in range(ne):
            off = pl.multiple_of(base + oa_ref[(bb * (nblk + 1) + kk) * ne + e] + j * win, ROW_ALIGN)
            out.append(pltpu.make_async_copy(
                ye_ref.at[bb, e, pl.ds(off, win), :],
                stack.at[sl, pl.ds(e * win, win), :],
                sem.at[sl]))
        return out

    @pl.when(step == 0)
    def _():
        stack[2] = jnp.zeros(stack.shape[1:], BF16)
        for c in copies(0, b, k, 0):
            c.start()

    @pl.when(step + 1 < n_steps)
    def _():
        nxt = step + 1
        for c in copies(1 - slot, nxt // nblk, nxt % nblk, 0):
            c.start()

    for c in copies(slot, b, k, 0):
        c.wait()

    def weighted_sum(j, sl):
        word = rank_ref[0]
        slot1 = lax.broadcasted_iota(I32, (tb, win), 1) + (j * win + 1)
        y = None
        for e0 in range(0, ne, 2):
            pieces = []
            for e in (e0, e0 + 1):
                wb = jnp.broadcast_to(word[:, e:e + 1], (tb, win))
                gate = pltpu.bitcast(wb & jnp.int32(-65536), F32)
                pieces.append(jnp.where((wb & 0xFFFF) == slot1, gate, 0.0).astype(BF16))
            part = _dot(jnp.concatenate(pieces, axis=1), stack[sl, e0 * win:(e0 + 2) * win, :])
            y = part if y is None else y + part
        return y

    y = weighted_sum(jnp.int32(0), slot)

    nws = [nw_ref[(b * nblk + k) * ne + e] for e in range(ne)]
    nwx = functools.reduce(jnp.maximum, nws)

    def extra(j, y):
        cs = copies(2, b, k, j)
        for e, c in enumerate(cs):
            pl.when(j < nws[e])(c.start)
        for e, c in enumerate(cs):
            pl.when(j < nws[e])(c.wait)
        return y + weighted_sum(j, 2)

    y = lax.fori_loop(1, jnp.maximum(nwx, 1), extra, y)
    out = xs_ref[0] + _mod_row(mod_ref, ctx, 5) * y
    if final:
        out = out * lax.rsqrt(jnp.mean(out * out, axis=-1, keepdims=True) + EPS) * fg_ref[...]
    o_ref[0] = out


def combine(oa, nw, rank, xs, mod_l, ye, final_g=None, *, ctx, tb, win, base=0):
    bsz, n, d = xs.shape
    ne = N_EXPERTS
    nblk = n // tb
    final = final_g is not None
    full = lambda a: pl.BlockSpec(a.shape, lambda b, k, *_: (0,) * a.ndim)
    ins = [rank, xs, mod_l]
    in_specs = [
        pl.BlockSpec((1, tb, ne), lambda b, k, *_: (b, k, 0)),
        pl.BlockSpec((1, tb, d), lambda b, k, *_: (b, k, 0)),
        full(mod_l),
    ]
    if final:
        ins.append(final_g)
        in_specs.append(full(final_g))
    ins.append(ye)
    in_specs.append(pl.BlockSpec(memory_space=pl.ANY))
    grid_spec = pltpu.PrefetchScalarGridSpec(
        num_scalar_prefetch=2,
        grid=(bsz, nblk),
        in_specs=in_specs,
        out_specs=pl.BlockSpec((1, tb, d), lambda b, k, *_: (b, k, 0)),
        scratch_shapes=[
            pltpu.VMEM((3, ne * win, d), BF16),
            pltpu.SemaphoreType.DMA((3,)),
        ],
    )
    return pl.pallas_call(
        functools.partial(_combine_kernel, ctx=ctx, nblk=nblk, n_steps=bsz * nblk, win=win, tb=tb, final=final,
                          base=base),
        grid_spec=grid_spec,
        out_shape=jax.ShapeDtypeStruct(xs.shape, F32),
        compiler_params=_cparams(("arbitrary", "arbitrary")),
        name="combine_final" if final else ("combine_ctx" if ctx else "combine"),
    )(oa.reshape(-1), nw.reshape(-1), *ins)


def moe(token_sets, mod_l, w_gate, w_up, w_down, layer, final_g=None):
    tm = 768 if token_sets[0][0].shape[1] >= 4096 else 128
    routed = []
    for s, (xs1, h2, aff, is_ctx) in enumerate(token_sets):
        n = xs1.shape[1]
        cap = max(1, (EC_CAPACITY * n) // N_EXPERTS)
        tb = min(512, n)
        win_c = LANES
        win_d = 96 if tb == 512 else LANES
        nblk = n // tb
        align = tm if s == 0 else ROW_ALIGN
        c_pad = -(-(cap + ROW_ALIGN * nblk + win_c) // align) * align
        rank, rank_t, oa, nw_d, nw_c = route(aff, tb=tb, win_d=win_d, win_c=win_c)
        tail = tuple((off, min(win_d, c_pad - off)) for off in range(cap, c_pad, win_d))
        xg = dispatch(oa, nw_d, rank_t, h2, tb=tb, win=win_d, rows=c_pad, tail=tail)
        routed.append(dict(rank=rank, oa=oa, nw_c=nw_c, xg=xg, tb=tb, win_c=win_c, nblk=nblk))
    assert len(routed) <= 2
    yes = expert_ffn(routed[0]["oa"], routed[0]["xg"], w_gate, w_up, w_down, layer,
                     routed[1]["xg"] if len(routed) == 2 else None, tm=tm, nblk=routed[0]["nblk"])
    outs = []
    for s, ((xs1, h2, aff, is_ctx), r, ye) in enumerate(zip(token_sets, routed, yes)):
        outs.append(combine(r["oa"], r["nw_c"], r["rank"], xs1, mod_l, ye, final_g if s == 0 else None, ctx=is_ctx,
                            tb=r["tb"], win=r["win_c"]))
    return outs


def _swa_in_kernel(x_ref, mod_ref, g_ref, w_ref, cos_ref, sin_ref, q_ref, k_ref, v_ref, *, ctx, scale):
    h = _rms_mod(x_ref[0], g_ref[...], _mod_row(mod_ref, ctx, 0), _mod_row(mod_ref, ctx, 1))
    p = _dot(h.astype(BF16), w_ref[...])
    cos = cos_ref[...]
    sin = sin_ref[...]
    cs = cos * scale
    ss = sin * scale
    lane = lax.broadcasted_iota(I32, cos.shape, 1)
    qd = SWA_HEAD_DIM // 4
    first = (lane % (2 * qd)) < qd

    def partner(x):
        return jnp.where(first, pltpu.roll(x, LANES - qd, 1), pltpu.roll(x, qd, 1))

    for j in range(8):
        a = p[:, j * 128:(j + 1) * 128]
        q_ref[0, :, j * 128:(j + 1) * 128] = (a * cs + partner(a) * ss).astype(BF16)
    low = lane < 64
    for pr in range(2):
        kx = p[:, 1024 + pr * 128:1024 + (pr + 1) * 128]
        kp = kx * cos + partner(kx) * sin
        vp = p[:, 1280 + pr * 128:1280 + (pr + 1) * 128]
        for src, dst in ((kp, k_ref), (vp, v_ref)):
            ev_lo = jnp.where(low, src, 0.0)
            od_hi = jnp.where(low, 0.0, src)
            ev_hi = pltpu.roll(ev_lo, 64, 1)
            od_lo = pltpu.roll(od_hi, 64, 1)
            base = pr * 512
            dst[0, :, base:base + 128] = ev_lo.astype(BF16)
            dst[0, :, base + 128:base + 256] = ev_hi.astype(BF16)
            dst[0, :, base + 256:base + 384] = od_lo.astype(BF16)
            dst[0, :, base + 384:base + 512] = od_hi.astype(BF16)


def swa_in(x, mod_l, g, w2, cos_t, sin_t, *, ctx, tm):
    bsz, n, d = x.shape
    tm = min(tm, n)
    full = lambda a: pl.BlockSpec(a.shape, lambda b, i: (0,) * a.ndim)
    row = lambda w: pl.BlockSpec((1, tm, w), lambda b, i: (b, i, 0))
    return pl.pallas_call(
        functools.partial(_swa_in_kernel, ctx=ctx, scale=SWA_HEAD_DIM ** -0.5 * math.log2(math.e)),
        grid=(bsz, n // tm),
        in_specs=[row(d), full(mod_l), full(g), full(w2),
                  pl.BlockSpec((tm, 128), lambda b, i: (i, 0)),
                  pl.BlockSpec((tm, 128), lambda b, i: (i, 0))],
        out_specs=[row(1024), row(1024), row(1024)],
        out_shape=[jax.ShapeDtypeStruct((bsz, n, 1024), BF16)] * 3,
        compiler_params=_cparams(("arbitrary", "arbitrary")),
        name="swa_in_ctx" if ctx else "swa_in",
    )(x, mod_l, g, w2, cos_t, sin_t)


def _swa_kernel(sink_ref, q_ref, km_ref, kp_ref, kn_ref, vm_ref, vp_ref, vn_ref, kc_ref, vc_ref,
                o_ref, kwin, vwin, *, tq, n_tiles):
    i = pl.program_id(1)
    blk = SWA_BLOCK
    kwin[0:blk] = kp_ref[0]
    kwin[blk:blk + tq] = km_ref[0]
    kwin[blk + tq:blk + tq + blk] = kn_ref[0]
    vwin[0:blk] = vp_ref[0]
    vwin[blk:blk + tq] = vm_ref[0]
    vwin[blk + tq:blk + tq + blk] = vn_ref[0]
    r = lax.broadcasted_iota(I32, (blk, 3 * blk), 0)
    s = lax.broadcasted_iota(I32, (blk, 3 * blk), 1)
    band = jnp.abs(r - (s - blk)) <= WINDOW
    low_lanes = lax.broadcasted_iota(I32, (blk, 128), 1) < 64

    def qblock(qb, carry):
        row0 = pl.multiple_of(qb * blk, blk)
        first = jnp.logical_and(i == 0, qb == 0)
        last = jnp.logical_and(i == n_tiles - 1, qb == tq // blk - 1)
        valid = jnp.logical_and(band, jnp.logical_and(jnp.logical_or(s >= blk, jnp.logical_not(first)),
                                                      jnp.logical_or(s < 2 * blk, jnp.logical_not(last))))
        bias = jnp.where(valid, 0.0, -1e30)
        for kvh in range(SWA_KV_HEADS):
            c0 = kvh * 256
            q4 = jnp.concatenate([q_ref[0, pl.ds(row0, blk), c0:c0 + 128],
                                  q_ref[0, pl.ds(row0, blk), c0 + 128:c0 + 256]], axis=0)
            kl = jnp.concatenate([kwin[pl.ds(row0, 3 * blk), c0:c0 + 128],
                                  kwin[pl.ds(row0, 3 * blk), c0 + 128:c0 + 256]], axis=0)
            vl = jnp.concatenate([vwin[pl.ds(row0, 3 * blk), c0:c0 + 128],
                                  vwin[pl.ds(row0, 3 * blk), c0 + 128:c0 + 256]], axis=0)
            s_loc = _dot_nt(q4, kl)
            s_ctx = _dot_nt(q4, kc_ref[0, kvh])
            lc = s_ctx.shape[1] // 2
            p_loc, p_ctx, inv = [], [], []
            for pp in range(2):
                pl_row, pc_row, inv_row = [], [], []
                for hf in range(2):
                    sk = sink_ref[kvh * 4 + pp * 2 + hf]
                    sl = s_loc[pp * blk:(pp + 1) * blk, hf * 3 * blk:(hf + 1) * 3 * blk] + bias
                    sc = s_ctx[pp * blk:(pp + 1) * blk, hf * lc:(hf + 1) * lc]
                    m = jnp.maximum(jnp.maximum(jnp.max(sl, axis=-1, keepdims=True),
                                                jnp.max(sc, axis=-1, keepdims=True)), sk)
                    el = jnp.exp2(sl - m)
                    ec = jnp.exp2(sc - m)
                    den = (jnp.sum(el, axis=-1, keepdims=True) + jnp.sum(ec, axis=-1, keepdims=True)
                           + jnp.exp2(sk - m))
                    pl_row.append(el.astype(BF16))
                    pc_row.append(ec.astype(BF16))
                    inv_row.append(1.0 / den)
                p_loc.append(jnp.concatenate(pl_row, axis=1))
                p_ctx.append(jnp.concatenate(pc_row, axis=1))
                inv.append(jnp.where(low_lanes, inv_row[0], inv_row[1]))
            o4 = _dot(jnp.concatenate(p_loc, axis=0), vl) + _dot(jnp.concatenate(p_ctx, axis=0), vc_ref[0, kvh])
            for pp in range(2):
                pair = kvh * 2 + pp
                o_ref[0, pl.ds(row0, blk), pair * 128:(pair + 1) * 128] = (
                    o4[pp * blk:(pp + 1) * blk] * inv[pp]).astype(BF16)
        return carry

    lax.fori_loop(0, tq // blk, qblock, 0)


def swa_attention(q, k2, v2, kc2, vc2, sink, *, tq):
    bsz, n, _ = q.shape
    lc = kc2.shape[1]
    tq = min(tq, n)
    n_tiles = n // tq
    rb = tq // SWA_BLOCK
    nb = n // SWA_BLOCK
    main = pl.BlockSpec((1, tq, 1024), lambda b, i, *_: (b, i, 0))
    prev = pl.BlockSpec((1, SWA_BLOCK, 1024), lambda b, i, *_: (b, jnp.maximum(i * rb - 1, 0), 0))
    nxt = pl.BlockSpec((1, SWA_BLOCK, 1024), lambda b, i, *_: (b, jnp.minimum((i + 1) * rb, nb - 1), 0))
    stack = lambda a: a.reshape(bsz, lc, SWA_KV_HEADS, 2, 128).transpose(0, 2, 3, 1, 4).reshape(
        bsz, SWA_KV_HEADS, 2 * lc, 128)
    kc2, vc2 = stack(kc2), stack(vc2)
    sink = sink * math.log2(math.e)
    cspec = pl.BlockSpec((1, SWA_KV_HEADS, 2 * lc, 128), lambda b, i, *_: (b, 0, 0, 0))
    grid_spec = pltpu.PrefetchScalarGridSpec(
        num_scalar_prefetch=1,
        grid=(bsz, n_tiles),
        in_specs=[main, main, prev, nxt, main, prev, nxt, cspec, cspec],
        out_specs=main,
        scratch_shapes=[pltpu.VMEM((tq + 2 * SWA_BLOCK, 1024), BF16)] * 2,
    )
    return pl.pallas_call(
        functools.partial(_swa_kernel, tq=tq, n_tiles=n_tiles),
        grid_spec=grid_spec,
        out_shape=jax.ShapeDtypeStruct((bsz, n, 1024), BF16),
        compiler_params=_cparams(("arbitrary", "arbitrary")),
        name="swa_attention",
    )(sink, q, k2, k2, k2, v2, v2, v2, kc2, vc2)


def _prep_ab(w_in, w_uq, w_ukv, w_out):
    d = w_in.shape[0]
    perm = _rope_perm(MLA_ROPE)
    o = 3 * CONV_DIM + MLA_Q_LORA + MLA_KV_LORA
    dq = MLA_NOPE + MLA_ROPE
    kr = w_in[:, o:]
    rot_lanes = lambda t: jnp.pad(t, ((0, 0),) * (t.ndim - 1) + ((MLA_NOPE, LANES - dq),))
    heads = lambda t: t.reshape(t.shape[0], MLA_HEADS * LANES)
    win2 = jnp.concatenate([w_in[:, :o], rot_lanes(kr), rot_lanes(kr[:, perm])], axis=1).astype(BF16)
    wq = w_uq.reshape(MLA_Q_LORA, MLA_HEADS, dq)
    qa = heads(jnp.pad(wq, ((0, 0), (0, 0), (0, LANES - dq))))
    qb = heads(rot_lanes(wq[:, :, MLA_NOPE:][:, :, perm]))
    wq2 = jnp.concatenate([qa, qb], axis=1).astype(BF16)
    wkv = w_ukv.reshape(MLA_KV_LORA, MLA_HEADS, MLA_NOPE + MLA_V)
    half_lanes = lambda t: heads(jnp.pad(t, ((0, 0), (0, 0), (0, LANES - t.shape[-1]))))
    wkv2 = jnp.concatenate([half_lanes(wkv[:, :, :MLA_NOPE]), half_lanes(wkv[:, :, MLA_NOPE:])], axis=1).astype(BF16)
    wc = w_out[:CONV_DIM].astype(BF16)
    wa = jnp.pad(w_out[CONV_DIM:].reshape(MLA_HEADS, MLA_V, d), ((0, 0), (0, LANES - MLA_V), (0, 0)))
    return win2, wq2, wkv2, wc, wa.reshape(MLA_HEADS * LANES, d).astype(BF16)


def _prep_swa(w_qkv):
    return w_qkv.astype(BF16)


def _pad_lanes(t, left, width=LANES, fill=0.0):
    n, w = t.shape
    return jnp.concatenate([jnp.full((n, left), fill, F32), t, jnp.zeros((n, width - left - w), F32)], axis=1)


def kernel(x, c, ctx, c_ctx, mod_w, mod_b, norm1_g, norm2_g, ab_w_in, conv_w, conv_b, mla_q_norm_g, mla_w_uq,
           mla_kv_norm_g, mla_w_ukv, ab_w_out, swa_w_qkv, swa_sink, swa_w_out, router_w, exp_w_gate, exp_w_up,
           exp_w_down, final_g):
    bsz, n, d = x.shape
    lc = ctx.shape[1]
    depth = mod_w.shape[0]
    assert bsz <= 2 and d == 1024

    cs = jnp.concatenate([c, c_ctx[None, :], jnp.zeros((8 - bsz - 1, d), F32)], axis=0)
    mod = modulation(cs, mod_w, mod_b)

    cos_m, sin_m = _rope_tables(n, MLA_ROPE)
    cos_mla = _pad_lanes(cos_m, 64, fill=1.0)
    sin_mla = _pad_lanes(sin_m, 64)
    cos_mla_c = jnp.concatenate([jnp.ones((lc, 96), F32), jnp.zeros((lc, 32), F32)], axis=1)
    zeros_c = jnp.zeros((lc, 128), F32)
    cos_s, sin_s = _rope_tables(n, SWA_HEAD_DIM)
    cos_swa = jnp.concatenate([cos_s, cos_s], axis=1)
    sin_swa = jnp.concatenate([sin_s, sin_s], axis=1)
    ones_c = jnp.ones((lc, 128), F32)

    row2 = lambda v: v.reshape(1, -1)
    xs, xc = x, ctx
    for layer in range(depth):
        need_ctx = layer < depth - 1
        last = layer == depth - 1
        mod_l = mod[layer]
        g1 = row2(norm1_g[layer])
        g2 = row2(norm2_g[layer])
        rw_f = jnp.concatenate([router_w[layer], jnp.zeros((d, LANES - N_EXPERTS), F32)], axis=1)
        rw_hi = rw_f.astype(BF16)
        rw = jnp.concatenate([rw_hi, (rw_f - rw_hi.astype(F32)).astype(BF16)], axis=1)
        wg, wu, wd = exp_w_gate, exp_w_up, exp_w_down
        if layer % 2 == 0:
            e = layer // 2
            win2, wq2, wkv2, wc, wa = _prep_ab(ab_w_in[e], mla_w_uq[e], mla_w_ukv[e], ab_w_out[e])
            qg, kvg = row2(mla_q_norm_g[e]), row2(mla_kv_norm_g[e])
            cw, cb = conv_w[e], row2(conv_b[e])
            gb, z, q, k, v = ab_in(xs, mod_l, g1, win2, qg, wq2, kvg, wkv2, cos_mla, sin_mla, ctx=False, tm=512)
            gbc, zc, qc, kc, vc = ab_in(xc, mod_l, g1, win2, qg, wq2, kvg, wkv2, cos_mla_c, zeros_c, ctx=True, tm=256)
            att = mla_attention_flat(q, k, v, kc, vc, tq=512, tk_max=1280, spt=12)
            xs1, h2, aff = mix_out(att, wa, xs, mod_l, g2, rw, (gb, z, cw, cb, wc), ctx=False, tm=512)
            if need_ctx:
                att_c = mla_attention_ctx(qc, kc, vc)
                xc1, hc2, affc = mix_out(att_c, wa, xc, mod_l, g2, rw, (gbc, zc, cw, cb, wc), ctx=True, tm=256)
        else:
            o = layer // 2
            w2 = _prep_swa(swa_w_qkv[o])
            wo = swa_w_out[o].astype(BF16)
            q, k2, v2 = swa_in(xs, mod_l, g1, w2, cos_swa, sin_swa, ctx=False, tm=512)
            qc, kc2, vc2 = swa_in(xc, mod_l, g1, w2, ones_c, zeros_c, ctx=True, tm=256)
            att = swa_attention(q, k2, v2, kc2, vc2, swa_sink[o], tq=512)
            xs1, h2, aff = mix_out(att, wo, xs, mod_l, g2, rw, ctx=False, tm=512)
            if need_ctx:
                raise NotImplementedError("context self-attention for windowed layers below the last")
        token_sets = [(xs1, h2, aff, False)] + ([(xc1, hc2, affc, True)] if need_ctx else [])
        outs = moe(token_sets, mod_l, wg, wu, wd, layer, final_g=row2(final_g) if last else None)
        xs = outs[0]
        if need_ctx:
            xc = outs[1]
    return xs
```

```python
import functools
import math

import jax
import jax.numpy as jnp
import numpy as np
from jax import lax
from jax.experimental import pallas as pl
from jax.experimental.pallas import tpu as pltpu

F32 = jnp.float32
BF16 = jnp.bfloat16
I32 = jnp.int32

GRID_W = 64
EPS = 1e-6
ROPE_THETA = 10000.0
CONV_DIM = 512
MLA_HEADS = 8
MLA_Q_LORA = 256
MLA_KV_LORA = 128
MLA_NOPE = 64
MLA_ROPE = 32
MLA_V = 64
SWA_HEADS = 16
SWA_KV_HEADS = 4
SWA_HEAD_DIM = 64
WINDOW = 128
SWA_BLOCK = 128
N_EXPERTS = 16
EC_CAPACITY = 2
N_MOD = 6

LANES = 128
ROW_ALIGN = 16
VMEM_LIMIT = 56 * 1024 * 1024


def _cparams(sem, vmem=VMEM_LIMIT):
    return pltpu.CompilerParams(dimension_semantics=sem, vmem_limit_bytes=vmem)


def _dot(a, b):
    return jnp.dot(a, b, preferred_element_type=F32)


def _dot_nt(a, b):
    return lax.dot_general(a, b, (((1,), (1,)), ((), ())), preferred_element_type=F32)


def _dot_tn(a, b):
    return lax.dot_general(a, b, (((0,), (0,)), ((), ())), preferred_element_type=F32)


def _dot_hi(a, b):
    return jnp.dot(a, b, preferred_element_type=F32, precision=lax.Precision.HIGHEST)


def _silu(x):
    return x * (1.0 / (1.0 + jnp.exp(-x)))


def _rms_mod(x, g, shift, scale):
    y = x * lax.rsqrt(jnp.mean(x * x, axis=-1, keepdims=True) + EPS)
    return (y * g) * (1.0 + scale) + shift


def _mod_row(mod_ref, ctx, which):
    r = 2 if ctx else pl.program_id(0)
    return mod_ref[pl.ds(r, 1), which * 1024:(which + 1) * 1024]


def _mod_kernel(cs_ref, w_ref, b_ref, o_ref):
    s = _silu(cs_ref[...])
    o_ref[0] = _dot_hi(s, w_ref[0]) + b_ref[0]


def modulation(cs, mod_w, mod_b):
    depth, d, n6 = mod_w.shape
    tn = 1536
    return pl.pallas_call(
        _mod_kernel,
        grid=(depth, n6 // tn),
        in_specs=[
            pl.BlockSpec((8, d), lambda l, j: (0, 0)),
            pl.BlockSpec((1, d, tn), lambda l, j: (l, 0, j)),
            pl.BlockSpec((1, 1, tn), lambda l, j: (l, 0, j)),
        ],
        out_specs=pl.BlockSpec((1, 8, tn), lambda l, j: (l, 0, j)),
        out_shape=jax.ShapeDtypeStruct((depth, 8, n6), F32),
        compiler_params=_cparams(("arbitrary", "arbitrary")),
        name="modulation",
    )(cs, mod_w, mod_b.reshape(depth, 1, n6))


def _rope_tables(n, dim):
    n_rows = n // GRID_W
    half = dim // 2
    qd = dim // 4
    freqs = ROPE_THETA ** (-jnp.arange(0, half, 2, dtype=F32) / half)
    ang_r = jnp.arange(n_rows, dtype=F32)[:, None] * freqs
    ang_c = jnp.arange(GRID_W, dtype=F32)[:, None] * freqs

    def table(fr, fc, sign):
        r = jnp.broadcast_to(fr[:, None, :], (n_rows, GRID_W, qd))
        c = jnp.broadcast_to(fc[None, :, :], (n_rows, GRID_W, qd))
        return jnp.concatenate([sign * r, r, sign * c, c], axis=-1).reshape(n, dim)

    return table(jnp.cos(ang_r), jnp.cos(ang_c), 1.0), table(jnp.sin(ang_r), jnp.sin(ang_c), -1.0)


def _rope_perm(dim):
    qd = dim // 4
    ch = np.arange(dim)
    pair = (ch // qd) % 2
    return np.where(pair == 0, ch + qd, ch - qd)


def _ab_in_kernel(x_ref, mod_ref, g_ref, win_ref, qg_ref, wq_ref, kvg_ref, wkv_ref, cos_ref, sin_ref,
                  gb_ref, z_ref, q_ref, k_ref, v_ref, *, ctx, scale):
    x = x_ref[0]
    h = _rms_mod(x, g_ref[...], _mod_row(mod_ref, ctx, 0), _mod_row(mod_ref, ctx, 1))
    p = _dot(h.astype(BF16), win_ref[...])
    gb_ref[0] = p[:, 0:512].astype(BF16)
    z_ref[0] = (p[:, 512:1024] * p[:, 1024:1536]).astype(BF16)
    ql = p[:, 1536:1792]
    kvl = p[:, 1792:1920]
    kra = p[:, 1920:2048]
    krb = p[:, 2048:2176]
    cos = cos_ref[...]
    sin = sin_ref[...]
    qn = ql * lax.rsqrt(jnp.mean(ql * ql, axis=-1, keepdims=True) + EPS) * qg_ref[...]
    qq = _dot(qn.astype(BF16), wq_ref[...])
    kvn = kvl * lax.rsqrt(jnp.mean(kvl * kvl, axis=-1, keepdims=True) + EPS) * kvg_ref[...]
    kv = _dot(kvn.astype(BF16), wkv_ref[...])
    krope = kra * cos + krb * sin
    cs = cos * scale
    ss = sin * scale
    ones_lane = lax.broadcasted_iota(I32, cos.shape, 1) == MLA_V
    for hd in range(MLA_HEADS):
        a = qq[:, hd * 128:(hd + 1) * 128]
        b = qq[:, 1024 + hd * 128:1024 + (hd + 1) * 128]
        q_ref[0, hd] = (a * cs + b * ss).astype(BF16)
        k_ref[0, hd] = (kv[:, hd * 128:(hd + 1) * 128] + krope).astype(BF16)
        v_ref[0, hd] = jnp.where(ones_lane, 1.0, kv[:, 1024 + hd * 128:1024 + (hd + 1) * 128]).astype(BF16)


def ab_in(x, mod_l, g, win2, qg, wq2, kvg, wkv2, cos_t, sin_t, *, ctx, tm):
    bsz, n, d = x.shape
    tm = min(tm, n)
    scale = (MLA_NOPE + MLA_ROPE) ** -0.5 * math.log2(math.e)
    full = lambda a: pl.BlockSpec(a.shape, lambda b, i: (0,) * a.ndim)
    hshape = jax.ShapeDtypeStruct((bsz, MLA_HEADS, n, 128), BF16)
    hspec = pl.BlockSpec((1, MLA_HEADS, tm, 128), lambda b, i: (b, 0, i, 0))
    return pl.pallas_call(
        functools.partial(_ab_in_kernel, ctx=ctx, scale=scale),
        grid=(bsz, n // tm),
        in_specs=[
            pl.BlockSpec((1, tm, d), lambda b, i: (b, i, 0)),
            full(mod_l), full(g), full(win2), full(qg), full(wq2), full(kvg), full(wkv2),
            pl.BlockSpec((tm, 128), lambda b, i: (i, 0)),
            pl.BlockSpec((tm, 128), lambda b, i: (i, 0)),
        ],
        out_specs=[
            pl.BlockSpec((1, tm, 512), lambda b, i: (b, i, 0)),
            pl.BlockSpec((1, tm, 512), lambda b, i: (b, i, 0)),
            hspec, hspec, hspec,
        ],
        out_shape=[
            jax.ShapeDtypeStruct((bsz, n, 512), BF16),
            jax.ShapeDtypeStruct((bsz, n, 512), BF16),
            hshape, hshape, hshape,
        ],
        compiler_params=_cparams(("arbitrary", "arbitrary")),
        name="ab_in_ctx" if ctx else "ab_in",
    )(x, mod_l, g, win2, qg, wq2, kvg, wkv2, cos_t, sin_t)


def _mla_ctx_kernel(q_ref, kc_ref, vc_ref, o_ref):
    s = _dot_nt(q_ref[0, 0], kc_ref[0, 0])
    m = jnp.max(s, axis=-1, keepdims=True)
    acc = _dot(jnp.exp2(s - m).astype(BF16), vc_ref[0, 0])
    o_ref[0] = (acc * (1.0 / acc[:, MLA_V:MLA_V + 1])).astype(BF16)


def mla_attention_ctx(q, kc, vc):
    bsz, nh, lc, dk = q.shape
    kv = pl.BlockSpec((1, 1, lc, dk), lambda b, h: (b, h, 0, 0))
    return pl.pallas_call(
        _mla_ctx_kernel,
        grid=(bsz, nh),
        in_specs=[kv, kv, kv],
        out_specs=pl.BlockSpec((1, lc, dk), lambda b, h: (b, 0, h)),
        out_shape=jax.ShapeDtypeStruct((bsz, lc, nh * dk), BF16),
        compiler_params=_cparams(("arbitrary", "arbitrary")),
        name="mla_attention_ctx",
    )(q, kc, vc)


def _mla_flat_kernel(q_ref, k_ref, v_ref, kc_ref, vc_ref, o_ref, kall, vall, s_a, s_b, p_a, p_b,
                     *, tq, tk, n_tiles, n_chunks, spt):
    n_stages = n_tiles * n_chunks
    n_lat, n_ctx = k_ref.shape[2], kc_ref.shape[2]
    cp = math.gcd(n_lat, 2048)

    def copy_rows(c, carry):
        sl = pl.ds(pl.multiple_of(c * cp, cp), cp)
        kall[sl, :] = k_ref[0, 0, sl, :]
        vall[sl, :] = v_ref[0, 0, sl, :]
        return carry

    lax.fori_loop(0, n_lat // cp, copy_rows, 0)
    kall[n_lat:n_lat + n_ctx, :] = kc_ref[0, 0]
    vall[n_lat:n_lat + n_ctx, :] = vc_ref[0, 0]

    def nxt(c):
        i, j = c
        wrap = j + 1 == n_chunks
        return jnp.where(wrap, i + 1, i), jnp.where(wrap, 0, j + 1)

    def rows(idx, size):
        return pl.ds(pl.multiple_of(idx * size, size), size)

    def scores(c, s_buf):
        s = _dot_nt(q_ref[0, 0, rows(c[0], tq), :], kall[rows(c[1], tk), :])
        s_buf[...] = s
        return jnp.max(s, axis=-1, keepdims=True)

    def weights(c, s_buf, p_buf, mx, m):
        m_prev = jnp.where(c[1] == 0, -1e30, m)
        m_new = jnp.maximum(m_prev, mx)
        p_buf[...] = jnp.exp2(s_buf[...] - m_new).astype(BF16)
        return m_new, jnp.exp2(m_prev - m_new)

    def wsum(c, p_buf, alpha, acc):
        acc = alpha * acc + _dot(p_buf[...], vall[rows(c[1], tk), :])
        o_ref[0, pl.ds(pl.multiple_of(c[0] * tq, tq), tq), :] = (
            acc * (1.0 / acc[:, MLA_V:MLA_V + 1])).astype(BF16)
        return acc

    def trip(c_pv, c_w, c_s, s_cur, s_nxt, p_prev, p_cur, mx_cur, alpha_prev, m, acc):
        mx_nxt = scores(c_s, s_nxt) if c_s is not None else None
        acc = wsum(c_pv, p_prev, alpha_prev, acc)
        m, alpha_cur = weights(c_w, s_cur, p_cur, mx_cur, m)
        return mx_nxt, alpha_cur, m, acc

    def pair(c, mx_b, alpha_a, m, acc):
        mx_a, alpha_b, m, acc = trip(c[0], c[1], c[2], s_b, s_a, p_a, p_b, mx_b, alpha_a, m, acc)
        mx_b, alpha_a, m, acc = trip(c[1], c[2], c[3], s_a, s_b, p_b, p_a, mx_a, alpha_b, m, acc)
        return mx_b, alpha_a, m, acc

    def chain(c, k):
        out = [c]
        for _ in range(k):
            out.append(nxt(out[-1]))
        return out

    zero = jnp.int32(0)
    c0 = (zero, zero)
    mx_a = scores(c0, s_a)
    m, alpha_a = weights(c0, s_a, p_a, mx_a, jnp.zeros((tq, 1), F32))
    mx_b = scores(nxt(c0), s_b)
    acc = jnp.zeros((tq, LANES), F32)

    def run(n_st, carry):
        i, j, mx_b, alpha_a, m, acc = carry
        c = chain((i, j), n_st + 1)
        for u in range(0, n_st, 2):
            mx_b, alpha_a, m, acc = pair(c[u:u + 4], mx_b, alpha_a, m, acc)
        return c[n_st][0], c[n_st][1], mx_b, alpha_a, m, acc

    carry = run((n_stages - 4) % spt, (zero, zero, mx_b, alpha_a, m, acc))
    i, j, mx_b, alpha_a, m, acc = lax.fori_loop(0, (n_stages - 4) // spt, lambda _, c: run(spt, c), carry)
    c = chain((i, j), 3)
    mx_b, alpha_a, m, acc = pair(c[0:4], mx_b, alpha_a, m, acc)
    mx_a, alpha_b, m, acc = trip(c[2], c[3], None, s_b, s_a, p_a, p_b, mx_b, alpha_a, m, acc)
    wsum(c[3], p_b, alpha_b, acc)


def mla_attention_flat(q, k, v, kc, vc, *, tq, tk_max, spt):
    bsz, nh, n, dk = q.shape
    lc = kc.shape[2]
    nk = n + lc
    tq = min(tq, n)
    tk = max(t for t in range(256, tk_max + 1, 256) if nk % t == 0)
    n_tiles, n_chunks = n // tq, nk // tk
    assert spt % 2 == 0 and (n_tiles * n_chunks) % 2 == 0 and n_tiles * n_chunks >= 4
    head = lambda rows_: pl.BlockSpec((1, 1, rows_, dk), lambda b, h: (b, h, 0, 0))
    return pl.pallas_call(
        functools.partial(_mla_flat_kernel, tq=tq, tk=tk, n_tiles=n_tiles, n_chunks=n_chunks, spt=spt),
        grid=(bsz, nh),
        in_specs=[head(n), head(n), head(n), head(lc), head(lc)],
        out_specs=pl.BlockSpec((1, n, dk), lambda b, h: (b, 0, h)),
        out_shape=jax.ShapeDtypeStruct((bsz, n, nh * dk), BF16),
        scratch_shapes=([pltpu.VMEM((nk, dk), BF16)] * 2 + [pltpu.VMEM((tq, tk), F32)] * 2
                        + [pltpu.VMEM((tq, tk), BF16)] * 2),
        compiler_params=_cparams(("arbitrary", "arbitrary")),
        name="mla_attention",
    )(q, k, v, kc, vc)


def _mix_out_kernel(*refs, ctx, has_conv, tm, n_tiles):
    if has_conv:
        (gb_ref, z_ref, zp_ref, zn_ref, cw_ref, cb_ref, wc_ref,
         att_ref, wa_ref, x_ref, mod_ref, g2_ref, rw_ref, xs_ref, h2_ref, aff_ref) = refs
    else:
        att_ref, wa_ref, x_ref, mod_ref, g2_ref, rw_ref, xs_ref, h2_ref, aff_ref = refs
    y = _dot(att_ref[0], wa_ref[...])
    if has_conv:
        i = pl.program_id(1)
        z = z_ref[0].astype(F32)
        rows = lax.broadcasted_iota(I32, z.shape, 0)
        zprev_halo = jnp.where(i > 0, zp_ref[0, ROW_ALIGN - 1:ROW_ALIGN, :].astype(F32), 0.0)
        znext_halo = jnp.where(i < n_tiles - 1, zn_ref[0, 0:1, :].astype(F32), 0.0)
        zprev = jnp.where(rows == 0, zprev_halo, pltpu.roll(z, 1, 0))
        znext = jnp.where(rows == tm - 1, znext_halo, pltpu.roll(z, tm - 1, 0))
        cw = cw_ref[...]
        conv = gb_ref[0].astype(F32) * (zprev * cw[0:1] + z * cw[1:2] + znext * cw[2:3] + cb_ref[...])
        y = y + _dot(conv.astype(BF16), wc_ref[...])
    xs = x_ref[0] + _mod_row(mod_ref, ctx, 2) * y
    xs_ref[0] = xs
    h2 = _rms_mod(xs, g2_ref[...], _mod_row(mod_ref, ctx, 3), _mod_row(mod_ref, ctx, 4))
    h2_ref[0] = h2.astype(BF16)
    h2_hi = h2.astype(BF16)
    h2_lo = (h2 - h2_hi.astype(F32)).astype(BF16)
    hh = _dot(h2_hi, rw_ref[...])
    logits = hh[:, :LANES] + hh[:, LANES:] + _dot(h2_lo, rw_ref[:, :LANES])
    lane = lax.broadcasted_iota(I32, logits.shape, 1)
    logits = jnp.where(lane < N_EXPERTS, logits, -1e30)
    mx = jnp.max(logits, axis=-1, keepdims=True)
    ex = jnp.exp(logits - mx)
    aff = ex / jnp.sum(ex, axis=-1, keepdims=True)
    aff_ref[0] = aff[:, :N_EXPERTS]


def mix_out(att, wa, x, mod_l, g2, rw, conv=None, *, ctx, tm):
    bsz, n, d = x.shape
    tm = min(tm, n)
    n_tiles = n // tm
    full = lambda a: pl.BlockSpec(a.shape, lambda b, i: (0,) * a.ndim)
    row = lambda w: pl.BlockSpec((1, tm, w), lambda b, i: (b, i, 0))
    ins, in_specs = [], []
    if conv is not None:
        gb, z, cw, cb, wc = conv
        r8 = tm // ROW_ALIGN
        nb8 = n // ROW_ALIGN
        ins += [gb, z, z, z, cw, cb, wc]
        in_specs += [
            row(512), row(512),
            pl.BlockSpec((1, ROW_ALIGN, 512), lambda b, i: (b, jnp.maximum(i * r8 - 1, 0), 0)),
            pl.BlockSpec((1, ROW_ALIGN, 512), lambda b, i: (b, jnp.minimum((i + 1) * r8, nb8 - 1), 0)),
            full(cw), full(cb), full(wc),
        ]
    ins += [att, wa, x, mod_l, g2, rw]
    in_specs += [row(att.shape[-1]), full(wa), row(d), full(mod_l), full(g2), full(rw)]
    return pl.pallas_call(
        functools.partial(_mix_out_kernel, ctx=ctx, has_conv=conv is not None, tm=tm, n_tiles=n_tiles),
        grid=(bsz, n_tiles),
        in_specs=in_specs,
        out_specs=[row(d), row(d), row(N_EXPERTS)],
        out_shape=[
            jax.ShapeDtypeStruct((bsz, n, d), F32),
            jax.ShapeDtypeStruct((bsz, n, d), BF16),
            jax.ShapeDtypeStruct((bsz, n, N_EXPERTS), F32),
        ],
        compiler_params=_cparams(("arbitrary", "arbitrary")),
        name=("mix_out_ctx" if ctx else "mix_out") + ("_conv" if conv is not None else ""),
    )(*ins)


def _route_kernel(aff_ref, aff8_ref, ls_ref, us_ref, eye_ref, rank_ref, rankt_ref, oa_ref, nwd_ref, nwc_ref,
                  thr_s, need_s, eq_s, oa_s, *, n, cap, tb, win_d, win_c, chunk):
    n_chunks = (n // 8) // chunk
    nblk = n // tb
    k = pl.program_id(1)

    def count(pred_fn):
        def body(c, acc):
            kb = pltpu.bitcast(aff8_ref[0, pl.ds(pl.multiple_of(c * chunk, chunk), chunk), :], I32)
            return acc + jnp.sum(pred_fn(kb).astype(I32), axis=0, keepdims=True)
        acc = lax.fori_loop(0, n_chunks, body, jnp.zeros((1, LANES), I32))
        for sh in (64, 32, 16):
            acc = acc + pltpu.roll(acc, sh, 1)
        return acc

    @pl.when(k == 0)
    def _():
        def bit_body(i, thr):
            cand = thr | jnp.left_shift(jnp.int32(1), 30 - i)
            cnt = count(lambda kb: kb >= cand)
            return jnp.where(cnt >= cap, cand, thr)

        thr = lax.fori_loop(0, 31, bit_body, jnp.zeros((1, LANES), I32))
        thr_s[...] = thr[:, :N_EXPERTS]
        need_s[...] = (cap - count(lambda kb: kb > thr))[:, :N_EXPERTS].astype(F32)
        eq_s[...] = jnp.zeros((1, N_EXPERTS), F32)
        oa_s[...] = jnp.zeros((1, N_EXPERTS), F32)

    thr = thr_s[...]
    off = pl.multiple_of(k * tb, tb)
    kb = pltpu.bitcast(aff_ref[0, pl.ds(off, tb), :], I32)
    gt = kb > thr
    eq = kb == thr
    eqf = jnp.where(eq, 1.0, 0.0)
    eqrank = eq_s[...] + _dot(ls_ref[...], eqf.astype(BF16))
    sel = jnp.logical_or(gt, jnp.logical_and(eq, eqrank < need_s[...]))
    self_ = jnp.where(sel, 1.0, 0.0)
    selb = self_.astype(BF16)
    lrank = _dot(ls_ref[...], selb)
    gate_bits = pltpu.bitcast(aff_ref[0, pl.ds(off, tb), :].astype(BF16).astype(F32), I32)
    rank_ref[0] = gate_bits | jnp.where(sel, lrank.astype(I32) + 1, 0)
    lrank_t = _dot_tn(selb, us_ref[...])
    sel_t = _dot_tn(selb, eye_ref[...])
    rankt_ref[0] = jnp.where(sel_t > 0.5, lrank_t, -1.0)
    c = jnp.sum(self_, axis=0, keepdims=True)
    oa_run = oa_s[...]
    oa_ref[0, pl.ds(k, 1), :] = oa_run.astype(I32)
    nwd_ref[0, pl.ds(k, 1), :] = jnp.ceil(c * (1.0 / win_d)).astype(I32)
    nwc_ref[0, pl.ds(k, 1), :] = jnp.ceil(c * (1.0 / win_c)).astype(I32)
    oa_new = oa_run + jnp.ceil(c * (1.0 / ROW_ALIGN)) * ROW_ALIGN
    oa_s[...] = oa_new
    eq_s[...] = eq_s[...] + jnp.sum(eqf, axis=0, keepdims=True)

    @pl.when(k == nblk - 1)
    def _():
        oa_ref[0, pl.ds(nblk, 1), :] = oa_new.astype(I32)


def route(aff, *, tb, win_d, win_c):
    bsz, n, ne = aff.shape
    cap = max(1, (EC_CAPACITY * n) // ne)
    nblk = n // tb
    assert ne * 8 == LANES
    aff8 = aff.reshape(bsz, n // 8, LANES)
    chunk = min(256, n // 8)
    ii = np.arange(tb)
    ls = jnp.asarray(ii[None, :] < ii[:, None], BF16)
    us = jnp.asarray(ii[:, None] < ii[None, :], BF16)
    eye = jnp.asarray(ii[:, None] == ii[None, :], BF16)
    full = lambda a: pl.BlockSpec(a.shape, lambda b, k: (0,) * a.ndim)
    return pl.pallas_call(
        functools.partial(_route_kernel, n=n, cap=cap, tb=tb, win_d=win_d, win_c=win_c, chunk=chunk),
        grid=(bsz, nblk),
        in_specs=[pl.BlockSpec((1, n, ne), lambda b, k: (b, 0, 0)),
                  pl.BlockSpec((1, n // 8, LANES), lambda b, k: (b, 0, 0)), full(ls), full(us), full(eye)],
        out_specs=[
            pl.BlockSpec((1, tb, ne), lambda b, k: (b, k, 0)),
            pl.BlockSpec((1, ne, tb), lambda b, k: (b, 0, k)),
            pl.BlockSpec((1, nblk + 1, ne), lambda b, k: (b, 0, 0)),
            pl.BlockSpec((1, nblk, ne), lambda b, k: (b, 0, 0)),
            pl.BlockSpec((1, nblk, ne), lambda b, k: (b, 0, 0)),
        ],
        out_shape=[
            jax.ShapeDtypeStruct((bsz, n, ne), I32),
            jax.ShapeDtypeStruct((bsz, ne, n), F32),
            jax.ShapeDtypeStruct((bsz, nblk + 1, ne), I32),
            jax.ShapeDtypeStruct((bsz, nblk, ne), I32),
            jax.ShapeDtypeStruct((bsz, nblk, ne), I32),
        ],
        scratch_shapes=[pltpu.VMEM((1, ne), I32), pltpu.VMEM((1, ne), F32),
                        pltpu.VMEM((1, ne), F32), pltpu.VMEM((1, ne), F32)],
        compiler_params=_cparams(("arbitrary", "arbitrary")),
        name="route",
    )(aff, aff8, ls, us, eye)


def _dispatch_kernel(oa_ref, nw_ref, rankt_ref, h_ref, *rest, nblk, n_steps, win, tb, group, tail, base):
    xg_ref, stack, sem = rest[-3:]
    b = pl.program_id(0)
    k = pl.program_id(1)
    step = b * nblk + k
    slot = step % 2
    ne = N_EXPERTS

    def fill(sl, j):
        h = h_ref[0]
        for g0 in range(0, ne, group):
            pieces = []
            for e in range(g0, g0 + group):
                r = rankt_ref[0, e:e + 1, :]
                srow = lax.broadcasted_iota(I32, (win, tb), 0).astype(F32) + (j * win).astype(F32)
                pieces.append((r == srow).astype(BF16))
            oh_t = jnp.concatenate(pieces, axis=0)
            stack[sl, g0 * win:(g0 + group) * win, :] = _dot(oh_t, h).astype(BF16)

    def copies(sl, bb, kk, j):
        out = []
        for e in range(ne):
            off = pl.multiple_of(base + oa_ref[(bb * (nblk + 1) + kk) * ne + e] + j * win, ROW_ALIGN)
            out.append(pltpu.make_async_copy(
                stack.at[sl, pl.ds(e * win, win), :],
                xg_ref.at[bb, e, pl.ds(off, win), :],
                sem.at[sl]))
        return out

    @pl.when(k == 0)
    def _():
        stack[2, 0:win, :] = jnp.zeros((win, stack.shape[2]), BF16)
        cs = [pltpu.make_async_copy(stack.at[2, pl.ds(0, sz), :], xg_ref.at[b, e, pl.ds(base + off, sz), :],
                                    sem.at[2])
              for e in range(ne) for off, sz in tail]
        for c in cs:
            c.start()
        for c in cs:
            c.wait()

    fill(slot, jnp.int32(0))

    @pl.when(step > 0)
    def _():
        for c in copies(1 - slot, b, k, 0):
            c.wait()

    for c in copies(slot, b, k, 0):
        c.start()

    nws = [nw_ref[(b * nblk + k) * ne + e] for e in range(ne)]
    nwx = functools.reduce(jnp.maximum, nws)

    def extra(j, carry):
        fill(2, j)
        cs = copies(2, b, k, j)
        for e, c in enumerate(cs):
            pl.when(j < nws[e])(c.start)
        for e, c in enumerate(cs):
            pl.when(j < nws[e])(c.wait)
        return carry

    lax.fori_loop(1, jnp.maximum(nwx, 1), extra, 0)

    @pl.when(step == n_steps - 1)
    def _():
        for c in copies(slot, b, k, 0):
            c.wait()


def dispatch(oa, nw, rank_t, h2, *, tb, win, rows, tail, base=0, into=None):
    bsz, n, d = h2.shape
    ne = N_EXPERTS
    nblk = n // tb
    group = 4
    ins = [oa.reshape(-1), nw.reshape(-1), rank_t, h2]
    in_specs = [
        pl.BlockSpec((1, ne, tb), lambda b, k, *_: (b, 0, k)),
        pl.BlockSpec((1, tb, d), lambda b, k, *_: (b, k, 0)),
    ]
    aliases = {}
    if into is not None:
        assert into.shape == (bsz, ne, rows, d)
        aliases = {len(ins): 0}
        ins.append(into)
        in_specs.append(pl.BlockSpec(memory_space=pl.ANY))
    grid_spec = pltpu.PrefetchScalarGridSpec(
        num_scalar_prefetch=2,
        grid=(bsz, nblk),
        in_specs=in_specs,
        out_specs=pl.BlockSpec(memory_space=pl.ANY),
        scratch_shapes=[
            pltpu.VMEM((3, ne * win, d), BF16),
            pltpu.SemaphoreType.DMA((3,)),
        ],
    )
    return pl.pallas_call(
        functools.partial(_dispatch_kernel, nblk=nblk, n_steps=bsz * nblk, win=win, tb=tb, group=group, tail=tail,
                          base=base),
        grid_spec=grid_spec,
        out_shape=jax.ShapeDtypeStruct((bsz, ne, rows, d), BF16),
        input_output_aliases=aliases,
        compiler_params=_cparams(("arbitrary", "arbitrary")),
        name="dispatch",
    )(*ins)


def _ffn_kernel(tot_ref, x_ref, *rest, tm, nblk, n_tiles, has_small):
    if has_small:
        xs_ref, wg_ref, wu_ref, wd_ref, y_ref, ys_ref, wgb, wub, wdb = rest
    else:
        wg_ref, wu_ref, wd_ref, y_ref, wgb, wub, wdb = rest
    e = pl.program_id(0)
    b = pl.program_id(1)
    i = pl.program_id(2)

    @pl.when(jnp.logical_and(b == 0, i == 0))
    def _():
        wgb[...] = wg_ref[0, 0].astype(BF16)
        wub[...] = wu_ref[0, 0].astype(BF16)
        wdb[...] = wd_ref[0, 0].astype(BF16)

    def ffn(x):
        hid = (_silu(_dot(x, wgb[...])) * _dot(x, wub[...])).astype(BF16)
        return _dot(hid, wdb[...]).astype(BF16)

    total = tot_ref[(b * (nblk + 1) + nblk) * N_EXPERTS + e]

    @pl.when(jnp.logical_and(i < n_tiles, i * tm < total))
    def _():
        y_ref[0, 0] = ffn(x_ref[0, 0])

    @pl.when(jnp.logical_and(i < n_tiles, i * tm >= total))
    def _():
        y_ref[0, 0] = jnp.zeros(y_ref.shape[2:], BF16)

    if has_small:
        @pl.when(i == n_tiles)
        def _():
            ys_ref[0, 0] = ffn(xs_ref[0, 0])


def expert_ffn(oa, xg, w_gate, w_up, w_down, layer, xg_small=None, *, tm, nblk):
    bsz, ne, c_pad, d = xg.shape
    f = w_gate.shape[-1]
    n_tiles = c_pad // tm
    has_small = xg_small is not None
    tile = pl.BlockSpec((1, 1, tm, d), lambda e, b, i, *_: (b, e, jnp.minimum(i, n_tiles - 1), 0))
    wspec = lambda r, c: pl.BlockSpec((1, 1, r, c), lambda e, b, i, *_: (layer, e, 0, 0))
    ins, in_specs, out_specs, out_shape = [xg], [tile], [tile], [jax.ShapeDtypeStruct(xg.shape, BF16)]
    if has_small:
        small = pl.BlockSpec((1, 1) + xg_small.shape[2:], lambda e, b, i, *_: (b, e, 0, 0))
        ins.append(xg_small)
        in_specs.append(small)
        out_specs.append(small)
        out_shape.append(jax.ShapeDtypeStruct(xg_small.shape, BF16))
    grid_spec = pltpu.PrefetchScalarGridSpec(
        num_scalar_prefetch=1,
        grid=(ne, bsz, n_tiles + int(has_small)),
        in_specs=in_specs + [wspec(d, f), wspec(d, f), wspec(f, d)],
        out_specs=out_specs,
        scratch_shapes=[pltpu.VMEM((d, f), BF16), pltpu.VMEM((d, f), BF16), pltpu.VMEM((f, d), BF16)],
    )
    return pl.pallas_call(
        functools.partial(_ffn_kernel, tm=tm, nblk=nblk, n_tiles=n_tiles, has_small=has_small),
        grid_spec=grid_spec,
        out_shape=out_shape,
        compiler_params=_cparams(("arbitrary", "arbitrary", "arbitrary")),
        name="expert_ffn",
    )(oa.reshape(-1), *ins, w_gate, w_up, w_down)


def _combine_kernel(oa_ref, nw_ref, rank_ref, xs_ref, mod_ref, *rest,
                    ctx, nblk, n_steps, win, tb, final, base):
    if final:
        fg_ref, ye_ref, o_ref, stack, sem = rest
    else:
        ye_ref, o_ref, stack, sem = rest
    b = pl.program_id(0)
    k = pl.program_id(1)
    step = b * nblk + k
    slot = step % 2
    ne = N_EXPERTS

    def copies(sl, bb, kk, j):
        out = []
        for e in range(ne):
            off = pl.multiple_of(base + oa_ref[(bb * (nblk + 1) + kk) * ne + e] + j * win, ROW_ALIGN)
            out.append(pltpu.make_async_copy(
                ye_ref.at[bb, e, pl.ds(off, win), :],
                stack.at[sl, pl.ds(e * win, win), :],
                sem.at[sl]))
        return out

    @pl.when(step == 0)
    def _():
        stack[2] = jnp.zeros(stack.shape[1:], BF16)
        for c in copies(0, b, k, 0):
            c.start()

    @pl.when(step + 1 < n_steps)
    def _():
        nxt = step + 1
        for c in copies(1 - slot, nxt // nblk, nxt % nblk, 0):
            c.start()

    for c in copies(slot, b, k, 0):
        c.wait()

    def weighted_sum(j, sl):
        word = rank_ref[0]
        slot1 = lax.broadcasted_iota(I32, (tb, win), 1) + (j * win + 1)
        y = None
        for e0 in range(0, ne, 2):
            pieces = []
            for e in (e0, e0 + 1):
                wb = jnp.broadcast_to(word[:, e:e + 1], (tb, win))
                gate = pltpu.bitcast(wb & jnp.int32(-65536), F32)
                pieces.append(jnp.where((wb & 0xFFFF) == slot1, gate, 0.0).astype(BF16))
            part = _dot(jnp.concatenate(pieces, axis=1), stack[sl, e0 * win:(e0 + 2) * win, :])
            y = part if y is None else y + part
        return y

    y = weighted_sum(jnp.int32(0), slot)

    nws = [nw_ref[(b * nblk + k) * ne + e] for e in range(ne)]
    nwx = functools.reduce(jnp.maximum, nws)

    def extra(j, y):
        cs = copies(2, b, k, j)
        for e, c in enumerate(cs):
            pl.when(j < nws[e])(c.start)
        for e, c in enumerate(cs):
            pl.when(j < nws[e])(c.wait)
        return y + weighted_sum(j, 2)

    y = lax.fori_loop(1, jnp.maximum(nwx, 1), extra, y)
    out = xs_ref[0] + _mod_row(mod_ref, ctx, 5) * y
    if final:
        out = out * lax.rsqrt(jnp.mean(out * out, axis=-1, keepdims=True) + EPS) * fg_ref[...]
    o_ref[0] = out


def combine(oa, nw, rank, xs, mod_l, ye, final_g=None, *, ctx, tb, win, base=0):
    bsz, n, d = xs.shape
    ne = N_EXPERTS
    nblk = n // tb
    final = final_g is not None
    full = lambda a: pl.BlockSpec(a.shape, lambda b, k, *_: (0,) * a.ndim)
    ins = [rank, xs, mod_l]
    in_specs = [
        pl.BlockSpec((1, tb, ne), lambda b, k, *_: (b, k, 0)),
        pl.BlockSpec((1, tb, d), lambda b, k, *_: (b, k, 0)),
        full(mod_l),
    ]
    if final:
        ins.append(final_g)
        in_specs.append(full(final_g))
    ins.append(ye)
    in_specs.append(pl.BlockSpec(memory_space=pl.ANY))
    grid_spec = pltpu.PrefetchScalarGridSpec(
        num_scalar_prefetch=2,
        grid=(bsz, nblk),
        in_specs=in_specs,
        out_specs=pl.BlockSpec((1, tb, d), lambda b, k, *_: (b, k, 0)),
        scratch_shapes=[
            pltpu.VMEM((3, ne * win, d), BF16),
            pltpu.SemaphoreType.DMA((3,)),
        ],
    )
    return pl.pallas_call(
        functools.partial(_combine_kernel, ctx=ctx, nblk=nblk, n_steps=bsz * nblk, win=win, tb=tb, final=final,
                          base=base),
        grid_spec=grid_spec,
        out_shape=jax.ShapeDtypeStruct(xs.shape, F32),
        compiler_params=_cparams(("arbitrary", "arbitrary")),
        name="combine_final" if final else ("combine_ctx" if ctx else "combine"),
    )(oa.reshape(-1), nw.reshape(-1), *ins)


def moe(token_sets, mod_l, w_gate, w_up, w_down, layer, final_g=None):
    tm = 768 if token_sets[0][0].shape[1] >= 4096 else 128
    routed = []
    for s, (xs1, h2, aff, is_ctx) in enumerate(token_sets):
        n = xs1.shape[1]
        cap = max(1, (EC_CAPACITY * n) // N_EXPERTS)
        tb = min(512, n)
        win_c = LANES
        win_d = 96 if tb == 512 else LANES
        nblk = n // tb
        align = tm if s == 0 else ROW_ALIGN
        c_pad = -(-(cap + ROW_ALIGN * nblk + win_c) // align) * align
        rank, rank_t, oa, nw_d, nw_c = route(aff, tb=tb, win_d=win_d, win_c=win_c)
        tail = tuple((off, min(win_d, c_pad - off)) for off in range(cap, c_pad, win_d))
        xg = dispatch(oa, nw_d, rank_t, h2, tb=tb, win=win_d, rows=c_pad, tail=tail)
        routed.append(dict(rank=rank, oa=oa, nw_c=nw_c, xg=xg, tb=tb, win_c=win_c, nblk=nblk))
    assert len(routed) <= 2
    yes = expert_ffn(routed[0]["oa"], routed[0]["xg"], w_gate, w_up, w_down, layer,
                     routed[1]["xg"] if len(routed) == 2 else None, tm=tm, nblk=routed[0]["nblk"])
    outs = []
    for s, ((xs1, h2, aff, is_ctx), r, ye) in enumerate(zip(token_sets, routed, yes)):
        outs.append(combine(r["oa"], r["nw_c"], r["rank"], xs1, mod_l, ye, final_g if s == 0 else None, ctx=is_ctx,
                            tb=r["tb"], win=r["win_c"]))
    return outs


def _swa_in_kernel(x_ref, mod_ref, g_ref, w_ref, cos_ref, sin_ref, q_ref, k_ref, v_ref, *, ctx, scale):
    h = _rms_mod(x_ref[0], g_ref[...], _mod_row(mod_ref, ctx, 0), _mod_row(mod_ref, ctx, 1))
    p = _dot(h.astype(BF16), w_ref[...])
    cos = cos_ref[...]
    sin = sin_ref[...]
    cs = cos * scale
    ss = sin * scale
    lane = lax.broadcasted_iota(I32, cos.shape, 1)
    qd = SWA_HEAD_DIM // 4
    first = (lane % (2 * qd)) < qd

    def partner(x):
        return jnp.where(first, pltpu.roll(x, LANES - qd, 1), pltpu.roll(x, qd, 1))

    for j in range(8):
        a = p[:, j * 128:(j + 1) * 128]
        q_ref[0, :, j * 128:(j + 1) * 128] = (a * cs + partner(a) * ss).astype(BF16)
    low = lane < 64
    for pr in range(2):
        kx = p[:, 1024 + pr * 128:1024 + (pr + 1) * 128]
        kp = kx * cos + partner(kx) * sin
        vp = p[:, 1280 + pr * 128:1280 + (pr + 1) * 128]
        for src, dst in ((kp, k_ref), (vp, v_ref)):
            ev_lo = jnp.where(low, src, 0.0)
            od_hi = jnp.where(low, 0.0, src)
            ev_hi = pltpu.roll(ev_lo, 64, 1)
            od_lo = pltpu.roll(od_hi, 64, 1)
            base = pr * 512
            dst[0, :, base:base + 128] = ev_lo.astype(BF16)
            dst[0, :, base + 128:base + 256] = ev_hi.astype(BF16)
            dst[0, :, base + 256:base + 384] = od_lo.astype(BF16)
            dst[0, :, base + 384:base + 512] = od_hi.astype(BF16)


def swa_in(x, mod_l, g, w2, cos_t, sin_t, *, ctx, tm):
    bsz, n, d = x.shape
    tm = min(tm, n)
    full = lambda a: pl.BlockSpec(a.shape, lambda b, i: (0,) * a.ndim)
    row = lambda w: pl.BlockSpec((1, tm, w), lambda b, i: (b, i, 0))
    return pl.pallas_call(
        functools.partial(_swa_in_kernel, ctx=ctx, scale=SWA_HEAD_DIM ** -0.5 * math.log2(math.e)),
        grid=(bsz, n // tm),
        in_specs=[row(d), full(mod_l), full(g), full(w2),
                  pl.BlockSpec((tm, 128), lambda b, i: (i, 0)),
                  pl.BlockSpec((tm, 128), lambda b, i: (i, 0))],
        out_specs=[row(1024), row(1024), row(1024)],
        out_shape=[jax.ShapeDtypeStruct((bsz, n, 1024), BF16)] * 3,
        compiler_params=_cparams(("arbitrary", "arbitrary")),
        name="swa_in_ctx" if ctx else "swa_in",
    )(x, mod_l, g, w2, cos_t, sin_t)


def _swa_kernel(sink_ref, q_ref, km_ref, kp_ref, kn_ref, vm_ref, vp_ref, vn_ref, kc_ref, vc_ref,
                o_ref, kwin, vwin, *, tq, n_tiles):
    i = pl.program_id(1)
    blk = SWA_BLOCK
    kwin[0:blk] = kp_ref[0]
    kwin[blk:blk + tq] = km_ref[0]
    kwin[blk + tq:blk + tq + blk] = kn_ref[0]
    vwin[0:blk] = vp_ref[0]
    vwin[blk:blk + tq] = vm_ref[0]
    vwin[blk + tq:blk + tq + blk] = vn_ref[0]
    r = lax.broadcasted_iota(I32, (blk, 3 * blk), 0)
    s = lax.broadcasted_iota(I32, (blk, 3 * blk), 1)
    band = jnp.abs(r - (s - blk)) <= WINDOW
    low_lanes = lax.broadcasted_iota(I32, (blk, 128), 1) < 64

    def qblock(qb, carry):
        row0 = pl.multiple_of(qb * blk, blk)
        first = jnp.logical_and(i == 0, qb == 0)
        last = jnp.logical_and(i == n_tiles - 1, qb == tq // blk - 1)
        valid = jnp.logical_and(band, jnp.logical_and(jnp.logical_or(s >= blk, jnp.logical_not(first)),
                                                      jnp.logical_or(s < 2 * blk, jnp.logical_not(last))))
        bias = jnp.where(valid, 0.0, -1e30)
        for kvh in range(SWA_KV_HEADS):
            c0 = kvh * 256
            q4 = jnp.concatenate([q_ref[0, pl.ds(row0, blk), c0:c0 + 128],
                                  q_ref[0, pl.ds(row0, blk), c0 + 128:c0 + 256]], axis=0)
            kl = jnp.concatenate([kwin[pl.ds(row0, 3 * blk), c0:c0 + 128],
                                  kwin[pl.ds(row0, 3 * blk), c0 + 128:c0 + 256]], axis=0)
            vl = jnp.concatenate([vwin[pl.ds(row0, 3 * blk), c0:c0 + 128],
                                  vwin[pl.ds(row0, 3 * blk), c0 + 128:c0 + 256]], axis=0)
            s_loc = _dot_nt(q4, kl)
            s_ctx = _dot_nt(q4, kc_ref[0, kvh])
            lc = s_ctx.shape[1] // 2
            p_loc, p_ctx, inv = [], [], []
            for pp in range(2):
                pl_row, pc_row, inv_row = [], [], []
                for hf in range(2):
                    sk = sink_ref[kvh * 4 + pp * 2 + hf]
                    sl = s_loc[pp * blk:(pp + 1) * blk, hf * 3 * blk:(hf + 1) * 3 * blk] + bias
                    sc = s_ctx[pp * blk:(pp + 1) * blk, hf * lc:(hf + 1) * lc]
                    m = jnp.maximum(jnp.maximum(jnp.max(sl, axis=-1, keepdims=True),
                                                jnp.max(sc, axis=-1, keepdims=True)), sk)
                    el = jnp.exp2(sl - m)
                    ec = jnp.exp2(sc - m)
                    den = (jnp.sum(el, axis=-1, keepdims=True) + jnp.sum(ec, axis=-1, keepdims=True)
                           + jnp.exp2(sk - m))
                    pl_row.append(el.astype(BF16))
                    pc_row.append(ec.astype(BF16))
                    inv_row.append(1.0 / den)
                p_loc.append(jnp.concatenate(pl_row, axis=1))
                p_ctx.append(jnp.concatenate(pc_row, axis=1))
                inv.append(jnp.where(low_lanes, inv_row[0], inv_row[1]))
            o4 = _dot(jnp.concatenate(p_loc, axis=0), vl) + _dot(jnp.concatenate(p_ctx, axis=0), vc_ref[0, kvh])
            for pp in range(2):
                pair = kvh * 2 + pp
                o_ref[0, pl.ds(row0, blk), pair * 128:(pair + 1) * 128] = (
                    o4[pp * blk:(pp + 1) * blk] * inv[pp]).astype(BF16)
        return carry

    lax.fori_loop(0, tq // blk, qblock, 0, unroll=2)


def swa_attention(q, k2, v2, kc2, vc2, sink, *, tq):
    bsz, n, _ = q.shape
    lc = kc2.shape[1]
    tq = min(tq, n)
    n_tiles = n // tq
    rb = tq // SWA_BLOCK
    nb = n // SWA_BLOCK
    main = pl.BlockSpec((1, tq, 1024), lambda b, i, *_: (b, i, 0))
    prev = pl.BlockSpec((1, SWA_BLOCK, 1024), lambda b, i, *_: (b, jnp.maximum(i * rb - 1, 0), 0))
    nxt = pl.BlockSpec((1, SWA_BLOCK, 1024), lambda b, i, *_: (b, jnp.minimum((i + 1) * rb, nb - 1), 0))
    stack = lambda a: a.reshape(bsz, lc, SWA_KV_HEADS, 2, 128).transpose(0, 2, 3, 1, 4).reshape(
        bsz, SWA_KV_HEADS, 2 * lc, 128)
    kc2, vc2 = stack(kc2), stack(vc2)
    sink = sink * math.log2(math.e)
    cspec = pl.BlockSpec((1, SWA_KV_HEADS, 2 * lc, 128), lambda b, i, *_: (b, 0, 0, 0))
    grid_spec = pltpu.PrefetchScalarGridSpec(
        num_scalar_prefetch=1,
        grid=(bsz, n_tiles),
        in_specs=[main, main, prev, nxt, main, prev, nxt, cspec, cspec],
        out_specs=main,
        scratch_shapes=[pltpu.VMEM((tq + 2 * SWA_BLOCK, 1024), BF16)] * 2,
    )
    return pl.pallas_call(
        functools.partial(_swa_kernel, tq=tq, n_tiles=n_tiles),
        grid_spec=grid_spec,
        out_shape=jax.ShapeDtypeStruct((bsz, n, 1024), BF16),
        compiler_params=_cparams(("arbitrary", "arbitrary")),
        name="swa_attention",
    )(sink, q, k2, k2, k2, v2, v2, v2, kc2, vc2)


def _prep_ab(w_in, w_uq, w_ukv, w_out):
    d = w_in.shape[0]
    perm = _rope_perm(MLA_ROPE)
    o = 3 * CONV_DIM + MLA_Q_LORA + MLA_KV_LORA
    dq = MLA_NOPE + MLA_ROPE
    kr = w_in[:, o:]
    rot_lanes = lambda t: jnp.pad(t, ((0, 0),) * (t.ndim - 1) + ((MLA_NOPE, LANES - dq),))
    heads = lambda t: t.reshape(t.shape[0], MLA_HEADS * LANES)
    win2 = jnp.concatenate([w_in[:, :o], rot_lanes(kr), rot_lanes(kr[:, perm])], axis=1).astype(BF16)
    wq = w_uq.reshape(MLA_Q_LORA, MLA_HEADS, dq)
    qa = heads(jnp.pad(wq, ((0, 0), (0, 0), (0, LANES - dq))))
    qb = heads(rot_lanes(wq[:, :, MLA_NOPE:][:, :, perm]))
    wq2 = jnp.concatenate([qa, qb], axis=1).astype(BF16)
    wkv = w_ukv.reshape(MLA_KV_LORA, MLA_HEADS, MLA_NOPE + MLA_V)
    half_lanes = lambda t: heads(jnp.pad(t, ((0, 0), (0, 0), (0, LANES - t.shape[-1]))))
    wkv2 = jnp.concatenate([half_lanes(wkv[:, :, :MLA_NOPE]), half_lanes(wkv[:, :, MLA_NOPE:])], axis=1).astype(BF16)
    wc = w_out[:CONV_DIM].astype(BF16)
    wa = jnp.pad(w_out[CONV_DIM:].reshape(MLA_HEADS, MLA_V, d), ((0, 0), (0, LANES - MLA_V), (0, 0)))
    return win2, wq2, wkv2, wc, wa.reshape(MLA_HEADS * LANES, d).astype(BF16)


def _prep_swa(w_qkv):
    return w_qkv.astype(BF16)


def _pad_lanes(t, left, width=LANES, fill=0.0):
    n, w = t.shape
    return jnp.concatenate([jnp.full((n, left), fill, F32), t, jnp.zeros((n, width - left - w), F32)], axis=1)


def kernel(x, c, ctx, c_ctx, mod_w, mod_b, norm1_g, norm2_g, ab_w_in, conv_w, conv_b, mla_q_norm_g, mla_w_uq,
           mla_kv_norm_g, mla_w_ukv, ab_w_out, swa_w_qkv, swa_sink, swa_w_out, router_w, exp_w_gate, exp_w_up,
           exp_w_down, final_g):
    bsz, n, d = x.shape
    lc = ctx.shape[1]
    depth = mod_w.shape[0]
    assert bsz <= 2 and d == 1024

    cs = jnp.concatenate([c, c_ctx[None, :], jnp.zeros((8 - bsz - 1, d), F32)], axis=0)
    mod = modulation(cs, mod_w, mod_b)

    cos_m, sin_m = _rope_tables(n, MLA_ROPE)
    cos_mla = _pad_lanes(cos_m, 64, fill=1.0)
    sin_mla = _pad_lanes(sin_m, 64)
    cos_mla_c = jnp.concatenate([jnp.ones((lc, 96), F32), jnp.zeros((lc, 32), F32)], axis=1)
    zeros_c = jnp.zeros((lc, 128), F32)
    cos_s, sin_s = _rope_tables(n, SWA_HEAD_DIM)
    cos_swa = jnp.concatenate([cos_s, cos_s], axis=1)
    sin_swa = jnp.concatenate([sin_s, sin_s], axis=1)
    ones_c = jnp.ones((lc, 128), F32)

    row2 = lambda v: v.reshape(1, -1)
    xs, xc = x, ctx
    for layer in range(depth):
        need_ctx = layer < depth - 1
        last = layer == depth - 1
        mod_l = mod[layer]
        g1 = row2(norm1_g[layer])
        g2 = row2(norm2_g[layer])
        rw_f = jnp.concatenate([router_w[layer], jnp.zeros((d, LANES - N_EXPERTS), F32)], axis=1)
        rw_hi = rw_f.astype(BF16)
        rw = jnp.concatenate([rw_hi, (rw_f - rw_hi.astype(F32)).astype(BF16)], axis=1)
        wg, wu, wd = exp_w_gate, exp_w_up, exp_w_down
        if layer % 2 == 0:
            e = layer // 2
            win2, wq2, wkv2, wc, wa = _prep_ab(ab_w_in[e], mla_w_uq[e], mla_w_ukv[e], ab_w_out[e])
            qg, kvg = row2(mla_q_norm_g[e]), row2(mla_kv_norm_g[e])
            cw, cb = conv_w[e], row2(conv_b[e])
            gb, z, q, k, v = ab_in(xs, mod_l, g1, win2, qg, wq2, kvg, wkv2, cos_mla, sin_mla, ctx=False, tm=512)
            gbc, zc, qc, kc, vc = ab_in(xc, mod_l, g1, win2, qg, wq2, kvg, wkv2, cos_mla_c, zeros_c, ctx=True, tm=256)
            att = mla_attention_flat(q, k, v, kc, vc, tq=256, tk_max=3328, spt=8)
            xs1, h2, aff = mix_out(att, wa, xs, mod_l, g2, rw, (gb, z, cw, cb, wc), ctx=False, tm=1024)
            if need_ctx:
                att_c = mla_attention_ctx(qc, kc, vc)
                xc1, hc2, affc = mix_out(att_c, wa, xc, mod_l, g2, rw, (gbc, zc, cw, cb, wc), ctx=True, tm=256)
        else:
            o = layer // 2
            w2 = _prep_swa(swa_w_qkv[o])
            wo = swa_w_out[o].astype(BF16)
            q, k2, v2 = swa_in(xs, mod_l, g1, w2, cos_swa, sin_swa, ctx=False, tm=1024)
            qc, kc2, vc2 = swa_in(xc, mod_l, g1, w2, ones_c, zeros_c, ctx=True, tm=256)
            att = swa_attention(q, k2, v2, kc2, vc2, swa_sink[o], tq=512)
            xs1, h2, aff = mix_out(att, wo, xs, mod_l, g2, rw, ctx=False, tm=1024)
            if need_ctx:
                raise NotImplementedError("context self-attention for windowed layers below the last")
        token_sets = [(xs1, h2, aff, False)] + ([(xc1, hc2, affc, True)] if need_ctx else [])
        outs = moe(token_sets, mod_l, wg, wu, wd, layer, final_g=row2(final_g) if last else None)
        xs = outs[0]
        if need_ctx:
            xc = outs[1]
    return xs
```

```python
import functools
import math

import jax
import jax.numpy as jnp
import numpy as np
from jax import lax
from jax.experimental import pallas as pl
from jax.experimental.pallas import tpu as pltpu

F32 = jnp.float32
BF16 = jnp.bfloat16
I32 = jnp.int32

GRID_W = 64
EPS = 1e-6
ROPE_THETA = 10000.0
CONV_DIM = 512
MLA_HEADS = 8
MLA_Q_LORA = 256
MLA_KV_LORA = 128
MLA_NOPE = 64
MLA_ROPE = 32
MLA_V = 64
SWA_HEADS = 16
SWA_KV_HEADS = 4
SWA_HEAD_DIM = 64
WINDOW = 128
SWA_BLOCK = 128
N_EXPERTS = 16
EC_CAPACITY = 2
N_MOD = 6

LANES = 128
ROW_ALIGN = 16
VMEM_LIMIT = 56 * 1024 * 1024


def _cparams(sem, vmem=VMEM_LIMIT):
    return pltpu.CompilerParams(dimension_semantics=sem, vmem_limit_bytes=vmem)


def _dot(a, b):
    return jnp.dot(a, b, preferred_element_type=F32)


def _dot_nt(a, b):
    return lax.dot_general(a, b, (((1,), (1,)), ((), ())), preferred_element_type=F32)


def _dot_tn(a, b):
    return lax.dot_general(a, b, (((0,), (0,)), ((), ())), preferred_element_type=F32)


def _dot_hi(a, b):
    return jnp.dot(a, b, preferred_element_type=F32, precision=lax.Precision.HIGHEST)


def _silu(x):
    return x * (1.0 / (1.0 + jnp.exp(-x)))


def _rms_mod(x, g, shift, scale):
    y = x * lax.rsqrt(jnp.mean(x * x, axis=-1, keepdims=True) + EPS)
    return (y * g) * (1.0 + scale) + shift


def _mod_row(mod_ref, ctx, which):
    r = 2 if ctx else pl.program_id(0)
    return mod_ref[pl.ds(r, 1), which * 1024:(which + 1) * 1024]


def _mod_kernel(cs_ref, w_ref, b_ref, o_ref):
    s = _silu(cs_ref[...])
    o_ref[0] = _dot_hi(s, w_ref[0]) + b_ref[0]


def modulation(cs, mod_w, mod_b):
    depth, d, n6 = mod_w.shape
    tn = 1536
    return pl.pallas_call(
        _mod_kernel,
        grid=(depth, n6 // tn),
        in_specs=[
            pl.BlockSpec((8, d), lambda l, j: (0, 0)),
            pl.BlockSpec((1, d, tn), lambda l, j: (l, 0, j)),
            pl.BlockSpec((1, 1, tn), lambda l, j: (l, 0, j)),
        ],
        out_specs=pl.BlockSpec((1, 8, tn), lambda l, j: (l, 0, j)),
        out_shape=jax.ShapeDtypeStruct((depth, 8, n6), F32),
        compiler_params=_cparams(("arbitrary", "arbitrary")),
        name="modulation",
    )(cs, mod_w, mod_b.reshape(depth, 1, n6))


def _rope_tables(n, dim):
    n_rows = n // GRID_W
    half = dim // 2
    qd = dim // 4
    freqs = ROPE_THETA ** (-jnp.arange(0, half, 2, dtype=F32) / half)
    ang_r = jnp.arange(n_rows, dtype=F32)[:, None] * freqs
    ang_c = jnp.arange(GRID_W, dtype=F32)[:, None] * freqs

    def table(fr, fc, sign):
        r = jnp.broadcast_to(fr[:, None, :], (n_rows, GRID_W, qd))
        c = jnp.broadcast_to(fc[None, :, :], (n_rows, GRID_W, qd))
        return jnp.concatenate([sign * r, r, sign * c, c], axis=-1).reshape(n, dim)

    return table(jnp.cos(ang_r), jnp.cos(ang_c), 1.0), table(jnp.sin(ang_r), jnp.sin(ang_c), -1.0)


def _rope_perm(dim):
    qd = dim // 4
    ch = np.arange(dim)
    pair = (ch // qd) % 2
    return np.where(pair == 0, ch + qd, ch - qd)


def _ab_in_kernel(x_ref, mod_ref, g_ref, win_ref, qg_ref, wq_ref, kvg_ref, wkv_ref, cos_ref, sin_ref,
                  gb_ref, z_ref, q_ref, k_ref, v_ref, *, ctx, scale):
    x = x_ref[0]
    h = _rms_mod(x, g_ref[...], _mod_row(mod_ref, ctx, 0), _mod_row(mod_ref, ctx, 1))
    p = _dot(h.astype(BF16), win_ref[...])
    gb_ref[0] = p[:, 0:512].astype(BF16)
    z_ref[0] = (p[:, 512:1024] * p[:, 1024:1536]).astype(BF16)
    ql = p[:, 1536:1792]
    kvl = p[:, 1792:1920]
    kra = p[:, 1920:2048]
    krb = p[:, 2048:2176]
    cos = cos_ref[...]
    sin = sin_ref[...]
    qn = ql * lax.rsqrt(jnp.mean(ql * ql, axis=-1, keepdims=True) + EPS) * qg_ref[...]
    qq = _dot(qn.astype(BF16), wq_ref[...])
    kvn = kvl * lax.rsqrt(jnp.mean(kvl * kvl, axis=-1, keepdims=True) + EPS) * kvg_ref[...]
    kv = _dot(kvn.astype(BF16), wkv_ref[...])
    krope = kra * cos + krb * sin
    cs = cos * scale
    ss = sin * scale
    ones_lane = lax.broadcasted_iota(I32, cos.shape, 1) == MLA_V
    for hd in range(MLA_HEADS):
        a = qq[:, hd * 128:(hd + 1) * 128]
        b = qq[:, 1024 + hd * 128:1024 + (hd + 1) * 128]
        q_ref[0, hd] = (a * cs + b * ss).astype(BF16)
        k_ref[0, hd] = (kv[:, hd * 128:(hd + 1) * 128] + krope).astype(BF16)
        v_ref[0, hd] = jnp.where(ones_lane, 1.0, kv[:, 1024 + hd * 128:1024 + (hd + 1) * 128]).astype(BF16)


def ab_in(x, mod_l, g, win2, qg, wq2, kvg, wkv2, cos_t, sin_t, *, ctx, tm):
    bsz, n, d = x.shape
    tm = min(tm, n)
    scale = (MLA_NOPE + MLA_ROPE) ** -0.5 * math.log2(math.e)
    full = lambda a: pl.BlockSpec(a.shape, lambda b, i: (0,) * a.ndim)
    hshape = jax.ShapeDtypeStruct((bsz, MLA_HEADS, n, 128), BF16)
    hspec = pl.BlockSpec((1, MLA_HEADS, tm, 128), lambda b, i: (b, 0, i, 0))
    return pl.pallas_call(
        functools.partial(_ab_in_kernel, ctx=ctx, scale=scale),
        grid=(bsz, n // tm),
        in_specs=[
            pl.BlockSpec((1, tm, d), lambda b, i: (b, i, 0)),
            full(mod_l), full(g), full(win2), full(qg), full(wq2), full(kvg), full(wkv2),
            pl.BlockSpec((tm, 128), lambda b, i: (i, 0)),
            pl.BlockSpec((tm, 128), lambda b, i: (i, 0)),
        ],
        out_specs=[
            pl.BlockSpec((1, tm, 512), lambda b, i: (b, i, 0)),
            pl.BlockSpec((1, tm, 512), lambda b, i: (b, i, 0)),
            hspec, hspec, hspec,
        ],
        out_shape=[
            jax.ShapeDtypeStruct((bsz, n, 512), BF16),
            jax.ShapeDtypeStruct((bsz, n, 512), BF16),
            hshape, hshape, hshape,
        ],
        compiler_params=_cparams(("arbitrary", "arbitrary")),
        name="ab_in_ctx" if ctx else "ab_in",
    )(x, mod_l, g, win2, qg, wq2, kvg, wkv2, cos_t, sin_t)


def _mla_ctx_kernel(q_ref, kc_ref, vc_ref, o_ref):
    s = _dot_nt(q_ref[0, 0], kc_ref[0, 0])
    m = jnp.max(s, axis=-1, keepdims=True)
    acc = _dot(jnp.exp2(s - m).astype(BF16), vc_ref[0, 0])
    o_ref[0] = (acc * (1.0 / acc[:, MLA_V:MLA_V + 1])).astype(BF16)


def mla_attention_ctx(q, kc, vc):
    bsz, nh, lc, dk = q.shape
    kv = pl.BlockSpec((1, 1, lc, dk), lambda b, h: (b, h, 0, 0))
    return pl.pallas_call(
        _mla_ctx_kernel,
        grid=(bsz, nh),
        in_specs=[kv, kv, kv],
        out_specs=pl.BlockSpec((1, lc, dk), lambda b, h: (b, 0, h)),
        out_shape=jax.ShapeDtypeStruct((bsz, lc, nh * dk), BF16),
        compiler_params=_cparams(("arbitrary", "arbitrary")),
        name="mla_attention_ctx",
    )(q, kc, vc)


def _mla_flat_kernel(q_ref, k_ref, v_ref, kc_ref, vc_ref, o_ref, kall, vall, s_a, s_b, p_a, p_b,
                     *, tq, tk, n_tiles, n_chunks, spt):
    n_stages = n_tiles * n_chunks
    n_lat, n_ctx = k_ref.shape[2], kc_ref.shape[2]
    cp = math.gcd(n_lat, 2048)

    def copy_rows(c, carry):
        sl = pl.ds(pl.multiple_of(c * cp, cp), cp)
        kall[sl, :] = k_ref[0, 0, sl, :]
        vall[sl, :] = v_ref[0, 0, sl, :]
        return carry

    lax.fori_loop(0, n_lat // cp, copy_rows, 0)
    kall[n_lat:n_lat + n_ctx, :] = kc_ref[0, 0]
    vall[n_lat:n_lat + n_ctx, :] = vc_ref[0, 0]

    def nxt(c):
        i, j = c
        wrap = j + 1 == n_chunks
        return jnp.where(wrap, i + 1, i), jnp.where(wrap, 0, j + 1)

    def rows(idx, size):
        return pl.ds(pl.multiple_of(idx * size, size), size)

    def scores(c, s_buf):
        s = _dot_nt(q_ref[0, 0, rows(c[0], tq), :], kall[rows(c[1], tk), :])
        s_buf[...] = s
        return jnp.max(s, axis=-1, keepdims=True)

    def weights(c, s_buf, p_buf, mx, m):
        m_prev = jnp.where(c[1] == 0, -1e30, m)
        m_new = jnp.maximum(m_prev, mx)
        p_buf[...] = jnp.exp2(s_buf[...] - m_new).astype(BF16)
        return m_new, jnp.exp2(m_prev - m_new)

    def wsum(c, p_buf, alpha, acc):
        acc = alpha * acc + _dot(p_buf[...], vall[rows(c[1], tk), :])
        o_ref[0, pl.ds(pl.multiple_of(c[0] * tq, tq), tq), :] = (
            acc * (1.0 / acc[:, MLA_V:MLA_V + 1])).astype(BF16)
        return acc

    def trip(c_pv, c_w, c_s, s_cur, s_nxt, p_prev, p_cur, mx_cur, alpha_prev, m, acc):
        mx_nxt = scores(c_s, s_nxt) if c_s is not None else None
        acc = wsum(c_pv, p_prev, alpha_prev, acc)
        m, alpha_cur = weights(c_w, s_cur, p_cur, mx_cur, m)
        return mx_nxt, alpha_cur, m, acc

    def pair(c, mx_b, alpha_a, m, acc):
        mx_a, alpha_b, m, acc = trip(c[0], c[1], c[2], s_b, s_a, p_a, p_b, mx_b, alpha_a, m, acc)
        mx_b, alpha_a, m, acc = trip(c[1], c[2], c[3], s_a, s_b, p_b, p_a, mx_a, alpha_b, m, acc)
        return mx_b, alpha_a, m, acc

    def chain(c, k):
        out = [c]
        for _ in range(k):
            out.append(nxt(out[-1]))
        return out

    zero = jnp.int32(0)
    c0 = (zero, zero)
    mx_a = scores(c0, s_a)
    m, alpha_a = weights(c0, s_a, p_a, mx_a, jnp.zeros((tq, 1), F32))
    mx_b = scores(nxt(c0), s_b)
    acc = jnp.zeros((tq, LANES), F32)

    def run(n_st, carry):
        i, j, mx_b, alpha_a, m, acc = carry
        c = chain((i, j), n_st + 1)
        for u in range(0, n_st, 2):
            mx_b, alpha_a, m, acc = pair(c[u:u + 4], mx_b, alpha_a, m, acc)
        return c[n_st][0], c[n_st][1], mx_b, alpha_a, m, acc

    carry = run((n_stages - 4) % spt, (zero, zero, mx_b, alpha_a, m, acc))
    i, j, mx_b, alpha_a, m, acc = lax.fori_loop(0, (n_stages - 4) // spt, lambda _, c: run(spt, c), carry)
    c = chain((i, j), 3)
    mx_b, alpha_a, m, acc = pair(c[0:4], mx_b, alpha_a, m, acc)
    mx_a, alpha_b, m, acc = trip(c[2], c[3], None, s_b, s_a, p_a, p_b, mx_b, alpha_a, m, acc)
    wsum(c[3], p_b, alpha_b, acc)


def mla_attention_flat(q, k, v, kc, vc, *, tq, tk_max, spt):
    bsz, nh, n, dk = q.shape
    lc = kc.shape[2]
    nk = n + lc
    tq = min(tq, n)
    tk = max(t for t in range(256, tk_max + 1, 256) if nk % t == 0)
    n_tiles, n_chunks = n // tq, nk // tk
    assert spt % 2 == 0 and (n_tiles * n_chunks) % 2 == 0 and n_tiles * n_chunks >= 4
    head = lambda rows_: pl.BlockSpec((1, 1, rows_, dk), lambda b, h: (b, h, 0, 0))
    return pl.pallas_call(
        functools.partial(_mla_flat_kernel, tq=tq, tk=tk, n_tiles=n_tiles, n_chunks=n_chunks, spt=spt),
        grid=(bsz, nh),
        in_specs=[head(n), head(n), head(n), head(lc), head(lc)],
        out_specs=pl.BlockSpec((1, n, dk), lambda b, h: (b, 0, h)),
        out_shape=jax.ShapeDtypeStruct((bsz, n, nh * dk), BF16),
        scratch_shapes=([pltpu.VMEM((nk, dk), BF16)] * 2 + [pltpu.VMEM((tq, tk), F32)] * 2
                        + [pltpu.VMEM((tq, tk), BF16)] * 2),
        compiler_params=_cparams(("arbitrary", "arbitrary")),
        name="mla_attention",
    )(q, k, v, kc, vc)


def _mix_out_kernel(*refs, ctx, has_conv, tm, n_tiles):
    if has_conv:
        (gb_ref, z_ref, zp_ref, zn_ref, cw_ref, cb_ref, wc_ref,
         att_ref, wa_ref, x_ref, mod_ref, g2_ref, rw_ref, xs_ref, h2_ref, aff_ref) = refs
    else:
        att_ref, wa_ref, x_ref, mod_ref, g2_ref, rw_ref, xs_ref, h2_ref, aff_ref = refs
    y = _dot(att_ref[0], wa_ref[...])
    if has_conv:
        i = pl.program_id(1)
        z = z_ref[0].astype(F32)
        rows = lax.broadcasted_iota(I32, z.shape, 0)
        zprev_halo = jnp.where(i > 0, zp_ref[0, ROW_ALIGN - 1:ROW_ALIGN, :].astype(F32), 0.0)
        znext_halo = jnp.where(i < n_tiles - 1, zn_ref[0, 0:1, :].astype(F32), 0.0)
        zprev = jnp.where(rows == 0, zprev_halo, pltpu.roll(z, 1, 0))
        znext = jnp.where(rows == tm - 1, znext_halo, pltpu.roll(z, tm - 1, 0))
        cw = cw_ref[...]
        conv = gb_ref[0].astype(F32) * (zprev * cw[0:1] + z * cw[1:2] + znext * cw[2:3] + cb_ref[...])
        y = y + _dot(conv.astype(BF16), wc_ref[...])
    xs = x_ref[0] + _mod_row(mod_ref, ctx, 2) * y
    xs_ref[0] = xs
    h2 = _rms_mod(xs, g2_ref[...], _mod_row(mod_ref, ctx, 3), _mod_row(mod_ref, ctx, 4))
    h2_ref[0] = h2.astype(BF16)
    h2_hi = h2.astype(BF16)
    h2_lo = (h2 - h2_hi.astype(F32)).astype(BF16)
    hh = _dot(h2_hi, rw_ref[...])
    logits = hh[:, :LANES] + hh[:, LANES:] + _dot(h2_lo, rw_ref[:, :LANES])
    lane = lax.broadcasted_iota(I32, logits.shape, 1)
    logits = jnp.where(lane < N_EXPERTS, logits, -1e30)
    mx = jnp.max(logits, axis=-1, keepdims=True)
    ex = jnp.exp(logits - mx)
    aff = ex / jnp.sum(ex, axis=-1, keepdims=True)
    aff_ref[0] = aff[:, :N_EXPERTS]


def mix_out(att, wa, x, mod_l, g2, rw, conv=None, *, ctx, tm):
    bsz, n, d = x.shape
    tm = min(tm, n)
    n_tiles = n // tm
    full = lambda a: pl.BlockSpec(a.shape, lambda b, i: (0,) * a.ndim)
    row = lambda w: pl.BlockSpec((1, tm, w), lambda b, i: (b, i, 0))
    ins, in_specs = [], []
    if conv is not None:
        gb, z, cw, cb, wc = conv
        r8 = tm // ROW_ALIGN
        nb8 = n // ROW_ALIGN
        ins += [gb, z, z, z, cw, cb, wc]
        in_specs += [
            row(512), row(512),
            pl.BlockSpec((1, ROW_ALIGN, 512), lambda b, i: (b, jnp.maximum(i * r8 - 1, 0), 0)),
            pl.BlockSpec((1, ROW_ALIGN, 512), lambda b, i: (b, jnp.minimum((i + 1) * r8, nb8 - 1), 0)),
            full(cw), full(cb), full(wc),
        ]
    ins += [att, wa, x, mod_l, g2, rw]
    in_specs += [row(att.shape[-1]), full(wa), row(d), full(mod_l), full(g2), full(rw)]
    return pl.pallas_call(
        functools.partial(_mix_out_kernel, ctx=ctx, has_conv=conv is not None, tm=tm, n_tiles=n_tiles),
        grid=(bsz, n_tiles),
        in_specs=in_specs,
        out_specs=[row(d), row(d), row(N_EXPERTS)],
        out_shape=[
            jax.ShapeDtypeStruct((bsz, n, d), F32),
            jax.ShapeDtypeStruct((bsz, n, d), BF16),
            jax.ShapeDtypeStruct((bsz, n, N_EXPERTS), F32),
        ],
        compiler_params=_cparams(("arbitrary", "arbitrary")),
        name=("mix_out_ctx" if ctx else "mix_out") + ("_conv" if conv is not None else ""),
    )(*ins)


def _route_kernel(aff_ref, aff8_ref, ls_ref, us_ref, eye_ref, rank_ref, rankt_ref, oa_ref, nwd_ref, nwc_ref,
                  thr_s, need_s, eq_s, oa_s, *, n, cap, tb, win_d, win_c, chunk):
    n_chunks = (n // 8) // chunk
    nblk = n // tb
    k = pl.program_id(1)

    def count(pred_fn):
        def body(c, acc):
            kb = pltpu.bitcast(aff8_ref[0, pl.ds(pl.multiple_of(c * chunk, chunk), chunk), :], I32)
            return acc + jnp.sum(pred_fn(kb).astype(I32), axis=0, keepdims=True)
        acc = lax.fori_loop(0, n_chunks, body, jnp.zeros((1, LANES), I32))
        for sh in (64, 32, 16):
            acc = acc + pltpu.roll(acc, sh, 1)
        return acc

    @pl.when(k == 0)
    def _():
        def bit_body(i, thr):
            cand = thr | jnp.left_shift(jnp.int32(1), 30 - i)
            cnt = count(lambda kb: kb >= cand)
            return jnp.where(cnt >= cap, cand, thr)

        thr = lax.fori_loop(0, 31, bit_body, jnp.zeros((1, LANES), I32))
        thr_s[...] = thr[:, :N_EXPERTS]
        need_s[...] = (cap - count(lambda kb: kb > thr))[:, :N_EXPERTS].astype(F32)
        eq_s[...] = jnp.zeros((1, N_EXPERTS), F32)
        oa_s[...] = jnp.zeros((1, N_EXPERTS), F32)

    thr = thr_s[...]
    off = pl.multiple_of(k * tb, tb)
    kb = pltpu.bitcast(aff_ref[0, pl.ds(off, tb), :], I32)
    gt = kb > thr
    eq = kb == thr
    eqf = jnp.where(eq, 1.0, 0.0)
    eqrank = eq_s[...] + _dot(ls_ref[...], eqf.astype(BF16))
    sel = jnp.logical_or(gt, jnp.logical_and(eq, eqrank < need_s[...]))
    self_ = jnp.where(sel, 1.0, 0.0)
    selb = self_.astype(BF16)
    lrank = _dot(ls_ref[...], selb)
    gate_bits = pltpu.bitcast(aff_ref[0, pl.ds(off, tb), :].astype(BF16).astype(F32), I32)
    rank_ref[0] = gate_bits | jnp.where(sel, lrank.astype(I32) + 1, 0)
    lrank_t = _dot_tn(selb, us_ref[...])
    sel_t = _dot_tn(selb, eye_ref[...])
    rankt_ref[0] = jnp.where(sel_t > 0.5, lrank_t, -1.0)
    c = jnp.sum(self_, axis=0, keepdims=True)
    oa_run = oa_s[...]
    oa_ref[0, pl.ds(k, 1), :] = oa_run.astype(I32)
    nwd_ref[0, pl.ds(k, 1), :] = jnp.ceil(c * (1.0 / win_d)).astype(I32)
    nwc_ref[0, pl.ds(k, 1), :] = jnp.ceil(c * (1.0 / win_c)).astype(I32)
    oa_new = oa_run + jnp.ceil(c * (1.0 / ROW_ALIGN)) * ROW_ALIGN
    oa_s[...] = oa_new
    eq_s[...] = eq_s[...] + jnp.sum(eqf, axis=0, keepdims=True)

    @pl.when(k == nblk - 1)
    def _():
        oa_ref[0, pl.ds(nblk, 1), :] = oa_new.astype(I32)


def route(aff, *, tb, win_d, win_c):
    bsz, n, ne = aff.shape
    cap = max(1, (EC_CAPACITY * n) // ne)
    nblk = n // tb
    assert ne * 8 == LANES
    aff8 = aff.reshape(bsz, n // 8, LANES)
    chunk = min(256, n // 8)
    ii = np.arange(tb)
    ls = jnp.asarray(ii[None, :] < ii[:, None], BF16)
    us = jnp.asarray(ii[:, None] < ii[None, :], BF16)
    eye = jnp.asarray(ii[:, None] == ii[None, :], BF16)
    full = lambda a: pl.BlockSpec(a.shape, lambda b, k: (0,) * a.ndim)
    return pl.pallas_call(
        functools.partial(_route_kernel, n=n, cap=cap, tb=tb, win_d=win_d, win_c=win_c, chunk=chunk),
        grid=(bsz, nblk),
        in_specs=[pl.BlockSpec((1, n, ne), lambda b, k: (b, 0, 0)),
                  pl.BlockSpec((1, n // 8, LANES), lambda b, k: (b, 0, 0)), full(ls), full(us), full(eye)],
        out_specs=[
            pl.BlockSpec((1, tb, ne), lambda b, k: (b, k, 0)),
            pl.BlockSpec((1, ne, tb), lambda b, k: (b, 0, k)),
            pl.BlockSpec((1, nblk + 1, ne), lambda b, k: (b, 0, 0)),
            pl.BlockSpec((1, nblk, ne), lambda b, k: (b, 0, 0)),
            pl.BlockSpec((1, nblk, ne), lambda b, k: (b, 0, 0)),
        ],
        out_shape=[
            jax.ShapeDtypeStruct((bsz, n, ne), I32),
            jax.ShapeDtypeStruct((bsz, ne, n), F32),
            jax.ShapeDtypeStruct((bsz, nblk + 1, ne), I32),
            jax.ShapeDtypeStruct((bsz, nblk, ne), I32),
            jax.ShapeDtypeStruct((bsz, nblk, ne), I32),
        ],
        scratch_shapes=[pltpu.VMEM((1, ne), I32), pltpu.VMEM((1, ne), F32),
                        pltpu.VMEM((1, ne), F32), pltpu.VMEM((1, ne), F32)],
        compiler_params=_cparams(("arbitrary", "arbitrary")),
        name="route",
    )(aff, aff8, ls, us, eye)


def _dispatch_kernel(oa_ref, nw_ref, rankt_ref, h_ref, xg_ref, stack, sem, *, nblk, n_steps, win, tb, group, tail):
    b = pl.program_id(0)
    k = pl.program_id(1)
    step = b * nblk + k
    slot = step % 2
    ne = N_EXPERTS

    def fill(sl, j):
        h = h_ref[0]
        for g0 in range(0, ne, group):
            pieces = []
            for e in range(g0, g0 + group):
                r = rankt_ref[0, e:e + 1, :]
                srow = lax.broadcasted_iota(I32, (win, tb), 0).astype(F32) + (j * win).astype(F32)
                pieces.append((r == srow).astype(BF16))
            oh_t = jnp.concatenate(pieces, axis=0)
            stack[sl, g0 * win:(g0 + group) * win, :] = _dot(oh_t, h).astype(BF16)

    def copies(sl, bb, kk, j):
        out = []
        for e in range(ne):
            off = pl.multiple_of(oa_ref[(bb * (nblk + 1) + kk) * ne + e] + j * win, ROW_ALIGN)
            out.append(pltpu.make_async_copy(
                stack.at[sl, pl.ds(e * win, win), :],
                xg_ref.at[bb, e, pl.ds(off, win), :],
                sem.at[sl]))
        return out

    @pl.when(k == 0)
    def _():
        stack[2, 0:win, :] = jnp.zeros((win, stack.shape[2]), BF16)
        cs = [pltpu.make_async_copy(stack.at[2, pl.ds(0, sz), :], xg_ref.at[b, e, pl.ds(off, sz), :], sem.at[2])
              for e in range(ne) for off, sz in tail]
        for c in cs:
            c.start()
        for c in cs:
            c.wait()

    fill(slot, jnp.int32(0))

    @pl.when(step > 0)
    def _():
        for c in copies(1 - slot, b, k, 0):
            c.wait()

    for c in copies(slot, b, k, 0):
        c.start()

    nws = [nw_ref[(b * nblk + k) * ne + e] for e in range(ne)]
    nwx = functools.reduce(jnp.maximum, nws)

    def extra(j, carry):
        fill(2, j)
        cs = copies(2, b, k, j)
        for e, c in enumerate(cs):
            pl.when(j < nws[e])(c.start)
        for e, c in enumerate(cs):
            pl.when(j < nws[e])(c.wait)
        return carry

    lax.fori_loop(1, jnp.maximum(nwx, 1), extra, 0)

    @pl.when(step == n_steps - 1)
    def _():
        for c in copies(slot, b, k, 0):
            c.wait()


def dispatch(oa, nw, rank_t, h2, *, tb, win, rows, tail):
    bsz, n, d = h2.shape
    ne = N_EXPERTS
    nblk = n // tb
    group = 4
    grid_spec = pltpu.PrefetchScalarGridSpec(
        num_scalar_prefetch=2,
        grid=(bsz, nblk),
        in_specs=[
            pl.BlockSpec((1, ne, tb), lambda b, k, *_: (b, 0, k)),
            pl.BlockSpec((1, tb, d), lambda b, k, *_: (b, k, 0)),
        ],
        out_specs=pl.BlockSpec(memory_space=pl.ANY),
        scratch_shapes=[
            pltpu.VMEM((3, ne * win, d), BF16),
            pltpu.SemaphoreType.DMA((3,)),
        ],
    )
    return pl.pallas_call(
        functools.partial(_dispatch_kernel, nblk=nblk, n_steps=bsz * nblk, win=win, tb=tb, group=group, tail=tail),
        grid_spec=grid_spec,
        out_shape=jax.ShapeDtypeStruct((bsz, ne, rows, d), BF16),
        compiler_params=_cparams(("arbitrary", "arbitrary")),
        name="dispatch",
    )(oa.reshape(-1), nw.reshape(-1), rank_t, h2)


def _ffn_kernel(tot_ref, x_ref, *rest, tm, nblk, n_tiles, has_small):
    if has_small:
        xs_ref, wg_ref, wu_ref, wd_ref, y_ref, ys_ref, wgb, wub, wdb = rest
    else:
        wg_ref, wu_ref, wd_ref, y_ref, wgb, wub, wdb = rest
    e = pl.program_id(0)
    b = pl.program_id(1)
    i = pl.program_id(2)

    @pl.when(jnp.logical_and(b == 0, i == 0))
    def _():
        wgb[...] = wg_ref[0, 0].astype(BF16)
        wub[...] = wu_ref[0, 0].astype(BF16)
        wdb[...] = wd_ref[0, 0].astype(BF16)

    def ffn(x):
        hid = (_silu(_dot(x, wgb[...])) * _dot(x, wub[...])).astype(BF16)
        return _dot(hid, wdb[...]).astype(BF16)

    total = tot_ref[(b * (nblk + 1) + nblk) * N_EXPERTS + e]

    @pl.when(jnp.logical_and(i < n_tiles, i * tm < total))
    def _():
        y_ref[0, 0] = ffn(x_ref[0, 0])

    @pl.when(jnp.logical_and(i < n_tiles, i * tm >= total))
    def _():
        y_ref[0, 0] = jnp.zeros(y_ref.shape[2:], BF16)

    if has_small:
        @pl.when(i == n_tiles)
        def _():
            ys_ref[0, 0] = ffn(xs_ref[0, 0])


def expert_ffn(oa, xg, w_gate, w_up, w_down, layer, xg_small=None, *, tm, nblk):
    bsz, ne, c_pad, d = xg.shape
    f = w_gate.shape[-1]
    n_tiles = c_pad // tm
    has_small = xg_small is not None
    tile = pl.BlockSpec((1, 1, tm, d), lambda e, b, i, *_: (b, e, jnp.minimum(i, n_tiles - 1), 0))
    wspec = lambda r, c: pl.BlockSpec((1, 1, r, c), lambda e, b, i, *_: (layer, e, 0, 0))
    ins, in_specs, out_specs, out_shape = [xg], [tile], [tile], [jax.ShapeDtypeStruct(xg.shape, BF16)]
    if has_small:
        small = pl.BlockSpec((1, 1) + xg_small.shape[2:], lambda e, b, i, *_: (b, e, 0, 0))
        ins.append(xg_small)
        in_specs.append(small)
        out_specs.append(small)
        out_shape.append(jax.ShapeDtypeStruct(xg_small.shape, BF16))
    grid_spec = pltpu.PrefetchScalarGridSpec(
        num_scalar_prefetch=1,
        grid=(ne, bsz, n_tiles + int(has_small)),
        in_specs=in_specs + [wspec(d, f), wspec(d, f), wspec(f, d)],
        out_specs=out_specs,
        scratch_shapes=[pltpu.VMEM((d, f), BF16), pltpu.VMEM((d, f), BF16), pltpu.VMEM((f, d), BF16)],
    )
    return pl.pallas_call(
        functools.partial(_ffn_kernel, tm=tm, nblk=nblk, n_tiles=n_tiles, has_small=has_small),
        grid_spec=grid_spec,
        out_shape=out_shape,
        compiler_params=_cparams(("arbitrary", "arbitrary", "arbitrary")),
        name="expert_ffn",
    )(oa.reshape(-1), *ins, w_gate, w_up, w_down)


def _combine_kernel(oa_ref, nw_ref, rank_ref, xs_ref, mod_ref, *rest,
                    ctx, nblk, n_steps, win, tb, final):
    if final:
        fg_ref, ye_ref, o_ref, stack, sem = rest
    else:
        ye_ref, o_ref, stack, sem = rest
    b = pl.program_id(0)
    k = pl.program_id(1)
    step = b * nblk + k
    slot = step % 2
    ne = N_EXPERTS

    def copies(sl, bb, kk, j):
        out = []
        for e in range(ne):
            off = pl.multiple_of(oa_ref[(bb * (nblk + 1) + kk) * ne + e] + j * win, ROW_ALIGN)
            out.append(pltpu.make_async_copy(
                ye_ref.at[bb, e, pl.ds(off, win), :],
                stack.at[sl, pl.ds(e * win, win), :],
                sem.at[sl]))
        return out

    @pl.when(step == 0)
    def _():
        stack[2] = jnp.zeros(stack.shape[1:], BF16)
        for c in copies(0, b, k, 0):
            c.start()

    @pl.when(step + 1 < n_steps)
    def _():
        nxt = step + 1
        for c in copies(1 - slot, nxt // nblk, nxt % nblk, 0):
            c.start()

    for c in copies(slot, b, k, 0):
        c.wait()

    def weighted_sum(j, sl):
        word = rank_ref[0]
        slot1 = lax.broadcasted_iota(I32, (tb, win), 1) + (j * win + 1)
        y = None
        for e0 in range(0, ne, 2):
            pieces = []
            for e in (e0, e0 + 1):
                wb = jnp.broadcast_to(word[:, e:e + 1], (tb, win))
                gate = pltpu.bitcast(wb & jnp.int32(-65536), F32)
                pieces.append(jnp.where((wb & 0xFFFF) == slot1, gate, 0.0).astype(BF16))
            part = _dot(jnp.concatenate(pieces, axis=1), stack[sl, e0 * win:(e0 + 2) * win, :])
            y = part if y is None else y + part
        return y

    y = weighted_sum(jnp.int32(0), slot)

    nws = [nw_ref[(b * nblk + k) * ne + e] for e in range(ne)]
    nwx = functools.reduce(jnp.maximum, nws)

    def extra(j, y):
        cs = copies(2, b, k, j)
        for e, c in enumerate(cs):
            pl.when(j < nws[e])(c.start)
        for e, c in enumerate(cs):
            pl.when(j < nws[e])(c.wait)
        return y + weighted_sum(j, 2)

    y = lax.fori_loop(1, jnp.maximum(nwx, 1), extra, y)
    out = xs_ref[0] + _mod_row(mod_ref, ctx, 5) * y
    if final:
        out = out * lax.rsqrt(jnp.mean(out * out, axis=-1, keepdims=True) + EPS) * fg_ref[...]
    o_ref[0] = out


def combine(oa, nw, rank, xs, mod_l, ye, final_g=None, *, ctx, tb, win):
    bsz, n, d = xs.shape
    ne = N_EXPERTS
    nblk = n // tb
    final = final_g is not None
    full = lambda a: pl.BlockSpec(a.shape, lambda b, k, *_: (0,) * a.ndim)
    ins = [rank, xs, mod_l]
    in_specs = [
        pl.BlockSpec((1, tb, ne), lambda b, k, *_: (b, k, 0)),
        pl.BlockSpec((1, tb, d), lambda b, k, *_: (b, k, 0)),
        full(mod_l),
    ]
    if final:
        ins.append(final_g)
        in_specs.append(full(final_g))
    ins.append(ye)
    in_specs.append(pl.BlockSpec(memory_space=pl.ANY))
    grid_spec = pltpu.PrefetchScalarGridSpec(
        num_scalar_prefetch=2,
        grid=(bsz, nblk),
        in_specs=in_specs,
        out_specs=pl.BlockSpec((1, tb, d), lambda b, k, *_: (b, k, 0)),
        scratch_shapes=[
            pltpu.VMEM((3, ne * win, d), BF16),
            pltpu.SemaphoreType.DMA((3,)),
        ],
    )
    return pl.pallas_call(
        functools.partial(_combine_kernel, ctx=ctx, nblk=nblk, n_steps=bsz * nblk, win=win, tb=tb, final=final),
        grid_spec=grid_spec,
        out_shape=jax.ShapeDtypeStruct(xs.shape, F32),
        compiler_params=_cparams(("arbitrary", "arbitrary")),
        name="combine_final" if final else ("combine_ctx" if ctx else "combine"),
    )(oa.reshape(-1), nw.reshape(-1), *ins)


def moe(token_sets, mod_l, w_gate, w_up, w_down, layer, final_g=None):
    tm = 768 if token_sets[0][0].shape[1] >= 4096 else 128
    routed = []
    for s, (xs1, h2, aff, is_ctx) in enumerate(token_sets):
        n = xs1.shape[1]
        cap = max(1, (EC_CAPACITY * n) // N_EXPERTS)
        tb = min(512, n)
        win_c = LANES
        win_d = 96 if tb == 512 else LANES
        nblk = n // tb
        align = tm if s == 0 else ROW_ALIGN
        c_pad = -(-(cap + ROW_ALIGN * nblk + win_c) // align) * align
        rank, rank_t, oa, nw_d, nw_c = route(aff, tb=tb, win_d=win_d, win_c=win_c)
        tail = tuple((off, min(win_d, c_pad - off)) for off in range(cap, c_pad, win_d))
        xg = dispatch(oa, nw_d, rank_t, h2, tb=tb, win=win_d, rows=c_pad, tail=tail)
        routed.append(dict(rank=rank, oa=oa, nw_c=nw_c, xg=xg, tb=tb, win_c=win_c, nblk=nblk))
    assert len(routed) <= 2
    yes = expert_ffn(routed[0]["oa"], routed[0]["xg"], w_gate, w_up, w_down, layer,
                     routed[1]["xg"] if len(routed) == 2 else None, tm=tm, nblk=routed[0]["nblk"])
    outs = []
    for s, ((xs1, h2, aff, is_ctx), r, ye) in enumerate(zip(token_sets, routed, yes)):
        outs.append(combine(r["oa"], r["nw_c"], r["rank"], xs1, mod_l, ye, final_g if s == 0 else None, ctx=is_ctx,
                            tb=r["tb"], win=r["win_c"]))
    return outs


def _swa_in_kernel(x_ref, mod_ref, g_ref, w_ref, cos_ref, sin_ref, q_ref, k_ref, v_ref, *, ctx, scale):
    h = _rms_mod(x_ref[0], g_ref[...], _mod_row(mod_ref, ctx, 0), _mod_row(mod_ref, ctx, 1))
    p = _dot(h.astype(BF16), w_ref[...])
    cos = cos_ref[...]
    sin = sin_ref[...]
    cs = cos * scale
    ss = sin * scale
    lane = lax.broadcasted_iota(I32, cos.shape, 1)
    qd = SWA_HEAD_DIM // 4
    first = (lane % (2 * qd)) < qd

    def partner(x):
        return jnp.where(first, pltpu.roll(x, LANES - qd, 1), pltpu.roll(x, qd, 1))

    for j in range(8):
        a = p[:, j * 128:(j + 1) * 128]
        q_ref[0, :, j * 128:(j + 1) * 128] = (a * cs + partner(a) * ss).astype(BF16)
    low = lane < 64
    for pr in range(2):
        kx = p[:, 1024 + pr * 128:1024 + (pr + 1) * 128]
        kp = kx * cos + partner(kx) * sin
        vp = p[:, 1280 + pr * 128:1280 + (pr + 1) * 128]
        for src, dst in ((kp, k_ref), (vp, v_ref)):
            ev_lo = jnp.where(low, src, 0.0)
            od_hi = jnp.where(low, 0.0, src)
            ev_hi = pltpu.roll(ev_lo, 64, 1)
            od_lo = pltpu.roll(od_hi, 64, 1)
            base = pr * 512
            dst[0, :, base:base + 128] = ev_lo.astype(BF16)
            dst[0, :, base + 128:base + 256] = ev_hi.astype(BF16)
            dst[0, :, base + 256:base + 384] = od_lo.astype(BF16)
            dst[0, :, base + 384:base + 512] = od_hi.astype(BF16)


def swa_in(x, mod_l, g, w2, cos_t, sin_t, *, ctx, tm):
    bsz, n, d = x.shape
    tm = min(tm, n)
    full = lambda a: pl.BlockSpec(a.shape, lambda b, i: (0,) * a.ndim)
    row = lambda w: pl.BlockSpec((1, tm, w), lambda b, i: (b, i, 0))
    return pl.pallas_call(
        functools.partial(_swa_in_kernel, ctx=ctx, scale=SWA_HEAD_DIM ** -0.5 * math.log2(math.e)),
        grid=(bsz, n // tm),
        in_specs=[row(d), full(mod_l), full(g), full(w2),
                  pl.BlockSpec((tm, 128), lambda b, i: (i, 0)),
                  pl.BlockSpec((tm, 128), lambda b, i: (i, 0))],
        out_specs=[row(1024), row(1024), row(1024)],
        out_shape=[jax.ShapeDtypeStruct((bsz, n, 1024), BF16)] * 3,
        compiler_params=_cparams(("arbitrary", "arbitrary")),
        name="swa_in_ctx" if ctx else "swa_in",
    )(x, mod_l, g, w2, cos_t, sin_t)


def _swa_kernel(sink_ref, q_ref, km_ref, kp_ref, kn_ref, vm_ref, vp_ref, vn_ref, kc_ref, vc_ref,
                o_ref, kwin, vwin, *, tq, n_tiles):
    i = pl.program_id(1)
    blk = SWA_BLOCK
    kwin[0:blk] = kp_ref[0]
    kwin[blk:blk + tq] = km_ref[0]
    kwin[blk + tq:blk + tq + blk] = kn_ref[0]
    vwin[0:blk] = vp_ref[0]
    vwin[blk:blk + tq] = vm_ref[0]
    vwin[blk + tq:blk + tq + blk] = vn_ref[0]
    r = lax.broadcasted_iota(I32, (blk, 3 * blk), 0)
    s = lax.broadcasted_iota(I32, (blk, 3 * blk), 1)
    band = jnp.abs(r - (s - blk)) <= WINDOW
    low_lanes = lax.broadcasted_iota(I32, (blk, 128), 1) < 64

    def qblock(qb, carry):
        row0 = pl.multiple_of(qb * blk, blk)
        first = jnp.logical_and(i == 0, qb == 0)
        last = jnp.logical_and(i == n_tiles - 1, qb == tq // blk - 1)
        valid = jnp.logical_and(band, jnp.logical_and(jnp.logical_or(s >= blk, jnp.logical_not(first)),
                                                      jnp.logical_or(s < 2 * blk, jnp.logical_not(last))))
        bias = jnp.where(valid, 0.0, -1e30)
        for kvh in range(SWA_KV_HEADS):
            c0 = kvh * 256
            q4 = jnp.concatenate([q_ref[0, pl.ds(row0, blk), c0:c0 + 128],
                                  q_ref[0, pl.ds(row0, blk), c0 + 128:c0 + 256]], axis=0)
            kl = jnp.concatenate([kwin[pl.ds(row0, 3 * blk), c0:c0 + 128],
                                  kwin[pl.ds(row0, 3 * blk), c0 + 128:c0 + 256]], axis=0)
            vl = jnp.concatenate([vwin[pl.ds(row0, 3 * blk), c0:c0 + 128],
                                  vwin[pl.ds(row0, 3 * blk), c0 + 128:c0 + 256]], axis=0)
            s_loc = _dot_nt(q4, kl)
            s_ctx = _dot_nt(q4, kc_ref[0, kvh])
            lc = s_ctx.shape[1] // 2
            p_loc, p_ctx, inv = [], [], []
            for pp in range(2):
                pl_row, pc_row, inv_row = [], [], []
                for hf in range(2):
                    sk = sink_ref[kvh * 4 + pp * 2 + hf]
                    sl = s_loc[pp * blk:(pp + 1) * blk, hf * 3 * blk:(hf + 1) * 3 * blk] + bias
                    sc = s_ctx[pp * blk:(pp + 1) * blk, hf * lc:(hf + 1) * lc]
                    m = jnp.maximum(jnp.maximum(jnp.max(sl, axis=-1, keepdims=True),
                                                jnp.max(sc, axis=-1, keepdims=True)), sk)
                    el = jnp.exp2(sl - m)
                    ec = jnp.exp2(sc - m)
                    den = (jnp.sum(el, axis=-1, keepdims=True) + jnp.sum(ec, axis=-1, keepdims=True)
                           + jnp.exp2(sk - m))
                    pl_row.append(el.astype(BF16))
                    pc_row.append(ec.astype(BF16))
                    inv_row.append(1.0 / den)
                p_loc.append(jnp.concatenate(pl_row, axis=1))
                p_ctx.append(jnp.concatenate(pc_row, axis=1))
                inv.append(jnp.where(low_lanes, inv_row[0], inv_row[1]))
            o4 = _dot(jnp.concatenate(p_loc, axis=0), vl) + _dot(jnp.concatenate(p_ctx, axis=0), vc_ref[0, kvh])
            for pp in range(2):
                pair = kvh * 2 + pp
                o_ref[0, pl.ds(row0, blk), pair * 128:(pair + 1) * 128] = (
                    o4[pp * blk:(pp + 1) * blk] * inv[pp]).astype(BF16)
        return carry

    lax.fori_loop(0, tq // blk, qblock, 0, unroll=2)


def swa_attention(q, k2, v2, kc2, vc2, sink, *, tq):
    bsz, n, _ = q.shape
    lc = kc2.shape[1]
    tq = min(tq, n)
    n_tiles = n // tq
    rb = tq // SWA_BLOCK
    nb = n // SWA_BLOCK
    main = pl.BlockSpec((1, tq, 1024), lambda b, i, *_: (b, i, 0))
    prev = pl.BlockSpec((1, SWA_BLOCK, 1024), lambda b, i, *_: (b, jnp.maximum(i * rb - 1, 0), 0))
    nxt = pl.BlockSpec((1, SWA_BLOCK, 1024), lambda b, i, *_: (b, jnp.minimum((i + 1) * rb, nb - 1), 0))
    stack = lambda a: a.reshape(bsz, lc, SWA_KV_HEADS, 2, 128).transpose(0, 2, 3, 1, 4).reshape(
        bsz, SWA_KV_HEADS, 2 * lc, 128)
    kc2, vc2 = stack(kc2), stack(vc2)
    sink = sink * math.log2(math.e)
    cspec = pl.BlockSpec((1, SWA_KV_HEADS, 2 * lc, 128), lambda b, i, *_: (b, 0, 0, 0))
    grid_spec = pltpu.PrefetchScalarGridSpec(
        num_scalar_prefetch=1,
        grid=(bsz, n_tiles),
        in_specs=[main, main, prev, nxt, main, prev, nxt, cspec, cspec],
        out_specs=main,
        scratch_shapes=[pltpu.VMEM((tq + 2 * SWA_BLOCK, 1024), BF16)] * 2,
    )
    return pl.pallas_call(
        functools.partial(_swa_kernel, tq=tq, n_tiles=n_tiles),
        grid_spec=grid_spec,
        out_shape=jax.ShapeDtypeStruct((bsz, n, 1024), BF16),
        compiler_params=_cparams(("arbitrary", "arbitrary")),
        name="swa_attention",
    )(sink, q, k2, k2, k2, v2, v2, v2, kc2, vc2)


def _prep_ab(w_in, w_uq, w_ukv, w_out):
    d = w_in.shape[0]
    perm = _rope_perm(MLA_ROPE)
    o = 3 * CONV_DIM + MLA_Q_LORA + MLA_KV_LORA
    dq = MLA_NOPE + MLA_ROPE
    kr = w_in[:, o:]
    rot_lanes = lambda t: jnp.pad(t, ((0, 0),) * (t.ndim - 1) + ((MLA_NOPE, LANES - dq),))
    heads = lambda t: t.reshape(t.shape[0], MLA_HEADS * LANES)
    win2 = jnp.concatenate([w_in[:, :o], rot_lanes(kr), rot_lanes(kr[:, perm])], axis=1).astype(BF16)
    wq = w_uq.reshape(MLA_Q_LORA, MLA_HEADS, dq)
    qa = heads(jnp.pad(wq, ((0, 0), (0, 0), (0, LANES - dq))))
    qb = heads(rot_lanes(wq[:, :, MLA_NOPE:][:, :, perm]))
    wq2 = jnp.concatenate([qa, qb], axis=1).astype(BF16)
    wkv = w_ukv.reshape(MLA_KV_LORA, MLA_HEADS, MLA_NOPE + MLA_V)
    half_lanes = lambda t: heads(jnp.pad(t, ((0, 0), (0, 0), (0, LANES - t.shape[-1]))))
    wkv2 = jnp.concatenate([half_lanes(wkv[:, :, :MLA_NOPE]), half_lanes(wkv[:, :, MLA_NOPE:])], axis=1).astype(BF16)
    wc = w_out[:CONV_DIM].astype(BF16)
    wa = jnp.pad(w_out[CONV_DIM:].reshape(MLA_HEADS, MLA_V, d), ((0, 0), (0, LANES - MLA_V), (0, 0)))
    return win2, wq2, wkv2, wc, wa.reshape(MLA_HEADS * LANES, d).astype(BF16)


def _prep_swa(w_qkv):
    return w_qkv.astype(BF16)


def _pad_lanes(t, left, width=LANES, fill=0.0):
    n, w = t.shape
    return jnp.concatenate([jnp.full((n, left), fill, F32), t, jnp.zeros((n, width - left - w), F32)], axis=1)


def kernel(x, c, ctx, c_ctx, mod_w, mod_b, norm1_g, norm2_g, ab_w_in, conv_w, conv_b, mla_q_norm_g, mla_w_uq,
           mla_kv_norm_g, mla_w_ukv, ab_w_out, swa_w_qkv, swa_sink, swa_w_out, router_w, exp_w_gate, exp_w_up,
           exp_w_down, final_g):
    bsz, n, d = x.shape
    lc = ctx.shape[1]
    depth = mod_w.shape[0]
    assert bsz <= 2 and d == 1024

    cs = jnp.concatenate([c, c_ctx[None, :], jnp.zeros((8 - bsz - 1, d), F32)], axis=0)
    mod = modulation(cs, mod_w, mod_b)

    cos_m, sin_m = _rope_tables(n, MLA_ROPE)
    cos_mla = _pad_lanes(cos_m, 64, fill=1.0)
    sin_mla = _pad_lanes(sin_m, 64)
    cos_mla_c = jnp.concatenate([jnp.ones((lc, 96), F32), jnp.zeros((lc, 32), F32)], axis=1)
    zeros_c = jnp.zeros((lc, 128), F32)
    cos_s, sin_s = _rope_tables(n, SWA_HEAD_DIM)
    cos_swa = jnp.concatenate([cos_s, cos_s], axis=1)
    sin_swa = jnp.concatenate([sin_s, sin_s], axis=1)
    ones_c = jnp.ones((lc, 128), F32)

    row2 = lambda v: v.reshape(1, -1)
    xs, xc = x, ctx
    for layer in range(depth):
        need_ctx = layer < depth - 1
        last = layer == depth - 1
        mod_l = mod[layer]
        g1 = row2(norm1_g[layer])
        g2 = row2(norm2_g[layer])
        rw_f = jnp.concatenate([router_w[layer], jnp.zeros((d, LANES - N_EXPERTS), F32)], axis=1)
        rw_hi = rw_f.astype(BF16)
        rw = jnp.concatenate([rw_hi, (rw_f - rw_hi.astype(F32)).astype(BF16)], axis=1)
        wg, wu, wd = exp_w_gate, exp_w_up, exp_w_down
        if layer % 2 == 0:
            e = layer // 2
            win2, wq2, wkv2, wc, wa = _prep_ab(ab_w_in[e], mla_w_uq[e], mla_w_ukv[e], ab_w_out[e])
            qg, kvg = row2(mla_q_norm_g[e]), row2(mla_kv_norm_g[e])
            cw, cb = conv_w[e], row2(conv_b[e])
            gb, z, q, k, v = ab_in(xs, mod_l, g1, win2, qg, wq2, kvg, wkv2, cos_mla, sin_mla, ctx=False, tm=512)
            gbc, zc, qc, kc, vc = ab_in(xc, mod_l, g1, win2, qg, wq2, kvg, wkv2, cos_mla_c, zeros_c, ctx=True, tm=256)
            att = mla_attention_flat(q, k, v, kc, vc, tq=256, tk_max=3328, spt=12)
            xs1, h2, aff = mix_out(att, wa, xs, mod_l, g2, rw, (gb, z, cw, cb, wc), ctx=False, tm=1024)
            if need_ctx:
                att_c = mla_attention_ctx(qc, kc, vc)
                xc1, hc2, affc = mix_out(att_c, wa, xc, mod_l, g2, rw, (gbc, zc, cw, cb, wc), ctx=True, tm=256)
        else:
            o = layer // 2
            w2 = _prep_swa(swa_w_qkv[o])
            wo = swa_w_out[o].astype(BF16)
            q, k2, v2 = swa_in(xs, mod_l, g1, w2, cos_swa, sin_swa, ctx=False, tm=1024)
            qc, kc2, vc2 = swa_in(xc, mod_l, g1, w2, ones_c, zeros_c, ctx=True, tm=256)
            att = swa_attention(q, k2, v2, kc2, vc2, swa_sink[o], tq=512)
            xs1, h2, aff = mix_out(att, wo, xs, mod_l, g2, rw, ctx=False, tm=1024)
            if need_ctx:
                raise NotImplementedError("context self-attention for windowed layers below the last")
        token_sets = [(xs1, h2, aff, False)] + ([(xc1, hc2, affc, True)] if need_ctx else [])
        outs = moe(token_sets, mod_l, wg, wu, wd, layer, final_g=row2(final_g) if last else None)
        xs = outs[0]
        if need_ctx:
            xc = outs[1]
    return xs
```

```python
import functools
import math

import jax
import jax.numpy as jnp
import numpy as np
from jax import lax
from jax.experimental import pallas as pl
from jax.experimental.pallas import tpu as pltpu

F32 = jnp.float32
BF16 = jnp.bfloat16
I32 = jnp.int32

GRID_W = 64
EPS = 1e-6
ROPE_THETA = 10000.0
CONV_DIM = 512
MLA_HEADS = 8
MLA_Q_LORA = 256
MLA_KV_LORA = 128
MLA_NOPE = 64
MLA_ROPE = 32
MLA_V = 64
SWA_HEADS = 16
SWA_KV_HEADS = 4
SWA_HEAD_DIM = 64
WINDOW = 128
SWA_BLOCK = 128
N_EXPERTS = 16
EC_CAPACITY = 2
N_MOD = 6

LANES = 128
ROW_ALIGN = 16
VMEM_LIMIT = 56 * 1024 * 1024


def _cparams(sem, vmem=VMEM_LIMIT):
    return pltpu.CompilerParams(dimension_semantics=sem, vmem_limit_bytes=vmem)


def _dot(a, b):
    return jnp.dot(a, b, preferred_element_type=F32)


def _dot_nt(a, b):
    return lax.dot_general(a, b, (((1,), (1,)), ((), ())), preferred_element_type=F32)


def _dot_tn(a, b):
    return lax.dot_general(a, b, (((0,), (0,)), ((), ())), preferred_element_type=F32)


def _dot_hi(a, b):
    return jnp.dot(a, b, preferred_element_type=F32, precision=lax.Precision.HIGHEST)


def _silu(x):
    return x * (1.0 / (1.0 + jnp.exp(-x)))


def _rms_mod(x, g, shift, scale):
    y = x * lax.rsqrt(jnp.mean(x * x, axis=-1, keepdims=True) + EPS)
    return (y * g) * (1.0 + scale) + shift


def _mod_row(mod_ref, ctx, which):
    r = 2 if ctx else pl.program_id(0)
    return mod_ref[pl.ds(r, 1), which * 1024:(which + 1) * 1024]


def _mod_kernel(cs_ref, w_ref, b_ref, o_ref):
    s = _silu(cs_ref[...])
    o_ref[0] = _dot_hi(s, w_ref[0]) + b_ref[0]


def modulation(cs, mod_w, mod_b):
    depth, d, n6 = mod_w.shape
    tn = 1536
    return pl.pallas_call(
        _mod_kernel,
        grid=(depth, n6 // tn),
        in_specs=[
            pl.BlockSpec((8, d), lambda l, j: (0, 0)),
            pl.BlockSpec((1, d, tn), lambda l, j: (l, 0, j)),
            pl.BlockSpec((1, 1, tn), lambda l, j: (l, 0, j)),
        ],
        out_specs=pl.BlockSpec((1, 8, tn), lambda l, j: (l, 0, j)),
        out_shape=jax.ShapeDtypeStruct((depth, 8, n6), F32),
        compiler_params=_cparams(("arbitrary", "arbitrary")),
        name="modulation",
    )(cs, mod_w, mod_b.reshape(depth, 1, n6))


def _rope_tables(n, dim):
    n_rows = n // GRID_W
    half = dim // 2
    qd = dim // 4
    freqs = ROPE_THETA ** (-jnp.arange(0, half, 2, dtype=F32) / half)
    ang_r = jnp.arange(n_rows, dtype=F32)[:, None] * freqs
    ang_c = jnp.arange(GRID_W, dtype=F32)[:, None] * freqs

    def table(fr, fc, sign):
        r = jnp.broadcast_to(fr[:, None, :], (n_rows, GRID_W, qd))
        c = jnp.broadcast_to(fc[None, :, :], (n_rows, GRID_W, qd))
        return jnp.concatenate([sign * r, r, sign * c, c], axis=-1).reshape(n, dim)

    return table(jnp.cos(ang_r), jnp.cos(ang_c), 1.0), table(jnp.sin(ang_r), jnp.sin(ang_c), -1.0)


def _rope_perm(dim):
    qd = dim // 4
    ch = np.arange(dim)
    pair = (ch // qd) % 2
    return np.where(pair == 0, ch + qd, ch - qd)


def _ab_in_kernel(x_ref, mod_ref, g_ref, win_ref, qg_ref, wq_ref, kvg_ref, wkv_ref, cos_ref, sin_ref,
                  gb_ref, z_ref, q_ref, k_ref, v_ref, *, ctx, scale):
    x = x_ref[0]
    h = _rms_mod(x, g_ref[...], _mod_row(mod_ref, ctx, 0), _mod_row(mod_ref, ctx, 1))
    p = _dot(h.astype(BF16), win_ref[...])
    gb_ref[0] = p[:, 0:512].astype(BF16)
    z_ref[0] = (p[:, 512:1024] * p[:, 1024:1536]).astype(BF16)
    ql = p[:, 1536:1792]
    kvl = p[:, 1792:1920]
    kra = p[:, 1920:2048]
    krb = p[:, 2048:2176]
    cos = cos_ref[...]
    sin = sin_ref[...]
    qn = ql * lax.rsqrt(jnp.mean(ql * ql, axis=-1, keepdims=True) + EPS) * qg_ref[...]
    qq = _dot(qn.astype(BF16), wq_ref[...])
    kvn = kvl * lax.rsqrt(jnp.mean(kvl * kvl, axis=-1, keepdims=True) + EPS) * kvg_ref[...]
    kv = _dot(kvn.astype(BF16), wkv_ref[...])
    krope = kra * cos + krb * sin
    cs = cos * scale
    ss = sin * scale
    ones_lane = lax.broadcasted_iota(I32, cos.shape, 1) == MLA_V
    for hd in range(MLA_HEADS):
        a = qq[:, hd * 128:(hd + 1) * 128]
        b = qq[:, 1024 + hd * 128:1024 + (hd + 1) * 128]
        q_ref[0, hd] = (a * cs + b * ss).astype(BF16)
        k_ref[0, hd] = (kv[:, hd * 128:(hd + 1) * 128] + krope).astype(BF16)
        v_ref[0, hd] = jnp.where(ones_lane, 1.0, kv[:, 1024 + hd * 128:1024 + (hd + 1) * 128]).astype(BF16)


def ab_in(x, mod_l, g, win2, qg, wq2, kvg, wkv2, cos_t, sin_t, *, ctx, tm):
    bsz, n, d = x.shape
    tm = min(tm, n)
    scale = (MLA_NOPE + MLA_ROPE) ** -0.5 * math.log2(math.e)
    full = lambda a: pl.BlockSpec(a.shape, lambda b, i: (0,) * a.ndim)
    hshape = jax.ShapeDtypeStruct((bsz, MLA_HEADS, n, 128), BF16)
    hspec = pl.BlockSpec((1, MLA_HEADS, tm, 128), lambda b, i: (b, 0, i, 0))
    return pl.pallas_call(
        functools.partial(_ab_in_kernel, ctx=ctx, scale=scale),
        grid=(bsz, n // tm),
        in_specs=[
            pl.BlockSpec((1, tm, d), lambda b, i: (b, i, 0)),
            full(mod_l), full(g), full(win2), full(qg), full(wq2), full(kvg), full(wkv2),
            pl.BlockSpec((tm, 128), lambda b, i: (i, 0)),
            pl.BlockSpec((tm, 128), lambda b, i: (i, 0)),
        ],
        out_specs=[
            pl.BlockSpec((1, tm, 512), lambda b, i: (b, i, 0)),
            pl.BlockSpec((1, tm, 512), lambda b, i: (b, i, 0)),
            hspec, hspec, hspec,
        ],
        out_shape=[
            jax.ShapeDtypeStruct((bsz, n, 512), BF16),
            jax.ShapeDtypeStruct((bsz, n, 512), BF16),
            hshape, hshape, hshape,
        ],
        compiler_params=_cparams(("arbitrary", "arbitrary")),
        name="ab_in_ctx" if ctx else "ab_in",
    )(x, mod_l, g, win2, qg, wq2, kvg, wkv2, cos_t, sin_t)


def _mla_ctx_kernel(q_ref, kc_ref, vc_ref, o_ref):
    s = _dot_nt(q_ref[0, 0], kc_ref[0, 0])
    m = jnp.max(s, axis=-1, keepdims=True)
    acc = _dot(jnp.exp2(s - m).astype(BF16), vc_ref[0, 0])
    o_ref[0] = (acc * (1.0 / acc[:, MLA_V:MLA_V + 1])).astype(BF16)


def mla_attention_ctx(q, kc, vc):
    bsz, nh, lc, dk = q.shape
    kv = pl.BlockSpec((1, 1, lc, dk), lambda b, h: (b, h, 0, 0))
    return pl.pallas_call(
        _mla_ctx_kernel,
        grid=(bsz, nh),
        in_specs=[kv, kv, kv],
        out_specs=pl.BlockSpec((1, lc, dk), lambda b, h: (b, 0, h)),
        out_shape=jax.ShapeDtypeStruct((bsz, lc, nh * dk), BF16),
        compiler_params=_cparams(("arbitrary", "arbitrary")),
        name="mla_attention_ctx",
    )(q, kc, vc)


def _mla_flat_kernel(q_ref, k_ref, v_ref, kc_ref, vc_ref, o_ref, kall, vall, s_a, s_b, p_a, p_b,
                     *, tq, tk, n_tiles, n_chunks, spt):
    n_stages = n_tiles * n_chunks
    n_lat, n_ctx = k_ref.shape[2], kc_ref.shape[2]
    cp = math.gcd(n_lat, 2048)

    def copy_rows(c, carry):
        sl = pl.ds(pl.multiple_of(c * cp, cp), cp)
        kall[sl, :] = k_ref[0, 0, sl, :]
        vall[sl, :] = v_ref[0, 0, sl, :]
        return carry

    lax.fori_loop(0, n_lat // cp, copy_rows, 0)
    kall[n_lat:n_lat + n_ctx, :] = kc_ref[0, 0]
    vall[n_lat:n_lat + n_ctx, :] = vc_ref[0, 0]

    def nxt(c):
        i, j = c
        wrap = j + 1 == n_chunks
        return jnp.where(wrap, i + 1, i), jnp.where(wrap, 0, j + 1)

    def rows(idx, size):
        return pl.ds(pl.multiple_of(idx * size, size), size)

    def scores(c, s_buf):
        s = _dot_nt(q_ref[0, 0, rows(c[0], tq), :], kall[rows(c[1], tk), :])
        s_buf[...] = s
        return jnp.max(s, axis=-1, keepdims=True)

    def weights(c, s_buf, p_buf, mx, m):
        m_prev = jnp.where(c[1] == 0, -1e30, m)
        m_new = jnp.maximum(m_prev, mx)
        p_buf[...] = jnp.exp2(s_buf[...] - m_new).astype(BF16)
        return m_new, jnp.exp2(m_prev - m_new)

    def wsum(c, p_buf, alpha, acc):
        acc = alpha * acc + _dot(p_buf[...], vall[rows(c[1], tk), :])
        o_ref[0, pl.ds(pl.multiple_of(c[0] * tq, tq), tq), :] = (
            acc * (1.0 / acc[:, MLA_V:MLA_V + 1])).astype(BF16)
        return acc

    def trip(c_pv, c_w, c_s, s_cur, s_nxt, p_prev, p_cur, mx_cur, alpha_prev, m, acc):
        mx_nxt = scores(c_s, s_nxt) if c_s is not None else None
        acc = wsum(c_pv, p_prev, alpha_prev, acc)
        m, alpha_cur = weights(c_w, s_cur, p_cur, mx_cur, m)
        return mx_nxt, alpha_cur, m, acc

    def pair(c, mx_b, alpha_a, m, acc):
        mx_a, alpha_b, m, acc = trip(c[0], c[1], c[2], s_b, s_a, p_a, p_b, mx_b, alpha_a, m, acc)
        mx_b, alpha_a, m, acc = trip(c[1], c[2], c[3], s_a, s_b, p_b, p_a, mx_a, alpha_b, m, acc)
        return mx_b, alpha_a, m, acc

    def chain(c, k):
        out = [c]
        for _ in range(k):
            out.append(nxt(out[-1]))
        return out

    zero = jnp.int32(0)
    c0 = (zero, zero)
    mx_a = scores(c0, s_a)
    m, alpha_a = weights(c0, s_a, p_a, mx_a, jnp.zeros((tq, 1), F32))
    mx_b = scores(nxt(c0), s_b)
    acc = jnp.zeros((tq, LANES), F32)

    def run(n_st, carry):
        i, j, mx_b, alpha_a, m, acc = carry
        c = chain((i, j), n_st + 1)
        for u in range(0, n_st, 2):
            mx_b, alpha_a, m, acc = pair(c[u:u + 4], mx_b, alpha_a, m, acc)
        return c[n_st][0], c[n_st][1], mx_b, alpha_a, m, acc

    carry = run((n_stages - 4) % spt, (zero, zero, mx_b, alpha_a, m, acc))
    i, j, mx_b, alpha_a, m, acc = lax.fori_loop(0, (n_stages - 4) // spt, lambda _, c: run(spt, c), carry)
    c = chain((i, j), 3)
    mx_b, alpha_a, m, acc = pair(c[0:4], mx_b, alpha_a, m, acc)
    mx_a, alpha_b, m, acc = trip(c[2], c[3], None, s_b, s_a, p_a, p_b, mx_b, alpha_a, m, acc)
    wsum(c[3], p_b, alpha_b, acc)


def mla_attention_flat(q, k, v, kc, vc, *, tq, tk_max, spt):
    bsz, nh, n, dk = q.shape
    lc = kc.shape[2]
    nk = n + lc
    tq = min(tq, n)
    tk = max(t for t in range(256, tk_max + 1, 256) if nk % t == 0)
    n_tiles, n_chunks = n // tq, nk // tk
    assert spt % 2 == 0 and (n_tiles * n_chunks) % 2 == 0 and n_tiles * n_chunks >= 4
    head = lambda rows_: pl.BlockSpec((1, 1, rows_, dk), lambda b, h: (b, h, 0, 0))
    return pl.pallas_call(
        functools.partial(_mla_flat_kernel, tq=tq, tk=tk, n_tiles=n_tiles, n_chunks=n_chunks, spt=spt),
        grid=(bsz, nh),
        in_specs=[head(n), head(n), head(n), head(lc), head(lc)],
        out_specs=pl.BlockSpec((1, n, dk), lambda b, h: (b, 0, h)),
        out_shape=jax.ShapeDtypeStruct((bsz, n, nh * dk), BF16),
        scratch_shapes=([pltpu.VMEM((nk, dk), BF16)] * 2 + [pltpu.VMEM((tq, tk), F32)] * 2
                        + [pltpu.VMEM((tq, tk), BF16)] * 2),
        compiler_params=_cparams(("arbitrary", "arbitrary")),
        name="mla_attention",
    )(q, k, v, kc, vc)


def _mix_out_kernel(*refs, ctx, has_conv, tm, n_tiles):
    if has_conv:
        (gb_ref, z_ref, zp_ref, zn_ref, cw_ref, cb_ref, wc_ref,
         att_ref, wa_ref, x_ref, mod_ref, g2_ref, rw_ref, xs_ref, h2_ref, aff_ref) = refs
    else:
        att_ref, wa_ref, x_ref, mod_ref, g2_ref, rw_ref, xs_ref, h2_ref, aff_ref = refs
    y = _dot(att_ref[0], wa_ref[...])
    if has_conv:
        i = pl.program_id(1)
        z = z_ref[0].astype(F32)
        rows = lax.broadcasted_iota(I32, z.shape, 0)
        zprev_halo = jnp.where(i > 0, zp_ref[0, ROW_ALIGN - 1:ROW_ALIGN, :].astype(F32), 0.0)
        znext_halo = jnp.where(i < n_tiles - 1, zn_ref[0, 0:1, :].astype(F32), 0.0)
        zprev = jnp.where(rows == 0, zprev_halo, pltpu.roll(z, 1, 0))
        znext = jnp.where(rows == tm - 1, znext_halo, pltpu.roll(z, tm - 1, 0))
        cw = cw_ref[...]
        conv = gb_ref[0].astype(F32) * (zprev * cw[0:1] + z * cw[1:2] + znext * cw[2:3] + cb_ref[...])
        y = y + _dot(conv.astype(BF16), wc_ref[...])
    xs = x_ref[0] + _mod_row(mod_ref, ctx, 2) * y
    xs_ref[0] = xs
    h2 = _rms_mod(xs, g2_ref[...], _mod_row(mod_ref, ctx, 3), _mod_row(mod_ref, ctx, 4))
    h2_ref[0] = h2.astype(BF16)
    h2_hi = h2.astype(BF16)
    h2_lo = (h2 - h2_hi.astype(F32)).astype(BF16)
    hh = _dot(h2_hi, rw_ref[...])
    logits = hh[:, :LANES] + hh[:, LANES:] + _dot(h2_lo, rw_ref[:, :LANES])
    lane = lax.broadcasted_iota(I32, logits.shape, 1)
    logits = jnp.where(lane < N_EXPERTS, logits, -1e30)
    mx = jnp.max(logits, axis=-1, keepdims=True)
    ex = jnp.exp(logits - mx)
    aff = ex / jnp.sum(ex, axis=-1, keepdims=True)
    aff_ref[0] = aff[:, :N_EXPERTS]


def mix_out(att, wa, x, mod_l, g2, rw, conv=None, *, ctx, tm):
    bsz, n, d = x.shape
    tm = min(tm, n)
    n_tiles = n // tm
    full = lambda a: pl.BlockSpec(a.shape, lambda b, i: (0,) * a.ndim)
    row = lambda w: pl.BlockSpec((1, tm, w), lambda b, i: (b, i, 0))
    ins, in_specs = [], []
    if conv is not None:
        gb, z, cw, cb, wc = conv
        r8 = tm // ROW_ALIGN
        nb8 = n // ROW_ALIGN
        ins += [gb, z, z, z, cw, cb, wc]
        in_specs += [
            row(512), row(512),
            pl.BlockSpec((1, ROW_ALIGN, 512), lambda b, i: (b, jnp.maximum(i * r8 - 1, 0), 0)),
            pl.BlockSpec((1, ROW_ALIGN, 512), lambda b, i: (b, jnp.minimum((i + 1) * r8, nb8 - 1), 0)),
            full(cw), full(cb), full(wc),
        ]
    ins += [att, wa, x, mod_l, g2, rw]
    in_specs += [row(att.shape[-1]), full(wa), row(d), full(mod_l), full(g2), full(rw)]
    return pl.pallas_call(
        functools.partial(_mix_out_kernel, ctx=ctx, has_conv=conv is not None, tm=tm, n_tiles=n_tiles),
        grid=(bsz, n_tiles),
        in_specs=in_specs,
        out_specs=[row(d), row(d), row(N_EXPERTS)],
        out_shape=[
            jax.ShapeDtypeStruct((bsz, n, d), F32),
            jax.ShapeDtypeStruct((bsz, n, d), BF16),
            jax.ShapeDtypeStruct((bsz, n, N_EXPERTS), F32),
        ],
        compiler_params=_cparams(("arbitrary", "arbitrary")),
        name=("mix_out_ctx" if ctx else "mix_out") + ("_conv" if conv is not None else ""),
    )(*ins)


def _route_kernel(aff_ref, aff8_ref, ls_ref, us_ref, eye_ref, rank_ref, rankt_ref, oa_ref, nwd_ref, nwc_ref,
                  thr_s, need_s, eq_s, oa_s, *, n, cap, tb, win_d, win_c, chunk):
    n_chunks = (n // 8) // chunk
    nblk = n // tb
    k = pl.program_id(1)

    def count(pred_fn):
        def body(c, acc):
            kb = pltpu.bitcast(aff8_ref[0, pl.ds(pl.multiple_of(c * chunk, chunk), chunk), :], I32)
            return acc + jnp.sum(pred_fn(kb).astype(I32), axis=0, keepdims=True)
        acc = lax.fori_loop(0, n_chunks, body, jnp.zeros((1, LANES), I32))
        for sh in (64, 32, 16):
            acc = acc + pltpu.roll(acc, sh, 1)
        return acc

    @pl.when(k == 0)
    def _():
        def bit_body(i, thr):
            cand = thr | jnp.left_shift(jnp.int32(1), 30 - i)
            cnt = count(lambda kb: kb >= cand)
            return jnp.where(cnt >= cap, cand, thr)

        thr = lax.fori_loop(0, 31, bit_body, jnp.zeros((1, LANES), I32))
        thr_s[...] = thr[:, :N_EXPERTS]
        need_s[...] = (cap - count(lambda kb: kb > thr))[:, :N_EXPERTS].astype(F32)
        eq_s[...] = jnp.zeros((1, N_EXPERTS), F32)
        oa_s[...] = jnp.zeros((1, N_EXPERTS), F32)

    thr = thr_s[...]
    off = pl.multiple_of(k * tb, tb)
    kb = pltpu.bitcast(aff_ref[0, pl.ds(off, tb), :], I32)
    gt = kb > thr
    eq = kb == thr
    eqf = jnp.where(eq, 1.0, 0.0)
    eqrank = eq_s[...] + _dot(ls_ref[...], eqf.astype(BF16))
    sel = jnp.logical_or(gt, jnp.logical_and(eq, eqrank < need_s[...]))
    self_ = jnp.where(sel, 1.0, 0.0)
    selb = self_.astype(BF16)
    lrank = _dot(ls_ref[...], selb)
    gate_bits = pltpu.bitcast(aff_ref[0, pl.ds(off, tb), :].astype(BF16).astype(F32), I32)
    rank_ref[0] = gate_bits | jnp.where(sel, lrank.astype(I32) + 1, 0)
    lrank_t = _dot_tn(selb, us_ref[...])
    sel_t = _dot_tn(selb, eye_ref[...])
    rankt_ref[0] = jnp.where(sel_t > 0.5, lrank_t, -1.0)
    c = jnp.sum(self_, axis=0, keepdims=True)
    oa_run = oa_s[...]
    oa_ref[0, pl.ds(k, 1), :] = oa_run.astype(I32)
    nwd_ref[0, pl.ds(k, 1), :] = jnp.ceil(c * (1.0 / win_d)).astype(I32)
    nwc_ref[0, pl.ds(k, 1), :] = jnp.ceil(c * (1.0 / win_c)).astype(I32)
    oa_new = oa_run + jnp.ceil(c * (1.0 / ROW_ALIGN)) * ROW_ALIGN
    oa_s[...] = oa_new
    eq_s[...] = eq_s[...] + jnp.sum(eqf, axis=0, keepdims=True)

    @pl.when(k == nblk - 1)
    def _():
        oa_ref[0, pl.ds(nblk, 1), :] = oa_new.astype(I32)


def route(aff, *, tb, win_d, win_c):
    bsz, n, ne = aff.shape
    cap = max(1, (EC_CAPACITY * n) // ne)
    nblk = n // tb
    assert ne * 8 == LANES
    aff8 = aff.reshape(bsz, n // 8, LANES)
    chunk = min(256, n // 8)
    ii = np.arange(tb)
    ls = jnp.asarray(ii[None, :] < ii[:, None], BF16)
    us = jnp.asarray(ii[:, None] < ii[None, :], BF16)
    eye = jnp.asarray(ii[:, None] == ii[None, :], BF16)
    full = lambda a: pl.BlockSpec(a.shape, lambda b, k: (0,) * a.ndim)
    return pl.pallas_call(
        functools.partial(_route_kernel, n=n, cap=cap, tb=tb, win_d=win_d, win_c=win_c, chunk=chunk),
        grid=(bsz, nblk),
        in_specs=[pl.BlockSpec((1, n, ne), lambda b, k: (b, 0, 0)),
                  pl.BlockSpec((1, n // 8, LANES), lambda b, k: (b, 0, 0)), full(ls), full(us), full(eye)],
        out_specs=[
            pl.BlockSpec((1, tb, ne), lambda b, k: (b, k, 0)),
            pl.BlockSpec((1, ne, tb), lambda b, k: (b, 0, k)),
            pl.BlockSpec((1, nblk + 1, ne), lambda b, k: (b, 0, 0)),
            pl.BlockSpec((1, nblk, ne), lambda b, k: (b, 0, 0)),
            pl.BlockSpec((1, nblk, ne), lambda b, k: (b, 0, 0)),
        ],
        out_shape=[
            jax.ShapeDtypeStruct((bsz, n, ne), I32),
            jax.ShapeDtypeStruct((bsz, ne, n), F32),
            jax.ShapeDtypeStruct((bsz, nblk + 1, ne), I32),
            jax.ShapeDtypeStruct((bsz, nblk, ne), I32),
            jax.ShapeDtypeStruct((bsz, nblk, ne), I32),
        ],
        scratch_shapes=[pltpu.VMEM((1, ne), I32), pltpu.VMEM((1, ne), F32),
                        pltpu.VMEM((1, ne), F32), pltpu.VMEM((1, ne), F32)],
        compiler_params=_cparams(("arbitrary", "arbitrary")),
        name="route",
    )(aff, aff8, ls, us, eye)


def _dispatch_kernel(oa_ref, nw_ref, rankt_ref, h_ref, xg_ref, stack, sem, *, nblk, n_steps, win, tb, group, tail):
    b = pl.program_id(0)
    k = pl.program_id(1)
    step = b * nblk + k
    slot = step % 2
    ne = N_EXPERTS

    def fill(sl, j):
        h = h_ref[0]
        for g0 in range(0, ne, group):
            pieces = []
            for e in range(g0, g0 + group):
                r = rankt_ref[0, e:e + 1, :]
                srow = lax.broadcasted_iota(I32, (win, tb), 0).astype(F32) + (j * win).astype(F32)
                pieces.append((r == srow).astype(BF16))
            oh_t = jnp.concatenate(pieces, axis=0)
            stack[sl, g0 * win:(g0 + group) * win, :] = _dot(oh_t, h).astype(BF16)

    def copies(sl, bb, kk, j):
        out = []
        for e in range(ne):
            off = pl.multiple_of(oa_ref[(bb * (nblk + 1) + kk) * ne + e] + j * win, ROW_ALIGN)
            out.append(pltpu.make_async_copy(
                stack.at[sl, pl.ds(e * win, win), :],
                xg_ref.at[bb, e, pl.ds(off, win), :],
                sem.at[sl]))
        return out

    @pl.when(k == 0)
    def _():
        stack[2, 0:win, :] = jnp.zeros((win, stack.shape[2]), BF16)
        cs = [pltpu.make_async_copy(stack.at[2, pl.ds(0, sz), :], xg_ref.at[b, e, pl.ds(off, sz), :], sem.at[2])
              for e in range(ne) for off, sz in tail]
        for c in cs:
            c.start()
        for c in cs:
            c.wait()

    fill(slot, jnp.int32(0))

    @pl.when(step > 0)
    def _():
        for c in copies(1 - slot, b, k, 0):
            c.wait()

    for c in copies(slot, b, k, 0):
        c.start()

    nws = [nw_ref[(b * nblk + k) * ne + e] for e in range(ne)]
    nwx = functools.reduce(jnp.maximum, nws)

    def extra(j, carry):
        fill(2, j)
        cs = copies(2, b, k, j)
        for e, c in enumerate(cs):
            pl.when(j < nws[e])(c.start)
        for e, c in enumerate(cs):
            pl.when(j < nws[e])(c.wait)
        return carry

    lax.fori_loop(1, jnp.maximum(nwx, 1), extra, 0)

    @pl.when(step == n_steps - 1)
    def _():
        for c in copies(slot, b, k, 0):
            c.wait()


def dispatch(oa, nw, rank_t, h2, *, tb, win, rows, tail):
    bsz, n, d = h2.shape
    ne = N_EXPERTS
    nblk = n // tb
    group = 8
    grid_spec = pltpu.PrefetchScalarGridSpec(
        num_scalar_prefetch=2,
        grid=(bsz, nblk),
        in_specs=[
            pl.BlockSpec((1, ne, tb), lambda b, k, *_: (b, 0, k)),
            pl.BlockSpec((1, tb, d), lambda b, k, *_: (b, k, 0)),
        ],
        out_specs=pl.BlockSpec(memory_space=pl.ANY),
        scratch_shapes=[
            pltpu.VMEM((3, ne * win, d), BF16),
            pltpu.SemaphoreType.DMA((3,)),
        ],
    )
    return pl.pallas_call(
        functools.partial(_dispatch_kernel, nblk=nblk, n_steps=bsz * nblk, win=win, tb=tb, group=group, tail=tail),
        grid_spec=grid_spec,
        out_shape=jax.ShapeDtypeStruct((bsz, ne, rows, d), BF16),
        compiler_params=_cparams(("arbitrary", "arbitrary")),
        name="dispatch",
    )(oa.reshape(-1), nw.reshape(-1), rank_t, h2)


def _ffn_kernel(tot_ref, x_ref, *rest, tm, nblk, n_tiles, has_small):
    if has_small:
        xs_ref, wg_ref, wu_ref, wd_ref, y_ref, ys_ref, wgb, wub, wdb = rest
    else:
        wg_ref, wu_ref, wd_ref, y_ref, wgb, wub, wdb = rest
    e = pl.program_id(0)
    b = pl.program_id(1)
    i = pl.program_id(2)

    @pl.when(jnp.logical_and(b == 0, i == 0))
    def _():
        wgb[...] = wg_ref[0, 0].astype(BF16)
        wub[...] = wu_ref[0, 0].astype(BF16)
        wdb[...] = wd_ref[0, 0].astype(BF16)

    def ffn(x):
        hid = (_silu(_dot(x, wgb[...])) * _dot(x, wub[...])).astype(BF16)
        return _dot(hid, wdb[...]).astype(BF16)

    total = tot_ref[(b * (nblk + 1) + nblk) * N_EXPERTS + e]

    @pl.when(jnp.logical_and(i < n_tiles, i * tm < total))
    def _():
        y_ref[0, 0] = ffn(x_ref[0, 0])

    @pl.when(jnp.logical_and(i < n_tiles, i * tm >= total))
    def _():
        y_ref[0, 0] = jnp.zeros(y_ref.shape[2:], BF16)

    if has_small:
        @pl.when(i == n_tiles)
        def _():
            ys_ref[0, 0] = ffn(xs_ref[0, 0])


def expert_ffn(oa, xg, w_gate, w_up, w_down, layer, xg_small=None, *, tm, nblk):
    bsz, ne, c_pad, d = xg.shape
    f = w_gate.shape[-1]
    n_tiles = c_pad // tm
    has_small = xg_small is not None
    tile = pl.BlockSpec((1, 1, tm, d), lambda e, b, i, *_: (b, e, jnp.minimum(i, n_tiles - 1), 0))
    wspec = lambda r, c: pl.BlockSpec((1, 1, r, c), lambda e, b, i, *_: (layer, e, 0, 0))
    ins, in_specs, out_specs, out_shape = [xg], [tile], [tile], [jax.ShapeDtypeStruct(xg.shape, BF16)]
    if has_small:
        small = pl.BlockSpec((1, 1) + xg_small.shape[2:], lambda e, b, i, *_: (b, e, 0, 0))
        ins.append(xg_small)
        in_specs.append(small)
        out_specs.append(small)
        out_shape.append(jax.ShapeDtypeStruct(xg_small.shape, BF16))
    grid_spec = pltpu.PrefetchScalarGridSpec(
        num_scalar_prefetch=1,
        grid=(ne, bsz, n_tiles + int(has_small)),
        in_specs=in_specs + [wspec(d, f), wspec(d, f), wspec(f, d)],
        out_specs=out_specs,
        scratch_shapes=[pltpu.VMEM((d, f), BF16), pltpu.VMEM((d, f), BF16), pltpu.VMEM((f, d), BF16)],
    )
    return pl.pallas_call(
        functools.partial(_ffn_kernel, tm=tm, nblk=nblk, n_tiles=n_tiles, has_small=has_small),
        grid_spec=grid_spec,
        out_shape=out_shape,
        compiler_params=_cparams(("arbitrary", "arbitrary", "arbitrary")),
        name="expert_ffn",
    )(oa.reshape(-1), *ins, w_gate, w_up, w_down)


def _combine_kernel(oa_ref, nw_ref, rank_ref, xs_ref, mod_ref, *rest,
                    ctx, nblk, n_steps, win, tb, final):
    if final:
        fg_ref, ye_ref, o_ref, stack, sem = rest
    else:
        ye_ref, o_ref, stack, sem = rest
    b = pl.program_id(0)
    k = pl.program_id(1)
    step = b * nblk + k
    slot = step % 2
    ne = N_EXPERTS

    def copies(sl, bb, kk, j):
        out = []
        for e in range(ne):
            off = pl.multiple_of(oa_ref[(bb * (nblk + 1) + kk) * ne + e] + j * win, ROW_ALIGN)
            out.append(pltpu.make_async_copy(
                ye_ref.at[bb, e, pl.ds(off, win), :],
                stack.at[sl, pl.ds(e * win, win), :],
                sem.at[sl]))
        return out

    @pl.when(step == 0)
    def _():
        stack[2] = jnp.zeros(stack.shape[1:], BF16)
        for c in copies(0, b, k, 0):
            c.start()

    @pl.when(step + 1 < n_steps)
    def _():
        nxt = step + 1
        for c in copies(1 - slot, nxt // nblk, nxt % nblk, 0):
            c.start()

    for c in copies(slot, b, k, 0):
        c.wait()

    def weighted_sum(j, sl):
        word = rank_ref[0]
        slot1 = lax.broadcasted_iota(I32, (tb, win), 1) + (j * win + 1)
        y = None
        for e0 in range(0, ne, 2):
            pieces = []
            for e in (e0, e0 + 1):
                wb = jnp.broadcast_to(word[:, e:e + 1], (tb, win))
                gate = pltpu.bitcast(wb & jnp.int32(-65536), F32)
                pieces.append(jnp.where((wb & 0xFFFF) == slot1, gate, 0.0).astype(BF16))
            part = _dot(jnp.concatenate(pieces, axis=1), stack[sl, e0 * win:(e0 + 2) * win, :])
            y = part if y is None else y + part
        return y

    y = weighted_sum(jnp.int32(0), slot)

    nws = [nw_ref[(b * nblk + k) * ne + e] for e in range(ne)]
    nwx = functools.reduce(jnp.maximum, nws)

    def extra(j, y):
        cs = copies(2, b, k, j)
        for e, c in enumerate(cs):
            pl.when(j < nws[e])(c.start)
        for e, c in enumerate(cs):
            pl.when(j < nws[e])(c.wait)
        return y + weighted_sum(j, 2)

    y = lax.fori_loop(1, jnp.maximum(nwx, 1), extra, y)
    out = xs_ref[0] + _mod_row(mod_ref, ctx, 5) * y
    if final:
        out = out * lax.rsqrt(jnp.mean(out * out, axis=-1, keepdims=True) + EPS) * fg_ref[...]
    o_ref[0] = out


def combine(oa, nw, rank, xs, mod_l, ye, final_g=None, *, ctx, tb, win):
    bsz, n, d = xs.shape
    ne = N_EXPERTS
    nblk = n // tb
    final = final_g is not None
    full = lambda a: pl.BlockSpec(a.shape, lambda b, k, *_: (0,) * a.ndim)
    ins = [rank, xs, mod_l]
    in_specs = [
        pl.BlockSpec((1, tb, ne), lambda b, k, *_: (b, k, 0)),
        pl.BlockSpec((1, tb, d), lambda b, k, *_: (b, k, 0)),
        full(mod_l),
    ]
    if final:
        ins.append(final_g)
        in_specs.append(full(final_g))
    ins.append(ye)
    in_specs.append(pl.BlockSpec(memory_space=pl.ANY))
    grid_spec = pltpu.PrefetchScalarGridSpec(
        num_scalar_prefetch=2,
        grid=(bsz, nblk),
        in_specs=in_specs,
        out_specs=pl.BlockSpec((1, tb, d), lambda b, k, *_: (b, k, 0)),
        scratch_shapes=[
            pltpu.VMEM((3, ne * win, d), BF16),
            pltpu.SemaphoreType.DMA((3,)),
        ],
    )
    return pl.pallas_call(
        functools.partial(_combine_kernel, ctx=ctx, nblk=nblk, n_steps=bsz * nblk, win=win, tb=tb, final=final),
        grid_spec=grid_spec,
        out_shape=jax.ShapeDtypeStruct(xs.shape, F32),
        compiler_params=_cparams(("arbitrary", "arbitrary")),
        name="combine_final" if final else ("combine_ctx" if ctx else "combine"),
    )(oa.reshape(-1), nw.reshape(-1), *ins)


def moe(token_sets, mod_l, w_gate, w_up, w_down, layer, final_g=None):
    tm = 768 if token_sets[0][0].shape[1] >= 4096 else 128
    routed = []
    for s, (xs1, h2, aff, is_ctx) in enumerate(token_sets):
        n = xs1.shape[1]
        cap = max(1, (EC_CAPACITY * n) // N_EXPERTS)
        tb = min(512, n)
        win_c = LANES
        win_d = 96 if tb == 512 else LANES
        nblk = n // tb
        align = tm if s == 0 else ROW_ALIGN
        c_pad = -(-(cap + ROW_ALIGN * nblk + win_c) // align) * align
        rank, rank_t, oa, nw_d, nw_c = route(aff, tb=tb, win_d=win_d, win_c=win_c)
        tail = tuple((off, min(win_d, c_pad - off)) for off in range(cap, c_pad, win_d))
        xg = dispatch(oa, nw_d, rank_t, h2, tb=tb, win=win_d, rows=c_pad, tail=tail)
        routed.append(dict(rank=rank, oa=oa, nw_c=nw_c, xg=xg, tb=tb, win_c=win_c, nblk=nblk))
    assert len(routed) <= 2
    yes = expert_ffn(routed[0]["oa"], routed[0]["xg"], w_gate, w_up, w_down, layer,
                     routed[1]["xg"] if len(routed) == 2 else None, tm=tm, nblk=routed[0]["nblk"])
    outs = []
    for s, ((xs1, h2, aff, is_ctx), r, ye) in enumerate(zip(token_sets, routed, yes)):
        outs.append(combine(r["oa"], r["nw_c"], r["rank"], xs1, mod_l, ye, final_g if s == 0 else None, ctx=is_ctx,
                            tb=r["tb"], win=r["win_c"]))
    return outs


def _swa_in_kernel(x_ref, mod_ref, g_ref, w_ref, cos_ref, sin_ref, q_ref, k_ref, v_ref, *, ctx, scale):
    h = _rms_mod(x_ref[0], g_ref[...], _mod_row(mod_ref, ctx, 0), _mod_row(mod_ref, ctx, 1))
    p = _dot(h.astype(BF16), w_ref[...])
    cos = cos_ref[...]
    sin = sin_ref[...]
    cs = cos * scale
    ss = sin * scale
    lane = lax.broadcasted_iota(I32, cos.shape, 1)
    qd = SWA_HEAD_DIM // 4
    first = (lane % (2 * qd)) < qd

    def partner(x):
        return jnp.where(first, pltpu.roll(x, LANES - qd, 1), pltpu.roll(x, qd, 1))

    for j in range(8):
        a = p[:, j * 128:(j + 1) * 128]
        q_ref[0, :, j * 128:(j + 1) * 128] = (a * cs + partner(a) * ss).astype(BF16)
    low = lane < 64
    for pr in range(2):
        kx = p[:, 1024 + pr * 128:1024 + (pr + 1) * 128]
        kp = kx * cos + partner(kx) * sin
        vp = p[:, 1280 + pr * 128:1280 + (pr + 1) * 128]
        for src, dst in ((kp, k_ref), (vp, v_ref)):
            ev_lo = jnp.where(low, src, 0.0)
            od_hi = jnp.where(low, 0.0, src)
            ev_hi = pltpu.roll(ev_lo, 64, 1)
            od_lo = pltpu.roll(od_hi, 64, 1)
            base = pr * 512
            dst[0, :, base:base + 128] = ev_lo.astype(BF16)
            dst[0, :, base + 128:base + 256] = ev_hi.astype(BF16)
            dst[0, :, base + 256:base + 384] = od_lo.astype(BF16)
            dst[0, :, base + 384:base + 512] = od_hi.astype(BF16)


def swa_in(x, mod_l, g, w2, cos_t, sin_t, *, ctx, tm):
    bsz, n, d = x.shape
    tm = min(tm, n)
    full = lambda a: pl.BlockSpec(a.shape, lambda b, i: (0,) * a.ndim)
    row = lambda w: pl.BlockSpec((1, tm, w), lambda b, i: (b, i, 0))
    return pl.pallas_call(
        functools.partial(_swa_in_kernel, ctx=ctx, scale=SWA_HEAD_DIM ** -0.5 * math.log2(math.e)),
        grid=(bsz, n // tm),
        in_specs=[row(d), full(mod_l), full(g), full(w2),
                  pl.BlockSpec((tm, 128), lambda b, i: (i, 0)),
                  pl.BlockSpec((tm, 128), lambda b, i: (i, 0))],
        out_specs=[row(1024), row(1024), row(1024)],
        out_shape=[jax.ShapeDtypeStruct((bsz, n, 1024), BF16)] * 3,
        compiler_params=_cparams(("arbitrary", "arbitrary")),
        name="swa_in_ctx" if ctx else "swa_in",
    )(x, mod_l, g, w2, cos_t, sin_t)


def _swa_kernel(sink_ref, q_ref, km_ref, kp_ref, kn_ref, vm_ref, vp_ref, vn_ref, kc_ref, vc_ref,
                o_ref, kwin, vwin, *, tq, n_tiles):
    i = pl.program_id(1)
    blk = SWA_BLOCK
    kwin[0:blk] = kp_ref[0]
    kwin[blk:blk + tq] = km_ref[0]
    kwin[blk + tq:blk + tq + blk] = kn_ref[0]
    vwin[0:blk] = vp_ref[0]
    vwin[blk:blk + tq] = vm_ref[0]
    vwin[blk + tq:blk + tq + blk] = vn_ref[0]
    r = lax.broadcasted_iota(I32, (blk, 3 * blk), 0)
    s = lax.broadcasted_iota(I32, (blk, 3 * blk), 1)
    band = jnp.abs(r - (s - blk)) <= WINDOW
    low_lanes = lax.broadcasted_iota(I32, (blk, 128), 1) < 64

    def qblock(qb, carry):
        row0 = pl.multiple_of(qb * blk, blk)
        first = jnp.logical_and(i == 0, qb == 0)
        last = jnp.logical_and(i == n_tiles - 1, qb == tq // blk - 1)
        valid = jnp.logical_and(band, jnp.logical_and(jnp.logical_or(s >= blk, jnp.logical_not(first)),
                                                      jnp.logical_or(s < 2 * blk, jnp.logical_not(last))))
        bias = jnp.where(valid, 0.0, -1e30)
        for kvh in range(SWA_KV_HEADS):
            c0 = kvh * 256
            q4 = jnp.concatenate([q_ref[0, pl.ds(row0, blk), c0:c0 + 128],
                                  q_ref[0, pl.ds(row0, blk), c0 + 128:c0 + 256]], axis=0)
            kl = jnp.concatenate([kwin[pl.ds(row0, 3 * blk), c0:c0 + 128],
                                  kwin[pl.ds(row0, 3 * blk), c0 + 128:c0 + 256]], axis=0)
            vl = jnp.concatenate([vwin[pl.ds(row0, 3 * blk), c0:c0 + 128],
                                  vwin[pl.ds(row0, 3 * blk), c0 + 128:c0 + 256]], axis=0)
            s_loc = _dot_nt(q4, kl)
            s_ctx = _dot_nt(q4, kc_ref[0, kvh])
            lc = s_ctx.shape[1] // 2
            p_loc, p_ctx, inv = [], [], []
            for pp in range(2):
                pl_row, pc_row, inv_row = [], [], []
                for hf in range(2):
                    sk = sink_ref[kvh * 4 + pp * 2 + hf]
                    sl = s_loc[pp * blk:(pp + 1) * blk, hf * 3 * blk:(hf + 1) * 3 * blk] + bias
                    sc = s_ctx[pp * blk:(pp + 1) * blk, hf * lc:(hf + 1) * lc]
                    m = jnp.maximum(jnp.maximum(jnp.max(sl, axis=-1, keepdims=True),
                                                jnp.max(sc, axis=-1, keepdims=True)), sk)
                    el = jnp.exp2(sl - m)
                    ec = jnp.exp2(sc - m)
                    den = (jnp.sum(el, axis=-1, keepdims=True) + jnp.sum(ec, axis=-1, keepdims=True)
                           + jnp.exp2(sk - m))
                    pl_row.append(el.astype(BF16))
                    pc_row.append(ec.astype(BF16))
                    inv_row.append(1.0 / den)
                p_loc.append(jnp.concatenate(pl_row, axis=1))
                p_ctx.append(jnp.concatenate(pc_row, axis=1))
                inv.append(jnp.where(low_lanes, inv_row[0], inv_row[1]))
            o4 = _dot(jnp.concatenate(p_loc, axis=0), vl) + _dot(jnp.concatenate(p_ctx, axis=0), vc_ref[0, kvh])
            for pp in range(2):
                pair = kvh * 2 + pp
                o_ref[0, pl.ds(row0, blk), pair * 128:(pair + 1) * 128] = (
                    o4[pp * blk:(pp + 1) * blk] * inv[pp]).astype(BF16)
        return carry

    lax.fori_loop(0, tq // blk, qblock, 0, unroll=2)


def swa_attention(q, k2, v2, kc2, vc2, sink, *, tq):
    bsz, n, _ = q.shape
    lc = kc2.shape[1]
    tq = min(tq, n)
    n_tiles = n // tq
    rb = tq // SWA_BLOCK
    nb = n // SWA_BLOCK
    main = pl.BlockSpec((1, tq, 1024), lambda b, i, *_: (b, i, 0))
    prev = pl.BlockSpec((1, SWA_BLOCK, 1024), lambda b, i, *_: (b, jnp.maximum(i * rb - 1, 0), 0))
    nxt = pl.BlockSpec((1, SWA_BLOCK, 1024), lambda b, i, *_: (b, jnp.minimum((i + 1) * rb, nb - 1), 0))
    stack = lambda a: a.reshape(bsz, lc, SWA_KV_HEADS, 2, 128).transpose(0, 2, 3, 1, 4).reshape(
        bsz, SWA_KV_HEADS, 2 * lc, 128)
    kc2, vc2 = stack(kc2), stack(vc2)
    sink = sink * math.log2(math.e)
    cspec = pl.BlockSpec((1, SWA_KV_HEADS, 2 * lc, 128), lambda b, i, *_: (b, 0, 0, 0))
    grid_spec = pltpu.PrefetchScalarGridSpec(
        num_scalar_prefetch=1,
        grid=(bsz, n_tiles),
        in_specs=[main, main, prev, nxt, main, prev, nxt, cspec, cspec],
        out_specs=main,
        scratch_shapes=[pltpu.VMEM((tq + 2 * SWA_BLOCK, 1024), BF16)] * 2,
    )
    return pl.pallas_call(
        functools.partial(_swa_kernel, tq=tq, n_tiles=n_tiles),
        grid_spec=grid_spec,
        out_shape=jax.ShapeDtypeStruct((bsz, n, 1024), BF16),
        compiler_params=_cparams(("arbitrary", "arbitrary")),
        name="swa_attention",
    )(sink, q, k2, k2, k2, v2, v2, v2, kc2, vc2)


def _prep_ab(w_in, w_uq, w_ukv, w_out):
    d = w_in.shape[0]
    perm = _rope_perm(MLA_ROPE)
    o = 3 * CONV_DIM + MLA_Q_LORA + MLA_KV_LORA
    dq = MLA_NOPE + MLA_ROPE
    kr = w_in[:, o:]
    rot_lanes = lambda t: jnp.pad(t, ((0, 0),) * (t.ndim - 1) + ((MLA_NOPE, LANES - dq),))
    heads = lambda t: t.reshape(t.shape[0], MLA_HEADS * LANES)
    win2 = jnp.concatenate([w_in[:, :o], rot_lanes(kr), rot_lanes(kr[:, perm])], axis=1).astype(BF16)
    wq = w_uq.reshape(MLA_Q_LORA, MLA_HEADS, dq)
    qa = heads(jnp.pad(wq, ((0, 0), (0, 0), (0, LANES - dq))))
    qb = heads(rot_lanes(wq[:, :, MLA_NOPE:][:, :, perm]))
    wq2 = jnp.concatenate([qa, qb], axis=1).astype(BF16)
    wkv = w_ukv.reshape(MLA_KV_LORA, MLA_HEADS, MLA_NOPE + MLA_V)
    half_lanes = lambda t: heads(jnp.pad(t, ((0, 0), (0, 0), (0, LANES - t.shape[-1]))))
    wkv2 = jnp.concatenate([half_lanes(wkv[:, :, :MLA_NOPE]), half_lanes(wkv[:, :, MLA_NOPE:])], axis=1).astype(BF16)
    wc = w_out[:CONV_DIM].astype(BF16)
    wa = jnp.pad(w_out[CONV_DIM:].reshape(MLA_HEADS, MLA_V, d), ((0, 0), (0, LANES - MLA_V), (0, 0)))
    return win2, wq2, wkv2, wc, wa.reshape(MLA_HEADS * LANES, d).astype(BF16)


def _prep_swa(w_qkv):
    return w_qkv.astype(BF16)


def _pad_lanes(t, left, width=LANES, fill=0.0):
    n, w = t.shape
    return jnp.concatenate([jnp.full((n, left), fill, F32), t, jnp.zeros((n, width - left - w), F32)], axis=1)


def kernel(x, c, ctx, c_ctx, mod_w, mod_b, norm1_g, norm2_g, ab_w_in, conv_w, conv_b, mla_q_norm_g, mla_w_uq,
           mla_kv_norm_g, mla_w_ukv, ab_w_out, swa_w_qkv, swa_sink, swa_w_out, router_w, exp_w_gate, exp_w_up,
           exp_w_down, final_g):
    bsz, n, d = x.shape
    lc = ctx.shape[1]
    depth = mod_w.shape[0]
    assert bsz <= 2 and d == 1024

    cs = jnp.concatenate([c, c_ctx[None, :], jnp.zeros((8 - bsz - 1, d), F32)], axis=0)
    mod = modulation(cs, mod_w, mod_b)

    cos_m, sin_m = _rope_tables(n, MLA_ROPE)
    cos_mla = _pad_lanes(cos_m, 64, fill=1.0)
    sin_mla = _pad_lanes(sin_m, 64)
    cos_mla_c = jnp.concatenate([jnp.ones((lc, 96), F32), jnp.zeros((lc, 32), F32)], axis=1)
    zeros_c = jnp.zeros((lc, 128), F32)
    cos_s, sin_s = _rope_tables(n, SWA_HEAD_DIM)
    cos_swa = jnp.concatenate([cos_s, cos_s], axis=1)
    sin_swa = jnp.concatenate([sin_s, sin_s], axis=1)
    ones_c = jnp.ones((lc, 128), F32)

    row2 = lambda v: v.reshape(1, -1)
    xs, xc = x, ctx
    for layer in range(depth):
        need_ctx = layer < depth - 1
        last = layer == depth - 1
        mod_l = mod[layer]
        g1 = row2(norm1_g[layer])
        g2 = row2(norm2_g[layer])
        rw_f = jnp.concatenate([router_w[layer], jnp.zeros((d, LANES - N_EXPERTS), F32)], axis=1)
        rw_hi = rw_f.astype(BF16)
        rw = jnp.concatenate([rw_hi, (rw_f - rw_hi.astype(F32)).astype(BF16)], axis=1)
        wg, wu, wd = exp_w_gate, exp_w_up, exp_w_down
        if layer % 2 == 0:
            e = layer // 2
            win2, wq2, wkv2, wc, wa = _prep_ab(ab_w_in[e], mla_w_uq[e], mla_w_ukv[e], ab_w_out[e])
            qg, kvg = row2(mla_q_norm_g[e]), row2(mla_kv_norm_g[e])
            cw, cb = conv_w[e], row2(conv_b[e])
            gb, z, q, k, v = ab_in(xs, mod_l, g1, win2, qg, wq2, kvg, wkv2, cos_mla, sin_mla, ctx=False, tm=1024)
            gbc, zc, qc, kc, vc = ab_in(xc, mod_l, g1, win2, qg, wq2, kvg, wkv2, cos_mla_c, zeros_c, ctx=True, tm=256)
            att = mla_attention_flat(q, k, v, kc, vc, tq=256, tk_max=3328, spt=12)
            xs1, h2, aff = mix_out(att, wa, xs, mod_l, g2, rw, (gb, z, cw, cb, wc), ctx=False, tm=1024)
            if need_ctx:
                att_c = mla_attention_ctx(qc, kc, vc)
                xc1, hc2, affc = mix_out(att_c, wa, xc, mod_l, g2, rw, (gbc, zc, cw, cb, wc), ctx=True, tm=256)
        else:
            o = layer // 2
            w2 = _prep_swa(swa_w_qkv[o])
            wo = swa_w_out[o].astype(BF16)
            q, k2, v2 = swa_in(xs, mod_l, g1, w2, cos_swa, sin_swa, ctx=False, tm=1024)
            qc, kc2, vc2 = swa_in(xc, mod_l, g1, w2, ones_c, zeros_c, ctx=True, tm=256)
            att = swa_attention(q, k2, v2, kc2, vc2, swa_sink[o], tq=512)
            xs1, h2, aff = mix_out(att, wo, xs, mod_l, g2, rw, ctx=False, tm=1024)
            if need_ctx:
                raise NotImplementedError("context self-attention for windowed layers below the last")
        token_sets = [(xs1, h2, aff, False)] + ([(xc1, hc2, affc, True)] if need_ctx else [])
        outs = moe(token_sets, mod_l, wg, wu, wd, layer, final_g=row2(final_g) if last else None)
        xs = outs[0]
        if need_ctx:
            xc = outs[1]
    return xs
```

```python
import functools
import math

import jax
import jax.numpy as jnp
import numpy as np
from jax import lax
from jax.experimental import pallas as pl
from jax.experimental.pallas import tpu as pltpu

F32 = jnp.float32
BF16 = jnp.bfloat16
I32 = jnp.int32

GRID_W = 64
EPS = 1e-6
ROPE_THETA = 10000.0
CONV_DIM = 512
MLA_HEADS = 8
MLA_Q_LORA = 256
MLA_KV_LORA = 128
MLA_NOPE = 64
MLA_ROPE = 32
MLA_V = 64
SWA_HEADS = 16
SWA_KV_HEADS = 4
SWA_HEAD_DIM = 64
WINDOW = 128
SWA_BLOCK = 128
N_EXPERTS = 16
EC_CAPACITY = 2
N_MOD = 6

LANES = 128
ROW_ALIGN = 16
VMEM_LIMIT = 56 * 1024 * 1024


def _cparams(sem, vmem=VMEM_LIMIT):
    return pltpu.CompilerParams(dimension_semantics=sem, vmem_limit_bytes=vmem)


def _dot(a, b):
    return jnp.dot(a, b, preferred_element_type=F32)


def _dot_nt(a, b):
    return lax.dot_general(a, b, (((1,), (1,)), ((), ())), preferred_element_type=F32)


def _dot_tn(a, b):
    return lax.dot_general(a, b, (((0,), (0,)), ((), ())), preferred_element_type=F32)


def _dot_hi(a, b):
    return jnp.dot(a, b, preferred_element_type=F32, precision=lax.Precision.HIGHEST)


def _silu(x):
    return x * (1.0 / (1.0 + jnp.exp(-x)))


def _rms_mod(x, g, shift, scale):
    y = x * lax.rsqrt(jnp.mean(x * x, axis=-1, keepdims=True) + EPS)
    return (y * g) * (1.0 + scale) + shift


def _mod_row(mod_ref, ctx, which):
    r = 2 if ctx else pl.program_id(0)
    return mod_ref[pl.ds(r, 1), which * 1024:(which + 1) * 1024]


def _mod_kernel(cs_ref, w_ref, b_ref, o_ref):
    s = _silu(cs_ref[...])
    o_ref[0] = _dot_hi(s, w_ref[0]) + b_ref[0]


def modulation(cs, mod_w, mod_b):
    depth, d, n6 = mod_w.shape
    tn = 1536
    return pl.pallas_call(
        _mod_kernel,
        grid=(depth, n6 // tn),
        in_specs=[
            pl.BlockSpec((8, d), lambda l, j: (0, 0)),
            pl.BlockSpec((1, d, tn), lambda l, j: (l, 0, j)),
            pl.BlockSpec((1, 1, tn), lambda l, j: (l, 0, j)),
        ],
        out_specs=pl.BlockSpec((1, 8, tn), lambda l, j: (l, 0, j)),
        out_shape=jax.ShapeDtypeStruct((depth, 8, n6), F32),
        compiler_params=_cparams(("arbitrary", "arbitrary")),
        name="modulation",
    )(cs, mod_w, mod_b.reshape(depth, 1, n6))


def _rope_tables(n, dim):
    n_rows = n // GRID_W
    half = dim // 2
    qd = dim // 4
    freqs = ROPE_THETA ** (-jnp.arange(0, half, 2, dtype=F32) / half)
    ang_r = jnp.arange(n_rows, dtype=F32)[:, None] * freqs
    ang_c = jnp.arange(GRID_W, dtype=F32)[:, None] * freqs

    def table(fr, fc, sign):
        r = jnp.broadcast_to(fr[:, None, :], (n_rows, GRID_W, qd))
        c = jnp.broadcast_to(fc[None, :, :], (n_rows, GRID_W, qd))
        return jnp.concatenate([sign * r, r, sign * c, c], axis=-1).reshape(n, dim)

    return table(jnp.cos(ang_r), jnp.cos(ang_c), 1.0), table(jnp.sin(ang_r), jnp.sin(ang_c), -1.0)


def _rope_perm(dim):
    qd = dim // 4
    ch = np.arange(dim)
    pair = (ch // qd) % 2
    return np.where(pair == 0, ch + qd, ch - qd)


def _ab_in_kernel(x_ref, mod_ref, g_ref, win_ref, qg_ref, wq_ref, kvg_ref, wkv_ref, cos_ref, sin_ref,
                  gb_ref, z_ref, q_ref, k_ref, v_ref, *, ctx, scale):
    x = x_ref[0]
    h = _rms_mod(x, g_ref[...], _mod_row(mod_ref, ctx, 0), _mod_row(mod_ref, ctx, 1))
    p = _dot(h.astype(BF16), win_ref[...])
    gb_ref[0] = p[:, 0:512].astype(BF16)
    z_ref[0] = (p[:, 512:1024] * p[:, 1024:1536]).astype(BF16)
    ql = p[:, 1536:1792]
    kvl = p[:, 1792:1920]
    kra = p[:, 1920:2048]
    krb = p[:, 2048:2176]
    cos = cos_ref[...]
    sin = sin_ref[...]
    qn = ql * lax.rsqrt(jnp.mean(ql * ql, axis=-1, keepdims=True) + EPS) * qg_ref[...]
    qq = _dot(qn.astype(BF16), wq_ref[...])
    kvn = kvl * lax.rsqrt(jnp.mean(kvl * kvl, axis=-1, keepdims=True) + EPS) * kvg_ref[...]
    kv = _dot(kvn.astype(BF16), wkv_ref[...])
    krope = kra * cos + krb * sin
    cs = cos * scale
    ss = sin * scale
    ones_lane = lax.broadcasted_iota(I32, cos.shape, 1) == MLA_V
    for hd in range(MLA_HEADS):
        a = qq[:, hd * 128:(hd + 1) * 128]
        b = qq[:, 1024 + hd * 128:1024 + (hd + 1) * 128]
        q_ref[0, hd] = (a * cs + b * ss).astype(BF16)
        k_ref[0, hd] = (kv[:, hd * 128:(hd + 1) * 128] + krope).astype(BF16)
        v_ref[0, hd] = jnp.where(ones_lane, 1.0, kv[:, 1024 + hd * 128:1024 + (hd + 1) * 128]).astype(BF16)


def ab_in(x, mod_l, g, win2, qg, wq2, kvg, wkv2, cos_t, sin_t, *, ctx, tm):
    bsz, n, d = x.shape
    tm = min(tm, n)
    scale = (MLA_NOPE + MLA_ROPE) ** -0.5 * math.log2(math.e)
    full = lambda a: pl.BlockSpec(a.shape, lambda b, i: (0,) * a.ndim)
    hshape = jax.ShapeDtypeStruct((bsz, MLA_HEADS, n, 128), BF16)
    hspec = pl.BlockSpec((1, MLA_HEADS, tm, 128), lambda b, i: (b, 0, i, 0))
    return pl.pallas_call(
        functools.partial(_ab_in_kernel, ctx=ctx, scale=scale),
        grid=(bsz, n // tm),
        in_specs=[
            pl.BlockSpec((1, tm, d), lambda b, i: (b, i, 0)),
            full(mod_l), full(g), full(win2), full(qg), full(wq2), full(kvg), full(wkv2),
            pl.BlockSpec((tm, 128), lambda b, i: (i, 0)),
            pl.BlockSpec((tm, 128), lambda b, i: (i, 0)),
        ],
        out_specs=[
            pl.BlockSpec((1, tm, 512), lambda b, i: (b, i, 0)),
            pl.BlockSpec((1, tm, 512), lambda b, i: (b, i, 0)),
            hspec, hspec, hspec,
        ],
        out_shape=[
            jax.ShapeDtypeStruct((bsz, n, 512), BF16),
            jax.ShapeDtypeStruct((bsz, n, 512), BF16),
            hshape, hshape, hshape,
        ],
        compiler_params=_cparams(("arbitrary", "arbitrary")),
        name="ab_in_ctx" if ctx else "ab_in",
    )(x, mod_l, g, win2, qg, wq2, kvg, wkv2, cos_t, sin_t)


def _mla_ctx_kernel(q_ref, kc_ref, vc_ref, o_ref):
    s = _dot_nt(q_ref[0, 0], kc_ref[0, 0])
    m = jnp.max(s, axis=-1, keepdims=True)
    acc = _dot(jnp.exp2(s - m).astype(BF16), vc_ref[0, 0])
    o_ref[0] = (acc * (1.0 / acc[:, MLA_V:MLA_V + 1])).astype(BF16)


def mla_attention_ctx(q, kc, vc):
    bsz, nh, lc, dk = q.shape
    kv = pl.BlockSpec((1, 1, lc, dk), lambda b, h: (b, h, 0, 0))
    return pl.pallas_call(
        _mla_ctx_kernel,
        grid=(bsz, nh),
        in_specs=[kv, kv, kv],
        out_specs=pl.BlockSpec((1, lc, dk), lambda b, h: (b, 0, h)),
        out_shape=jax.ShapeDtypeStruct((bsz, lc, nh * dk), BF16),
        compiler_params=_cparams(("arbitrary", "arbitrary")),
        name="mla_attention_ctx",
    )(q, kc, vc)


def _mla_flat_kernel(q_ref, k_ref, v_ref, kc_ref, vc_ref, o_ref, kall, vall, s_a, s_b, p_a, p_b,
                     *, tq, tk, n_tiles, n_chunks, spt):
    n_stages = n_tiles * n_chunks
    n_lat, n_ctx = k_ref.shape[2], kc_ref.shape[2]
    cp = math.gcd(n_lat, 2048)

    def copy_rows(c, carry):
        sl = pl.ds(pl.multiple_of(c * cp, cp), cp)
        kall[sl, :] = k_ref[0, 0, sl, :]
        vall[sl, :] = v_ref[0, 0, sl, :]
        return carry

    lax.fori_loop(0, n_lat // cp, copy_rows, 0)
    kall[n_lat:n_lat + n_ctx, :] = kc_ref[0, 0]
    vall[n_lat:n_lat + n_ctx, :] = vc_ref[0, 0]

    def nxt(c):
        i, j = c
        wrap = j + 1 == n_chunks
        return jnp.where(wrap, i + 1, i), jnp.where(wrap, 0, j + 1)

    def rows(idx, size):
        return pl.ds(pl.multiple_of(idx * size, size), size)

    def scores(c, s_buf):
        s = _dot_nt(q_ref[0, 0, rows(c[0], tq), :], kall[rows(c[1], tk), :])
        s_buf[...] = s
        return jnp.max(s, axis=-1, keepdims=True)

    def weights(c, s_buf, p_buf, mx, m):
        m_prev = jnp.where(c[1] == 0, -1e30, m)
        m_new = jnp.maximum(m_prev, mx)
        p_buf[...] = jnp.exp2(s_buf[...] - m_new).astype(BF16)
        return m_new, jnp.exp2(m_prev - m_new)

    def wsum(c, p_buf, alpha, acc):
        acc = alpha * acc + _dot(p_buf[...], vall[rows(c[1], tk), :])
        o_ref[0, pl.ds(pl.multiple_of(c[0] * tq, tq), tq), :] = (
            acc * (1.0 / acc[:, MLA_V:MLA_V + 1])).astype(BF16)
        return acc

    def trip(c_pv, c_w, c_s, s_cur, s_nxt, p_prev, p_cur, mx_cur, alpha_prev, m, acc):
        mx_nxt = scores(c_s, s_nxt) if c_s is not None else None
        acc = wsum(c_pv, p_prev, alpha_prev, acc)
        m, alpha_cur = weights(c_w, s_cur, p_cur, mx_cur, m)
        return mx_nxt, alpha_cur, m, acc

    def pair(c, mx_b, alpha_a, m, acc):
        mx_a, alpha_b, m, acc = trip(c[0], c[1], c[2], s_b, s_a, p_a, p_b, mx_b, alpha_a, m, acc)
        mx_b, alpha_a, m, acc = trip(c[1], c[2], c[3], s_a, s_b, p_b, p_a, mx_a, alpha_b, m, acc)
        return mx_b, alpha_a, m, acc

    def chain(c, k):
        out = [c]
        for _ in range(k):
            out.append(nxt(out[-1]))
        return out

    zero = jnp.int32(0)
    c0 = (zero, zero)
    mx_a = scores(c0, s_a)
    m, alpha_a = weights(c0, s_a, p_a, mx_a, jnp.zeros((tq, 1), F32))
    mx_b = scores(nxt(c0), s_b)
    acc = jnp.zeros((tq, LANES), F32)

    def run(n_st, carry):
        i, j, mx_b, alpha_a, m, acc = carry
        c = chain((i, j), n_st + 1)
        for u in range(0, n_st, 2):
            mx_b, alpha_a, m, acc = pair(c[u:u + 4], mx_b, alpha_a, m, acc)
        return c[n_st][0], c[n_st][1], mx_b, alpha_a, m, acc

    carry = run((n_stages - 4) % spt, (zero, zero, mx_b, alpha_a, m, acc))
    i, j, mx_b, alpha_a, m, acc = lax.fori_loop(0, (n_stages - 4) // spt, lambda _, c: run(spt, c), carry)
    c = chain((i, j), 3)
    mx_b, alpha_a, m, acc = pair(c[0:4], mx_b, alpha_a, m, acc)
    mx_a, alpha_b, m, acc = trip(c[2], c[3], None, s_b, s_a, p_a, p_b, mx_b, alpha_a, m, acc)
    wsum(c[3], p_b, alpha_b, acc)


def mla_attention_flat(q, k, v, kc, vc, *, tq, tk_max, spt):
    bsz, nh, n, dk = q.shape
    lc = kc.shape[2]
    nk = n + lc
    tq = min(tq, n)
    tk = max(t for t in range(256, tk_max + 1, 256) if nk % t == 0)
    n_tiles, n_chunks = n // tq, nk // tk
    assert spt % 2 == 0 and (n_tiles * n_chunks) % 2 == 0 and n_tiles * n_chunks >= 4
    head = lambda rows_: pl.BlockSpec((1, 1, rows_, dk), lambda b, h: (b, h, 0, 0))
    return pl.pallas_call(
        functools.partial(_mla_flat_kernel, tq=tq, tk=tk, n_tiles=n_tiles, n_chunks=n_chunks, spt=spt),
        grid=(bsz, nh),
        in_specs=[head(n), head(n), head(n), head(lc), head(lc)],
        out_specs=pl.BlockSpec((1, n, dk), lambda b, h: (b, 0, h)),
        out_shape=jax.ShapeDtypeStruct((bsz, n, nh * dk), BF16),
        scratch_shapes=([pltpu.VMEM((nk, dk), BF16)] * 2 + [pltpu.VMEM((tq, tk), F32)] * 2
                        + [pltpu.VMEM((tq, tk), BF16)] * 2),
        compiler_params=_cparams(("arbitrary", "arbitrary")),
        name="mla_attention",
    )(q, k, v, kc, vc)


def _mix_out_kernel(*refs, ctx, has_conv, tm, n_tiles):
    if has_conv:
        (gb_ref, z_ref, zp_ref, zn_ref, cw_ref, cb_ref, wc_ref,
         att_ref, wa_ref, x_ref, mod_ref, g2_ref, rw_ref, xs_ref, h2_ref, aff_ref) = refs
    else:
        att_ref, wa_ref, x_ref, mod_ref, g2_ref, rw_ref, xs_ref, h2_ref, aff_ref = refs
    y = _dot(att_ref[0], wa_ref[...])
    if has_conv:
        i = pl.program_id(1)
        z = z_ref[0].astype(F32)
        rows = lax.broadcasted_iota(I32, z.shape, 0)
        zprev_halo = jnp.where(i > 0, zp_ref[0, ROW_ALIGN - 1:ROW_ALIGN, :].astype(F32), 0.0)
        znext_halo = jnp.where(i < n_tiles - 1, zn_ref[0, 0:1, :].astype(F32), 0.0)
        zprev = jnp.where(rows == 0, zprev_halo, pltpu.roll(z, 1, 0))
        znext = jnp.where(rows == tm - 1, znext_halo, pltpu.roll(z, tm - 1, 0))
        cw = cw_ref[...]
        conv = gb_ref[0].astype(F32) * (zprev * cw[0:1] + z * cw[1:2] + znext * cw[2:3] + cb_ref[...])
        y = y + _dot(conv.astype(BF16), wc_ref[...])
    xs = x_ref[0] + _mod_row(mod_ref, ctx, 2) * y
    xs_ref[0] = xs
    h2 = _rms_mod(xs, g2_ref[...], _mod_row(mod_ref, ctx, 3), _mod_row(mod_ref, ctx, 4))
    h2_ref[0] = h2.astype(BF16)
    h2_hi = h2.astype(BF16)
    h2_lo = (h2 - h2_hi.astype(F32)).astype(BF16)
    hh = _dot(h2_hi, rw_ref[...])
    logits = hh[:, :LANES] + hh[:, LANES:] + _dot(h2_lo, rw_ref[:, :LANES])
    lane = lax.broadcasted_iota(I32, logits.shape, 1)
    logits = jnp.where(lane < N_EXPERTS, logits, -1e30)
    mx = jnp.max(logits, axis=-1, keepdims=True)
    ex = jnp.exp(logits - mx)
    aff = ex / jnp.sum(ex, axis=-1, keepdims=True)
    aff_ref[0] = aff[:, :N_EXPERTS]


def mix_out(att, wa, x, mod_l, g2, rw, conv=None, *, ctx, tm):
    bsz, n, d = x.shape
    tm = min(tm, n)
    n_tiles = n // tm
    full = lambda a: pl.BlockSpec(a.shape, lambda b, i: (0,) * a.ndim)
    row = lambda w: pl.BlockSpec((1, tm, w), lambda b, i: (b, i, 0))
    ins, in_specs = [], []
    if conv is not None:
        gb, z, cw, cb, wc = conv
        r8 = tm // ROW_ALIGN
        nb8 = n // ROW_ALIGN
        ins += [gb, z, z, z, cw, cb, wc]
        in_specs += [
            row(512), row(512),
            pl.BlockSpec((1, ROW_ALIGN, 512), lambda b, i: (b, jnp.maximum(i * r8 - 1, 0), 0)),
            pl.BlockSpec((1, ROW_ALIGN, 512), lambda b, i: (b, jnp.minimum((i + 1) * r8, nb8 - 1), 0)),
            full(cw), full(cb), full(wc),
        ]
    ins += [att, wa, x, mod_l, g2, rw]
    in_specs += [row(att.shape[-1]), full(wa), row(d), full(mod_l), full(g2), full(rw)]
    return pl.pallas_call(
        functools.partial(_mix_out_kernel, ctx=ctx, has_conv=conv is not None, tm=tm, n_tiles=n_tiles),
        grid=(bsz, n_tiles),
        in_specs=in_specs,
        out_specs=[row(d), row(d), row(N_EXPERTS)],
        out_shape=[
            jax.ShapeDtypeStruct((bsz, n, d), F32),
            jax.ShapeDtypeStruct((bsz, n, d), BF16),
            jax.ShapeDtypeStruct((bsz, n, N_EXPERTS), F32),
        ],
        compiler_params=_cparams(("arbitrary", "arbitrary")),
        name=("mix_out_ctx" if ctx else "mix_out") + ("_conv" if conv is not None else ""),
    )(*ins)


def _route_kernel(aff_ref, aff8_ref, ls_ref, us_ref, eye_ref, rank_ref, rankt_ref, oa_ref, nwd_ref, nwc_ref,
                  thr_s, need_s, eq_s, oa_s, *, n, cap, tb, win_d, win_c, chunk):
    n_chunks = (n // 8) // chunk
    nblk = n // tb
    k = pl.program_id(1)

    def count(pred_fn):
        def body(c, acc):
            kb = pltpu.bitcast(aff8_ref[0, pl.ds(pl.multiple_of(c * chunk, chunk), chunk), :], I32)
            return acc + jnp.sum(pred_fn(kb).astype(I32), axis=0, keepdims=True)
        acc = lax.fori_loop(0, n_chunks, body, jnp.zeros((1, LANES), I32))
        for sh in (64, 32, 16):
            acc = acc + pltpu.roll(acc, sh, 1)
        return acc

    @pl.when(k == 0)
    def _():
        def bit_body(i, thr):
            cand = thr | jnp.left_shift(jnp.int32(1), 30 - i)
            cnt = count(lambda kb: kb >= cand)
            return jnp.where(cnt >= cap, cand, thr)

        thr = lax.fori_loop(0, 31, bit_body, jnp.zeros((1, LANES), I32))
        thr_s[...] = thr[:, :N_EXPERTS]
        need_s[...] = (cap - count(lambda kb: kb > thr))[:, :N_EXPERTS].astype(F32)
        eq_s[...] = jnp.zeros((1, N_EXPERTS), F32)
        oa_s[...] = jnp.zeros((1, N_EXPERTS), F32)

    thr = thr_s[...]
    off = pl.multiple_of(k * tb, tb)
    kb = pltpu.bitcast(aff_ref[0, pl.ds(off, tb), :], I32)
    gt = kb > thr
    eq = kb == thr
    eqf = jnp.where(eq, 1.0, 0.0)
    eqrank = eq_s[...] + _dot(ls_ref[...], eqf.astype(BF16))
    sel = jnp.logical_or(gt, jnp.logical_and(eq, eqrank < need_s[...]))
    self_ = jnp.where(sel, 1.0, 0.0)
    selb = self_.astype(BF16)
    lrank = _dot(ls_ref[...], selb)
    gate_bits = pltpu.bitcast(aff_ref[0, pl.ds(off, tb), :].astype(BF16).astype(F32), I32)
    rank_ref[0] = gate_bits | jnp.where(sel, lrank.astype(I32) + 1, 0)
    lrank_t = _dot_tn(selb, us_ref[...])
    sel_t = _dot_tn(selb, eye_ref[...])
    rankt_ref[0] = jnp.where(sel_t > 0.5, lrank_t, -1.0)
    c = jnp.sum(self_, axis=0, keepdims=True)
    oa_run = oa_s[...]
    oa_ref[0, pl.ds(k, 1), :] = oa_run.astype(I32)
    nwd_ref[0, pl.ds(k, 1), :] = jnp.ceil(c * (1.0 / win_d)).astype(I32)
    nwc_ref[0, pl.ds(k, 1), :] = jnp.ceil(c * (1.0 / win_c)).astype(I32)
    oa_new = oa_run + jnp.ceil(c * (1.0 / ROW_ALIGN)) * ROW_ALIGN
    oa_s[...] = oa_new
    eq_s[...] = eq_s[...] + jnp.sum(eqf, axis=0, keepdims=True)

    @pl.when(k == nblk - 1)
    def _():
        oa_ref[0, pl.ds(nblk, 1), :] = oa_new.astype(I32)


def route(aff, *, tb, win_d, win_c):
    bsz, n, ne = aff.shape
    cap = max(1, (EC_CAPACITY * n) // ne)
    nblk = n // tb
    assert ne * 8 == LANES
    aff8 = aff.reshape(bsz, n // 8, LANES)
    chunk = min(256, n // 8)
    ii = np.arange(tb)
    ls = jnp.asarray(ii[None, :] < ii[:, None], BF16)
    us = jnp.asarray(ii[:, None] < ii[None, :], BF16)
    eye = jnp.asarray(ii[:, None] == ii[None, :], BF16)
    full = lambda a: pl.BlockSpec(a.shape, lambda b, k: (0,) * a.ndim)
    return pl.pallas_call(
        functools.partial(_route_kernel, n=n, cap=cap, tb=tb, win_d=win_d, win_c=win_c, chunk=chunk),
        grid=(bsz, nblk),
        in_specs=[pl.BlockSpec((1, n, ne), lambda b, k: (b, 0, 0)),
                  pl.BlockSpec((1, n // 8, LANES), lambda b, k: (b, 0, 0)), full(ls), full(us), full(eye)],
        out_specs=[
            pl.BlockSpec((1, tb, ne), lambda b, k: (b, k, 0)),
            pl.BlockSpec((1, ne, tb), lambda b, k: (b, 0, k)),
            pl.BlockSpec((1, nblk + 1, ne), lambda b, k: (b, 0, 0)),
            pl.BlockSpec((1, nblk, ne), lambda b, k: (b, 0, 0)),
            pl.BlockSpec((1, nblk, ne), lambda b, k: (b, 0, 0)),
        ],
        out_shape=[
            jax.ShapeDtypeStruct((bsz, n, ne), I32),
            jax.ShapeDtypeStruct((bsz, ne, n), F32),
            jax.ShapeDtypeStruct((bsz, nblk + 1, ne), I32),
            jax.ShapeDtypeStruct((bsz, nblk, ne), I32),
            jax.ShapeDtypeStruct((bsz, nblk, ne), I32),
        ],
        scratch_shapes=[pltpu.VMEM((1, ne), I32), pltpu.VMEM((1, ne), F32),
                        pltpu.VMEM((1, ne), F32), pltpu.VMEM((1, ne), F32)],
        compiler_params=_cparams(("arbitrary", "arbitrary")),
        name="route",
    )(aff, aff8, ls, us, eye)


def _dispatch_kernel(oa_ref, nw_ref, rankt_ref, h_ref, xg_ref, stack, sem, *, nblk, n_steps, win, tb, group, tail):
    b = pl.program_id(0)
    k = pl.program_id(1)
    step = b * nblk + k
    slot = step % 2
    ne = N_EXPERTS

    def fill(sl, j):
        h = h_ref[0]
        for g0 in range(0, ne, group):
            pieces = []
            for e in range(g0, g0 + group):
                r = rankt_ref[0, e:e + 1, :]
                srow = lax.broadcasted_iota(I32, (win, tb), 0).astype(F32) + (j * win).astype(F32)
                pieces.append((r == srow).astype(BF16))
            oh_t = jnp.concatenate(pieces, axis=0)
            stack[sl, g0 * win:(g0 + group) * win, :] = _dot(oh_t, h).astype(BF16)

    def copies(sl, bb, kk, j):
        out = []
        for e in range(ne):
            off = pl.multiple_of(oa_ref[(bb * (nblk + 1) + kk) * ne + e] + j * win, ROW_ALIGN)
            out.append(pltpu.make_async_copy(
                stack.at[sl, pl.ds(e * win, win), :],
                xg_ref.at[bb, e, pl.ds(off, win), :],
                sem.at[sl]))
        return out

    @pl.when(k == 0)
    def _():
        stack[2, 0:win, :] = jnp.zeros((win, stack.shape[2]), BF16)
        cs = [pltpu.make_async_copy(stack.at[2, pl.ds(0, sz), :], xg_ref.at[b, e, pl.ds(off, sz), :], sem.at[2])
              for e in range(ne) for off, sz in tail]
        for c in cs:
            c.start()
        for c in cs:
            c.wait()

    fill(slot, jnp.int32(0))

    @pl.when(step > 0)
    def _():
        for c in copies(1 - slot, b, k, 0):
            c.wait()

    for c in copies(slot, b, k, 0):
        c.start()

    nws = [nw_ref[(b * nblk + k) * ne + e] for e in range(ne)]
    nwx = functools.reduce(jnp.maximum, nws)

    def extra(j, carry):
        fill(2, j)
        cs = copies(2, b, k, j)
        for e, c in enumerate(cs):
            pl.when(j < nws[e])(c.start)
        for e, c in enumerate(cs):
            pl.when(j < nws[e])(c.wait)
        return carry

    lax.fori_loop(1, jnp.maximum(nwx, 1), extra, 0)

    @pl.when(step == n_steps - 1)
    def _():
        for c in copies(slot, b, k, 0):
            c.wait()


def dispatch(oa, nw, rank_t, h2, *, tb, win, rows, tail):
    bsz, n, d = h2.shape
    ne = N_EXPERTS
    nblk = n // tb
    group = 8
    grid_spec = pltpu.PrefetchScalarGridSpec(
        num_scalar_prefetch=2,
        grid=(bsz, nblk),
        in_specs=[
            pl.BlockSpec((1, ne, tb), lambda b, k, *_: (b, 0, k)),
            pl.BlockSpec((1, tb, d), lambda b, k, *_: (b, k, 0)),
        ],
        out_specs=pl.BlockSpec(memory_space=pl.ANY),
        scratch_shapes=[
            pltpu.VMEM((3, ne * win, d), BF16),
            pltpu.SemaphoreType.DMA((3,)),
        ],
    )
    return pl.pallas_call(
        functools.partial(_dispatch_kernel, nblk=nblk, n_steps=bsz * nblk, win=win, tb=tb, group=group, tail=tail),
        grid_spec=grid_spec,
        out_shape=jax.ShapeDtypeStruct((bsz, ne, rows, d), BF16),
        compiler_params=_cparams(("arbitrary", "arbitrary")),
        name="dispatch",
    )(oa.reshape(-1), nw.reshape(-1), rank_t, h2)


def _ffn_kernel(tot_ref, x_ref, *rest, tm, nblk, n_tiles, has_small):
    if has_small:
        xs_ref, wg_ref, wu_ref, wd_ref, y_ref, ys_ref, wgb, wub, wdb = rest
    else:
        wg_ref, wu_ref, wd_ref, y_ref, wgb, wub, wdb = rest
    e = pl.program_id(0)
    b = pl.program_id(1)
    i = pl.program_id(2)

    @pl.when(jnp.logical_and(b == 0, i == 0))
    def _():
        wgb[...] = wg_ref[0, 0].astype(BF16)
        wub[...] = wu_ref[0, 0].astype(BF16)
        wdb[...] = wd_ref[0, 0].astype(BF16)

    def ffn(x):
        hid = (_silu(_dot(x, wgb[...])) * _dot(x, wub[...])).astype(BF16)
        return _dot(hid, wdb[...]).astype(BF16)

    total = tot_ref[(b * (nblk + 1) + nblk) * N_EXPERTS + e]

    t = n_tiles - i if has_small else n_tiles - 1 - i
    is_tile = i >= int(has_small)

    @pl.when(jnp.logical_and(is_tile, t * tm < total))
    def _():
        y_ref[0, 0] = ffn(x_ref[0, 0])

    @pl.when(jnp.logical_and(is_tile, t * tm >= total))
    def _():
        y_ref[0, 0] = jnp.zeros(y_ref.shape[2:], BF16)

    if has_small:
        @pl.when(i == 0)
        def _():
            ys_ref[0, 0] = ffn(xs_ref[0, 0])


def expert_ffn(oa, xg, w_gate, w_up, w_down, layer, xg_small=None, *, tm, nblk):
    bsz, ne, c_pad, d = xg.shape
    f = w_gate.shape[-1]
    n_tiles = c_pad // tm
    has_small = xg_small is not None
    last = n_tiles - 1
    tile = pl.BlockSpec((1, 1, tm, d), lambda e, b, i, *_: (b, e, jnp.clip(last + int(has_small) - i, 0, last), 0))
    wspec = lambda r, c: pl.BlockSpec((1, 1, r, c), lambda e, b, i, *_: (layer, e, 0, 0))
    ins, in_specs, out_specs, out_shape = [xg], [tile], [tile], [jax.ShapeDtypeStruct(xg.shape, BF16)]
    if has_small:
        small = pl.BlockSpec((1, 1) + xg_small.shape[2:], lambda e, b, i, *_: (b, e, 0, 0))
        ins.append(xg_small)
        in_specs.append(small)
        out_specs.append(small)
        out_shape.append(jax.ShapeDtypeStruct(xg_small.shape, BF16))
    grid_spec = pltpu.PrefetchScalarGridSpec(
        num_scalar_prefetch=1,
        grid=(ne, bsz, n_tiles + int(has_small)),
        in_specs=in_specs + [wspec(d, f), wspec(d, f), wspec(f, d)],
        out_specs=out_specs,
        scratch_shapes=[pltpu.VMEM((d, f), BF16), pltpu.VMEM((d, f), BF16), pltpu.VMEM((f, d), BF16)],
    )
    return pl.pallas_call(
        functools.partial(_ffn_kernel, tm=tm, nblk=nblk, n_tiles=n_tiles, has_small=has_small),
        grid_spec=grid_spec,
        out_shape=out_shape,
        compiler_params=_cparams(("arbitrary", "arbitrary", "arbitrary")),
        name="expert_ffn",
    )(oa.reshape(-1), *ins, w_gate, w_up, w_down)


def _combine_kernel(oa_ref, nw_ref, rank_ref, xs_ref, mod_ref, *rest,
                    ctx, nblk, n_steps, win, tb, final):
    if final:
        fg_ref, ye_ref, o_ref, stack, sem = rest
    else:
        ye_ref, o_ref, stack, sem = rest
    b = pl.program_id(0)
    k = pl.program_id(1)
    step = b * nblk + k
    slot = step % 2
    ne = N_EXPERTS

    def copies(sl, bb, kk, j):
        out = []
        for e in range(ne):
            off = pl.multiple_of(oa_ref[(bb * (nblk + 1) + kk) * ne + e] + j * win, ROW_ALIGN)
            out.append(pltpu.make_async_copy(
                ye_ref.at[bb, e, pl.ds(off, win), :],
                stack.at[sl, pl.ds(e * win, win), :],
                sem.at[sl]))
        return out

    @pl.when(step == 0)
    def _():
        stack[2] = jnp.zeros(stack.shape[1:], BF16)
        for c in copies(0, b, k, 0):
            c.start()

    @pl.when(step + 1 < n_steps)
    def _():
        nxt = step + 1
        for c in copies(1 - slot, nxt // nblk, nxt % nblk, 0):
            c.start()

    for c in copies(slot, b, k, 0):
        c.wait()

    def weighted_sum(j, sl):
        word = rank_ref[0]
        slot1 = lax.broadcasted_iota(I32, (tb, win), 1) + (j * win + 1)
        y = None
        for e0 in range(0, ne, 2):
            pieces = []
            for e in (e0, e0 + 1):
                wb = jnp.broadcast_to(word[:, e:e + 1], (tb, win))
                gate = pltpu.bitcast(wb & jnp.int32(-65536), F32)
                pieces.append(jnp.where((wb & 0xFFFF) == slot1, gate, 0.0).astype(BF16))
            part = _dot(jnp.concatenate(pieces, axis=1), stack[sl, e0 * win:(e0 + 2) * win, :])
            y = part if y is None else y + part
        return y

    y = weighted_sum(jnp.int32(0), slot)

    nws = [nw_ref[(b * nblk + k) * ne + e] for e in range(ne)]
    nwx = functools.reduce(jnp.maximum, nws)

    def extra(j, y):
        cs = copies(2, b, k, j)
        for e, c in enumerate(cs):
            pl.when(j < nws[e])(c.start)
        for e, c in enumerate(cs):
            pl.when(j < nws[e])(c.wait)
        return y + weighted_sum(j, 2)

    y = lax.fori_loop(1, jnp.maximum(nwx, 1), extra, y)
    out = xs_ref[0] + _mod_row(mod_ref, ctx, 5) * y
    if final:
        out = out * lax.rsqrt(jnp.mean(out * out, axis=-1, keepdims=True) + EPS) * fg_ref[...]
    o_ref[0] = out


def combine(oa, nw, rank, xs, mod_l, ye, final_g=None, *, ctx, tb, win):
    bsz, n, d = xs.shape
    ne = N_EXPERTS
    nblk = n // tb
    final = final_g is not None
    full = lambda a: pl.BlockSpec(a.shape, lambda b, k, *_: (0,) * a.ndim)
    ins = [rank, xs, mod_l]
    in_specs = [
        pl.BlockSpec((1, tb, ne), lambda b, k, *_: (b, k, 0)),
        pl.BlockSpec((1, tb, d), lambda b, k, *_: (b, k, 0)),
        full(mod_l),
    ]
    if final:
        ins.append(final_g)
        in_specs.append(full(final_g))
    ins.append(ye)
    in_specs.append(pl.BlockSpec(memory_space=pl.ANY))
    grid_spec = pltpu.PrefetchScalarGridSpec(
        num_scalar_prefetch=2,
        grid=(bsz, nblk),
        in_specs=in_specs,
        out_specs=pl.BlockSpec((1, tb, d), lambda b, k, *_: (b, k, 0)),
        scratch_shapes=[
            pltpu.VMEM((3, ne * win, d), BF16),
            pltpu.SemaphoreType.DMA((3,)),
        ],
    )
    return pl.pallas_call(
        functools.partial(_combine_kernel, ctx=ctx, nblk=nblk, n_steps=bsz * nblk, win=win, tb=tb, final=final),
        grid_spec=grid_spec,
        out_shape=jax.ShapeDtypeStruct(xs.shape, F32),
        compiler_params=_cparams(("arbitrary", "arbitrary")),
        name="combine_final" if final else ("combine_ctx" if ctx else "combine"),
    )(oa.reshape(-1), nw.reshape(-1), *ins)


def moe(token_sets, mod_l, w_gate, w_up, w_down, layer, final_g=None):
    tm = 768 if token_sets[0][0].shape[1] >= 4096 else 128
    routed = []
    for s, (xs1, h2, aff, is_ctx) in enumerate(token_sets):
        n = xs1.shape[1]
        cap = max(1, (EC_CAPACITY * n) // N_EXPERTS)
        tb = min(512, n)
        win_c = LANES
        win_d = 96 if tb == 512 else LANES
        nblk = n // tb
        align = tm if s == 0 else ROW_ALIGN
        c_pad = -(-(cap + ROW_ALIGN * nblk + win_c) // align) * align
        rank, rank_t, oa, nw_d, nw_c = route(aff, tb=tb, win_d=win_d, win_c=win_c)
        tail = tuple((off, min(win_d, c_pad - off)) for off in range(cap, c_pad, win_d))
        xg = dispatch(oa, nw_d, rank_t, h2, tb=tb, win=win_d, rows=c_pad, tail=tail)
        routed.append(dict(rank=rank, oa=oa, nw_c=nw_c, xg=xg, tb=tb, win_c=win_c, nblk=nblk))
    assert len(routed) <= 2
    yes = expert_ffn(routed[0]["oa"], routed[0]["xg"], w_gate, w_up, w_down, layer,
                     routed[1]["xg"] if len(routed) == 2 else None, tm=tm, nblk=routed[0]["nblk"])
    outs = []
    for s, ((xs1, h2, aff, is_ctx), r, ye) in enumerate(zip(token_sets, routed, yes)):
        outs.append(combine(r["oa"], r["nw_c"], r["rank"], xs1, mod_l, ye, final_g if s == 0 else None, ctx=is_ctx,
                            tb=r["tb"], win=r["win_c"]))
    return outs


def _swa_in_kernel(x_ref, mod_ref, g_ref, w_ref, cos_ref, sin_ref, q_ref, k_ref, v_ref, *, ctx, scale):
    h = _rms_mod(x_ref[0], g_ref[...], _mod_row(mod_ref, ctx, 0), _mod_row(mod_ref, ctx, 1))
    p = _dot(h.astype(BF16), w_ref[...])
    cos = cos_ref[...]
    sin = sin_ref[...]
    cs = cos * scale
    ss = sin * scale
    lane = lax.broadcasted_iota(I32, cos.shape, 1)
    qd = SWA_HEAD_DIM // 4
    first = (lane % (2 * qd)) < qd

    def partner(x):
        return jnp.where(first, pltpu.roll(x, LANES - qd, 1), pltpu.roll(x, qd, 1))

    for j in range(8):
        a = p[:, j * 128:(j + 1) * 128]
        q_ref[0, :, j * 128:(j + 1) * 128] = (a * cs + partner(a) * ss).astype(BF16)
    low = lane < 64
    for pr in range(2):
        kx = p[:, 1024 + pr * 128:1024 + (pr + 1) * 128]
        kp = kx * cos + partner(kx) * sin
        vp = p[:, 1280 + pr * 128:1280 + (pr + 1) * 128]
        for src, dst in ((kp, k_ref), (vp, v_ref)):
            ev_lo = jnp.where(low, src, 0.0)
            od_hi = jnp.where(low, 0.0, src)
            ev_hi = pltpu.roll(ev_lo, 64, 1)
            od_lo = pltpu.roll(od_hi, 64, 1)
            base = pr * 512
            dst[0, :, base:base + 128] = ev_lo.astype(BF16)
            dst[0, :, base + 128:base + 256] = ev_hi.astype(BF16)
            dst[0, :, base + 256:base + 384] = od_lo.astype(BF16)
            dst[0, :, base + 384:base + 512] = od_hi.astype(BF16)


def swa_in(x, mod_l, g, w2, cos_t, sin_t, *, ctx, tm):
    bsz, n, d = x.shape
    tm = min(tm, n)
    full = lambda a: pl.BlockSpec(a.shape, lambda b, i: (0,) * a.ndim)
    row = lambda w: pl.BlockSpec((1, tm, w), lambda b, i: (b, i, 0))
    return pl.pallas_call(
        functools.partial(_swa_in_kernel, ctx=ctx, scale=SWA_HEAD_DIM ** -0.5 * math.log2(math.e)),
        grid=(bsz, n // tm),
        in_specs=[row(d), full(mod_l), full(g), full(w2),
                  pl.BlockSpec((tm, 128), lambda b, i: (i, 0)),
                  pl.BlockSpec((tm, 128), lambda b, i: (i, 0))],
        out_specs=[row(1024), row(1024), row(1024)],
        out_shape=[jax.ShapeDtypeStruct((bsz, n, 1024), BF16)] * 3,
        compiler_params=_cparams(("arbitrary", "arbitrary")),
        name="swa_in_ctx" if ctx else "swa_in",
    )(x, mod_l, g, w2, cos_t, sin_t)


def _swa_kernel(sink_ref, q_ref, km_ref, kp_ref, kn_ref, vm_ref, vp_ref, vn_ref, kc_ref, vc_ref,
                o_ref, kwin, vwin, *, tq, n_tiles):
    i = pl.program_id(1)
    blk = SWA_BLOCK
    kwin[0:blk] = kp_ref[0]
    kwin[blk:blk + tq] = km_ref[0]
    kwin[blk + tq:blk + tq + blk] = kn_ref[0]
    vwin[0:blk] = vp_ref[0]
    vwin[blk:blk + tq] = vm_ref[0]
    vwin[blk + tq:blk + tq + blk] = vn_ref[0]
    r = lax.broadcasted_iota(I32, (blk, 3 * blk), 0)
    s = lax.broadcasted_iota(I32, (blk, 3 * blk), 1)
    band = jnp.abs(r - (s - blk)) <= WINDOW
    low_lanes = lax.broadcasted_iota(I32, (blk, 128), 1) < 64

    def qblock(qb, carry):
        row0 = pl.multiple_of(qb * blk, blk)
        first = jnp.logical_and(i == 0, qb == 0)
        last = jnp.logical_and(i == n_tiles - 1, qb == tq // blk - 1)
        valid = jnp.logical_and(band, jnp.logical_and(jnp.logical_or(s >= blk, jnp.logical_not(first)),
                                                      jnp.logical_or(s < 2 * blk, jnp.logical_not(last))))
        bias = jnp.where(valid, 0.0, -1e30)
        for kvh in range(SWA_KV_HEADS):
            c0 = kvh * 256
            q4 = jnp.concatenate([q_ref[0, pl.ds(row0, blk), c0:c0 + 128],
                                  q_ref[0, pl.ds(row0, blk), c0 + 128:c0 + 256]], axis=0)
            kl = jnp.concatenate([kwin[pl.ds(row0, 3 * blk), c0:c0 + 128],
                                  kwin[pl.ds(row0, 3 * blk), c0 + 128:c0 + 256]], axis=0)
            vl = jnp.concatenate([vwin[pl.ds(row0, 3 * blk), c0:c0 + 128],
                                  vwin[pl.ds(row0, 3 * blk), c0 + 128:c0 + 256]], axis=0)
            s_loc = _dot_nt(q4, kl)
            s_ctx = _dot_nt(q4, kc_ref[0, kvh])
            lc = s_ctx.shape[1] // 2
            p_loc, p_ctx, inv = [], [], []
            for pp in range(2):
                pl_row, pc_row, inv_row = [], [], []
                for hf in range(2):
                    sk = sink_ref[kvh * 4 + pp * 2 + hf]
                    sl = s_loc[pp * blk:(pp + 1) * blk, hf * 3 * blk:(hf + 1) * 3 * blk] + bias
                    sc = s_ctx[pp * blk:(pp + 1) * blk, hf * lc:(hf + 1) * lc]
                    m = jnp.maximum(jnp.maximum(jnp.max(sl, axis=-1, keepdims=True),
                                                jnp.max(sc, axis=-1, keepdims=True)), sk)
                    el = jnp.exp2(sl - m)
                    ec = jnp.exp2(sc - m)
                    den = (jnp.sum(el, axis=-1, keepdims=True) + jnp.sum(ec, axis=-1, keepdims=True)
                           + jnp.exp2(sk - m))
                    pl_row.append(el.astype(BF16))
                    pc_row.append(ec.astype(BF16))
                    inv_row.append(1.0 / den)
                p_loc.append(jnp.concatenate(pl_row, axis=1))
                p_ctx.append(jnp.concatenate(pc_row, axis=1))
                inv.append(jnp.where(low_lanes, inv_row[0], inv_row[1]))
            o4 = _dot(jnp.concatenate(p_loc, axis=0), vl) + _dot(jnp.concatenate(p_ctx, axis=0), vc_ref[0, kvh])
            for pp in range(2):
                pair = kvh * 2 + pp
                o_ref[0, pl.ds(row0, blk), pair * 128:(pair + 1) * 128] = (
                    o4[pp * blk:(pp + 1) * blk] * inv[pp]).astype(BF16)
        return carry

    lax.fori_loop(0, tq // blk, qblock, 0, unroll=2)


def swa_attention(q, k2, v2, kc2, vc2, sink, *, tq):
    bsz, n, _ = q.shape
    lc = kc2.shape[1]
    tq = min(tq, n)
    n_tiles = n // tq
    rb = tq // SWA_BLOCK
    nb = n // SWA_BLOCK
    main = pl.BlockSpec((1, tq, 1024), lambda b, i, *_: (b, i, 0))
    prev = pl.BlockSpec((1, SWA_BLOCK, 1024), lambda b, i, *_: (b, jnp.maximum(i * rb - 1, 0), 0))
    nxt = pl.BlockSpec((1, SWA_BLOCK, 1024), lambda b, i, *_: (b, jnp.minimum((i + 1) * rb, nb - 1), 0))
    stack = lambda a: a.reshape(bsz, lc, SWA_KV_HEADS, 2, 128).transpose(0, 2, 3, 1, 4).reshape(
        bsz, SWA_KV_HEADS, 2 * lc, 128)
    kc2, vc2 = stack(kc2), stack(vc2)
    sink = sink * math.log2(math.e)
    cspec = pl.BlockSpec((1, SWA_KV_HEADS, 2 * lc, 128), lambda b, i, *_: (b, 0, 0, 0))
    grid_spec = pltpu.PrefetchScalarGridSpec(
        num_scalar_prefetch=1,
        grid=(bsz, n_tiles),
        in_specs=[main, main, prev, nxt, main, prev, nxt, cspec, cspec],
        out_specs=main,
        scratch_shapes=[pltpu.VMEM((tq + 2 * SWA_BLOCK, 1024), BF16)] * 2,
    )
    return pl.pallas_call(
        functools.partial(_swa_kernel, tq=tq, n_tiles=n_tiles),
        grid_spec=grid_spec,
        out_shape=jax.ShapeDtypeStruct((bsz, n, 1024), BF16),
        compiler_params=_cparams(("arbitrary", "arbitrary")),
        name="swa_attention",
    )(sink, q, k2, k2, k2, v2, v2, v2, kc2, vc2)


def _prep_ab(w_in, w_uq, w_ukv, w_out):
    d = w_in.shape[0]
    perm = _rope_perm(MLA_ROPE)
    o = 3 * CONV_DIM + MLA_Q_LORA + MLA_KV_LORA
    dq = MLA_NOPE + MLA_ROPE
    kr = w_in[:, o:]
    rot_lanes = lambda t: jnp.pad(t, ((0, 0),) * (t.ndim - 1) + ((MLA_NOPE, LANES - dq),))
    heads = lambda t: t.reshape(t.shape[0], MLA_HEADS * LANES)
    win2 = jnp.concatenate([w_in[:, :o], rot_lanes(kr), rot_lanes(kr[:, perm])], axis=1).astype(BF16)
    wq = w_uq.reshape(MLA_Q_LORA, MLA_HEADS, dq)
    qa = heads(jnp.pad(wq, ((0, 0), (0, 0), (0, LANES - dq))))
    qb = heads(rot_lanes(wq[:, :, MLA_NOPE:][:, :, perm]))
    wq2 = jnp.concatenate([qa, qb], axis=1).astype(BF16)
    wkv = w_ukv.reshape(MLA_KV_LORA, MLA_HEADS, MLA_NOPE + MLA_V)
    half_lanes = lambda t: heads(jnp.pad(t, ((0, 0), (0, 0), (0, LANES - t.shape[-1]))))
    wkv2 = jnp.concatenate([half_lanes(wkv[:, :, :MLA_NOPE]), half_lanes(wkv[:, :, MLA_NOPE:])], axis=1).astype(BF16)
    wc = w_out[:CONV_DIM].astype(BF16)
    wa = jnp.pad(w_out[CONV_DIM:].reshape(MLA_HEADS, MLA_V, d), ((0, 0), (0, LANES - MLA_V), (0, 0)))
    return win2, wq2, wkv2, wc, wa.reshape(MLA_HEADS * LANES, d).astype(BF16)


def _prep_swa(w_qkv):
    return w_qkv.astype(BF16)


def _pad_lanes(t, left, width=LANES, fill=0.0):
    n, w = t.shape
    return jnp.concatenate([jnp.full((n, left), fill, F32), t, jnp.zeros((n, width - left - w), F32)], axis=1)


def kernel(x, c, ctx, c_ctx, mod_w, mod_b, norm1_g, norm2_g, ab_w_in, conv_w, conv_b, mla_q_norm_g, mla_w_uq,
           mla_kv_norm_g, mla_w_ukv, ab_w_out, swa_w_qkv, swa_sink, swa_w_out, router_w, exp_w_gate, exp_w_up,
           exp_w_down, final_g):
    bsz, n, d = x.shape
    lc = ctx.shape[1]
    depth = mod_w.shape[0]
    assert bsz <= 2 and d == 1024

    cs = jnp.concatenate([c, c_ctx[None, :], jnp.zeros((8 - bsz - 1, d), F32)], axis=0)
    mod = modulation(cs, mod_w, mod_b)

    cos_m, sin_m = _rope_tables(n, MLA_ROPE)
    cos_mla = _pad_lanes(cos_m, 64, fill=1.0)
    sin_mla = _pad_lanes(sin_m, 64)
    cos_mla_c = jnp.concatenate([jnp.ones((lc, 96), F32), jnp.zeros((lc, 32), F32)], axis=1)
    zeros_c = jnp.zeros((lc, 128), F32)
    cos_s, sin_s = _rope_tables(n, SWA_HEAD_DIM)
    cos_swa = jnp.concatenate([cos_s, cos_s], axis=1)
    sin_swa = jnp.concatenate([sin_s, sin_s], axis=1)
    ones_c = jnp.ones((lc, 128), F32)

    row2 = lambda v: v.reshape(1, -1)
    xs, xc = x, ctx
    for layer in range(depth):
        need_ctx = layer < depth - 1
        last = layer == depth - 1
        mod_l = mod[layer]
        g1 = row2(norm1_g[layer])
        g2 = row2(norm2_g[layer])
        rw_f = jnp.concatenate([router_w[layer], jnp.zeros((d, LANES - N_EXPERTS), F32)], axis=1)
        rw_hi = rw_f.astype(BF16)
        rw = jnp.concatenate([rw_hi, (rw_f - rw_hi.astype(F32)).astype(BF16)], axis=1)
        wg, wu, wd = exp_w_gate, exp_w_up, exp_w_down
        if layer % 2 == 0:
            e = layer // 2
            win2, wq2, wkv2, wc, wa = _prep_ab(ab_w_in[e], mla_w_uq[e], mla_w_ukv[e], ab_w_out[e])
            qg, kvg = row2(mla_q_norm_g[e]), row2(mla_kv_norm_g[e])
            cw, cb = conv_w[e], row2(conv_b[e])
            gb, z, q, k, v = ab_in(xs, mod_l, g1, win2, qg, wq2, kvg, wkv2, cos_mla, sin_mla, ctx=False, tm=1024)
            gbc, zc, qc, kc, vc = ab_in(xc, mod_l, g1, win2, qg, wq2, kvg, wkv2, cos_mla_c, zeros_c, ctx=True, tm=256)
            att = mla_attention_flat(q, k, v, kc, vc, tq=256, tk_max=3328, spt=12)
            xs1, h2, aff = mix_out(att, wa, xs, mod_l, g2, rw, (gb, z, cw, cb, wc), ctx=False, tm=1024)
            if need_ctx:
                att_c = mla_attention_ctx(qc, kc, vc)
                xc1, hc2, affc = mix_out(att_c, wa, xc, mod_l, g2, rw, (gbc, zc, cw, cb, wc), ctx=True, tm=256)
        else:
            o = layer // 2
            w2 = _prep_swa(swa_w_qkv[o])
            wo = swa_w_out[o].astype(BF16)
            q, k2, v2 = swa_in(xs, mod_l, g1, w2, cos_swa, sin_swa, ctx=False, tm=1024)
            qc, kc2, vc2 = swa_in(xc, mod_l, g1, w2, ones_c, zeros_c, ctx=True, tm=256)
            att = swa_attention(q, k2, v2, kc2, vc2, swa_sink[o], tq=512)
            xs1, h2, aff = mix_out(att, wo, xs, mod_l, g2, rw, ctx=False, tm=1024)
            if need_ctx:
                raise NotImplementedError("context self-attention for windowed layers below the last")
        token_sets = [(xs1, h2, aff, False)] + ([(xc1, hc2, affc, True)] if need_ctx else [])
        outs = moe(token_sets, mod_l, wg, wu, wd, layer, final_g=row2(final_g) if last else None)
        xs = outs[0]
        if need_ctx:
            xc = outs[1]
    return xs
```
